```python
import jax, jax.numpy as jnp
from jax import lax
import numpy as np

D_MODEL = 1024
BATCH = 16
SEQ = 256
DEPTH = 2
DEC_BATCH = 4
DEC_SEQ = 4096
PAST_LEN = 256

GRID_W = 64
D_RNN = 1024
N_LRU_BLOCKS = 16
LRU_BLOCK = D_RNN // N_LRU_BLOCKS
CONV_W = 4
LRU_C = 8.0
N_HEADS = 16
HEAD_DIM = 64
D_ATT = N_HEADS * HEAD_DIM
WIN_H = 8
WIN_W = 16
Q_BLOCK_W = WIN_W
K_BLOCK_W = 2 * WIN_W
CTX_Q_BLOCK = 128
N_EXPERTS = 16
N_GROUPS = 4
EXPERTS_PER_GROUP = N_EXPERTS // N_GROUPS
TOP_K = 2
D_EXPERT = 512
N_MOD = 6
D_IN = 2 * D_RNN + 3 * D_ATT + 2 * D_MODEL
EPS = 1e-6
NEG_INF = -1e30

kernel_name = 'hybrid_rglru_natten_moe_prefix_step'


def rms_norm(x, g):
    xf = x.astype(jnp.float32)
    y = xf * lax.rsqrt(jnp.mean(xf * xf, axis=-1, keepdims=True) + EPS)
    return (y * g.astype(jnp.float32)).astype(x.dtype)


def modulation(cvec, w_ada, b_ada):
    m = jax.nn.silu(cvec) @ w_ada + b_ada
    m = m.reshape(m.shape[:-1] + (1, N_MOD, D_MODEL))
    return tuple(m[..., i, :] for i in range(N_MOD))


def modulate(x, g, shift, scale):
    return rms_norm(x, g) * (1 + scale) + shift


def in_projection(h, w_in):
    p = h @ w_in
    b, t = p.shape[:2]
    x_rnn, gate_rnn, q, k, v, gates = jnp.split(
        p, [D_RNN, 2 * D_RNN, 2 * D_RNN + D_ATT, 2 * D_RNN + 2 * D_ATT, 2 * D_RNN + 3 * D_ATT], axis=-1)
    heads = lambda z: z.reshape(b, t, N_HEADS, HEAD_DIM)
    return x_rnn, gate_rnn, heads(q), heads(k), heads(v), gates


def conv_centred(u, w, b):
    t = u.shape[1]
    up = jnp.pad(u, ((0, 0), (CONV_W // 2, CONV_W - 1 - CONV_W // 2), (0, 0)))
    y = b
    for j in range(CONV_W):
        y = y + up[:, j:j + t] * w[j]
    return y


def _linear_combine(e1, e2):
    a1, b1 = e1
    a2, b2 = e2
    return a1 * a2, a2 * b1 + b2


def lru_direction(u, w_a, b_a, w_x, b_x, lam, h0, reverse):
    if reverse:
        u = jnp.flip(u, axis=1)
    b, t, _ = u.shape
    ub = u.reshape(b, t, N_LRU_BLOCKS, LRU_BLOCK)
    r = jax.nn.sigmoid(jnp.einsum('btnc,ncd->btnd', ub, w_a).reshape(b, t, D_RNN) + b_a)
    i = jax.nn.sigmoid(jnp.einsum('btnc,ncd->btnd', ub, w_x).reshape(b, t, D_RNN) + b_x)
    log_a = -LRU_C * jax.nn.softplus(-lam.astype(jnp.float32)) * r.astype(jnp.float32)
    a = jnp.exp(log_a)
    inp = jnp.sqrt(-jnp.expm1(2.0 * log_a)) * (i * u).astype(jnp.float32)
    inp = inp.at[:, 0].add(a[:, 0] * h0.astype(jnp.float32))
    _, h = lax.associative_scan(_linear_combine, (a, inp), axis=1)
    h_final = h[:, -1]
    if reverse:
        h = jnp.flip(h, axis=1)
    return h.astype(u.dtype), h_final.astype(u.dtype)


def rglru_branch(x_rnn, gate_rnn, h0, conv_w, conv_b, lru_wa, lru_ba, lru_wx, lru_bx, lru_lam):
    u = conv_centred(x_rnn, conv_w, conv_b)
    h_f, fin_f = lru_direction(u, lru_wa[0], lru_ba[0], lru_wx[0], lru_bx[0], lru_lam[0], h0[:, 0], False)
    h_b, fin_b = lru_direction(u, lru_wa[1], lru_ba[1], lru_wx[1], lru_bx[1], lru_lam[1], h0[:, 1], True)
    y = (h_f + h_b) * jax.nn.gelu(gate_rnn)
    return y, jnp.stack([fin_f, fin_b], axis=1)


def context_attention(q, k, v):
    b, p_len, h, hd = q.shape
    n_qb = p_len // CTX_Q_BLOCK
    qb = jnp.moveaxis((q * HEAD_DIM ** -0.5).reshape(b, n_qb, CTX_Q_BLOCK, h, hd), 1, 0)

    def block(qi):
        s = jnp.einsum('bqhd,bkhd->bhqk', qi, k).astype(jnp.float32)
        p = jax.nn.softmax(s, axis=-1).astype(v.dtype)
        return jnp.einsum('bhqk,bkhd->bqhd', p, v)

    o = lax.map(block, qb)
    return jnp.moveaxis(o, 0, 1).reshape(b, p_len, h * hd)


def neighbourhood_attention(q, k, v, k_ctx, v_ctx, rpb):
    b, t, h, hd = q.shape
    rows = t // GRID_W
    kh = min(WIN_H, rows)
    n_cb = GRID_W // Q_BLOCK_W
    n_keys = kh * K_BLOCK_W
    cb = jnp.arange(n_cb)
    key_cols = jnp.clip(cb * Q_BLOCK_W - WIN_W // 2, 0, GRID_W - K_BLOCK_W)[:, None] + jnp.arange(K_BLOCK_W)
    q_cols = cb[:, None] * Q_BLOCK_W + jnp.arange(Q_BLOCK_W)
    q_start = jnp.clip(q_cols - WIN_W // 2, 0, GRID_W - WIN_W)
    kc = key_cols[:, None, :]
    col_valid = (kc >= q_start[..., None]) & (kc < q_start[..., None] + WIN_W)
    dc_idx = jnp.clip(kc - q_cols[..., None] + WIN_W - 1, 0, 2 * WIN_W - 2)
    mask = jnp.broadcast_to(col_valid[:, None, :, None, :],
                            (n_cb, 1, Q_BLOCK_W, kh, K_BLOCK_W)).reshape(n_cb, 1, Q_BLOCK_W, n_keys)
    k_grid = k.reshape(b, rows, GRID_W, h, hd)
    v_grid = v.reshape(b, rows, GRID_W, h, hd)
    q_rows = jnp.moveaxis((q * HEAD_DIM ** -0.5).reshape(b, rows, GRID_W, h, hd), 1, 0)

    def row_block(args):
        r, q_row = args
        start = jnp.clip(r - kh // 2, 0, rows - kh)
        dr_idx = start + jnp.arange(kh) - r + WIN_H - 1
        bias = rpb[:, dr_idx[None, None, :, None], dc_idx[:, :, None, :]]
        bias = jnp.transpose(bias, (1, 0, 2, 3, 4)).reshape(n_cb, h, Q_BLOCK_W, n_keys)

        def gather(grid):
            sl = lax.dynamic_slice_in_dim(grid, start, kh, axis=1)
            blk = sl[:, :, key_cols]
            return jnp.transpose(blk, (0, 2, 1, 3, 4, 5)).reshape(b, n_cb, n_keys, h, hd)

        k_blk = gather(k_grid)
        v_blk = gather(v_grid)
        qb = q_row.reshape(b, n_cb, Q_BLOCK_W, h, hd)
        s_lat = jnp.einsum('bcqhd,bckhd->bchqk', qb, k_blk).astype(jnp.float32) + bias.astype(jnp.float32)
        s_lat = jnp.where(mask, s_lat, NEG_INF)
        s_ctx = jnp.einsum('bcqhd,bphd->bchqp', qb, k_ctx).astype(jnp.float32)
        p = jax.nn.softmax(jnp.concatenate([s_lat, s_ctx], axis=-1), axis=-1).astype(v.dtype)
        o = (jnp.einsum('bchqk,bckhd->bcqhd', p[..., :n_keys], v_blk)
             + jnp.einsum('bchqp,bphd->bcqhd', p[..., n_keys:], v_ctx))
        return o.reshape(b, GRID_W, h, hd)

    out = lax.map(row_block, (jnp.arange(rows), q_rows))
    return jnp.moveaxis(out, 0, 1).reshape(b, t, h * hd)


def merge_branches(y_rnn, y_att, gates, w_br_rnn, w_br_att, w_out):
    g_rnn, g_att = jnp.split(jax.nn.sigmoid(gates), 2, axis=-1)
    return (g_rnn * (y_rnn @ w_br_rnn) + g_att * (y_att @ w_br_att)) @ w_out


def moe_ffn(h, w_router, router_bias, w_exp_gate, w_exp_up, w_exp_down):
    shape = h.shape
    t = h.reshape(-1, D_MODEL)
    n = t.shape[0]
    scores = jax.nn.sigmoid((t @ w_router).astype(jnp.float32))
    sel = (scores + router_bias.astype(jnp.float32)).reshape(n, N_GROUPS, EXPERTS_PER_GROUP)
    group_score = lax.top_k(sel, TOP_K)[0].sum(-1)
    best_group = jnp.argmax(group_score, axis=-1)
    in_group = (best_group[:, None] == jnp.arange(N_GROUPS))[..., None]
    masked = jnp.where(in_group, sel, NEG_INF).reshape(n, N_EXPERTS)
    _, idx = lax.top_k(masked, TOP_K)
    w = jnp.take_along_axis(scores, idx, axis=-1)
    w = w / jnp.sum(w, axis=-1, keepdims=True)
    combine = jnp.sum(jax.nn.one_hot(idx, N_EXPERTS, dtype=jnp.float32) * w[..., None], axis=1).astype(h.dtype)
    y = jnp.zeros_like(t)
    for e in range(N_EXPERTS):
        a = jax.nn.silu(t @ w_exp_gate[e]) * (t @ w_exp_up[e])
        y = y + combine[:, e:e + 1] * (a @ w_exp_down[e])
    return y.reshape(shape)


def trunk_layer(x, cvec, ctx_k, ctx_v, h0, w_ada, b_ada, norm_g, w_in, conv_w, conv_b, lru_wa, lru_ba, lru_wx,
                lru_bx, lru_lam, rpb, w_br_rnn, w_br_att, w_out, w_router, router_bias, w_exp_gate, w_exp_up,
                w_exp_down):
    sh1, sc1, g1, sh2, sc2, g2 = modulation(cvec, w_ada, b_ada)
    h = modulate(x, norm_g[0], sh1, sc1)
    x_rnn, gate_rnn, q, k, v, gates = in_projection(h, w_in)
    if h0 is None:
        h0 = jnp.zeros((x.shape[0], 2, D_RNN), x.dtype)
    y_rnn, h_fin = rglru_branch(x_rnn, gate_rnn, h0, conv_w, conv_b, lru_wa, lru_ba, lru_wx, lru_bx, lru_lam)
    if ctx_k is None:
        y_att = context_attention(q, k, v)
        ctx_k, ctx_v = k, v
    else:
        y_att = neighbourhood_attention(q, k, v, ctx_k, ctx_v, rpb)
    x = x + g1 * merge_branches(y_rnn, y_att, gates, w_br_rnn, w_br_att, w_out)
    h = modulate(x, norm_g[1], sh2, sc2)
    x = x + g2 * moe_ffn(h, w_router, router_bias, w_exp_gate, w_exp_up, w_exp_down)
    return x, ctx_k, ctx_v, h_fin


def setup_inputs(seed: int = 0) -> dict:
    key = jax.random.key(seed)
    ks = jax.random.split(key, 28)
    nrm = lambda k, shape, s: jax.random.normal(k, shape, jnp.float32) * s
    a0 = jax.random.uniform(ks[13], (DEPTH, 2, D_RNN), jnp.float32, 0.9, 0.999)
    return {
        'x_prompt': nrm(ks[0], (BATCH, SEQ, D_MODEL), 1.0),
        'x_sample': nrm(ks[1], (DEC_BATCH, DEC_SEQ, D_MODEL), 1.0),
        'cache_k': nrm(ks[2], (DEC_BATCH, DEPTH, PAST_LEN, N_HEADS, HEAD_DIM), 1.0),
        'cache_v': nrm(ks[3], (DEC_BATCH, DEPTH, PAST_LEN, N_HEADS, HEAD_DIM), 1.0),
        'state_lru': nrm(ks[4], (DEC_BATCH, DEPTH, 2, D_RNN), 0.5),
        'c': nrm(ks[5], (DEC_BATCH, D_MODEL), 1.0),
        'c_ctx': nrm(ks[6], (D_MODEL,), 1.0),
        'w_ada': nrm(ks[7], (DEPTH, D_MODEL, N_MOD * D_MODEL), 0.5 * D_MODEL ** -0.5),
        'b_ada': nrm(ks[8], (DEPTH, N_MOD * D_MODEL), 0.02),
        'norm_g': 1.0 + nrm(ks[9], (DEPTH, 2, D_MODEL), 0.1),
        'w_in': nrm(ks[10], (DEPTH, D_MODEL, D_IN), D_MODEL ** -0.5),
        'conv_w': nrm(ks[11], (DEPTH, CONV_W, D_RNN), CONV_W ** -0.5),
        'conv_b': nrm(ks[12], (DEPTH, D_RNN), 0.02),
        'lru_wa': nrm(ks[14], (DEPTH, 2, N_LRU_BLOCKS, LRU_BLOCK, LRU_BLOCK), LRU_BLOCK ** -0.5),
        'lru_ba': nrm(ks[15], (DEPTH, 2, D_RNN), 0.1),
        'lru_wx': nrm(ks[16], (DEPTH, 2, N_LRU_BLOCKS, LRU_BLOCK, LRU_BLOCK), LRU_BLOCK ** -0.5),
        'lru_bx': nrm(ks[17], (DEPTH, 2, D_RNN), 0.1),
        'lru_lam': jnp.log(a0) - jnp.log1p(-a0),
        'rpb': nrm(ks[18], (DEPTH, N_HEADS, 2 * WIN_H - 1, 2 * WIN_W - 1), 0.1),
        'w_br_rnn': nrm(ks[19], (DEPTH, D_RNN, D_MODEL), D_RNN ** -0.5),
        'w_br_att': nrm(ks[20], (DEPTH, D_ATT, D_MODEL), D_ATT ** -0.5),
        'w_out': nrm(ks[21], (DEPTH, D_MODEL, D_MODEL), D_MODEL ** -0.5),
        'w_router': nrm(ks[22], (D_MODEL, N_EXPERTS), D_MODEL ** -0.5),
        'router_bias': nrm(ks[23], (N_EXPERTS,), 0.01),
        'w_exp_gate': nrm(ks[24], (DEPTH, N_EXPERTS, D_MODEL, D_EXPERT), D_MODEL ** -0.5),
        'w_exp_up': nrm(ks[25], (DEPTH, N_EXPERTS, D_MODEL, D_EXPERT), D_MODEL ** -0.5),
        'w_exp_down': nrm(ks[26], (DEPTH, N_EXPERTS, D_EXPERT, D_MODEL), D_EXPERT ** -0.5),
        'final_norm_g': 1.0 + nrm(ks[27], (D_MODEL,), 0.1),
    }


def reference(x_prompt, x_sample, cache_k, cache_v, state_lru, c, c_ctx, w_ada, b_ada, norm_g, w_in, conv_w,
              conv_b, lru_wa, lru_ba, lru_wx, lru_bx, lru_lam, rpb, w_br_rnn, w_br_att, w_out, w_router,
              router_bias, w_exp_gate, w_exp_up, w_exp_down, final_norm_g):
    xp = x_prompt
    xs = x_sample
    ks, vs, hs = [], [], []
    for l in range(DEPTH):
        xp, k_l, v_l, h_l = trunk_layer(
            xp, c_ctx, None, None, None, w_ada[l], b_ada[l], norm_g[l], w_in[l], conv_w[l], conv_b[l],
            lru_wa[l], lru_ba[l], lru_wx[l], lru_bx[l], lru_lam[l], rpb[l], w_br_rnn[l], w_br_att[l], w_out[l],
            w_router, router_bias, w_exp_gate[l], w_exp_up[l], w_exp_down[l])
        ks.append(k_l)
        vs.append(v_l)
        hs.append(h_l)
        xs, _, _, _ = trunk_layer(
            xs, c, cache_k[:, l], cache_v[:, l], state_lru[:, l], w_ada[l], b_ada[l], norm_g[l], w_in[l],
            conv_w[l], conv_b[l], lru_wa[l], lru_ba[l], lru_wx[l], lru_bx[l], lru_lam[l], rpb[l], w_br_rnn[l],
            w_br_att[l], w_out[l], w_router, router_bias, w_exp_gate[l], w_exp_up[l], w_exp_down[l])
    y_prompt = rms_norm(xp, final_norm_g)
    y_sample = rms_norm(xs, final_norm_g)
    new_cache_k = jnp.stack(ks, axis=1)
    new_cache_v = jnp.stack(vs, axis=1)
    new_state_lru = jnp.stack(hs, axis=1)
    return (y_prompt, y_sample, new_cache_k, new_cache_v, new_state_lru)
```

```python
import jax, jax.numpy as jnp
from jax import lax
import numpy as np
from jax.experimental import pallas as pl
from jax.experimental.pallas import tpu as pltpu

D_MODEL = 1024
BATCH = 16
SEQ = 256
DEPTH = 2
DEC_BATCH = 4
DEC_SEQ = 4096
PAST_LEN = 256

GRID_W = 64
D_RNN = 1024
N_LRU_BLOCKS = 16
LRU_BLOCK = D_RNN // N_LRU_BLOCKS
CONV_W = 4
LRU_C = 8.0
N_HEADS = 16
HEAD_DIM = 64
D_ATT = N_HEADS * HEAD_DIM
WIN_H = 8
WIN_W = 16
Q_BLOCK_W = WIN_W
K_BLOCK_W = 2 * WIN_W
CTX_Q_BLOCK = 128
N_EXPERTS = 16
N_GROUPS = 4
EXPERTS_PER_GROUP = N_EXPERTS // N_GROUPS
TOP_K = 2
D_EXPERT = 512
N_MOD = 6
D_IN = 2 * D_RNN + 3 * D_ATT + 2 * D_MODEL
EPS = 1e-6
NEG_INF = -1e30


def rms_norm(x, g):
    xf = x.astype(jnp.float32)
    y = xf * lax.rsqrt(jnp.mean(xf * xf, axis=-1, keepdims=True) + EPS)
    return (y * g.astype(jnp.float32)).astype(x.dtype)


def modulation(cvec, w_ada, b_ada):
    m = jax.nn.silu(cvec) @ w_ada + b_ada
    m = m.reshape(m.shape[:-1] + (1, N_MOD, D_MODEL))
    return tuple(m[..., i, :] for i in range(N_MOD))


def modulate(x, g, shift, scale):
    return rms_norm(x, g) * (1 + scale) + shift


def in_projection(h, w_in):
    p = h @ w_in
    b, t = p.shape[:2]
    x_rnn, gate_rnn, q, k, v, gates = jnp.split(
        p, [D_RNN, 2 * D_RNN, 2 * D_RNN + D_ATT, 2 * D_RNN + 2 * D_ATT, 2 * D_RNN + 3 * D_ATT], axis=-1)
    heads = lambda z: z.reshape(b, t, N_HEADS, HEAD_DIM)
    return x_rnn, gate_rnn, heads(q), heads(k), heads(v), gates


def conv_centred(u, w, b):
    t = u.shape[1]
    up = jnp.pad(u, ((0, 0), (CONV_W // 2, CONV_W - 1 - CONV_W // 2), (0, 0)))
    y = b
    for j in range(CONV_W):
        y = y + up[:, j:j + t] * w[j]
    return y


def _linear_combine(e1, e2):
    a1, b1 = e1
    a2, b2 = e2
    return a1 * a2, a2 * b1 + b2


def lru_direction(u, w_a, b_a, w_x, b_x, lam, h0, reverse):
    if reverse:
        u = jnp.flip(u, axis=1)
    b, t, _ = u.shape
    ub = u.reshape(b, t, N_LRU_BLOCKS, LRU_BLOCK)
    r = jax.nn.sigmoid(jnp.einsum('btnc,ncd->btnd', ub, w_a).reshape(b, t, D_RNN) + b_a)
    i = jax.nn.sigmoid(jnp.einsum('btnc,ncd->btnd', ub, w_x).reshape(b, t, D_RNN) + b_x)
    log_a = -LRU_C * jax.nn.softplus(-lam.astype(jnp.float32)) * r.astype(jnp.float32)
    a = jnp.exp(log_a)
    inp = jnp.sqrt(-jnp.expm1(2.0 * log_a)) * (i * u).astype(jnp.float32)
    inp = inp.at[:, 0].add(a[:, 0] * h0.astype(jnp.float32))
    _, h = lax.associative_scan(_linear_combine, (a, inp), axis=1)
    h_final = h[:, -1]
    if reverse:
        h = jnp.flip(h, axis=1)
    return h.astype(u.dtype), h_final.astype(u.dtype)


def rglru_branch(x_rnn, gate_rnn, h0, conv_w, conv_b, lru_wa, lru_ba, lru_wx, lru_bx, lru_lam):
    u = conv_centred(x_rnn, conv_w, conv_b)
    h_f, fin_f = lru_direction(u, lru_wa[0], lru_ba[0], lru_wx[0], lru_bx[0], lru_lam[0], h0[:, 0], False)
    h_b, fin_b = lru_direction(u, lru_wa[1], lru_ba[1], lru_wx[1], lru_bx[1], lru_lam[1], h0[:, 1], True)
    y = (h_f + h_b) * jax.nn.gelu(gate_rnn)
    return y, jnp.stack([fin_f, fin_b], axis=1)


def context_attention(q, k, v):
    b, p_len, h, hd = q.shape
    n_qb = p_len // CTX_Q_BLOCK
    qb = jnp.moveaxis((q * HEAD_DIM ** -0.5).reshape(b, n_qb, CTX_Q_BLOCK, h, hd), 1, 0)

    def block(qi):
        s = jnp.einsum('bqhd,bkhd->bhqk', qi, k).astype(jnp.float32)
        p = jax.nn.softmax(s, axis=-1).astype(v.dtype)
        return jnp.einsum('bhqk,bkhd->bqhd', p, v)

    o = lax.map(block, qb)
    return jnp.moveaxis(o, 0, 1).reshape(b, p_len, h * hd)


def neighbourhood_attention(q, k, v, k_ctx, v_ctx, rpb):
    b, t, h, hd = q.shape
    rows = t // GRID_W
    kh = min(WIN_H, rows)
    n_cb = GRID_W // Q_BLOCK_W
    n_keys = kh * K_BLOCK_W
    cb = jnp.arange(n_cb)
    key_cols = jnp.clip(cb * Q_BLOCK_W - WIN_W // 2, 0, GRID_W - K_BLOCK_W)[:, None] + jnp.arange(K_BLOCK_W)
    q_cols = cb[:, None] * Q_BLOCK_W + jnp.arange(Q_BLOCK_W)
    q_start = jnp.clip(q_cols - WIN_W // 2, 0, GRID_W - WIN_W)
    kc = key_cols[:, None, :]
    col_valid = (kc >= q_start[..., None]) & (kc < q_start[..., None] + WIN_W)
    dc_idx = jnp.clip(kc - q_cols[..., None] + WIN_W - 1, 0, 2 * WIN_W - 2)
    mask = jnp.broadcast_to(col_valid[:, None, :, None, :],
                            (n_cb, 1, Q_BLOCK_W, kh, K_BLOCK_W)).reshape(n_cb, 1, Q_BLOCK_W, n_keys)
    k_grid = k.reshape(b, rows, GRID_W, h, hd)
    v_grid = v.reshape(b, rows, GRID_W, h, hd)
    q_rows = jnp.moveaxis((q * HEAD_DIM ** -0.5).reshape(b, rows, GRID_W, h, hd), 1, 0)

    def row_block(args):
        r, q_row = args
        start = jnp.clip(r - kh // 2, 0, rows - kh)
        dr_idx = start + jnp.arange(kh) - r + WIN_H - 1
        bias = rpb[:, dr_idx[None, None, :, None], dc_idx[:, :, None, :]]
        bias = jnp.transpose(bias, (1, 0, 2, 3, 4)).reshape(n_cb, h, Q_BLOCK_W, n_keys)

        def gather(grid):
            sl = lax.dynamic_slice_in_dim(grid, start, kh, axis=1)
            blk = sl[:, :, key_cols]
            return jnp.transpose(blk, (0, 2, 1, 3, 4, 5)).reshape(b, n_cb, n_keys, h, hd)

        k_blk = gather(k_grid)
        v_blk = gather(v_grid)
        qb = q_row.reshape(b, n_cb, Q_BLOCK_W, h, hd)
        s_lat = jnp.einsum('bcqhd,bckhd->bchqk', qb, k_blk).astype(jnp.float32) + bias.astype(jnp.float32)
        s_lat = jnp.where(mask, s_lat, NEG_INF)
        s_ctx = jnp.einsum('bcqhd,bphd->bchqp', qb, k_ctx).astype(jnp.float32)
        p = jax.nn.softmax(jnp.concatenate([s_lat, s_ctx], axis=-1), axis=-1).astype(v.dtype)
        o = (jnp.einsum('bchqk,bckhd->bcqhd', p[..., :n_keys], v_blk)
             + jnp.einsum('bchqp,bphd->bcqhd', p[..., n_keys:], v_ctx))
        return o.reshape(b, GRID_W, h, hd)

    out = lax.map(row_block, (jnp.arange(rows), q_rows))
    return jnp.moveaxis(out, 0, 1).reshape(b, t, h * hd)


Q_ROWS = 4
Q_TILE = Q_ROWS * GRID_W
KEY_ROWS = 12
KEY_TILE = KEY_ROWS * GRID_W
N_Q_TILES = DEC_SEQ // Q_TILE
ATT_LANES = 256
HEADS_PER_STEP = ATT_LANES // HEAD_DIM


def _window_block(t):
    return jnp.clip(t - 1, 0, N_Q_TILES - KEY_ROWS // Q_ROWS)


def _natten_bias(rpb):
    rows = DEC_SEQ // GRID_W
    tiles = np.array([0, 1, N_Q_TILES - 1])
    a = np.arange(Q_ROWS)
    i = np.arange(KEY_ROWS)
    c = np.arange(GRID_W)
    r = tiles[:, None] * Q_ROWS + a[None, :]
    wstart = np.clip(tiles - 1, 0, N_Q_TILES - KEY_ROWS // Q_ROWS) * Q_ROWS
    kr = wstart[:, None] + i[None, :]
    start_r = np.clip(r - WIN_H // 2, 0, rows - WIN_H)
    row_valid = (kr[:, None, :] >= start_r[..., None]) & (kr[:, None, :] < start_r[..., None] + WIN_H)
    dr = np.clip(kr[:, None, :] - r[..., None] + WIN_H - 1, 0, 2 * WIN_H - 2)
    q_start = np.clip(c - WIN_W // 2, 0, GRID_W - WIN_W)
    col_valid = (c[None, :] >= q_start[:, None]) & (c[None, :] < q_start[:, None] + WIN_W)
    dc = np.clip(c[None, :] - c[:, None] + WIN_W - 1, 0, 2 * WIN_W - 2)
    dr_full = np.broadcast_to(dr[:, :, None, :, None], (3, Q_ROWS, GRID_W, KEY_ROWS, GRID_W))
    dc_full = np.broadcast_to(dc[None, None, :, None, :], (3, Q_ROWS, GRID_W, KEY_ROWS, GRID_W))
    valid = row_valid[:, :, None, :, None] & col_valid[None, None, :, None, :]
    bias = rpb[:, dr_full.reshape(3, Q_TILE, KEY_TILE), dc_full.reshape(3, Q_TILE, KEY_TILE)]
    bias = jnp.where(valid.reshape(1, 3, Q_TILE, KEY_TILE), bias.astype(jnp.float32), NEG_INF)
    return jnp.transpose(bias, (1, 0, 2, 3))


def _natten_kernel(q_ref, k0_ref, k1_ref, k2_ref, v0_ref, v1_ref, v2_ref, kc_ref, vc_ref, bias_ref, o_ref):
    nt = (((1,), (1,)), ((), ()))
    lane_head = lax.broadcasted_iota(jnp.int32, (1, 2 * HEAD_DIM), 1) // HEAD_DIM
    k_refs = (k0_ref, k1_ref, k2_ref)
    v_refs = (v0_ref, v1_ref, v2_ref)
    outs = []
    for hp in range(ATT_LANES // (2 * HEAD_DIM)):
        sl = slice(2 * HEAD_DIM * hp, 2 * HEAD_DIM * (hp + 1))
        q2 = q_ref[0, :, sl]
        acc = jnp.zeros((Q_TILE, 2 * HEAD_DIM), jnp.float32)
        for j in range(2):
            mine = lane_head == j
            qh = jnp.where(mine, q2, jnp.zeros_like(q2))
            s = [lax.dot_general(qh, kr[0, :, sl], nt, preferred_element_type=jnp.float32)
                 + bias_ref[0, 2 * hp + j, :, Q_TILE * d:Q_TILE * (d + 1)] for d, kr in enumerate(k_refs)]
            s.append(lax.dot_general(qh, kc_ref[0, :, sl], nt, preferred_element_type=jnp.float32))
            m = s[0].max(axis=-1, keepdims=True)
            for sd in s[1:]:
                m = jnp.maximum(m, sd.max(axis=-1, keepdims=True))
            p = [jnp.exp(sd - m) for sd in s]
            l = p[0].sum(axis=-1, keepdims=True)
            for pd in p[1:]:
                l = l + pd.sum(axis=-1, keepdims=True)
            o = jnp.dot(p[3].astype(jnp.bfloat16), vc_ref[0, :, sl], preferred_element_type=jnp.float32)
            for d, vr in enumerate(v_refs):
                o = o + jnp.dot(p[d].astype(jnp.bfloat16), vr[0, :, sl], preferred_element_type=jnp.float32)
            acc = acc + jnp.where(mine, o / l, 0.0)
        outs.append(acc)
    o_ref[0] = jnp.concatenate(outs, axis=1).astype(o_ref.dtype)


def natten_pallas(q, k, v, k_ctx, v_ctx, bias):
    b, t, _ = q.shape
    n_lb = D_ATT // ATT_LANES

    def variant(ti):
        return jnp.where(ti == 0, 0, jnp.where(ti == N_Q_TILES - 1, 2, 1))

    tile = lambda: pl.BlockSpec((1, Q_TILE, ATT_LANES), lambda lb, ti, bi: (bi, ti, lb))
    win = lambda d: pl.BlockSpec((1, Q_TILE, ATT_LANES), lambda lb, ti, bi: (bi, _window_block(ti) + d, lb))
    ctx = lambda: pl.BlockSpec((1, PAST_LEN, ATT_LANES), lambda lb, ti, bi: (bi, 0, lb))
    return pl.pallas_call(
        _natten_kernel,
        grid=(n_lb, N_Q_TILES, b),
        in_specs=[tile(), win(0), win(1), win(2), win(0), win(1), win(2), ctx(), ctx(),
                  pl.BlockSpec((1, HEADS_PER_STEP, Q_TILE, KEY_TILE), lambda lb, ti, bi: (variant(ti), lb, 0, 0))],
        out_specs=tile(),
        out_shape=jax.ShapeDtypeStruct((b, t, D_ATT), jnp.bfloat16),
        compiler_params=pltpu.CompilerParams(dimension_semantics=("arbitrary", "arbitrary", "arbitrary")),
        name="natten",
    )(q, k, k, k, v, v, v, k_ctx, v_ctx, bias)


def neighbourhood_attention_pallas(q, k, v, k_ctx, v_ctx, rpb):
    b, t, h, hd = q.shape
    flat = lambda z: z.reshape(z.shape[0], z.shape[1], h * hd).astype(jnp.bfloat16)
    o = natten_pallas(flat(q * HEAD_DIM ** -0.5), flat(k), flat(v), flat(k_ctx), flat(v_ctx), _natten_bias(rpb))
    return o.astype(jnp.float32)


def merge_branches(y_rnn, y_att, gates, w_br_rnn, w_br_att, w_out):
    g_rnn, g_att = jnp.split(jax.nn.sigmoid(gates), 2, axis=-1)
    return (g_rnn * (y_rnn @ w_br_rnn) + g_att * (y_att @ w_br_att)) @ w_out


def moe_ffn(h, w_router, router_bias, w_exp_gate, w_exp_up, w_exp_down):
    shape = h.shape
    t = h.reshape(-1, D_MODEL)
    n = t.shape[0]
    scores = jax.nn.sigmoid((t @ w_router).astype(jnp.float32))
    sel = (scores + router_bias.astype(jnp.float32)).reshape(n, N_GROUPS, EXPERTS_PER_GROUP)
    group_score = lax.top_k(sel, TOP_K)[0].sum(-1)
    best_group = jnp.argmax(group_score, axis=-1)
    in_group = (best_group[:, None] == jnp.arange(N_GROUPS))[..., None]
    masked = jnp.where(in_group, sel, NEG_INF).reshape(n, N_EXPERTS)
    _, idx = lax.top_k(masked, TOP_K)
    w = jnp.take_along_axis(scores, idx, axis=-1)
    w = w / jnp.sum(w, axis=-1, keepdims=True)
    combine = jnp.sum(jax.nn.one_hot(idx, N_EXPERTS, dtype=jnp.float32) * w[..., None], axis=1).astype(h.dtype)
    y = jnp.zeros_like(t)
    for e in range(N_EXPERTS):
        a = jax.nn.silu(t @ w_exp_gate[e]) * (t @ w_exp_up[e])
        y = y + combine[:, e:e + 1] * (a @ w_exp_down[e])
    return y.reshape(shape)


def trunk_layer(x, cvec, ctx_k, ctx_v, h0, w_ada, b_ada, norm_g, w_in, conv_w, conv_b, lru_wa, lru_ba, lru_wx,
                lru_bx, lru_lam, rpb, w_br_rnn, w_br_att, w_out, w_router, router_bias, w_exp_gate, w_exp_up,
                w_exp_down):
    sh1, sc1, g1, sh2, sc2, g2 = modulation(cvec, w_ada, b_ada)
    h = modulate(x, norm_g[0], sh1, sc1)
    x_rnn, gate_rnn, q, k, v, gates = in_projection(h, w_in)
    if h0 is None:
        h0 = jnp.zeros((x.shape[0], 2, D_RNN), x.dtype)
    y_rnn, h_fin = rglru_branch(x_rnn, gate_rnn, h0, conv_w, conv_b, lru_wa, lru_ba, lru_wx, lru_bx, lru_lam)
    if ctx_k is None:
        y_att = context_attention(q, k, v)
        ctx_k, ctx_v = k, v
    else:
        y_att = neighbourhood_attention_pallas(q, k, v, ctx_k, ctx_v, rpb)
    x = x + g1 * merge_branches(y_rnn, y_att, gates, w_br_rnn, w_br_att, w_out)
    h = modulate(x, norm_g[1], sh2, sc2)
    x = x + g2 * moe_ffn(h, w_router, router_bias, w_exp_gate, w_exp_up, w_exp_down)
    return x, ctx_k, ctx_v, h_fin


def _rms_kernel(x_ref, g_ref, o_ref):
    x = x_ref[...]
    y = x * lax.rsqrt(jnp.mean(x * x, axis=-1, keepdims=True) + EPS)
    o_ref[...] = y * g_ref[...]


def _pallas_rms_norm(x, g):
    shape = x.shape
    x2 = x.reshape(-1, shape[-1])
    n = x2.shape[0]
    tm = 512
    out = pl.pallas_call(
        _rms_kernel,
        grid=(n // tm,),
        in_specs=[pl.BlockSpec((tm, shape[-1]), lambda i: (i, 0)),
                  pl.BlockSpec((1, shape[-1]), lambda i: (0, 0))],
        out_specs=pl.BlockSpec((tm, shape[-1]), lambda i: (i, 0)),
        out_shape=jax.ShapeDtypeStruct(x2.shape, x2.dtype),
    )(x2, g.reshape(1, -1))
    return out.reshape(shape)


def kernel(x_prompt, x_sample, cache_k, cache_v, state_lru, c, c_ctx, w_ada, b_ada, norm_g, w_in, conv_w,
           conv_b, lru_wa, lru_ba, lru_wx, lru_bx, lru_lam, rpb, w_br_rnn, w_br_att, w_out, w_router,
           router_bias, w_exp_gate, w_exp_up, w_exp_down, final_norm_g):
    xp = x_prompt
    xs = x_sample
    ks, vs, hs = [], [], []
    for l in range(DEPTH):
        xp, k_l, v_l, h_l = trunk_layer(
            xp, c_ctx, None, None, None, w_ada[l], b_ada[l], norm_g[l], w_in[l], conv_w[l], conv_b[l],
            lru_wa[l], lru_ba[l], lru_wx[l], lru_bx[l], lru_lam[l], rpb[l], w_br_rnn[l], w_br_att[l], w_out[l],
            w_router, router_bias, w_exp_gate[l], w_exp_up[l], w_exp_down[l])
        ks.append(k_l)
        vs.append(v_l)
        hs.append(h_l)
        xs, _, _, _ = trunk_layer(
            xs, c, cache_k[:, l], cache_v[:, l], state_lru[:, l], w_ada[l], b_ada[l], norm_g[l], w_in[l],
            conv_w[l], conv_b[l], lru_wa[l], lru_ba[l], lru_wx[l], lru_bx[l], lru_lam[l], rpb[l], w_br_rnn[l],
            w_br_att[l], w_out[l], w_router, router_bias, w_exp_gate[l], w_exp_up[l], w_exp_down[l])
    y_prompt = _pallas_rms_norm(xp, final_norm_g)
    y_sample = _pallas_rms_norm(xs, final_norm_g)
    new_cache_k = jnp.stack(ks, axis=1)
    new_cache_v = jnp.stack(vs, axis=1)
    new_state_lru = jnp.stack(hs, axis=1)
    return (y_prompt, y_sample, new_cache_k, new_cache_v, new_state_lru)
```

```python
import functools

import jax
import jax.numpy as jnp
import numpy as np
from jax import lax
from jax.experimental import pallas as pl
from jax.experimental.pallas import tpu as pltpu

D_MODEL = 1024
BATCH = 16
SEQ = 256
DEPTH = 2
DEC_BATCH = 4
DEC_SEQ = 4096
PAST_LEN = 256

GRID_W = 64
D_RNN = 1024
N_LRU_BLOCKS = 16
LRU_BLOCK = D_RNN // N_LRU_BLOCKS
CONV_W = 4
LRU_C = 8.0
N_HEADS = 16
HEAD_DIM = 64
D_ATT = N_HEADS * HEAD_DIM
WIN_H = 8
WIN_W = 16
N_EXPERTS = 16
N_GROUPS = 4
EXPERTS_PER_GROUP = N_EXPERTS // N_GROUPS
D_EXPERT = 512
N_MOD = 6
D_IN = 2 * D_RNN + 3 * D_ATT + 2 * D_MODEL
EPS = 1e-6
NEG_INF = -1e30

BF16 = jnp.bfloat16
F32 = jnp.float32

V7X_LANES = 128
V7X_SUBLANES = 8
V7X_MXU_DIM = 256
V7X_VMEM_BYTES = 64 * 1024 * 1024
VMEM_LIMIT = V7X_VMEM_BYTES - 8 * 1024 * 1024

CHUNK = D_MODEL
N_CHUNKS = D_IN // CHUNK
C_XRNN, C_GRNN, C_Q, C_K, C_V, C_GATE_R, C_GATE_A = range(N_CHUNKS)

NT_DIMS = (((1,), (1,)), ((), ()))


def _params(*sem):
    return pltpu.CompilerParams(dimension_semantics=sem, vmem_limit_bytes=VMEM_LIMIT)


def _adaln_kernel(c_ref, w_ref, b_ref, o_ref):
    cv = c_ref[...]
    s = cv * jax.nn.sigmoid(cv)
    o_ref[0] = jnp.dot(s.astype(BF16), w_ref[0].astype(BF16), preferred_element_type=F32) + b_ref[0]


def adaln_pallas(cvecs, w_ada, b_ada):
    r = cvecs.shape[0]
    return pl.pallas_call(
        _adaln_kernel,
        grid=(DEPTH, N_MOD),
        in_specs=[pl.BlockSpec((r, D_MODEL), lambda l, j: (0, 0)),
                  pl.BlockSpec((1, D_MODEL, D_MODEL), lambda l, j: (l, 0, j)),
                  pl.BlockSpec((1, 1, D_MODEL), lambda l, j: (l, 0, j))],
        out_specs=pl.BlockSpec((1, r, D_MODEL), lambda l, j: (l, 0, j)),
        out_shape=jax.ShapeDtypeStruct((DEPTH, r, N_MOD * D_MODEL), F32),
        compiler_params=_params("arbitrary", "arbitrary"),
        name="adaln",
    )(cvecs, w_ada, b_ada.reshape(DEPTH, 1, N_MOD * D_MODEL))


def _mod_spec(tm, seq_len, per_seq):
    if per_seq:
        return pl.BlockSpec((1, N_MOD, D_MODEL), lambda i, *_: (i * tm // seq_len, 0, 0))
    return pl.BlockSpec((1, N_MOD, D_MODEL), lambda i, *_: (0, 0, 0))


def _rms(x, g):
    return x * lax.rsqrt(jnp.mean(x * x, axis=-1, keepdims=True) + EPS) * g


def _inproj_kernel(x_ref, mod_ref, g_ref, w_ref, *refs, emit_kv):
    if emit_kv:
        p_ref, k32_ref, v32_ref, h_scr = refs
    else:
        p_ref, h_scr = refs
    j = pl.program_id(1)

    @pl.when(j == 0)
    def _():
        y = _rms(x_ref[...], g_ref[...])
        h_scr[...] = (y * (1.0 + mod_ref[0, 1:2, :]) + mod_ref[0, 0:1, :]).astype(BF16)

    acc = jnp.dot(h_scr[...], w_ref[...], preferred_element_type=F32)
    p_ref[...] = (acc * jnp.where(j == C_Q, HEAD_DIM ** -0.5, 1.0)).astype(BF16)
    if emit_kv:
        @pl.when(j == C_K)
        def _():
            k32_ref[...] = acc

        @pl.when(j == C_V)
        def _():
            v32_ref[...] = acc


def inproj_pallas(x, mod, g, w_in, seq_len, per_seq, tm, emit_kv):
    n = x.shape[0]
    row = lambda i, j: (i, 0)
    out_shape = [jax.ShapeDtypeStruct((n, D_IN), BF16)]
    out_specs = [pl.BlockSpec((tm, CHUNK), lambda i, j: (i, j))]
    if emit_kv:
        out_shape += [jax.ShapeDtypeStruct((n, D_ATT), F32)] * 2
        out_specs += [pl.BlockSpec((tm, D_ATT), row)] * 2
    return pl.pallas_call(
        functools.partial(_inproj_kernel, emit_kv=emit_kv),
        grid=(n // tm, N_CHUNKS),
        in_specs=[pl.BlockSpec((tm, D_MODEL), row), _mod_spec(tm, seq_len, per_seq),
                  pl.BlockSpec((1, D_MODEL), lambda i, j: (0, 0)),
                  pl.BlockSpec((D_MODEL, CHUNK), lambda i, j: (0, j))],
        out_specs=out_specs,
        out_shape=out_shape,
        scratch_shapes=[pltpu.VMEM((tm, D_MODEL), BF16)],
        compiler_params=_params("arbitrary", "arbitrary"),
        name="inproj",
    )(x, mod, g, w_in)


LRU_CB = 512
LRU_TC = 256
LRU_SUB = V7X_MXU_DIM
LRU_HALO = 16


def _scan_groups(a, b, carry, reverse):
    tc, c = a.shape
    ng = tc // V7X_SUBLANES
    a3 = a.reshape(ng, V7X_SUBLANES, c)
    b3 = b.reshape(ng, V7X_SUBLANES, c)
    sub = lax.broadcasted_iota(jnp.int32, (1, V7X_SUBLANES, 1), 1)
    s = 1
    while s < V7X_SUBLANES:
        shift = V7X_SUBLANES - s if reverse else s
        ok = (sub < V7X_SUBLANES - s) if reverse else (sub >= s)
        a_sh = pltpu.roll(a3, shift, axis=1)
        b_sh = pltpu.roll(b3, shift, axis=1)
        b3 = jnp.where(ok, a3 * b_sh + b3, b3)
        a3 = jnp.where(ok, a3 * a_sh, a3)
        s *= 2
    hs = [None] * ng
    order = range(ng - 1, -1, -1) if reverse else range(ng)
    edge = 0 if reverse else V7X_SUBLANES - 1
    for gi in order:
        h = a3[gi] * carry + b3[gi]
        carry = h[edge:edge + 1, :]
        hs[gi] = h
    return jnp.concatenate(hs, axis=0), carry


def _lru_kernel(x_ref, gate_ref, h0_ref, cw_ref, cb_ref, wg_ref, gb_ref, lam_ref, y_ref, fin_ref, hf_scr):
    t_len = x_ref.shape[1]
    cb = x_ref.shape[2]
    n_chunks = t_len // LRU_TC

    def conv_chunk(c):
        t0 = pl.multiple_of(c * LRU_TC, LRU_TC)
        cur = x_ref[0, pl.ds(t0, LRU_TC), :].astype(F32)
        lo = pl.multiple_of(jnp.maximum(t0 - LRU_HALO, 0), LRU_HALO)
        hi = pl.multiple_of(jnp.minimum(t0 + LRU_TC, t_len - LRU_HALO), LRU_HALO)
        prev = jnp.where(c > 0, x_ref[0, pl.ds(lo, LRU_HALO), :].astype(F32), 0.0)
        nxt = jnp.where(c < n_chunks - 1, x_ref[0, pl.ds(hi, LRU_HALO), :].astype(F32), 0.0)
        ext = jnp.concatenate([prev, cur, nxt], axis=0)
        n_ext = LRU_TC + 2 * LRU_HALO
        u = cb_ref[...] + jnp.zeros((LRU_TC, cb), F32)
        for j in range(CONV_W):
            off = j - CONV_W // 2
            sh = ext if off == 0 else pltpu.roll(ext, (-off) % n_ext, axis=0)
            u = u + sh[LRU_HALO:LRU_HALO + LRU_TC, :] * cw_ref[j:j + 1, :]
        return t0, u

    def gates(u, d):
        ub = u.astype(BF16)
        pre = [jnp.dot(ub[:, LRU_SUB * s:LRU_SUB * (s + 1)], wg_ref[d, s], preferred_element_type=F32)
               for s in range(cb // LRU_SUB)]
        pre_a = jnp.concatenate([p[:, :LRU_SUB] for p in pre], axis=1)
        pre_x = jnp.concatenate([p[:, LRU_SUB:] for p in pre], axis=1)
        r = jax.nn.sigmoid(pre_a + gb_ref[2 * d:2 * d + 1, :])
        i = jax.nn.sigmoid(pre_x + gb_ref[2 * d + 1:2 * d + 2, :])
        log_a = (-LRU_C * jax.nn.softplus(-lam_ref[d:d + 1, :])) * r
        a = jnp.exp(log_a)
        th = jnp.tanh(log_a)
        inp = jnp.sqrt(-2.0 * th / (1.0 - th)) * (i * u)
        return a, inp

    def fwd(c, carry):
        t0, u = conv_chunk(c)
        a, inp = gates(u, 0)
        h, carry = _scan_groups(a, inp, carry, reverse=False)
        hf_scr[pl.ds(t0, LRU_TC), :] = h
        return carry

    fin_f = lax.fori_loop(0, n_chunks, fwd, h0_ref[0, 0:1, :])

    def bwd(k, carry):
        c = n_chunks - 1 - k
        t0, u = conv_chunk(c)
        a, inp = gates(u, 1)
        h, carry = _scan_groups(a, inp, carry, reverse=True)
        g = gate_ref[0, pl.ds(t0, LRU_TC), :].astype(F32)
        y_ref[0, pl.ds(t0, LRU_TC), :] = ((hf_scr[pl.ds(t0, LRU_TC), :] + h) * jax.nn.gelu(g)).astype(y_ref.dtype)
        return carry

    fin_b = lax.fori_loop(0, n_chunks, bwd, h0_ref[0, 1:2, :])
    fin_ref[0, 0:1, :] = fin_f
    fin_ref[0, 1:2, :] = fin_b


def _lru_gate_weights(lru_wa, lru_wx):
    per = LRU_SUB // LRU_BLOCK
    eye = jnp.eye(per, dtype=F32)

    def dense(w):
        w = w.reshape(2, D_RNN // LRU_SUB, per, LRU_BLOCK, LRU_BLOCK)
        full = w[:, :, :, :, None, :] * eye[None, None, :, None, :, None]
        return full.reshape(2, D_RNN // LRU_SUB, LRU_SUB, LRU_SUB)

    return jnp.concatenate([dense(lru_wa), dense(lru_wx)], axis=-1).astype(BF16)


def lru_pallas(p, h0, conv_w, conv_b, wg, gb, lam):
    b, t, _ = p.shape
    n_cb = D_RNN // LRU_CB
    return pl.pallas_call(
        _lru_kernel,
        grid=(b, n_cb),
        in_specs=[pl.BlockSpec((1, t, LRU_CB), lambda bi, ci: (bi, 0, C_XRNN * n_cb + ci)),
                  pl.BlockSpec((1, t, LRU_CB), lambda bi, ci: (bi, 0, C_GRNN * n_cb + ci)),
                  pl.BlockSpec((1, 2, LRU_CB), lambda bi, ci: (bi, 0, ci)),
                  pl.BlockSpec((CONV_W, LRU_CB), lambda bi, ci: (0, ci)),
                  pl.BlockSpec((1, LRU_CB), lambda bi, ci: (0, ci)),
                  pl.BlockSpec((2, LRU_CB // LRU_SUB, LRU_SUB, 2 * LRU_SUB), lambda bi, ci: (0, ci, 0, 0)),
                  pl.BlockSpec((4, LRU_CB), lambda bi, ci: (0, ci)),
                  pl.BlockSpec((2, LRU_CB), lambda bi, ci: (0, ci))],
        out_specs=[pl.BlockSpec((1, t, LRU_CB), lambda bi, ci: (bi, 0, ci)),
                   pl.BlockSpec((1, 2, LRU_CB), lambda bi, ci: (bi, 0, ci))],
        out_shape=[jax.ShapeDtypeStruct((b, t, D_RNN), BF16), jax.ShapeDtypeStruct((b, 2, D_RNN), F32)],
        scratch_shapes=[pltpu.VMEM((t, LRU_CB), F32)],
        compiler_params=_params("arbitrary", "arbitrary"),
        name="lru",
    )(p, p, h0, conv_w, conv_b, wg, gb, lam)


Q_ROWS = 4
Q_TILE = Q_ROWS * GRID_W
KEY_ROWS = 12
KEY_TILE = KEY_ROWS * GRID_W
N_Q_TILES = DEC_SEQ // Q_TILE
ATT_LANES = 256
HEADS_PER_STEP = ATT_LANES // HEAD_DIM
PAIR = 2 * HEAD_DIM


def _window_block(t):
    return jnp.clip(t - 1, 0, N_Q_TILES - KEY_ROWS // Q_ROWS)


def _natten_bias(rpb):
    rows = DEC_SEQ // GRID_W
    tiles = np.array([0, 1, N_Q_TILES - 1])
    a = np.arange(Q_ROWS)
    i = np.arange(KEY_ROWS)
    c = np.arange(GRID_W)
    r = tiles[:, None] * Q_ROWS + a[None, :]
    wstart = np.clip(tiles - 1, 0, N_Q_TILES - KEY_ROWS // Q_ROWS) * Q_ROWS
    kr = wstart[:, None] + i[None, :]
    start_r = np.clip(r - WIN_H // 2, 0, rows - WIN_H)
    row_valid = (kr[:, None, :] >= start_r[..., None]) & (kr[:, None, :] < start_r[..., None] + WIN_H)
    dr = kr[:, None, :] - r[..., None] + WIN_H - 1
    q_start = np.clip(c - WIN_W // 2, 0, GRID_W - WIN_W)
    col_valid = (c[None, :] >= q_start[:, None]) & (c[None, :] < q_start[:, None] + WIN_W)
    dc = c[None, :] - c[:, None] + WIN_W - 1
    sel_r = ((dr[None] == np.arange(2 * WIN_H - 1)[:, None, None, None]) & row_valid[None]).astype(np.float32)
    sel_c = ((dc[None] == np.arange(2 * WIN_W - 1)[:, None, None]) & col_valid[None]).astype(np.float32)
    valid = row_valid[:, :, None, :, None] & col_valid[None, None, :, None, :]
    bias = jnp.einsum('lhrd,rvai,dck->lvhacik', rpb.astype(F32), jnp.asarray(sel_r), jnp.asarray(sel_c),
                      precision=lax.Precision.HIGHEST)
    bias = jnp.where(valid[None, :, None], bias, NEG_INF)
    return bias.reshape(DEPTH, 3, N_HEADS, Q_TILE, KEY_TILE)


def _attend(q2, keys, vals, biases):
    lane_head = lax.broadcasted_iota(jnp.int32, (1, PAIR), 1) // HEAD_DIM
    acc = jnp.zeros((q2.shape[0], PAIR), F32)
    for j in range(2):
        mine = lane_head == j
        qh = jnp.where(mine, q2, jnp.zeros_like(q2))
        s = []
        for kb, bb in zip(keys, biases[j]):
            sd = lax.dot_general(qh, kb, NT_DIMS, preferred_element_type=F32)
            s.append(sd if bb is None else sd + bb)
        m = s[0].max(axis=-1, keepdims=True)
        for sd in s[1:]:
            m = jnp.maximum(m, sd.max(axis=-1, keepdims=True))
        p = [jnp.exp(sd - m) for sd in s]
        l = p[0].sum(axis=-1, keepdims=True)
        for pd in p[1:]:
            l = l + pd.sum(axis=-1, keepdims=True)
        o = jnp.dot(p[0].astype(BF16), vals[0], preferred_element_type=F32)
        for pd, vb in zip(p[1:], vals[1:]):
            o = o + jnp.dot(pd.astype(BF16), vb, preferred_element_type=F32)
        acc = acc + jnp.where(mine, o / l, 0.0)
    return acc


def _natten_kernel(q_ref, k0_ref, k1_ref, k2_ref, v0_ref, v1_ref, v2_ref, kc_ref, vc_ref, bias_ref, o_ref):
    k_refs = (k0_ref, k1_ref, k2_ref, kc_ref)
    v_refs = (v0_ref, v1_ref, v2_ref, vc_ref)
    outs = []
    for hp in range(ATT_LANES // PAIR):
        sl = slice(PAIR * hp, PAIR * (hp + 1))
        biases = [[bias_ref[0, 0, 2 * hp + j, :, Q_TILE * d:Q_TILE * (d + 1)] for d in range(3)] + [None]
                  for j in range(2)]
        outs.append(_attend(q_ref[0, :, sl], [r[0, :, sl] for r in k_refs], [r[0, :, sl] for r in v_refs], biases))
    o_ref[0] = jnp.concatenate(outs, axis=1).astype(o_ref.dtype)


def natten_pallas(p, k_ctx, v_ctx, bias, layer):
    b, t, _ = p.shape
    n_lb = D_ATT // ATT_LANES

    def variant(ti):
        return jnp.where(ti == 0, 0, jnp.where(ti == N_Q_TILES - 1, 2, 1))

    q_spec = pl.BlockSpec((1, Q_TILE, ATT_LANES), lambda lb, ti, bi: (bi, ti, C_Q * n_lb + lb))
    win = lambda ch, d: pl.BlockSpec((1, Q_TILE, ATT_LANES),
                                     lambda lb, ti, bi: (bi, _window_block(ti) + d, ch * n_lb + lb))
    ctx = pl.BlockSpec((1, PAST_LEN, ATT_LANES), lambda lb, ti, bi: (bi, 0, lb))
    return pl.pallas_call(
        _natten_kernel,
        grid=(n_lb, N_Q_TILES, b),
        in_specs=[q_spec, win(C_K, 0), win(C_K, 1), win(C_K, 2), win(C_V, 0), win(C_V, 1), win(C_V, 2), ctx, ctx,
                  pl.BlockSpec((1, 1, HEADS_PER_STEP, Q_TILE, KEY_TILE),
                               lambda lb, ti, bi: (layer, variant(ti), lb, 0, 0))],
        out_specs=pl.BlockSpec((1, Q_TILE, ATT_LANES), lambda lb, ti, bi: (bi, ti, lb)),
        out_shape=jax.ShapeDtypeStruct((b, t, D_ATT), BF16),
        compiler_params=_params("arbitrary", "arbitrary", "arbitrary"),
        name="natten",
    )(p, p, p, p, p, p, p, k_ctx, v_ctx, bias)


def _ctxatt_kernel(q_ref, k_ref, v_ref, o_ref):
    outs = []
    for hp in range(ATT_LANES // PAIR):
        sl = slice(PAIR * hp, PAIR * (hp + 1))
        outs.append(_attend(q_ref[0, :, sl], [k_ref[0, :, sl]], [v_ref[0, :, sl]], [[None], [None]]))
    o_ref[0] = jnp.concatenate(outs, axis=1).astype(o_ref.dtype)


def ctxatt_pallas(p):
    b, t, _ = p.shape
    n_lb = D_ATT // ATT_LANES
    blk = lambda ch: pl.BlockSpec((1, t, ATT_LANES), lambda lb, bi: (bi, 0, ch * n_lb + lb))
    return pl.pallas_call(
        _ctxatt_kernel,
        grid=(n_lb, b),
        in_specs=[blk(C_Q), blk(C_K), blk(C_V)],
        out_specs=pl.BlockSpec((1, t, ATT_LANES), lambda lb, bi: (bi, 0, lb)),
        out_shape=jax.ShapeDtypeStruct((b, t, D_ATT), BF16),
        compiler_params=_params("arbitrary", "arbitrary"),
        name="ctxatt",
    )(p, p, p)


def _split_bf16(x):
    hi = x.astype(BF16)
    return hi, (x - hi.astype(F32)).astype(BF16)


def _route(logits_t, rb_ref):
    score = [jax.nn.sigmoid(logits_t[e:e + 1, :]) for e in range(N_EXPERTS)]
    sel = [score[e] + rb_ref[e] for e in range(N_EXPERTS)]
    best_g = None
    for g in range(N_GROUPS):
        v = sel[EXPERTS_PER_GROUP * g:EXPERTS_PER_GROUP * (g + 1)]
        top2 = None
        for i in range(EXPERTS_PER_GROUP):
            for j in range(i + 1, EXPERTS_PER_GROUP):
                pair = v[i] + v[j]
                top2 = pair if top2 is None else jnp.maximum(top2, pair)
        if best_g is None:
            best_g, best_v = jnp.zeros_like(top2, dtype=jnp.int32), top2
        else:
            upd = top2 > best_v
            best_g = jnp.where(upd, g, best_g)
            best_v = jnp.where(upd, top2, best_v)

    def in_best(vals, j):
        out = vals[j]
        for g in range(1, N_GROUPS):
            out = jnp.where(best_g == g, vals[EXPERTS_PER_GROUP * g + j], out)
        return out

    v = [in_best(sel, j) for j in range(EXPERTS_PER_GROUP)]
    sc = [in_best(score, j) for j in range(EXPERTS_PER_GROUP)]

    def first_argmax(vals):
        idx, top = jnp.zeros_like(best_g), vals[0]
        for j in range(1, EXPERTS_PER_GROUP):
            upd = vals[j] > top
            idx = jnp.where(upd, j, idx)
            top = jnp.where(upd, vals[j], top)
        return idx

    i1 = first_argmax(v)
    i2 = first_argmax([jnp.where(i1 == j, -jnp.inf, v[j]) for j in range(EXPERTS_PER_GROUP)])
    pick = lambda idx: sum(jnp.where(idx == j, sc[j], 0.0) for j in range(EXPERTS_PER_GROUP))
    w1, w2 = pick(i1), pick(i2)
    den = w1 + w2
    c1, c2 = w1 / den, w2 / den
    rows = []
    for e in range(N_EXPERTS):
        g, j = divmod(e, EXPERTS_PER_GROUP)
        rows.append(jnp.where(best_g == g, jnp.where(i1 == j, c1, 0.0) + jnp.where(i2 == j, c2, 0.0), 0.0))
    return jnp.concatenate(rows, axis=0)


def _merge_kernel(rb_ref, yr_ref, ya_ref, gr_ref, ga_ref, x_ref, mod_ref, g_ref, wr_ref, wa_ref, wo_ref, wrt_ref,
                  x1_ref, h2_ref, comb_ref):
    m = (jax.nn.sigmoid(gr_ref[...].astype(F32)) * jnp.dot(yr_ref[...], wr_ref[...], preferred_element_type=F32)
         + jax.nn.sigmoid(ga_ref[...].astype(F32)) * jnp.dot(ya_ref[...], wa_ref[...], preferred_element_type=F32))
    o = jnp.dot(m.astype(BF16), wo_ref[...], preferred_element_type=F32)
    x1 = x_ref[...] + mod_ref[0, 2:3, :] * o
    x1_ref[...] = x1
    h2 = _rms(x1, g_ref[...]) * (1.0 + mod_ref[0, 4:5, :]) + mod_ref[0, 3:4, :]
    h2_ref[...] = h2.astype(BF16)
    h_hi, h_lo = _split_bf16(h2)
    w_hi, w_lo = _split_bf16(wrt_ref[...])
    dot_nt = lambda a, b: lax.dot_general(a, b, NT_DIMS, preferred_element_type=F32)
    logits_t = dot_nt(w_hi, h_hi) + (dot_nt(w_hi, h_lo) + dot_nt(w_lo, h_hi))
    comb_t = _route(logits_t, rb_ref)
    tm = comb_t.shape[1]
    padded = jnp.concatenate([comb_t, jnp.zeros((V7X_LANES - N_EXPERTS, tm), F32)], axis=0)
    comb_ref[...] = padded.T


def merge_pallas(y_rnn, y_att, p, x, mod, g, w_br_rnn, w_br_att, w_out, w_router_t, router_bias, seq_len, per_seq, tm):
    n = x.shape[0]
    row = lambda i: (i, 0)
    full = lambda shape: pl.BlockSpec(shape, lambda i: (0, 0))
    return pl.pallas_call(
        _merge_kernel,
        grid=(n // tm,),
        in_specs=[pl.BlockSpec(memory_space=pltpu.SMEM),
                  pl.BlockSpec((tm, D_RNN), row), pl.BlockSpec((tm, D_ATT), row),
                  pl.BlockSpec((tm, CHUNK), lambda i: (i, C_GATE_R)), pl.BlockSpec((tm, CHUNK), lambda i: (i, C_GATE_A)),
                  pl.BlockSpec((tm, D_MODEL), row), _mod_spec(tm, seq_len, per_seq), full((1, D_MODEL)),
                  full((D_RNN, D_MODEL)), full((D_ATT, D_MODEL)), full((D_MODEL, D_MODEL)),
                  full((N_EXPERTS, D_MODEL))],
        out_specs=[pl.BlockSpec((tm, D_MODEL), row), pl.BlockSpec((tm, D_MODEL), row),
                   pl.BlockSpec((tm, V7X_LANES), row)],
        out_shape=[jax.ShapeDtypeStruct((n, D_MODEL), F32), jax.ShapeDtypeStruct((n, D_MODEL), BF16),
                   jax.ShapeDtypeStruct((n, V7X_LANES), F32)],
        compiler_params=_params("arbitrary"),
        name="merge",
    )(router_bias, y_rnn, y_att, p, p, x, mod, g, w_br_rnn, w_br_att, w_out, w_router_t)


def _moe_kernel(h_ref, comb_ref, x_ref, mod_ref, wg_ref, wu_ref, wd_ref, o_ref, acc_scr):
    e = pl.program_id(1)

    @pl.when(e == 0)
    def _():
        acc_scr[...] = jnp.zeros_like(acc_scr)

    h = h_ref[...]
    gate = jnp.dot(h, wg_ref[0], preferred_element_type=F32)
    up = jnp.dot(h, wu_ref[0], preferred_element_type=F32)
    act = (gate * jax.nn.sigmoid(gate)) * up
    y = jnp.dot(act.astype(BF16), wd_ref[0], preferred_element_type=F32)
    lane = lax.broadcasted_iota(jnp.int32, (1, V7X_LANES), 1)
    w_col = jnp.sum(jnp.where(lane == e, comb_ref[...], 0.0), axis=-1, keepdims=True)
    acc_scr[...] += w_col * y

    @pl.when(e == N_EXPERTS - 1)
    def _():
        o_ref[...] = x_ref[...] + mod_ref[0, 5:6, :] * acc_scr[...]


def moe_pallas(h2, comb, x1, mod, w_gate, w_up, w_down, seq_len, per_seq, tm):
    n = x1.shape[0]
    row = lambda i, e: (i, 0)
    return pl.pallas_call(
        _moe_kernel,
        grid=(n // tm, N_EXPERTS),
        in_specs=[pl.BlockSpec((tm, D_MODEL), row), pl.BlockSpec((tm, V7X_LANES), row),
                  pl.BlockSpec((tm, D_MODEL), row), _mod_spec(tm, seq_len, per_seq),
                  pl.BlockSpec((1, D_MODEL, D_EXPERT), lambda i, e: (e, 0, 0)),
                  pl.BlockSpec((1, D_MODEL, D_EXPERT), lambda i, e: (e, 0, 0)),
                  pl.BlockSpec((1, D_EXPERT, D_MODEL), lambda i, e: (e, 0, 0))],
        out_specs=pl.BlockSpec((tm, D_MODEL), row),
        out_shape=jax.ShapeDtypeStruct((n, D_MODEL), F32),
        scratch_shapes=[pltpu.VMEM((tm, D_MODEL), F32)],
        compiler_params=_params("arbitrary", "arbitrary"),
        name="moe",
    )(h2, comb, x1, mod, w_gate, w_up, w_down)


def _rms_kernel(x_ref, g_ref, o_ref):
    o_ref[...] = _rms(x_ref[...], g_ref[...])


def final_norm_pallas(x, g, tm=1024):
    n = x.shape[0]
    return pl.pallas_call(
        _rms_kernel,
        grid=(n // tm,),
        in_specs=[pl.BlockSpec((tm, D_MODEL), lambda i: (i, 0)), pl.BlockSpec((1, D_MODEL), lambda i: (0, 0))],
        out_specs=pl.BlockSpec((tm, D_MODEL), lambda i: (i, 0)),
        out_shape=jax.ShapeDtypeStruct(x.shape, x.dtype),
        compiler_params=_params("arbitrary"),
        name="final_norm",
    )(x, g)


def _layer(x, mod, seq_len, per_seq, tiles, lw, h0, ctx_kv, bias, layer):
    n = x.shape[0]
    b = n // seq_len
    tm_proj, tm_moe = tiles
    emit_kv = ctx_kv is None
    outs = inproj_pallas(x, mod, lw['norm_g'][0:1], lw['w_in'], seq_len, per_seq, tm_proj, emit_kv)
    p = outs[0]
    p3 = p.reshape(b, seq_len, D_IN)
    y_rnn, h_fin = lru_pallas(p3, h0, lw['conv_w'], lw['conv_b'], lw['wg'], lw['gb'], lw['lam'])
    if emit_kv:
        y_att = ctxatt_pallas(p3)
    else:
        y_att = natten_pallas(p3, ctx_kv[0], ctx_kv[1], bias, layer)
    x1, h2, comb = merge_pallas(y_rnn.reshape(n, D_RNN), y_att.reshape(n, D_ATT), p, x, mod, lw['norm_g'][1:2],
                                lw['w_br_rnn'], lw['w_br_att'], lw['w_out'], lw['w_router_t'], lw['router_bias'],
                                seq_len, per_seq, tm_proj)
    x2 = moe_pallas(h2, comb, x1, mod, lw['w_exp_gate'], lw['w_exp_up'], lw['w_exp_down'], seq_len, per_seq, tm_moe)
    kv = (outs[1], outs[2]) if emit_kv else None
    return x2, kv, h_fin


def kernel(x_prompt, x_sample, cache_k, cache_v, state_lru, c, c_ctx, w_ada, b_ada, norm_g, w_in, conv_w,
           conv_b, lru_wa, lru_ba, lru_wx, lru_bx, lru_lam, rpb, w_br_rnn, w_br_att, w_out, w_router,
           router_bias, w_exp_gate, w_exp_up, w_exp_down, final_norm_g):
    cvecs = jnp.concatenate([c, c_ctx[None, :], jnp.zeros((V7X_SUBLANES - DEC_BATCH - 1, D_MODEL), F32)], axis=0)
    mods = adaln_pallas(cvecs, w_ada, b_ada).reshape(DEPTH, V7X_SUBLANES, N_MOD, D_MODEL)
    bias = _natten_bias(rpb)
    w_router_t = w_router.T
    xp = x_prompt.reshape(BATCH * SEQ, D_MODEL)
    xs = x_sample.reshape(DEC_BATCH * DEC_SEQ, D_MODEL)
    zeros_h0 = jnp.zeros((BATCH, 2, D_RNN), F32)
    ks, vs, hs = [], [], []
    for l in range(DEPTH):
        lw = dict(
            norm_g=norm_g[l], w_in=w_in[l].astype(BF16), conv_w=conv_w[l], conv_b=conv_b[l][None, :],
            wg=_lru_gate_weights(lru_wa[l], lru_wx[l]),
            gb=jnp.stack([lru_ba[l, 0], lru_bx[l, 0], lru_ba[l, 1], lru_bx[l, 1]], axis=0), lam=lru_lam[l],
            w_br_rnn=w_br_rnn[l].astype(BF16), w_br_att=w_br_att[l].astype(BF16), w_out=w_out[l].astype(BF16),
            w_router_t=w_router_t, router_bias=router_bias,
            w_exp_gate=w_exp_gate[l].astype(BF16), w_exp_up=w_exp_up[l].astype(BF16),
            w_exp_down=w_exp_down[l].astype(BF16))
        xp, kv, h_l = _layer(xp, mods[l, DEC_BATCH:DEC_BATCH + 1], SEQ, False, (256, 1024), lw, zeros_h0, None,
                             None, l)
        ks.append(kv[0].reshape(BATCH, SEQ, N_HEADS, HEAD_DIM))
        vs.append(kv[1].reshape(BATCH, SEQ, N_HEADS, HEAD_DIM))
        hs.append(h_l)
        ctx_kv = (cache_k[:, l].reshape(DEC_BATCH, PAST_LEN, D_ATT).astype(BF16),
                  cache_v[:, l].reshape(DEC_BATCH, PAST_LEN, D_ATT).astype(BF16))
        xs, _, _ = _layer(xs, mods[l, :DEC_BATCH], DEC_SEQ, True, (512, 1024), lw, state_lru[:, l], ctx_kv, bias, l)
    y_prompt = final_norm_pallas(xp, final_norm_g[None, :]).reshape(BATCH, SEQ, D_MODEL)
    y_sample = final_norm_pallas(xs, final_norm_g[None, :]).reshape(DEC_BATCH, DEC_SEQ, D_MODEL)
    return (y_prompt, y_sample, jnp.stack(ks, axis=1), jnp.stack(vs, axis=1), jnp.stack(hs, axis=1))
```

```python
import functools

import jax
import jax.numpy as jnp
import numpy as np
from jax import lax
from jax.experimental import pallas as pl
from jax.experimental.pallas import tpu as pltpu

D_MODEL = 1024
BATCH = 16
SEQ = 256
DEPTH = 2
DEC_BATCH = 4
DEC_SEQ = 4096
PAST_LEN = 256

GRID_W = 64
D_RNN = 1024
N_LRU_BLOCKS = 16
LRU_BLOCK = D_RNN // N_LRU_BLOCKS
CONV_W = 4
LRU_C = 8.0
N_HEADS = 16
HEAD_DIM = 64
D_ATT = N_HEADS * HEAD_DIM
WIN_H = 8
WIN_W = 16
N_EXPERTS = 16
N_GROUPS = 4
EXPERTS_PER_GROUP = N_EXPERTS // N_GROUPS
D_EXPERT = 512
N_MOD = 6
D_IN = 2 * D_RNN + 3 * D_ATT + 2 * D_MODEL
EPS = 1e-6
NEG_INF = -1e30

BF16 = jnp.bfloat16
F32 = jnp.float32

V7X_LANES = 128
V7X_SUBLANES = 8
V7X_MXU_DIM = 256
V7X_VMEM_BYTES = 64 * 1024 * 1024
VMEM_LIMIT = V7X_VMEM_BYTES - 8 * 1024 * 1024

CHUNK = D_MODEL
N_CHUNKS = D_IN // CHUNK
C_XRNN, C_GRNN, C_Q, C_K, C_V, C_GATE_R, C_GATE_A = range(N_CHUNKS)

NT_DIMS = (((1,), (1,)), ((), ()))


def _params(*sem):
    return pltpu.CompilerParams(dimension_semantics=sem, vmem_limit_bytes=VMEM_LIMIT)


def _adaln_kernel(c_ref, w_ref, b_ref, o_ref):
    cv = c_ref[...]
    s = cv * jax.nn.sigmoid(cv)
    o_ref[0] = jnp.dot(s.astype(BF16), w_ref[0].astype(BF16), preferred_element_type=F32) + b_ref[0]


def adaln_pallas(cvecs, w_ada, b_ada):
    r = cvecs.shape[0]
    return pl.pallas_call(
        _adaln_kernel,
        grid=(DEPTH, N_MOD),
        in_specs=[pl.BlockSpec((r, D_MODEL), lambda l, j: (0, 0)),
                  pl.BlockSpec((1, D_MODEL, D_MODEL), lambda l, j: (l, 0, j)),
                  pl.BlockSpec((1, 1, D_MODEL), lambda l, j: (l, 0, j))],
        out_specs=pl.BlockSpec((1, r, D_MODEL), lambda l, j: (l, 0, j)),
        out_shape=jax.ShapeDtypeStruct((DEPTH, r, N_MOD * D_MODEL), F32),
        compiler_params=_params("arbitrary", "arbitrary"),
        name="adaln",
    )(cvecs, w_ada, b_ada.reshape(DEPTH, 1, N_MOD * D_MODEL))


def _mod_spec(tm, seq_len, per_seq):
    if per_seq:
        return pl.BlockSpec((1, N_MOD, D_MODEL), lambda i, *_: (i * tm // seq_len, 0, 0))
    return pl.BlockSpec((1, N_MOD, D_MODEL), lambda i, *_: (0, 0, 0))


def _rms(x, g):
    return x * lax.rsqrt(jnp.mean(x * x, axis=-1, keepdims=True) + EPS) * g


def _inproj_kernel(x_ref, mod_ref, g_ref, w_ref, *refs, emit_kv):
    if emit_kv:
        p_ref, k32_ref, v32_ref, h_scr = refs
    else:
        p_ref, h_scr = refs
    j = pl.program_id(1)

    @pl.when(j == 0)
    def _():
        y = _rms(x_ref[...], g_ref[...])
        h_scr[...] = (y * (1.0 + mod_ref[0, 1:2, :]) + mod_ref[0, 0:1, :]).astype(BF16)

    acc = jnp.dot(h_scr[...], w_ref[...], preferred_element_type=F32)
    p_ref[...] = (acc * jnp.where(j == C_Q, HEAD_DIM ** -0.5, 1.0)).astype(BF16)
    if emit_kv:
        @pl.when(j == C_K)
        def _():
            k32_ref[...] = acc

        @pl.when(j == C_V)
        def _():
            v32_ref[...] = acc


def inproj_pallas(x, mod, g, w_in, seq_len, per_seq, tm, emit_kv):
    n = x.shape[0]
    row = lambda i, j: (i, 0)
    out_shape = [jax.ShapeDtypeStruct((n, D_IN), BF16)]
    out_specs = [pl.BlockSpec((tm, CHUNK), lambda i, j: (i, j))]
    if emit_kv:
        out_shape += [jax.ShapeDtypeStruct((n, D_ATT), F32)] * 2
        out_specs += [pl.BlockSpec((tm, D_ATT), row)] * 2
    return pl.pallas_call(
        functools.partial(_inproj_kernel, emit_kv=emit_kv),
        grid=(n // tm, N_CHUNKS),
        in_specs=[pl.BlockSpec((tm, D_MODEL), row), _mod_spec(tm, seq_len, per_seq),
                  pl.BlockSpec((1, D_MODEL), lambda i, j: (0, 0)),
                  pl.BlockSpec((D_MODEL, CHUNK), lambda i, j: (0, j))],
        out_specs=out_specs,
        out_shape=out_shape,
        scratch_shapes=[pltpu.VMEM((tm, D_MODEL), BF16)],
        compiler_params=_params("arbitrary", "arbitrary"),
        name="inproj",
    )(x, mod, g, w_in)


LRU_CB = 512
LRU_TC = 256
LRU_SUB = V7X_MXU_DIM
LRU_HALO = 16


def _scan_groups(a, b, carry, reverse):
    tc, c = a.shape
    ng = tc // V7X_SUBLANES
    a3 = a.reshape(ng, V7X_SUBLANES, c)
    b3 = b.reshape(ng, V7X_SUBLANES, c)
    sub = lax.broadcasted_iota(jnp.int32, (1, V7X_SUBLANES, 1), 1)
    s = 1
    while s < V7X_SUBLANES:
        shift = V7X_SUBLANES - s if reverse else s
        ok = (sub < V7X_SUBLANES - s) if reverse else (sub >= s)
        a_sh = pltpu.roll(a3, shift, axis=1)
        b_sh = pltpu.roll(b3, shift, axis=1)
        b3 = jnp.where(ok, a3 * b_sh + b3, b3)
        a3 = jnp.where(ok, a3 * a_sh, a3)
        s *= 2
    hs = [None] * ng
    order = range(ng - 1, -1, -1) if reverse else range(ng)
    edge = 0 if reverse else V7X_SUBLANES - 1
    for gi in order:
        h = a3[gi] * carry + b3[gi]
        carry = h[edge:edge + 1, :]
        hs[gi] = h
    return jnp.concatenate(hs, axis=0), carry


def _lru_kernel(x_ref, gate_ref, h0_ref, cw_ref, cb_ref, wg_ref, gb_ref, lam_ref, y_ref, fin_ref, hf_scr):
    t_len = x_ref.shape[1]
    cb = x_ref.shape[2]
    n_chunks = t_len // LRU_TC

    def conv_chunk(c):
        t0 = pl.multiple_of(c * LRU_TC, LRU_TC)
        cur = x_ref[0, pl.ds(t0, LRU_TC), :].astype(F32)
        lo = pl.multiple_of(jnp.maximum(t0 - LRU_HALO, 0), LRU_HALO)
        hi = pl.multiple_of(jnp.minimum(t0 + LRU_TC, t_len - LRU_HALO), LRU_HALO)
        prev = jnp.where(c > 0, x_ref[0, pl.ds(lo, LRU_HALO), :].astype(F32), 0.0)
        nxt = jnp.where(c < n_chunks - 1, x_ref[0, pl.ds(hi, LRU_HALO), :].astype(F32), 0.0)
        ext = jnp.concatenate([prev, cur, nxt], axis=0)
        n_ext = LRU_TC + 2 * LRU_HALO
        u = cb_ref[...] + jnp.zeros((LRU_TC, cb), F32)
        for j in range(CONV_W):
            off = j - CONV_W // 2
            sh = ext if off == 0 else pltpu.roll(ext, (-off) % n_ext, axis=0)
            u = u + sh[LRU_HALO:LRU_HALO + LRU_TC, :] * cw_ref[j:j + 1, :]
        return t0, u

    def gates(u, d):
        ub = u.astype(BF16)
        pre = [jnp.dot(ub[:, LRU_SUB * s:LRU_SUB * (s + 1)], wg_ref[d, s], preferred_element_type=F32)
               for s in range(cb // LRU_SUB)]
        pre_a = jnp.concatenate([p[:, :LRU_SUB] for p in pre], axis=1)
        pre_x = jnp.concatenate([p[:, LRU_SUB:] for p in pre], axis=1)
        r = jax.nn.sigmoid(pre_a + gb_ref[2 * d:2 * d + 1, :])
        i = jax.nn.sigmoid(pre_x + gb_ref[2 * d + 1:2 * d + 2, :])
        log_a = (-LRU_C * jax.nn.softplus(-lam_ref[d:d + 1, :])) * r
        a = jnp.exp(log_a)
        th = jnp.tanh(log_a)
        inp = jnp.sqrt(-2.0 * th / (1.0 - th)) * (i * u)
        return a, inp

    def fwd(c, carry):
        t0, u = conv_chunk(c)
        a, inp = gates(u, 0)
        h, carry = _scan_groups(a, inp, carry, reverse=False)
        hf_scr[pl.ds(t0, LRU_TC), :] = h
        return carry

    fin_f = lax.fori_loop(0, n_chunks, fwd, h0_ref[0, 0:1, :])

    def bwd(k, carry):
        c = n_chunks - 1 - k
        t0, u = conv_chunk(c)
        a, inp = gates(u, 1)
        h, carry = _scan_groups(a, inp, carry, reverse=True)
        g = gate_ref[0, pl.ds(t0, LRU_TC), :].astype(F32)
        y_ref[0, pl.ds(t0, LRU_TC), :] = ((hf_scr[pl.ds(t0, LRU_TC), :] + h) * jax.nn.gelu(g)).astype(y_ref.dtype)
        return carry

    fin_b = lax.fori_loop(0, n_chunks, bwd, h0_ref[0, 1:2, :])
    fin_ref[0, 0:1, :] = fin_f
    fin_ref[0, 1:2, :] = fin_b


def _lru_gate_weights(lru_wa, lru_wx):
    per = LRU_SUB // LRU_BLOCK
    eye = jnp.eye(per, dtype=F32)

    def dense(w):
        w = w.reshape(2, D_RNN // LRU_SUB, per, LRU_BLOCK, LRU_BLOCK)
        full = w[:, :, :, :, None, :] * eye[None, None, :, None, :, None]
        return full.reshape(2, D_RNN // LRU_SUB, LRU_SUB, LRU_SUB)

    return jnp.concatenate([dense(lru_wa), dense(lru_wx)], axis=-1).astype(BF16)


def lru_pallas(p, h0, conv_w, conv_b, wg, gb, lam):
    b, t, _ = p.shape
    n_cb = D_RNN // LRU_CB
    return pl.pallas_call(
        _lru_kernel,
        grid=(b, n_cb),
        in_specs=[pl.BlockSpec((1, t, LRU_CB), lambda bi, ci: (bi, 0, C_XRNN * n_cb + ci)),
                  pl.BlockSpec((1, t, LRU_CB), lambda bi, ci: (bi, 0, C_GRNN * n_cb + ci)),
                  pl.BlockSpec((1, 2, LRU_CB), lambda bi, ci: (bi, 0, ci)),
                  pl.BlockSpec((CONV_W, LRU_CB), lambda bi, ci: (0, ci)),
                  pl.BlockSpec((1, LRU_CB), lambda bi, ci: (0, ci)),
                  pl.BlockSpec((2, LRU_CB // LRU_SUB, LRU_SUB, 2 * LRU_SUB), lambda bi, ci: (0, ci, 0, 0)),
                  pl.BlockSpec((4, LRU_CB), lambda bi, ci: (0, ci)),
                  pl.BlockSpec((2, LRU_CB), lambda bi, ci: (0, ci))],
        out_specs=[pl.BlockSpec((1, t, LRU_CB), lambda bi, ci: (bi, 0, ci)),
                   pl.BlockSpec((1, 2, LRU_CB), lambda bi, ci: (bi, 0, ci))],
        out_shape=[jax.ShapeDtypeStruct((b, t, D_RNN), BF16), jax.ShapeDtypeStruct((b, 2, D_RNN), F32)],
        scratch_shapes=[pltpu.VMEM((t, LRU_CB), F32)],
        compiler_params=_params("arbitrary", "arbitrary"),
        name="lru",
    )(p, p, h0, conv_w, conv_b, wg, gb, lam)


Q_ROWS = 4
Q_TILE = Q_ROWS * GRID_W
KEY_ROWS = 12
KEY_TILE = KEY_ROWS * GRID_W
N_Q_TILES = DEC_SEQ // Q_TILE
ATT_LANES = 256
HEADS_PER_STEP = ATT_LANES // HEAD_DIM
PAIR = 2 * HEAD_DIM


def _window_block(t):
    return jnp.clip(t - 1, 0, N_Q_TILES - KEY_ROWS // Q_ROWS)


N_DR_PAIRS = 2 * WIN_H


def _natten_tables(rpb):
    c = np.arange(GRID_W)
    q_start = np.clip(c - WIN_W // 2, 0, GRID_W - WIN_W)
    col_valid = (c[None, :] >= q_start[:, None]) & (c[None, :] < q_start[:, None] + WIN_W)
    dc = c[None, :] - c[:, None] + WIN_W - 1
    sel_c = ((dc[None] == np.arange(2 * WIN_W - 1)[:, None, None]) & col_valid[None]).astype(np.float32)
    t = jnp.einsum('lhrd,dck->lhrck', rpb.astype(F32), jnp.asarray(sel_c), precision=lax.Precision.HIGHEST)
    t = jnp.where(col_valid[None, None, None], t, NEG_INF)
    t = jnp.pad(t, ((0, 0), (0, 0), (1, 1), (0, 0), (0, 0)), constant_values=NEG_INF)
    return jnp.concatenate([t[:, :, :-1], t[:, :, 1:]], axis=-1)


def _attend(q2, keys, vals, biases):
    lane_head = lax.broadcasted_iota(jnp.int32, (1, PAIR), 1) // HEAD_DIM
    acc = jnp.zeros((q2.shape[0], PAIR), F32)
    for j in range(2):
        mine = lane_head == j
        qh = jnp.where(mine, q2, jnp.zeros_like(q2))
        s = []
        for kb, bb in zip(keys, biases[j]):
            sd = lax.dot_general(qh, kb, NT_DIMS, preferred_element_type=F32)
            s.append(sd if bb is None else sd + bb)
        m = s[0].max(axis=-1, keepdims=True)
        for sd in s[1:]:
            m = jnp.maximum(m, sd.max(axis=-1, keepdims=True))
        p = [jnp.exp(sd - m) for sd in s]
        l = p[0].sum(axis=-1, keepdims=True)
        for pd in p[1:]:
            l = l + pd.sum(axis=-1, keepdims=True)
        o = jnp.dot(p[0].astype(BF16), vals[0], preferred_element_type=F32)
        for pd, vb in zip(p[1:], vals[1:]):
            o = o + jnp.dot(pd.astype(BF16), vb, preferred_element_type=F32)
        acc = acc + jnp.where(mine, o / l, 0.0)
    return acc


def _natten_kernel(q_ref, k0_ref, k1_ref, k2_ref, v0_ref, v1_ref, v2_ref, kc_ref, vc_ref, tt_ref, o_ref):
    ti = pl.program_id(1)
    rows = DEC_SEQ // GRID_W
    wstart = _window_block(ti) * Q_ROWS
    first_row = lax.broadcasted_iota(jnp.int32, (1, 2 * GRID_W), 1) < GRID_W
    pieces = {}
    for a in range(Q_ROWS):
        r = ti * Q_ROWS + a
        start_r = jnp.clip(r - WIN_H // 2, 0, rows - WIN_H)
        for i in range(0, KEY_ROWS, 2):
            kr = wstart + i
            ok = [((kr + e >= start_r) & (kr + e < start_r + WIN_H)).astype(jnp.int32) for e in range(2)]
            pieces[a, i] = (jnp.clip(kr - r + WIN_H, 0, N_DR_PAIRS - 1), jnp.where(first_row, ok[0], ok[1]) > 0)

    def bias_block(h, d):
        return jnp.concatenate(
            [jnp.concatenate([jnp.where(pieces[a, i][1], tt_ref[0, h, pieces[a, i][0]], NEG_INF)
                              for i in range(Q_ROWS * d, Q_ROWS * (d + 1), 2)], axis=1)
             for a in range(Q_ROWS)], axis=0)

    k_refs = (k0_ref, k1_ref, k2_ref, kc_ref)
    v_refs = (v0_ref, v1_ref, v2_ref, vc_ref)
    outs = []
    for hp in range(ATT_LANES // PAIR):
        sl = slice(PAIR * hp, PAIR * (hp + 1))
        biases = [[bias_block(2 * hp + j, d) for d in range(3)] + [None] for j in range(2)]
        outs.append(_attend(q_ref[0, :, sl], [r[0, :, sl] for r in k_refs], [r[0, :, sl] for r in v_refs], biases))
    o_ref[0] = jnp.concatenate(outs, axis=1).astype(o_ref.dtype)


def natten_pallas(p, k_ctx, v_ctx, tables, layer):
    b, t, _ = p.shape
    n_lb = D_ATT // ATT_LANES
    q_spec = pl.BlockSpec((1, Q_TILE, ATT_LANES), lambda lb, ti, bi: (bi, ti, C_Q * n_lb + lb))
    win = lambda ch, d: pl.BlockSpec((1, Q_TILE, ATT_LANES),
                                     lambda lb, ti, bi: (bi, _window_block(ti) + d, ch * n_lb + lb))
    ctx = pl.BlockSpec((1, PAST_LEN, ATT_LANES), lambda lb, ti, bi: (bi, 0, lb))
    return pl.pallas_call(
        _natten_kernel,
        grid=(n_lb, N_Q_TILES, b),
        in_specs=[q_spec, win(C_K, 0), win(C_K, 1), win(C_K, 2), win(C_V, 0), win(C_V, 1), win(C_V, 2), ctx, ctx,
                  pl.BlockSpec((1, HEADS_PER_STEP, N_DR_PAIRS, GRID_W, 2 * GRID_W),
                               lambda lb, ti, bi: (layer, lb, 0, 0, 0))],
        out_specs=pl.BlockSpec((1, Q_TILE, ATT_LANES), lambda lb, ti, bi: (bi, ti, lb)),
        out_shape=jax.ShapeDtypeStruct((b, t, D_ATT), BF16),
        compiler_params=_params("arbitrary", "arbitrary", "arbitrary"),
        name="natten",
    )(p, p, p, p, p, p, p, k_ctx, v_ctx, tables)


def _ctxatt_kernel(q_ref, k_ref, v_ref, o_ref):
    outs = []
    for hp in range(ATT_LANES // PAIR):
        sl = slice(PAIR * hp, PAIR * (hp + 1))
        outs.append(_attend(q_ref[0, :, sl], [k_ref[0, :, sl]], [v_ref[0, :, sl]], [[None], [None]]))
    o_ref[0] = jnp.concatenate(outs, axis=1).astype(o_ref.dtype)


def ctxatt_pallas(p):
    b, t, _ = p.shape
    n_lb = D_ATT // ATT_LANES
    blk = lambda ch: pl.BlockSpec((1, t, ATT_LANES), lambda lb, bi: (bi, 0, ch * n_lb + lb))
    return pl.pallas_call(
        _ctxatt_kernel,
        grid=(n_lb, b),
        in_specs=[blk(C_Q), blk(C_K), blk(C_V)],
        out_specs=pl.BlockSpec((1, t, ATT_LANES), lambda lb, bi: (bi, 0, lb)),
        out_shape=jax.ShapeDtypeStruct((b, t, D_ATT), BF16),
        compiler_params=_params("arbitrary", "arbitrary"),
        name="ctxatt",
    )(p, p, p)


def _split_bf16(x):
    hi = x.astype(BF16)
    return hi, (x - hi.astype(F32)).astype(BF16)


def _route(logits_t, rb_ref):
    score = [jax.nn.sigmoid(logits_t[e:e + 1, :]) for e in range(N_EXPERTS)]
    sel = [score[e] + rb_ref[e] for e in range(N_EXPERTS)]
    best_g = None
    for g in range(N_GROUPS):
        v = sel[EXPERTS_PER_GROUP * g:EXPERTS_PER_GROUP * (g + 1)]
        top2 = None
        for i in range(EXPERTS_PER_GROUP):
            for j in range(i + 1, EXPERTS_PER_GROUP):
                pair = v[i] + v[j]
                top2 = pair if top2 is None else jnp.maximum(top2, pair)
        if best_g is None:
            best_g, best_v = jnp.zeros_like(top2, dtype=jnp.int32), top2
        else:
            upd = top2 > best_v
            best_g = jnp.where(upd, g, best_g)
            best_v = jnp.where(upd, top2, best_v)

    def in_best(vals, j):
        out = vals[j]
        for g in range(1, N_GROUPS):
            out = jnp.where(best_g == g, vals[EXPERTS_PER_GROUP * g + j], out)
        return out

    v = [in_best(sel, j) for j in range(EXPERTS_PER_GROUP)]
    sc = [in_best(score, j) for j in range(EXPERTS_PER_GROUP)]

    def first_argmax(vals):
        idx, top = jnp.zeros_like(best_g), vals[0]
        for j in range(1, EXPERTS_PER_GROUP):
            upd = vals[j] > top
            idx = jnp.where(upd, j, idx)
            top = jnp.where(upd, vals[j], top)
        return idx

    i1 = first_argmax(v)
    i2 = first_argmax([jnp.where(i1 == j, -jnp.inf, v[j]) for j in range(EXPERTS_PER_GROUP)])
    pick = lambda idx: sum(jnp.where(idx == j, sc[j], 0.0) for j in range(EXPERTS_PER_GROUP))
    w1, w2 = pick(i1), pick(i2)
    den = w1 + w2
    c1, c2 = w1 / den, w2 / den
    rows = []
    for e in range(N_EXPERTS):
        g, j = divmod(e, EXPERTS_PER_GROUP)
        rows.append(jnp.where(best_g == g, jnp.where(i1 == j, c1, 0.0) + jnp.where(i2 == j, c2, 0.0), 0.0))
    return jnp.concatenate(rows, axis=0)


def _merge_kernel(rb_ref, yr_ref, ya_ref, gr_ref, ga_ref, x_ref, mod_ref, g_ref, wr_ref, wa_ref, wo_ref, wrt_ref,
                  x1_ref, h2_ref, comb_ref):
    m = (jax.nn.sigmoid(gr_ref[...].astype(F32)) * jnp.dot(yr_ref[...], wr_ref[...], preferred_element_type=F32)
         + jax.nn.sigmoid(ga_ref[...].astype(F32)) * jnp.dot(ya_ref[...], wa_ref[...], preferred_element_type=F32))
    o = jnp.dot(m.astype(BF16), wo_ref[...], preferred_element_type=F32)
    x1 = x_ref[...] + mod_ref[0, 2:3, :] * o
    x1_ref[...] = x1
    h2 = _rms(x1, g_ref[...]) * (1.0 + mod_ref[0, 4:5, :]) + mod_ref[0, 3:4, :]
    h2_ref[...] = h2.astype(BF16)
    h_hi, h_lo = _split_bf16(h2)
    w_hi, w_lo = _split_bf16(wrt_ref[...])
    dot_nt = lambda a, b: lax.dot_general(a, b, NT_DIMS, preferred_element_type=F32)
    logits_t = dot_nt(w_hi, h_hi) + (dot_nt(w_hi, h_lo) + dot_nt(w_lo, h_hi))
    comb_t = _route(logits_t, rb_ref)
    tm = comb_t.shape[1]
    padded = jnp.concatenate([comb_t, jnp.zeros((V7X_LANES - N_EXPERTS, tm), F32)], axis=0)
    comb_ref[...] = padded.T


def merge_pallas(y_rnn, y_att, p, x, mod, g, w_br_rnn, w_br_att, w_out, w_router_t, router_bias, seq_len, per_seq, tm):
    n = x.shape[0]
    row = lambda i: (i, 0)
    full = lambda shape: pl.BlockSpec(shape, lambda i: (0, 0))
    return pl.pallas_call(
        _merge_kernel,
        grid=(n // tm,),
        in_specs=[pl.BlockSpec(memory_space=pltpu.SMEM),
                  pl.BlockSpec((tm, D_RNN), row), pl.BlockSpec((tm, D_ATT), row),
                  pl.BlockSpec((tm, CHUNK), lambda i: (i, C_GATE_R)), pl.BlockSpec((tm, CHUNK), lambda i: (i, C_GATE_A)),
                  pl.BlockSpec((tm, D_MODEL), row), _mod_spec(tm, seq_len, per_seq), full((1, D_MODEL)),
                  full((D_RNN, D_MODEL)), full((D_ATT, D_MODEL)), full((D_MODEL, D_MODEL)),
                  full((N_EXPERTS, D_MODEL))],
        out_specs=[pl.BlockSpec((tm, D_MODEL), row), pl.BlockSpec((tm, D_MODEL), row),
                   pl.BlockSpec((tm, V7X_LANES), row)],
        out_shape=[jax.ShapeDtypeStruct((n, D_MODEL), F32), jax.ShapeDtypeStruct((n, D_MODEL), BF16),
                   jax.ShapeDtypeStruct((n, V7X_LANES), F32)],
        compiler_params=_params("arbitrary"),
        name="merge",
    )(router_bias, y_rnn, y_att, p, p, x, mod, g, w_br_rnn, w_br_att, w_out, w_router_t)


def _moe_kernel(h_ref, comb_ref, x_ref, mod_ref, gf_ref, wg_ref, wu_ref, wd_ref, o_ref, acc_scr, *, final_norm):
    e = pl.program_id(1)

    @pl.when(e == 0)
    def _():
        acc_scr[...] = jnp.zeros_like(acc_scr)

    h = h_ref[...]
    gate = jnp.dot(h, wg_ref[0], preferred_element_type=F32)
    up = jnp.dot(h, wu_ref[0], preferred_element_type=F32)
    act = (gate * jax.nn.sigmoid(gate)) * up
    y = jnp.dot(act.astype(BF16), wd_ref[0], preferred_element_type=F32)
    lane = lax.broadcasted_iota(jnp.int32, (1, V7X_LANES), 1)
    w_col = jnp.sum(jnp.where(lane == e, comb_ref[...], 0.0), axis=-1, keepdims=True)
    acc_scr[...] += w_col * y

    @pl.when(e == N_EXPERTS - 1)
    def _():
        x2 = x_ref[...] + mod_ref[0, 5:6, :] * acc_scr[...]
        o_ref[...] = _rms(x2, gf_ref[...]) if final_norm else x2


def moe_pallas(h2, comb, x1, mod, g_final, w_gate, w_up, w_down, seq_len, per_seq, tm, final_norm):
    n = x1.shape[0]
    row = lambda i, e: (i, 0)
    return pl.pallas_call(
        functools.partial(_moe_kernel, final_norm=final_norm),
        grid=(n // tm, N_EXPERTS),
        in_specs=[pl.BlockSpec((tm, D_MODEL), row), pl.BlockSpec((tm, V7X_LANES), row),
                  pl.BlockSpec((tm, D_MODEL), row), _mod_spec(tm, seq_len, per_seq),
                  pl.BlockSpec((1, D_MODEL), lambda i, e: (0, 0)),
                  pl.BlockSpec((1, D_MODEL, D_EXPERT), lambda i, e: (e, 0, 0)),
                  pl.BlockSpec((1, D_MODEL, D_EXPERT), lambda i, e: (e, 0, 0)),
                  pl.BlockSpec((1, D_EXPERT, D_MODEL), lambda i, e: (e, 0, 0))],
        out_specs=pl.BlockSpec((tm, D_MODEL), row),
        out_shape=jax.ShapeDtypeStruct((n, D_MODEL), F32),
        scratch_shapes=[pltpu.VMEM((tm, D_MODEL), F32)],
        compiler_params=_params("arbitrary", "arbitrary"),
        name="moe",
    )(h2, comb, x1, mod, g_final, w_gate, w_up, w_down)


TM_PROJ = 1024
TM_MERGE = 512
TM_MOE = 1024


def _layer(x, mod, seq_len, per_seq, lw, h0, ctx_kv, tables, layer):
    n = x.shape[0]
    b = n // seq_len
    emit_kv = ctx_kv is None
    outs = inproj_pallas(x, mod, lw['norm_g'][0:1], lw['w_in'], seq_len, per_seq, TM_PROJ, emit_kv)
    p = outs[0]
    p3 = p.reshape(b, seq_len, D_IN)
    y_rnn, h_fin = lru_pallas(p3, h0, lw['conv_w'], lw['conv_b'], lw['wg'], lw['gb'], lw['lam'])
    if emit_kv:
        y_att = ctxatt_pallas(p3)
    else:
        y_att = natten_pallas(p3, ctx_kv[0], ctx_kv[1], tables, layer)
    x1, h2, comb = merge_pallas(y_rnn.reshape(n, D_RNN), y_att.reshape(n, D_ATT), p, x, mod, lw['norm_g'][1:2],
                                lw['w_br_rnn'], lw['w_br_att'], lw['w_out'], lw['w_router_t'], lw['router_bias'],
                                seq_len, per_seq, TM_MERGE)
    x2 = moe_pallas(h2, comb, x1, mod, lw['g_final'], lw['w_exp_gate'], lw['w_exp_up'], lw['w_exp_down'],
                    seq_len, per_seq, TM_MOE, layer == DEPTH - 1)
    kv = (outs[1], outs[2]) if emit_kv else None
    return x2, kv, h_fin


def kernel(x_prompt, x_sample, cache_k, cache_v, state_lru, c, c_ctx, w_ada, b_ada, norm_g, w_in, conv_w,
           conv_b, lru_wa, lru_ba, lru_wx, lru_bx, lru_lam, rpb, w_br_rnn, w_br_att, w_out, w_router,
           router_bias, w_exp_gate, w_exp_up, w_exp_down, final_norm_g):
    cvecs = jnp.concatenate([c, c_ctx[None, :], jnp.zeros((V7X_SUBLANES - DEC_BATCH - 1, D_MODEL), F32)], axis=0)
    mods = adaln_pallas(cvecs, w_ada, b_ada).reshape(DEPTH, V7X_SUBLANES, N_MOD, D_MODEL)
    tables = _natten_tables(rpb)
    w_router_t = w_router.T
    xp = x_prompt.reshape(BATCH * SEQ, D_MODEL)
    xs = x_sample.reshape(DEC_BATCH * DEC_SEQ, D_MODEL)
    zeros_h0 = jnp.zeros((BATCH, 2, D_RNN), F32)
    ks, vs, hs = [], [], []
    for l in range(DEPTH):
        lw = dict(
            norm_g=norm_g[l], w_in=w_in[l].astype(BF16), conv_w=conv_w[l], conv_b=conv_b[l][None, :],
            wg=_lru_gate_weights(lru_wa[l], lru_wx[l]),
            gb=jnp.stack([lru_ba[l, 0], lru_bx[l, 0], lru_ba[l, 1], lru_bx[l, 1]], axis=0), lam=lru_lam[l],
            w_br_rnn=w_br_rnn[l].astype(BF16), w_br_att=w_br_att[l].astype(BF16), w_out=w_out[l].astype(BF16),
            w_router_t=w_router_t, router_bias=router_bias, g_final=final_norm_g[None, :],
            w_exp_gate=w_exp_gate[l].astype(BF16), w_exp_up=w_exp_up[l].astype(BF16),
            w_exp_down=w_exp_down[l].astype(BF16))
        xp, kv, h_l = _layer(xp, mods[l, DEC_BATCH:DEC_BATCH + 1], SEQ, False, lw, zeros_h0, None, None, l)
        ks.append(kv[0].reshape(BATCH, SEQ, N_HEADS, HEAD_DIM))
        vs.append(kv[1].reshape(BATCH, SEQ, N_HEADS, HEAD_DIM))
        hs.append(h_l)
        ctx_kv = (cache_k[:, l].reshape(DEC_BATCH, PAST_LEN, D_ATT).astype(BF16),
                  cache_v[:, l].reshape(DEC_BATCH, PAST_LEN, D_ATT).astype(BF16))
        xs, _, _ = _layer(xs, mods[l, :DEC_BATCH], DEC_SEQ, True, lw, state_lru[:, l], ctx_kv, tables, l)
    y_prompt = xp.reshape(BATCH, SEQ, D_MODEL)
    y_sample = xs.reshape(DEC_BATCH, DEC_SEQ, D_MODEL)
    return (y_prompt, y_sample, jnp.stack(ks, axis=1), jnp.stack(vs, axis=1), jnp.stack(hs, axis=1))
```

```python
import functools

import jax
import jax.numpy as jnp
import numpy as np
from jax import lax
from jax.experimental import pallas as pl
from jax.experimental.pallas import tpu as pltpu

D_MODEL = 1024
BATCH = 16
SEQ = 256
DEPTH = 2
DEC_BATCH = 4
DEC_SEQ = 4096
PAST_LEN = 256

GRID_W = 64
D_RNN = 1024
N_LRU_BLOCKS = 16
LRU_BLOCK = D_RNN // N_LRU_BLOCKS
CONV_W = 4
LRU_C = 8.0
N_HEADS = 16
HEAD_DIM = 64
D_ATT = N_HEADS * HEAD_DIM
WIN_H = 8
WIN_W = 16
N_EXPERTS = 16
N_GROUPS = 4
EXPERTS_PER_GROUP = N_EXPERTS // N_GROUPS
D_EXPERT = 512
N_MOD = 6
D_IN = 2 * D_RNN + 3 * D_ATT + 2 * D_MODEL
EPS = 1e-6
NEG_INF = -1e30

BF16 = jnp.bfloat16
F32 = jnp.float32

V7X_LANES = 128
V7X_SUBLANES = 8
V7X_MXU_DIM = 256
V7X_VMEM_BYTES = 64 * 1024 * 1024
VMEM_LIMIT = V7X_VMEM_BYTES - 8 * 1024 * 1024

CHUNK = D_MODEL
N_CHUNKS = D_IN // CHUNK
C_XRNN, C_GRNN, C_Q, C_K, C_V, C_GATE_R, C_GATE_A = range(N_CHUNKS)

NT_DIMS = (((1,), (1,)), ((), ()))


def _params(*sem):
    return pltpu.CompilerParams(dimension_semantics=sem, vmem_limit_bytes=VMEM_LIMIT)


def _adaln_kernel(c_ref, w_ref, b_ref, o_ref):
    cv = c_ref[...]
    s = cv * jax.nn.sigmoid(cv)
    o_ref[0] = jnp.dot(s.astype(BF16), w_ref[0].astype(BF16), preferred_element_type=F32) + b_ref[0]


def adaln_pallas(cvecs, w_ada, b_ada):
    r = cvecs.shape[0]
    return pl.pallas_call(
        _adaln_kernel,
        grid=(DEPTH, N_MOD),
        in_specs=[pl.BlockSpec((r, D_MODEL), lambda l, j: (0, 0)),
                  pl.BlockSpec((1, D_MODEL, D_MODEL), lambda l, j: (l, 0, j)),
                  pl.BlockSpec((1, 1, D_MODEL), lambda l, j: (l, 0, j))],
        out_specs=pl.BlockSpec((1, r, D_MODEL), lambda l, j: (l, 0, j)),
        out_shape=jax.ShapeDtypeStruct((DEPTH, r, N_MOD * D_MODEL), F32),
        compiler_params=_params("arbitrary", "arbitrary"),
        name="adaln",
    )(cvecs, w_ada, b_ada.reshape(DEPTH, 1, N_MOD * D_MODEL))


def _mod_spec(tm, seq_len, per_seq):
    if per_seq:
        return pl.BlockSpec((1, N_MOD, D_MODEL), lambda i, *_: (i * tm // seq_len, 0, 0))
    return pl.BlockSpec((1, N_MOD, D_MODEL), lambda i, *_: (0, 0, 0))


def _rms(x, g):
    return x * lax.rsqrt(jnp.mean(x * x, axis=-1, keepdims=True) + EPS) * g


def _inproj_kernel(x_ref, mod_ref, g_ref, w_ref, *refs, emit_kv):
    if emit_kv:
        p_ref, k32_ref, v32_ref, h_scr = refs
    else:
        p_ref, h_scr = refs
    j = pl.program_id(1)

    @pl.when(j == 0)
    def _():
        y = _rms(x_ref[...], g_ref[...])
        h_scr[...] = (y * (1.0 + mod_ref[0, 1:2, :]) + mod_ref[0, 0:1, :]).astype(BF16)

    acc = jnp.dot(h_scr[...], w_ref[...], preferred_element_type=F32)
    p_ref[...] = (acc * jnp.where(j == C_Q, HEAD_DIM ** -0.5, 1.0)).astype(BF16)
    if emit_kv:
        @pl.when(j == C_K)
        def _():
            k32_ref[...] = acc

        @pl.when(j == C_V)
        def _():
            v32_ref[...] = acc


def inproj_pallas(x, mod, g, w_in, seq_len, per_seq, tm, emit_kv):
    n = x.shape[0]
    row = lambda i, j: (i, 0)
    out_shape = [jax.ShapeDtypeStruct((n, D_IN), BF16)]
    out_specs = [pl.BlockSpec((tm, CHUNK), lambda i, j: (i, j))]
    if emit_kv:
        out_shape += [jax.ShapeDtypeStruct((n, D_ATT), F32)] * 2
        out_specs += [pl.BlockSpec((tm, D_ATT), row)] * 2
    return pl.pallas_call(
        functools.partial(_inproj_kernel, emit_kv=emit_kv),
        grid=(n // tm, N_CHUNKS),
        in_specs=[pl.BlockSpec((tm, D_MODEL), row), _mod_spec(tm, seq_len, per_seq),
                  pl.BlockSpec((1, D_MODEL), lambda i, j: (0, 0)),
                  pl.BlockSpec((D_MODEL, CHUNK), lambda i, j: (0, j))],
        out_specs=out_specs,
        out_shape=out_shape,
        scratch_shapes=[pltpu.VMEM((tm, D_MODEL), BF16)],
        compiler_params=_params("arbitrary", "arbitrary"),
        name="inproj",
    )(x, mod, g, w_in)


LRU_CB = 512
LRU_TC = 256
LRU_SUB = V7X_MXU_DIM
LRU_HALO = 16


def _scan_groups(a, b, carry, reverse):
    tc, c = a.shape
    ng = tc // V7X_SUBLANES
    a3 = a.reshape(ng, V7X_SUBLANES, c)
    b3 = b.reshape(ng, V7X_SUBLANES, c)
    sub = lax.broadcasted_iota(jnp.int32, (1, V7X_SUBLANES, 1), 1)
    s = 1
    while s < V7X_SUBLANES:
        shift = V7X_SUBLANES - s if reverse else s
        ok = (sub < V7X_SUBLANES - s) if reverse else (sub >= s)
        a_sh = pltpu.roll(a3, shift, axis=1)
        b_sh = pltpu.roll(b3, shift, axis=1)
        b3 = jnp.where(ok, a3 * b_sh + b3, b3)
        a3 = jnp.where(ok, a3 * a_sh, a3)
        s *= 2
    hs = [None] * ng
    order = range(ng - 1, -1, -1) if reverse else range(ng)
    edge = 0 if reverse else V7X_SUBLANES - 1
    for gi in order:
        h = a3[gi] * carry + b3[gi]
        carry = h[edge:edge + 1, :]
        hs[gi] = h
    return jnp.concatenate(hs, axis=0), carry


def _lru_kernel(x_ref, gate_ref, h0_ref, cw_ref, cb_ref, wg_ref, gb_ref, lam_ref, y_ref, fin_ref, hf_scr):
    t_len = x_ref.shape[1]
    cb = x_ref.shape[2]
    n_chunks = t_len // LRU_TC

    def conv_chunk(c):
        t0 = pl.multiple_of(c * LRU_TC, LRU_TC)
        cur = x_ref[0, pl.ds(t0, LRU_TC), :].astype(F32)
        lo = pl.multiple_of(jnp.maximum(t0 - LRU_HALO, 0), LRU_HALO)
        hi = pl.multiple_of(jnp.minimum(t0 + LRU_TC, t_len - LRU_HALO), LRU_HALO)
        prev = jnp.where(c > 0, x_ref[0, pl.ds(lo, LRU_HALO), :].astype(F32), 0.0)
        nxt = jnp.where(c < n_chunks - 1, x_ref[0, pl.ds(hi, LRU_HALO), :].astype(F32), 0.0)
        ext = jnp.concatenate([prev, cur, nxt], axis=0)
        n_ext = LRU_TC + 2 * LRU_HALO
        u = cb_ref[...] + jnp.zeros((LRU_TC, cb), F32)
        for j in range(CONV_W):
            off = j - CONV_W // 2
            sh = ext if off == 0 else pltpu.roll(ext, (-off) % n_ext, axis=0)
            u = u + sh[LRU_HALO:LRU_HALO + LRU_TC, :] * cw_ref[j:j + 1, :]
        return t0, u

    def gates(u, d):
        ub = u.astype(BF16)
        pre = [jnp.dot(ub[:, LRU_SUB * s:LRU_SUB * (s + 1)], wg_ref[d, s], preferred_element_type=F32)
               for s in range(cb // LRU_SUB)]
        pre_a = jnp.concatenate([p[:, :LRU_SUB] for p in pre], axis=1)
        pre_x = jnp.concatenate([p[:, LRU_SUB:] for p in pre], axis=1)
        r = jax.nn.sigmoid(pre_a + gb_ref[2 * d:2 * d + 1, :])
        i = jax.nn.sigmoid(pre_x + gb_ref[2 * d + 1:2 * d + 2, :])
        log_a = (-LRU_C * jax.nn.softplus(-lam_ref[d:d + 1, :])) * r
        a = jnp.exp(log_a)
        th = jnp.tanh(log_a)
        inp = jnp.sqrt(-2.0 * th / (1.0 - th)) * (i * u)
        return a, inp

    def fwd(c, carry):
        t0, u = conv_chunk(c)
        a, inp = gates(u, 0)
        h, carry = _scan_groups(a, inp, carry, reverse=False)
        hf_scr[pl.ds(t0, LRU_TC), :] = h
        return carry

    fin_f = lax.fori_loop(0, n_chunks, fwd, h0_ref[0, 0:1, :])

    def bwd(k, carry):
        c = n_chunks - 1 - k
        t0, u = conv_chunk(c)
        a, inp = gates(u, 1)
        h, carry = _scan_groups(a, inp, carry, reverse=True)
        g = gate_ref[0, pl.ds(t0, LRU_TC), :].astype(F32)
        y_ref[0, pl.ds(t0, LRU_TC), :] = ((hf_scr[pl.ds(t0, LRU_TC), :] + h) * jax.nn.gelu(g)).astype(y_ref.dtype)
        return carry

    fin_b = lax.fori_loop(0, n_chunks, bwd, h0_ref[0, 1:2, :])
    fin_ref[0, 0:1, :] = fin_f
    fin_ref[0, 1:2, :] = fin_b


def _lru_gate_weights(lru_wa, lru_wx):
    per = LRU_SUB // LRU_BLOCK
    eye = jnp.eye(per, dtype=F32)

    def dense(w):
        w = w.reshape(2, D_RNN // LRU_SUB, per, LRU_BLOCK, LRU_BLOCK)
        full = w[:, :, :, :, None, :] * eye[None, None, :, None, :, None]
        return full.reshape(2, D_RNN // LRU_SUB, LRU_SUB, LRU_SUB)

    return jnp.concatenate([dense(lru_wa), dense(lru_wx)], axis=-1).astype(BF16)


def lru_pallas(p, h0, conv_w, conv_b, wg, gb, lam):
    b, t, _ = p.shape
    n_cb = D_RNN // LRU_CB
    return pl.pallas_call(
        _lru_kernel,
        grid=(b, n_cb),
        in_specs=[pl.BlockSpec((1, t, LRU_CB), lambda bi, ci: (bi, 0, C_XRNN * n_cb + ci)),
                  pl.BlockSpec((1, t, LRU_CB), lambda bi, ci: (bi, 0, C_GRNN * n_cb + ci)),
                  pl.BlockSpec((1, 2, LRU_CB), lambda bi, ci: (bi, 0, ci)),
                  pl.BlockSpec((CONV_W, LRU_CB), lambda bi, ci: (0, ci)),
                  pl.BlockSpec((1, LRU_CB), lambda bi, ci: (0, ci)),
                  pl.BlockSpec((2, LRU_CB // LRU_SUB, LRU_SUB, 2 * LRU_SUB), lambda bi, ci: (0, ci, 0, 0)),
                  pl.BlockSpec((4, LRU_CB), lambda bi, ci: (0, ci)),
                  pl.BlockSpec((2, LRU_CB), lambda bi, ci: (0, ci))],
        out_specs=[pl.BlockSpec((1, t, LRU_CB), lambda bi, ci: (bi, 0, ci)),
                   pl.BlockSpec((1, 2, LRU_CB), lambda bi, ci: (bi, 0, ci))],
        out_shape=[jax.ShapeDtypeStruct((b, t, D_RNN), BF16), jax.ShapeDtypeStruct((b, 2, D_RNN), F32)],
        scratch_shapes=[pltpu.VMEM((t, LRU_CB), F32)],
        compiler_params=_params("arbitrary", "arbitrary"),
        name="lru",
    )(p, p, h0, conv_w, conv_b, wg, gb, lam)


Q_ROWS = 4
Q_TILE = Q_ROWS * GRID_W
KEY_ROWS = 12
KEY_TILE = KEY_ROWS * GRID_W
N_Q_TILES = DEC_SEQ // Q_TILE
ATT_LANES = 256
HEADS_PER_STEP = ATT_LANES // HEAD_DIM
PAIR = 2 * HEAD_DIM


def _window_block(t):
    return jnp.clip(t - 1, 0, N_Q_TILES - KEY_ROWS // Q_ROWS)


N_DR_PAIRS = 2 * WIN_H


def _natten_tables(rpb):
    c = np.arange(GRID_W)
    q_start = np.clip(c - WIN_W // 2, 0, GRID_W - WIN_W)
    col_valid = (c[None, :] >= q_start[:, None]) & (c[None, :] < q_start[:, None] + WIN_W)
    dc = c[None, :] - c[:, None] + WIN_W - 1
    sel_c = ((dc[None] == np.arange(2 * WIN_W - 1)[:, None, None]) & col_valid[None]).astype(np.float32)
    t = jnp.einsum('lhrd,dck->lhrck', rpb.astype(F32), jnp.asarray(sel_c), precision=lax.Precision.HIGHEST)
    t = jnp.where(col_valid[None, None, None], t, NEG_INF)
    t = jnp.pad(t, ((0, 0), (0, 0), (1, 1), (0, 0), (0, 0)), constant_values=NEG_INF)
    return jnp.concatenate([t[:, :, :-1], t[:, :, 1:]], axis=-1)


def _attend(q2, keys, vals, biases):
    lane_head = lax.broadcasted_iota(jnp.int32, (1, PAIR), 1) // HEAD_DIM
    acc = jnp.zeros((q2.shape[0], PAIR), F32)
    for j in range(2):
        mine = lane_head == j
        qh = jnp.where(mine, q2, jnp.zeros_like(q2))
        s = []
        for kb, bb in zip(keys, biases[j]):
            sd = lax.dot_general(qh, kb, NT_DIMS, preferred_element_type=F32)
            s.append(sd if bb is None else sd + bb)
        m = s[0].max(axis=-1, keepdims=True)
        for sd in s[1:]:
            m = jnp.maximum(m, sd.max(axis=-1, keepdims=True))
        p = [jnp.exp(sd - m) for sd in s]
        l = p[0].sum(axis=-1, keepdims=True)
        for pd in p[1:]:
            l = l + pd.sum(axis=-1, keepdims=True)
        o = jnp.dot(p[0].astype(BF16), vals[0], preferred_element_type=F32)
        for pd, vb in zip(p[1:], vals[1:]):
            o = o + jnp.dot(pd.astype(BF16), vb, preferred_element_type=F32)
        acc = acc + jnp.where(mine, o / l, 0.0)
    return acc


def _natten_kernel(q_ref, k0_ref, k1_ref, k2_ref, v0_ref, v1_ref, v2_ref, kc_ref, vc_ref, tt_ref, o_ref):
    ti = pl.program_id(1)
    rows = DEC_SEQ // GRID_W
    wstart = _window_block(ti) * Q_ROWS
    first_row = lax.broadcasted_iota(jnp.int32, (1, 2 * GRID_W), 1) < GRID_W
    pieces = {}
    for a in range(Q_ROWS):
        r = ti * Q_ROWS + a
        start_r = jnp.clip(r - WIN_H // 2, 0, rows - WIN_H)
        for i in range(0, KEY_ROWS, 2):
            kr = wstart + i
            ok = [((kr + e >= start_r) & (kr + e < start_r + WIN_H)).astype(jnp.int32) for e in range(2)]
            pieces[a, i] = (jnp.clip(kr - r + WIN_H, 0, N_DR_PAIRS - 1), jnp.where(first_row, ok[0], ok[1]) > 0)

    def bias_block(h, d):
        return jnp.concatenate(
            [jnp.concatenate([jnp.where(pieces[a, i][1], tt_ref[0, h, pieces[a, i][0]], NEG_INF)
                              for i in range(Q_ROWS * d, Q_ROWS * (d + 1), 2)], axis=1)
             for a in range(Q_ROWS)], axis=0)

    k_refs = (k0_ref, k1_ref, k2_ref, kc_ref)
    v_refs = (v0_ref, v1_ref, v2_ref, vc_ref)
    outs = []
    for hp in range(ATT_LANES // PAIR):
        sl = slice(PAIR * hp, PAIR * (hp + 1))
        biases = [[bias_block(2 * hp + j, d) for d in range(3)] + [None] for j in range(2)]
        outs.append(_attend(q_ref[0, :, sl], [r[0, :, sl] for r in k_refs], [r[0, :, sl] for r in v_refs], biases))
    o_ref[0] = jnp.concatenate(outs, axis=1).astype(o_ref.dtype)


def natten_pallas(p, k_ctx, v_ctx, tables, layer):
    b, t, _ = p.shape
    n_lb = D_ATT // ATT_LANES
    q_spec = pl.BlockSpec((1, Q_TILE, ATT_LANES), lambda lb, ti, bi: (bi, ti, C_Q * n_lb + lb))
    win = lambda ch, d: pl.BlockSpec((1, Q_TILE, ATT_LANES),
                                     lambda lb, ti, bi: (bi, _window_block(ti) + d, ch * n_lb + lb))
    ctx = pl.BlockSpec((1, PAST_LEN, ATT_LANES), lambda lb, ti, bi: (bi, 0, lb))
    return pl.pallas_call(
        _natten_kernel,
        grid=(n_lb, N_Q_TILES, b),
        in_specs=[q_spec, win(C_K, 0), win(C_K, 1), win(C_K, 2), win(C_V, 0), win(C_V, 1), win(C_V, 2), ctx, ctx,
                  pl.BlockSpec((1, HEADS_PER_STEP, N_DR_PAIRS, GRID_W, 2 * GRID_W),
                               lambda lb, ti, bi: (layer, lb, 0, 0, 0))],
        out_specs=pl.BlockSpec((1, Q_TILE, ATT_LANES), lambda lb, ti, bi: (bi, ti, lb)),
        out_shape=jax.ShapeDtypeStruct((b, t, D_ATT), BF16),
        compiler_params=_params("arbitrary", "arbitrary", "arbitrary"),
        name="natten",
    )(p, p, p, p, p, p, p, k_ctx, v_ctx, tables)


def _ctxatt_kernel(q_ref, k_ref, v_ref, o_ref):
    outs = []
    for hp in range(ATT_LANES // PAIR):
        sl = slice(PAIR * hp, PAIR * (hp + 1))
        outs.append(_attend(q_ref[0, :, sl], [k_ref[0, :, sl]], [v_ref[0, :, sl]], [[None], [None]]))
    o_ref[0] = jnp.concatenate(outs, axis=1).astype(o_ref.dtype)


def ctxatt_pallas(p):
    b, t, _ = p.shape
    n_lb = D_ATT // ATT_LANES
    blk = lambda ch: pl.BlockSpec((1, t, ATT_LANES), lambda lb, bi: (bi, 0, ch * n_lb + lb))
    return pl.pallas_call(
        _ctxatt_kernel,
        grid=(n_lb, b),
        in_specs=[blk(C_Q), blk(C_K), blk(C_V)],
        out_specs=pl.BlockSpec((1, t, ATT_LANES), lambda lb, bi: (bi, 0, lb)),
        out_shape=jax.ShapeDtypeStruct((b, t, D_ATT), BF16),
        compiler_params=_params("arbitrary", "arbitrary"),
        name="ctxatt",
    )(p, p, p)


def _split_bf16(x):
    hi = x.astype(BF16)
    return hi, (x - hi.astype(F32)).astype(BF16)


def _route(logits_t, rb_ref):
    score = [jax.nn.sigmoid(logits_t[e:e + 1, :]) for e in range(N_EXPERTS)]
    sel = [score[e] + rb_ref[e] for e in range(N_EXPERTS)]
    best_g = None
    for g in range(N_GROUPS):
        v = sel[EXPERTS_PER_GROUP * g:EXPERTS_PER_GROUP * (g + 1)]
        top2 = None
        for i in range(EXPERTS_PER_GROUP):
            for j in range(i + 1, EXPERTS_PER_GROUP):
                pair = v[i] + v[j]
                top2 = pair if top2 is None else jnp.maximum(top2, pair)
        if best_g is None:
            best_g, best_v = jnp.zeros_like(top2, dtype=jnp.int32), top2
        else:
            upd = top2 > best_v
            best_g = jnp.where(upd, g, best_g)
            best_v = jnp.where(upd, top2, best_v)

    def in_best(vals, j):
        out = vals[j]
        for g in range(1, N_GROUPS):
            out = jnp.where(best_g == g, vals[EXPERTS_PER_GROUP * g + j], out)
        return out

    v = [in_best(sel, j) for j in range(EXPERTS_PER_GROUP)]
    sc = [in_best(score, j) for j in range(EXPERTS_PER_GROUP)]

    def first_argmax(vals):
        idx, top = jnp.zeros_like(best_g), vals[0]
        for j in range(1, EXPERTS_PER_GROUP):
            upd = vals[j] > top
            idx = jnp.where(upd, j, idx)
            top = jnp.where(upd, vals[j], top)
        return idx

    i1 = first_argmax(v)
    i2 = first_argmax([jnp.where(i1 == j, -jnp.inf, v[j]) for j in range(EXPERTS_PER_GROUP)])
    pick = lambda idx: sum(jnp.where(idx == j, sc[j], 0.0) for j in range(EXPERTS_PER_GROUP))
    w1, w2 = pick(i1), pick(i2)
    den = w1 + w2
    c1, c2 = w1 / den, w2 / den
    rows = []
    for e in range(N_EXPERTS):
        g, j = divmod(e, EXPERTS_PER_GROUP)
        rows.append(jnp.where(best_g == g, jnp.where(i1 == j, c1, 0.0) + jnp.where(i2 == j, c2, 0.0), 0.0))
    return jnp.concatenate(rows, axis=0), best_g


AUX_LANES = V7X_LANES
AUX_MID = N_EXPERTS
AUX_LO = 2 * N_EXPERTS
AUX_GROUP = 3 * N_EXPERTS
D_MOE_IN = D_MODEL + AUX_LANES


def _merge_kernel(rb_ref, yr_ref, ya_ref, gr_ref, ga_ref, x_ref, mod_ref, g_ref, wr_ref, wa_ref, wo_ref, wrt_ref,
                  x1_ref, h2_ref, gid_ref):
    m = (jax.nn.sigmoid(gr_ref[...].astype(F32)) * jnp.dot(yr_ref[...], wr_ref[...], preferred_element_type=F32)
         + jax.nn.sigmoid(ga_ref[...].astype(F32)) * jnp.dot(ya_ref[...], wa_ref[...], preferred_element_type=F32))
    o = jnp.dot(m.astype(BF16), wo_ref[...], preferred_element_type=F32)
    x1 = x_ref[...] + mod_ref[0, 2:3, :] * o
    x1_ref[...] = x1
    h2 = _rms(x1, g_ref[...]) * (1.0 + mod_ref[0, 4:5, :]) + mod_ref[0, 3:4, :]
    h_hi, h_lo = _split_bf16(h2)
    w_hi, w_lo = _split_bf16(wrt_ref[...])
    dot_nt = lambda a, b: lax.dot_general(a, b, NT_DIMS, preferred_element_type=F32)
    logits_t = dot_nt(w_hi, h_hi) + (dot_nt(w_hi, h_lo) + dot_nt(w_lo, h_hi))
    comb_t, group = _route(logits_t, rb_ref)
    tm = comb_t.shape[1]
    group = group.astype(F32)
    gid_ref[...] = jnp.concatenate([group, jnp.zeros((V7X_SUBLANES - 1, tm), F32)], axis=0)
    padded = jnp.concatenate([comb_t, jnp.zeros((AUX_GROUP - N_EXPERTS, tm), F32), group,
                              jnp.zeros((AUX_LANES - AUX_GROUP - 1, tm), F32)], axis=0)
    c = padded.T
    c_hi = c.astype(BF16).astype(F32)
    c_mid = (c - c_hi).astype(BF16).astype(F32)
    c_lo = c - c_hi - c_mid
    aux = c_hi + pltpu.roll(c_mid, AUX_MID, axis=1) + pltpu.roll(c_lo, AUX_LO, axis=1)
    h2_ref[...] = jnp.concatenate([h2.astype(BF16), aux.astype(BF16)], axis=1)


def merge_pallas(y_rnn, y_att, p, x, mod, g, w_br_rnn, w_br_att, w_out, w_router_t, router_bias, seq_len, per_seq, tm):
    n = x.shape[0]
    row = lambda i: (i, 0)
    full = lambda shape: pl.BlockSpec(shape, lambda i: (0, 0))
    return pl.pallas_call(
        _merge_kernel,
        grid=(n // tm,),
        in_specs=[pl.BlockSpec(memory_space=pltpu.SMEM),
                  pl.BlockSpec((tm, D_RNN), row), pl.BlockSpec((tm, D_ATT), row),
                  pl.BlockSpec((tm, CHUNK), lambda i: (i, C_GATE_R)), pl.BlockSpec((tm, CHUNK), lambda i: (i, C_GATE_A)),
                  pl.BlockSpec((tm, D_MODEL), row), _mod_spec(tm, seq_len, per_seq), full((1, D_MODEL)),
                  full((D_RNN, D_MODEL)), full((D_ATT, D_MODEL)), full((D_MODEL, D_MODEL)),
                  full((N_EXPERTS, D_MODEL))],
        out_specs=[pl.BlockSpec((tm, D_MODEL), row), pl.BlockSpec((tm, D_MOE_IN), row),
                   pl.BlockSpec((V7X_SUBLANES, tm), lambda i: (0, i))],
        out_shape=[jax.ShapeDtypeStruct((n, D_MODEL), F32), jax.ShapeDtypeStruct((n, D_MOE_IN), BF16),
                   jax.ShapeDtypeStruct((V7X_SUBLANES, n), F32)],
        compiler_params=_params("arbitrary"),
        name="merge",
    )(router_bias, y_rnn, y_att, p, p, x, mod, g, w_br_rnn, w_br_att, w_out, w_router_t)


MOE_TB = 1024
MOE_RT = 128
MOE_TBP = MOE_TB + N_GROUPS * MOE_RT


def _group_segments(gid_row):
    sub = lax.broadcasted_iota(jnp.int32, (V7X_SUBLANES, 1), 0).astype(F32)
    onehot = (gid_row == sub).astype(F32)
    cnt = jnp.sum(onehot, axis=1, keepdims=True)
    padded = jnp.floor((cnt + (MOE_RT - 1)) * (1.0 / MOE_RT)) * MOE_RT
    starts, run = [], jnp.zeros((1, 1), F32)
    for g in range(N_GROUPS):
        starts.append(run)
        run = run + padded[g:g + 1, :]
    return onehot, starts, [padded[g:g + 1, :] for g in range(N_GROUPS)], run


def _to_int(v):
    return v[0, 0].astype(jnp.int32)


def _moe_kernel(h_ref, gid_ref, x_ref, mod_ref, gf_ref, wg_ref, wu_ref, wd_ref, o_ref,
                p_scr, xs_scr, cs_scr, ys_scr, *, final_norm):
    e = pl.program_id(1)
    onehot, starts, sizes, used = _group_segments(gid_ref[0:1, :])
    lane = lax.broadcasted_iota(jnp.int32, (1, AUX_LANES), 1)

    @pl.when(e == 0)
    def _():
        t_row = lax.broadcasted_iota(jnp.int32, (MOE_TB, MOE_TB), 0)
        t_col = lax.broadcasted_iota(jnp.int32, (MOE_TB, MOE_TB), 1)
        earlier = (t_row < t_col).astype(BF16)
        rank = jnp.dot(onehot.astype(BF16), earlier, preferred_element_type=F32)
        pos = jnp.zeros((1, MOE_TB), F32)
        for g in range(N_GROUPS):
            pos = pos + onehot[g:g + 1, :] * (starts[g] + rank[g:g + 1, :])
        dest = lax.broadcasted_iota(jnp.int32, (MOE_TBP, 1), 0).astype(F32)
        p_scr[...] = (dest == pos).astype(BF16)
        for r0 in range(0, MOE_TBP, V7X_MXU_DIM):
            rows = pl.ds(r0, V7X_MXU_DIM)
            sorted_rows = jnp.dot(p_scr[rows, :], h_ref[...], preferred_element_type=F32)
            xs_scr[rows, :] = sorted_rows[:, :D_MODEL].astype(BF16)
            aux = sorted_rows[:, D_MODEL:]
            cs_scr[rows, :] = (aux + pltpu.roll(aux, AUX_LANES - AUX_MID, axis=1)
                               + pltpu.roll(aux, AUX_LANES - AUX_LO, axis=1))
        first_free = pl.multiple_of(_to_int(used), MOE_RT)

        def clear(k, carry):
            ys_scr[pl.ds(pl.multiple_of(first_free + k * MOE_RT, MOE_RT), MOE_RT), :] = jnp.zeros(
                (MOE_RT, D_MODEL), F32)
            return carry

        lax.fori_loop(0, (MOE_TBP - first_free) // MOE_RT, clear, 0)

    group = e // EXPERTS_PER_GROUP
    member = e % EXPERTS_PER_GROUP
    start_v, size_v = starts[0], sizes[0]
    for g in range(1, N_GROUPS):
        start_v = jnp.where(group == g, starts[g], start_v)
        size_v = jnp.where(group == g, sizes[g], size_v)
    seg_start = _to_int(start_v)
    n_tiles = _to_int(size_v) // MOE_RT

    def run_expert(r0, n_rows):
        rows = pl.ds(pl.multiple_of(r0, MOE_RT), n_rows)
        x = xs_scr[rows, :]
        gate = jnp.dot(x, wg_ref[0, 0], preferred_element_type=F32)
        up = jnp.dot(x, wu_ref[0, 0], preferred_element_type=F32)
        act = (gate * jax.nn.sigmoid(gate)) * up
        y = jnp.dot(act.astype(BF16), wd_ref[0, 0], preferred_element_type=F32)
        y = jnp.sum(jnp.where(lane == e, cs_scr[rows, :], 0.0), axis=-1, keepdims=True) * y

        @pl.when(member == 0)
        def _():
            ys_scr[rows, :] = y

        @pl.when(member != 0)
        def _():
            ys_scr[rows, :] += y

    def pair(k, carry):
        run_expert(seg_start + k * (2 * MOE_RT), 2 * MOE_RT)
        return carry

    lax.fori_loop(0, n_tiles // 2, pair, 0)

    @pl.when(n_tiles % 2 == 1)
    def _():
        run_expert(seg_start + (n_tiles - 1) * MOE_RT, MOE_RT)

    @pl.when(e == N_EXPERTS - 1)
    def _():
        y = lax.dot_general(p_scr[...], ys_scr[...].astype(BF16), (((0,), (0,)), ((), ())),
                            preferred_element_type=F32)
        x2 = x_ref[...] + mod_ref[0, 5:6, :] * y
        o_ref[...] = _rms(x2, gf_ref[...]) if final_norm else x2


def moe_pallas(h2x, gid, x1, mod, g_final, w_gate, w_up, w_down, layer, seq_len, per_seq, final_norm):
    n = x1.shape[0]
    row = lambda i, e: (i, 0)
    expert = lambda i, e: (layer, e, 0, 0)
    return pl.pallas_call(
        functools.partial(_moe_kernel, final_norm=final_norm),
        grid=(n // MOE_TB, N_EXPERTS),
        in_specs=[pl.BlockSpec((MOE_TB, D_MOE_IN), row), pl.BlockSpec((V7X_SUBLANES, MOE_TB), lambda i, e: (0, i)),
                  pl.BlockSpec((MOE_TB, D_MODEL), row), _mod_spec(MOE_TB, seq_len, per_seq),
                  pl.BlockSpec((1, D_MODEL), lambda i, e: (0, 0)),
                  pl.BlockSpec((1, 1, D_MODEL, D_EXPERT), expert),
                  pl.BlockSpec((1, 1, D_MODEL, D_EXPERT), expert),
                  pl.BlockSpec((1, 1, D_EXPERT, D_MODEL), expert)],
        out_specs=pl.BlockSpec((MOE_TB, D_MODEL), row),
        out_shape=jax.ShapeDtypeStruct((n, D_MODEL), F32),
        scratch_shapes=[pltpu.VMEM((MOE_TBP, MOE_TB), BF16), pltpu.VMEM((MOE_TBP, D_MODEL), BF16),
                        pltpu.VMEM((MOE_TBP, AUX_LANES), F32), pltpu.VMEM((MOE_TBP, D_MODEL), F32)],
        compiler_params=_params("arbitrary", "arbitrary"),
        name="moe",
    )(h2x, gid, x1, mod, g_final, w_gate, w_up, w_down)


TM_PROJ = 1024
TM_MERGE = 512


def _layer(x, mod, seq_len, per_seq, lw, h0, ctx_kv, tables, layer):
    n = x.shape[0]
    b = n // seq_len
    emit_kv = ctx_kv is None
    outs = inproj_pallas(x, mod, lw['norm_g'][0:1], lw['w_in'], seq_len, per_seq, TM_PROJ, emit_kv)
    p = outs[0]
    p3 = p.reshape(b, seq_len, D_IN)
    y_rnn, h_fin = lru_pallas(p3, h0, lw['conv_w'], lw['conv_b'], lw['wg'], lw['gb'], lw['lam'])
    if emit_kv:
        y_att = ctxatt_pallas(p3)
    else:
        y_att = natten_pallas(p3, ctx_kv[0], ctx_kv[1], tables, layer)
    x1, h2x, gid = merge_pallas(y_rnn.reshape(n, D_RNN), y_att.reshape(n, D_ATT), p, x, mod, lw['norm_g'][1:2],
                                lw['w_br_rnn'], lw['w_br_att'], lw['w_out'], lw['w_router_t'], lw['router_bias'],
                                seq_len, per_seq, TM_MERGE)
    x2 = moe_pallas(h2x, gid, x1, mod, lw['g_final'], lw['w_exp_gate'], lw['w_exp_up'], lw['w_exp_down'], layer,
                    seq_len, per_seq, layer == DEPTH - 1)
    kv = (outs[1], outs[2]) if emit_kv else None
    return x2, kv, h_fin


def kernel(x_prompt, x_sample, cache_k, cache_v, state_lru, c, c_ctx, w_ada, b_ada, norm_g, w_in, conv_w,
           conv_b, lru_wa, lru_ba, lru_wx, lru_bx, lru_lam, rpb, w_br_rnn, w_br_att, w_out, w_router,
           router_bias, w_exp_gate, w_exp_up, w_exp_down, final_norm_g):
    cvecs = jnp.concatenate([c, c_ctx[None, :], jnp.zeros((V7X_SUBLANES - DEC_BATCH - 1, D_MODEL), F32)], axis=0)
    mods = adaln_pallas(cvecs, w_ada, b_ada).reshape(DEPTH, V7X_SUBLANES, N_MOD, D_MODEL)
    tables = _natten_tables(rpb)
    w_router_t = w_router.T
    xp = x_prompt.reshape(BATCH * SEQ, D_MODEL)
    xs = x_sample.reshape(DEC_BATCH * DEC_SEQ, D_MODEL)
    zeros_h0 = jnp.zeros((BATCH, 2, D_RNN), F32)
    w_exp = [w.astype(BF16) for w in (w_exp_gate, w_exp_up, w_exp_down)]
    ks, vs, hs = [], [], []
    for l in range(DEPTH):
        lw = dict(
            norm_g=norm_g[l], w_in=w_in[l].astype(BF16), conv_w=conv_w[l], conv_b=conv_b[l][None, :],
            wg=_lru_gate_weights(lru_wa[l], lru_wx[l]),
            gb=jnp.stack([lru_ba[l, 0], lru_bx[l, 0], lru_ba[l, 1], lru_bx[l, 1]], axis=0), lam=lru_lam[l],
            w_br_rnn=w_br_rnn[l].astype(BF16), w_br_att=w_br_att[l].astype(BF16), w_out=w_out[l].astype(BF16),
            w_router_t=w_router_t, router_bias=router_bias, g_final=final_norm_g[None, :],
            w_exp_gate=w_exp[0], w_exp_up=w_exp[1], w_exp_down=w_exp[2])
        xp, kv, h_l = _layer(xp, mods[l, DEC_BATCH:DEC_BATCH + 1], SEQ, False, lw, zeros_h0, None, None, l)
        ks.append(kv[0].reshape(BATCH, SEQ, N_HEADS, HEAD_DIM))
        vs.append(kv[1].reshape(BATCH, SEQ, N_HEADS, HEAD_DIM))
        hs.append(h_l)
        ctx_kv = (cache_k[:, l].reshape(DEC_BATCH, PAST_LEN, D_ATT).astype(BF16),
                  cache_v[:, l].reshape(DEC_BATCH, PAST_LEN, D_ATT).astype(BF16))
        xs, _, _ = _layer(xs, mods[l, :DEC_BATCH], DEC_SEQ, True, lw, state_lru[:, l], ctx_kv, tables, l)
    y_prompt = xp.reshape(BATCH, SEQ, D_MODEL)
    y_sample = xs.reshape(DEC_BATCH, DEC_SEQ, D_MODEL)
    return (y_prompt, y_sample, jnp.stack(ks, axis=1), jnp.stack(vs, axis=1), jnp.stack(hs, axis=1))
```

```python
import functools

import jax
import jax.numpy as jnp
import numpy as np
from jax import lax
from jax.experimental import pallas as pl
from jax.experimental.pallas import tpu as pltpu

D_MODEL = 1024
BATCH = 16
SEQ = 256
DEPTH = 2
DEC_BATCH = 4
DEC_SEQ = 4096
PAST_LEN = 256

GRID_W = 64
D_RNN = 1024
N_LRU_BLOCKS = 16
LRU_BLOCK = D_RNN // N_LRU_BLOCKS
CONV_W = 4
LRU_C = 8.0
N_HEADS = 16
HEAD_DIM = 64
D_ATT = N_HEADS * HEAD_DIM
WIN_H = 8
WIN_W = 16
N_EXPERTS = 16
N_GROUPS = 4
EXPERTS_PER_GROUP = N_EXPERTS // N_GROUPS
D_EXPERT = 512
N_MOD = 6
D_IN = 2 * D_RNN + 3 * D_ATT + 2 * D_MODEL
EPS = 1e-6
NEG_INF = -1e30

BF16 = jnp.bfloat16
F32 = jnp.float32

V7X_LANES = 128
V7X_SUBLANES = 8
V7X_MXU_DIM = 256
V7X_VMEM_BYTES = 64 * 1024 * 1024
VMEM_LIMIT = V7X_VMEM_BYTES - 8 * 1024 * 1024

CHUNK = D_MODEL
N_CHUNKS = D_IN // CHUNK
C_XRNN, C_GRNN, C_Q, C_K, C_V, C_GATE_R, C_GATE_A = range(N_CHUNKS)

NT_DIMS = (((1,), (1,)), ((), ()))


def _params(*sem):
    return pltpu.CompilerParams(dimension_semantics=sem, vmem_limit_bytes=VMEM_LIMIT)


def _adaln_kernel(c_ref, w_ref, b_ref, o_ref):
    cv = c_ref[...]
    s = cv * jax.nn.sigmoid(cv)
    o_ref[0] = jnp.dot(s.astype(BF16), w_ref[0].astype(BF16), preferred_element_type=F32) + b_ref[0]


def adaln_pallas(cvecs, w_ada, b_ada):
    r = cvecs.shape[0]
    return pl.pallas_call(
        _adaln_kernel,
        grid=(DEPTH, N_MOD),
        in_specs=[pl.BlockSpec((r, D_MODEL), lambda l, j: (0, 0)),
                  pl.BlockSpec((1, D_MODEL, D_MODEL), lambda l, j: (l, 0, j)),
                  pl.BlockSpec((1, 1, D_MODEL), lambda l, j: (l, 0, j))],
        out_specs=pl.BlockSpec((1, r, D_MODEL), lambda l, j: (l, 0, j)),
        out_shape=jax.ShapeDtypeStruct((DEPTH, r, N_MOD * D_MODEL), F32),
        compiler_params=_params("arbitrary", "arbitrary"),
        name="adaln",
    )(cvecs, w_ada, b_ada.reshape(DEPTH, 1, N_MOD * D_MODEL))


def _mod_spec(tm, seq_len, per_seq):
    if per_seq:
        return pl.BlockSpec((1, N_MOD, D_MODEL), lambda i, *_: (i * tm // seq_len, 0, 0))
    return pl.BlockSpec((1, N_MOD, D_MODEL), lambda i, *_: (0, 0, 0))


def _rms(x, g):
    return x * lax.rsqrt(jnp.mean(x * x, axis=-1, keepdims=True) + EPS) * g


def _inproj_kernel(x_ref, mod_ref, g_ref, w_ref, *refs, emit_kv):
    if emit_kv:
        p_ref, k32_ref, v32_ref, h_scr = refs
    else:
        p_ref, h_scr = refs
    j = pl.program_id(1)

    @pl.when(j == 0)
    def _():
        y = _rms(x_ref[...], g_ref[...])
        h_scr[...] = (y * (1.0 + mod_ref[0, 1:2, :]) + mod_ref[0, 0:1, :]).astype(BF16)

    acc = jnp.dot(h_scr[...], w_ref[0], preferred_element_type=F32)
    p_ref[...] = (acc * jnp.where(j == C_Q, HEAD_DIM ** -0.5, 1.0)).astype(BF16)
    if emit_kv:
        @pl.when(j == C_K)
        def _():
            k32_ref[...] = acc

        @pl.when(j == C_V)
        def _():
            v32_ref[...] = acc


def inproj_pallas(x, mod, g, w_in, layer, seq_len, per_seq, tm, emit_kv):
    n = x.shape[0]
    row = lambda i, j: (i, 0)
    out_shape = [jax.ShapeDtypeStruct((n, D_IN), BF16)]
    out_specs = [pl.BlockSpec((tm, CHUNK), lambda i, j: (i, j))]
    if emit_kv:
        out_shape += [jax.ShapeDtypeStruct((n, D_ATT), F32)] * 2
        out_specs += [pl.BlockSpec((tm, D_ATT), row)] * 2
    return pl.pallas_call(
        functools.partial(_inproj_kernel, emit_kv=emit_kv),
        grid=(n // tm, N_CHUNKS),
        in_specs=[pl.BlockSpec((tm, D_MODEL), row), _mod_spec(tm, seq_len, per_seq),
                  pl.BlockSpec((1, D_MODEL), lambda i, j: (0, 0)),
                  pl.BlockSpec((1, D_MODEL, CHUNK), lambda i, j: (layer, 0, j))],
        out_specs=out_specs,
        out_shape=out_shape,
        scratch_shapes=[pltpu.VMEM((tm, D_MODEL), BF16)],
        compiler_params=_params("arbitrary", "arbitrary"),
        name="inproj",
    )(x, mod, g, w_in)


LRU_CB = 512
LRU_TC = 256
LRU_SUB = V7X_MXU_DIM
LRU_HALO = 16


def _scan_groups(a, b, carry, reverse):
    tc, c = a.shape
    ng = tc // V7X_SUBLANES
    a3 = a.reshape(ng, V7X_SUBLANES, c)
    b3 = b.reshape(ng, V7X_SUBLANES, c)
    sub = lax.broadcasted_iota(jnp.int32, (1, V7X_SUBLANES, 1), 1)
    s = 1
    while s < V7X_SUBLANES:
        shift = V7X_SUBLANES - s if reverse else s
        ok = (sub < V7X_SUBLANES - s) if reverse else (sub >= s)
        a_sh = pltpu.roll(a3, shift, axis=1)
        b_sh = pltpu.roll(b3, shift, axis=1)
        b3 = jnp.where(ok, a3 * b_sh + b3, b3)
        a3 = jnp.where(ok, a3 * a_sh, a3)
        s *= 2
    hs = [None] * ng
    order = range(ng - 1, -1, -1) if reverse else range(ng)
    edge = 0 if reverse else V7X_SUBLANES - 1
    for gi in order:
        h = a3[gi] * carry + b3[gi]
        carry = h[edge:edge + 1, :]
        hs[gi] = h
    return jnp.concatenate(hs, axis=0), carry


def _sigmoid(x):
    return 0.5 * jnp.tanh(0.5 * x) + 0.5


def _lru_kernel(x_ref, gate_ref, h0_ref, cw_ref, cb_ref, wg_ref, gb_ref, lam_ref, y_ref, fin_ref, hf_scr, u_scr):
    t_len = x_ref.shape[1]
    cb = x_ref.shape[2]
    n_chunks = t_len // LRU_TC

    def conv_chunk(c):
        t0 = pl.multiple_of(c * LRU_TC, LRU_TC)
        cur = x_ref[0, pl.ds(t0, LRU_TC), :].astype(F32)
        lo = pl.multiple_of(jnp.maximum(t0 - LRU_HALO, 0), LRU_HALO)
        hi = pl.multiple_of(jnp.minimum(t0 + LRU_TC, t_len - LRU_HALO), LRU_HALO)
        prev = jnp.where(c > 0, x_ref[0, pl.ds(lo, LRU_HALO), :].astype(F32), 0.0)
        nxt = jnp.where(c < n_chunks - 1, x_ref[0, pl.ds(hi, LRU_HALO), :].astype(F32), 0.0)
        ext = jnp.concatenate([prev, cur, nxt], axis=0)
        n_ext = LRU_TC + 2 * LRU_HALO
        u = cb_ref[...] + jnp.zeros((LRU_TC, cb), F32)
        for j in range(CONV_W):
            off = j - CONV_W // 2
            sh = ext if off == 0 else pltpu.roll(ext, (-off) % n_ext, axis=0)
            u = u + sh[LRU_HALO:LRU_HALO + LRU_TC, :] * cw_ref[j:j + 1, :]
        return t0, u

    def gates(u, d):
        ub = u.astype(BF16)
        pre = [jnp.dot(ub[:, LRU_SUB * s:LRU_SUB * (s + 1)], wg_ref[d, s], preferred_element_type=F32)
               for s in range(cb // LRU_SUB)]
        pre_a = jnp.concatenate([p[:, :LRU_SUB] for p in pre], axis=1)
        pre_x = jnp.concatenate([p[:, LRU_SUB:] for p in pre], axis=1)
        r = _sigmoid(pre_a + gb_ref[2 * d:2 * d + 1, :])
        i = _sigmoid(pre_x + gb_ref[2 * d + 1:2 * d + 2, :])
        log_a = (-LRU_C * jax.nn.softplus(-lam_ref[d:d + 1, :])) * r
        a = jnp.exp(log_a)
        th = jnp.tanh(log_a)
        inp = (jnp.sqrt(-2.0 * th) * lax.rsqrt(1.0 - th)) * (i * u)
        return a, inp

    def fwd(c, carry):
        t0, u = conv_chunk(c)
        u_scr[pl.ds(t0, LRU_TC), :] = u
        a, inp = gates(u, 0)
        h, carry = _scan_groups(a, inp, carry, reverse=False)
        hf_scr[pl.ds(t0, LRU_TC), :] = h
        return carry

    fin_f = lax.fori_loop(0, n_chunks, fwd, h0_ref[0, 0:1, :])

    def bwd(k, carry):
        t0 = pl.multiple_of((n_chunks - 1 - k) * LRU_TC, LRU_TC)
        a, inp = gates(u_scr[pl.ds(t0, LRU_TC), :], 1)
        h, carry = _scan_groups(a, inp, carry, reverse=True)
        g = gate_ref[0, pl.ds(t0, LRU_TC), :].astype(F32)
        y_ref[0, pl.ds(t0, LRU_TC), :] = ((hf_scr[pl.ds(t0, LRU_TC), :] + h) * jax.nn.gelu(g)).astype(y_ref.dtype)
        return carry

    fin_b = lax.fori_loop(0, n_chunks, bwd, h0_ref[0, 1:2, :])
    fin_ref[0, 0:1, :] = fin_f
    fin_ref[0, 1:2, :] = fin_b


def _lru_gate_weights(lru_wa, lru_wx):
    per = LRU_SUB // LRU_BLOCK
    eye = jnp.eye(per, dtype=F32)

    def dense(w):
        w = w.reshape(2, D_RNN // LRU_SUB, per, LRU_BLOCK, LRU_BLOCK)
        full = w[:, :, :, :, None, :] * eye[None, None, :, None, :, None]
        return full.reshape(2, D_RNN // LRU_SUB, LRU_SUB, LRU_SUB)

    return jnp.concatenate([dense(lru_wa), dense(lru_wx)], axis=-1).astype(BF16)


def lru_pallas(p, h0, conv_w, conv_b, wg, gb, lam):
    b, t, _ = p.shape
    n_cb = D_RNN // LRU_CB
    return pl.pallas_call(
        _lru_kernel,
        grid=(b, n_cb),
        in_specs=[pl.BlockSpec((1, t, LRU_CB), lambda bi, ci: (bi, 0, C_XRNN * n_cb + ci)),
                  pl.BlockSpec((1, t, LRU_CB), lambda bi, ci: (bi, 0, C_GRNN * n_cb + ci)),
                  pl.BlockSpec((1, 2, LRU_CB), lambda bi, ci: (bi, 0, ci)),
                  pl.BlockSpec((CONV_W, LRU_CB), lambda bi, ci: (0, ci)),
                  pl.BlockSpec((1, LRU_CB), lambda bi, ci: (0, ci)),
                  pl.BlockSpec((2, LRU_CB // LRU_SUB, LRU_SUB, 2 * LRU_SUB), lambda bi, ci: (0, ci, 0, 0)),
                  pl.BlockSpec((4, LRU_CB), lambda bi, ci: (0, ci)),
                  pl.BlockSpec((2, LRU_CB), lambda bi, ci: (0, ci))],
        out_specs=[pl.BlockSpec((1, t, LRU_CB), lambda bi, ci: (bi, 0, ci)),
                   pl.BlockSpec((1, 2, LRU_CB), lambda bi, ci: (bi, 0, ci))],
        out_shape=[jax.ShapeDtypeStruct((b, t, D_RNN), BF16), jax.ShapeDtypeStruct((b, 2, D_RNN), F32)],
        scratch_shapes=[pltpu.VMEM((t, LRU_CB), F32), pltpu.VMEM((t, LRU_CB), F32)],
        compiler_params=_params("arbitrary", "arbitrary"),
        name="lru",
    )(p, p, h0, conv_w, conv_b, wg, gb, lam)


Q_ROWS = 4
Q_TILE = Q_ROWS * GRID_W
KEY_ROWS = 12
KEY_TILE = KEY_ROWS * GRID_W
N_Q_TILES = DEC_SEQ // Q_TILE
ATT_LANES = 256
HEADS_PER_STEP = ATT_LANES // HEAD_DIM
PAIR = 2 * HEAD_DIM


def _window_block(t):
    return jnp.clip(t - 1, 0, N_Q_TILES - KEY_ROWS // Q_ROWS)


N_DR_PAIRS = 2 * WIN_H


def _natten_tables(rpb):
    c = np.arange(GRID_W)
    q_start = np.clip(c - WIN_W // 2, 0, GRID_W - WIN_W)
    col_valid = (c[None, :] >= q_start[:, None]) & (c[None, :] < q_start[:, None] + WIN_W)
    dc = c[None, :] - c[:, None] + WIN_W - 1
    sel_c = ((dc[None] == np.arange(2 * WIN_W - 1)[:, None, None]) & col_valid[None]).astype(np.float32)
    t = jnp.einsum('lhrd,dck->lhrck', rpb.astype(F32), jnp.asarray(sel_c), precision=lax.Precision.HIGHEST)
    t = jnp.where(col_valid[None, None, None], t, NEG_INF)
    t = jnp.pad(t, ((0, 0), (0, 0), (1, 1), (0, 0), (0, 0)), constant_values=NEG_INF)
    return jnp.concatenate([t[:, :, :-1], t[:, :, 1:]], axis=-1)


def _attend(q2, keys, vals, biases):
    lane_head = lax.broadcasted_iota(jnp.int32, (1, PAIR), 1) // HEAD_DIM
    acc = jnp.zeros((q2.shape[0], PAIR), F32)
    for j in range(2):
        mine = lane_head == j
        qh = jnp.where(mine, q2, jnp.zeros_like(q2))
        s = []
        for kb, bb in zip(keys, biases[j]):
            sd = lax.dot_general(qh, kb, NT_DIMS, preferred_element_type=F32)
            s.append(sd if bb is None else sd + bb)
        m = s[0].max(axis=-1, keepdims=True)
        for sd in s[1:]:
            m = jnp.maximum(m, sd.max(axis=-1, keepdims=True))
        o = None
        for sd, vb in zip(s, vals):
            pv = jnp.dot(jnp.exp(sd - m).astype(BF16), jnp.where(mine, vb, jnp.ones_like(vb)),
                         preferred_element_type=F32)
            o = pv if o is None else o + pv
        acc = acc + jnp.where(mine, o / pltpu.roll(o, HEAD_DIM, axis=1), 0.0)
    return acc


def _natten_kernel(q_ref, k0_ref, k1_ref, k2_ref, v0_ref, v1_ref, v2_ref, kc_ref, vc_ref, tt_ref, o_ref):
    ti = pl.program_id(1)
    rows = DEC_SEQ // GRID_W
    wstart = _window_block(ti) * Q_ROWS
    first_row = lax.broadcasted_iota(jnp.int32, (1, 2 * GRID_W), 1) < GRID_W
    pieces = {}
    for a in range(Q_ROWS):
        r = ti * Q_ROWS + a
        start_r = jnp.clip(r - WIN_H // 2, 0, rows - WIN_H)
        for i in range(0, KEY_ROWS, 2):
            kr = wstart + i
            ok = [((kr + e >= start_r) & (kr + e < start_r + WIN_H)).astype(jnp.int32) for e in range(2)]
            pieces[a, i] = (jnp.clip(kr - r + WIN_H, 0, N_DR_PAIRS - 1), jnp.where(first_row, ok[0], ok[1]) > 0)

    def bias_block(h, d):
        return jnp.concatenate(
            [jnp.concatenate([jnp.where(pieces[a, i][1], tt_ref[0, h, pieces[a, i][0]], NEG_INF)
                              for i in range(Q_ROWS * d, Q_ROWS * (d + 1), 2)], axis=1)
             for a in range(Q_ROWS)], axis=0)

    k_refs = (k0_ref, k1_ref, k2_ref, kc_ref)
    v_refs = (v0_ref, v1_ref, v2_ref, vc_ref)
    outs = []
    for hp in range(ATT_LANES // PAIR):
        sl = slice(PAIR * hp, PAIR * (hp + 1))
        biases = [[bias_block(2 * hp + j, d) for d in range(3)] + [None] for j in range(2)]
        outs.append(_attend(q_ref[0, :, sl], [r[0, :, sl] for r in k_refs], [r[0, :, sl] for r in v_refs], biases))
    o_ref[0] = jnp.concatenate(outs, axis=1).astype(o_ref.dtype)


def natten_pallas(p, k_ctx, v_ctx, tables, layer):
    b, t, _ = p.shape
    n_lb = D_ATT // ATT_LANES
    q_spec = pl.BlockSpec((1, Q_TILE, ATT_LANES), lambda lb, ti, bi: (bi, ti, C_Q * n_lb + lb))
    win = lambda ch, d: pl.BlockSpec((1, Q_TILE, ATT_LANES),
                                     lambda lb, ti, bi: (bi, _window_block(ti) + d, ch * n_lb + lb))
    ctx = pl.BlockSpec((1, PAST_LEN, ATT_LANES), lambda lb, ti, bi: (bi, 0, lb))
    return pl.pallas_call(
        _natten_kernel,
        grid=(n_lb, N_Q_TILES, b),
        in_specs=[q_spec, win(C_K, 0), win(C_K, 1), win(C_K, 2), win(C_V, 0), win(C_V, 1), win(C_V, 2), ctx, ctx,
                  pl.BlockSpec((1, HEADS_PER_STEP, N_DR_PAIRS, GRID_W, 2 * GRID_W),
                               lambda lb, ti, bi: (layer, lb, 0, 0, 0))],
        out_specs=pl.BlockSpec((1, Q_TILE, ATT_LANES), lambda lb, ti, bi: (bi, ti, lb)),
        out_shape=jax.ShapeDtypeStruct((b, t, D_ATT), BF16),
        compiler_params=_params("arbitrary", "arbitrary", "arbitrary"),
        name="natten",
    )(p, p, p, p, p, p, p, k_ctx, v_ctx, tables)


def _ctxatt_kernel(q_ref, k_ref, v_ref, o_ref):
    outs = []
    for hp in range(ATT_LANES // PAIR):
        sl = slice(PAIR * hp, PAIR * (hp + 1))
        outs.append(_attend(q_ref[0, :, sl], [k_ref[0, :, sl]], [v_ref[0, :, sl]], [[None], [None]]))
    o_ref[0] = jnp.concatenate(outs, axis=1).astype(o_ref.dtype)


def ctxatt_pallas(p):
    b, t, _ = p.shape
    n_lb = D_ATT // ATT_LANES
    blk = lambda ch: pl.BlockSpec((1, t, ATT_LANES), lambda lb, bi: (bi, 0, ch * n_lb + lb))
    return pl.pallas_call(
        _ctxatt_kernel,
        grid=(n_lb, b),
        in_specs=[blk(C_Q), blk(C_K), blk(C_V)],
        out_specs=pl.BlockSpec((1, t, ATT_LANES), lambda lb, bi: (bi, 0, lb)),
        out_shape=jax.ShapeDtypeStruct((b, t, D_ATT), BF16),
        compiler_params=_params("arbitrary", "arbitrary"),
        name="ctxatt",
    )(p, p, p)


def _split_bf16(x):
    hi = x.astype(BF16)
    return hi, (x - hi.astype(F32)).astype(BF16)


def _route(logits_t, rb_ref):
    score = [jax.nn.sigmoid(logits_t[e:e + 1, :]) for e in range(N_EXPERTS)]
    sel = [score[e] + rb_ref[e] for e in range(N_EXPERTS)]
    best_g = None
    for g in range(N_GROUPS):
        v = sel[EXPERTS_PER_GROUP * g:EXPERTS_PER_GROUP * (g + 1)]
        top2 = None
        for i in range(EXPERTS_PER_GROUP):
            for j in range(i + 1, EXPERTS_PER_GROUP):
                pair = v[i] + v[j]
                top2 = pair if top2 is None else jnp.maximum(top2, pair)
        if best_g is None:
            best_g, best_v = jnp.zeros_like(top2, dtype=jnp.int32), top2
        else:
            upd = top2 > best_v
            best_g = jnp.where(upd, g, best_g)
            best_v = jnp.where(upd, top2, best_v)

    def in_best(vals, j):
        out = vals[j]
        for g in range(1, N_GROUPS):
            out = jnp.where(best_g == g, vals[EXPERTS_PER_GROUP * g + j], out)
        return out

    v = [in_best(sel, j) for j in range(EXPERTS_PER_GROUP)]
    sc = [in_best(score, j) for j in range(EXPERTS_PER_GROUP)]

    def first_argmax(vals):
        idx, top = jnp.zeros_like(best_g), vals[0]
        for j in range(1, EXPERTS_PER_GROUP):
            upd = vals[j] > top
            idx = jnp.where(upd, j, idx)
            top = jnp.where(upd, vals[j], top)
        return idx

    i1 = first_argmax(v)
    i2 = first_argmax([jnp.where(i1 == j, -jnp.inf, v[j]) for j in range(EXPERTS_PER_GROUP)])
    pick = lambda idx: sum(jnp.where(idx == j, sc[j], 0.0) for j in range(EXPERTS_PER_GROUP))
    w1, w2 = pick(i1), pick(i2)
    den = w1 + w2
    c1, c2 = w1 / den, w2 / den
    rows = []
    for e in range(N_EXPERTS):
        g, j = divmod(e, EXPERTS_PER_GROUP)
        rows.append(jnp.where(best_g == g, jnp.where(i1 == j, c1, 0.0) + jnp.where(i2 == j, c2, 0.0), 0.0))
    return jnp.concatenate(rows, axis=0), best_g


AUX_LANES = V7X_LANES
AUX_MID = N_EXPERTS
AUX_LO = 2 * N_EXPERTS
AUX_GROUP = 3 * N_EXPERTS
D_MOE_IN = D_MODEL + AUX_LANES


def _merge_kernel(rb_ref, yr_ref, ya_ref, gr_ref, ga_ref, x_ref, mod_ref, g_ref, wr_ref, wa_ref, wo_ref, wrt_ref,
                  x1_ref, h2_ref, gid_ref):
    m = (_sigmoid(gr_ref[...].astype(F32)) * jnp.dot(yr_ref[...], wr_ref[0], preferred_element_type=F32)
         + _sigmoid(ga_ref[...].astype(F32)) * jnp.dot(ya_ref[...], wa_ref[0], preferred_element_type=F32))
    o = jnp.dot(m.astype(BF16), wo_ref[0], preferred_element_type=F32)
    x1 = x_ref[...] + mod_ref[0, 2:3, :] * o
    x1_ref[...] = x1
    h2 = _rms(x1, g_ref[...]) * (1.0 + mod_ref[0, 4:5, :]) + mod_ref[0, 3:4, :]
    h_hi, h_lo = _split_bf16(h2)
    w_hi, w_lo = _split_bf16(wrt_ref[...])
    dot_nt = lambda a, b: lax.dot_general(a, b, NT_DIMS, preferred_element_type=F32)
    logits_t = dot_nt(w_hi, h_hi) + (dot_nt(w_hi, h_lo) + dot_nt(w_lo, h_hi))
    comb_t, group = _route(logits_t, rb_ref)
    tm = comb_t.shape[1]
    group = group.astype(F32)
    gid_ref[...] = jnp.concatenate([group, jnp.zeros((V7X_SUBLANES - 1, tm), F32)], axis=0)
    padded = jnp.concatenate([comb_t, jnp.zeros((AUX_GROUP - N_EXPERTS, tm), F32), group,
                              jnp.zeros((AUX_LANES - AUX_GROUP - 1, tm), F32)], axis=0)
    c = padded.T
    c_hi = c.astype(BF16).astype(F32)
    c_mid = (c - c_hi).astype(BF16).astype(F32)
    c_lo = c - c_hi - c_mid
    aux = c_hi + pltpu.roll(c_mid, AUX_MID, axis=1) + pltpu.roll(c_lo, AUX_LO, axis=1)
    h2_ref[...] = jnp.concatenate([h2.astype(BF16), aux.astype(BF16)], axis=1)


def merge_pallas(y_rnn, y_att, p, x, mod, g, w_br_rnn, w_br_att, w_out, w_router_t, router_bias, layer, seq_len,
                 per_seq, tm):
    n = x.shape[0]
    row = lambda i: (i, 0)
    full = lambda shape: pl.BlockSpec(shape, lambda i: (0, 0))
    weight = pl.BlockSpec((1, D_MODEL, D_MODEL), lambda i: (layer, 0, 0))
    return pl.pallas_call(
        _merge_kernel,
        grid=(n // tm,),
        in_specs=[pl.BlockSpec(memory_space=pltpu.SMEM),
                  pl.BlockSpec((tm, D_RNN), row), pl.BlockSpec((tm, D_ATT), row),
                  pl.BlockSpec((tm, CHUNK), lambda i: (i, C_GATE_R)), pl.BlockSpec((tm, CHUNK), lambda i: (i, C_GATE_A)),
                  pl.BlockSpec((tm, D_MODEL), row), _mod_spec(tm, seq_len, per_seq), full((1, D_MODEL)),
                  weight, weight, weight, full((N_EXPERTS, D_MODEL))],
        out_specs=[pl.BlockSpec((tm, D_MODEL), row), pl.BlockSpec((tm, D_MOE_IN), row),
                   pl.BlockSpec((V7X_SUBLANES, tm), lambda i: (0, i))],
        out_shape=[jax.ShapeDtypeStruct((n, D_MODEL), F32), jax.ShapeDtypeStruct((n, D_MOE_IN), BF16),
                   jax.ShapeDtypeStruct((V7X_SUBLANES, n), F32)],
        compiler_params=_params("arbitrary"),
        name="merge",
    )(router_bias, y_rnn, y_att, p, p, x, mod, g, w_br_rnn, w_br_att, w_out, w_router_t)


MOE_TB = 1024
MOE_RT = 128
MOE_TBP = MOE_TB + N_GROUPS * MOE_RT
MOE_EPS = 2


def _group_segments(gid_row):
    sub = lax.broadcasted_iota(jnp.int32, (V7X_SUBLANES, 1), 0).astype(F32)
    onehot = (gid_row == sub).astype(F32)
    cnt = jnp.sum(onehot, axis=1, keepdims=True)
    padded = jnp.floor((cnt + (MOE_RT - 1)) * (1.0 / MOE_RT)) * MOE_RT
    starts, run = [], jnp.zeros((1, 1), F32)
    for g in range(N_GROUPS):
        starts.append(run)
        run = run + padded[g:g + 1, :]
    return onehot, starts, [padded[g:g + 1, :] for g in range(N_GROUPS)], run


def _to_int(v):
    return v[0, 0].astype(jnp.int32)


def _moe_kernel(h_ref, gid_ref, x_ref, mod_ref, gf_ref, wg_ref, wu_ref, wd_ref, o_ref,
                p_scr, xs_scr, cs_scr, ys_scr, *, final_norm):
    step = pl.program_id(1)
    onehot, starts, sizes, used = _group_segments(gid_ref[0:1, :])
    lane = lax.broadcasted_iota(jnp.int32, (1, AUX_LANES), 1)

    @pl.when(step == 0)
    def _():
        t_row = lax.broadcasted_iota(jnp.int32, (MOE_TB, MOE_TB), 0)
        t_col = lax.broadcasted_iota(jnp.int32, (MOE_TB, MOE_TB), 1)
        earlier = (t_row < t_col).astype(BF16)
        rank = jnp.dot(onehot.astype(BF16), earlier, preferred_element_type=F32)
        pos = jnp.zeros((1, MOE_TB), F32)
        for g in range(N_GROUPS):
            pos = pos + onehot[g:g + 1, :] * (starts[g] + rank[g:g + 1, :])
        dest = lax.broadcasted_iota(jnp.int32, (MOE_TBP, 1), 0).astype(F32)
        p_scr[...] = (dest == pos).astype(BF16)
        for r0 in range(0, MOE_TBP, V7X_MXU_DIM):
            rows = pl.ds(r0, V7X_MXU_DIM)
            sorted_rows = jnp.dot(p_scr[rows, :], h_ref[...], preferred_element_type=F32)
            xs_scr[rows, :] = sorted_rows[:, :D_MODEL].astype(BF16)
            aux = sorted_rows[:, D_MODEL:]
            cs_scr[rows, :] = (aux + pltpu.roll(aux, AUX_LANES - AUX_MID, axis=1)
                               + pltpu.roll(aux, AUX_LANES - AUX_LO, axis=1))
        first_free = pl.multiple_of(_to_int(used), MOE_RT)

        def clear(k, carry):
            ys_scr[pl.ds(pl.multiple_of(first_free + k * MOE_RT, MOE_RT), MOE_RT), :] = jnp.zeros(
                (MOE_RT, D_MODEL), F32)
            return carry

        lax.fori_loop(0, (MOE_TBP - first_free) // MOE_RT, clear, 0)

    steps_per_group = EXPERTS_PER_GROUP // MOE_EPS
    group = step // steps_per_group
    first_of_group = step % steps_per_group == 0
    start_v, size_v = starts[0], sizes[0]
    for g in range(1, N_GROUPS):
        start_v = jnp.where(group == g, starts[g], start_v)
        size_v = jnp.where(group == g, sizes[g], size_v)
    seg_start = _to_int(start_v)
    n_tiles = _to_int(size_v) // MOE_RT

    def run_expert(r0, n_rows):
        rows = pl.ds(pl.multiple_of(r0, MOE_RT), n_rows)
        x = xs_scr[rows, :]
        cs = cs_scr[rows, :]
        y = None
        for k in range(MOE_EPS):
            gate = jnp.dot(x, wg_ref[0, k], preferred_element_type=F32)
            up = jnp.dot(x, wu_ref[0, k], preferred_element_type=F32)
            act = (gate * _sigmoid(gate)) * up
            yk = jnp.dot(act.astype(BF16), wd_ref[0, k], preferred_element_type=F32)
            yk = jnp.sum(jnp.where(lane == step * MOE_EPS + k, cs, 0.0), axis=-1, keepdims=True) * yk
            y = yk if y is None else y + yk

        @pl.when(first_of_group)
        def _():
            ys_scr[rows, :] = y

        @pl.when(jnp.logical_not(first_of_group))
        def _():
            ys_scr[rows, :] += y

    def pair(k, carry):
        run_expert(seg_start + k * (2 * MOE_RT), 2 * MOE_RT)
        return carry

    lax.fori_loop(0, n_tiles // 2, pair, 0)

    @pl.when(n_tiles % 2 == 1)
    def _():
        run_expert(seg_start + (n_tiles - 1) * MOE_RT, MOE_RT)

    @pl.when(step == N_EXPERTS // MOE_EPS - 1)
    def _():
        y = lax.dot_general(p_scr[...], ys_scr[...].astype(BF16), (((0,), (0,)), ((), ())),
                            preferred_element_type=F32)
        x2 = x_ref[...] + mod_ref[0, 5:6, :] * y
        o_ref[...] = _rms(x2, gf_ref[...]) if final_norm else x2


def moe_pallas(h2x, gid, x1, mod, g_final, w_gate, w_up, w_down, layer, seq_len, per_seq, final_norm):
    n = x1.shape[0]
    row = lambda i, e: (i, 0)
    expert = lambda i, e: (layer, e, 0, 0)
    return pl.pallas_call(
        functools.partial(_moe_kernel, final_norm=final_norm),
        grid=(n // MOE_TB, N_EXPERTS // MOE_EPS),
        in_specs=[pl.BlockSpec((MOE_TB, D_MOE_IN), row), pl.BlockSpec((V7X_SUBLANES, MOE_TB), lambda i, e: (0, i)),
                  pl.BlockSpec((MOE_TB, D_MODEL), row), _mod_spec(MOE_TB, seq_len, per_seq),
                  pl.BlockSpec((1, D_MODEL), lambda i, e: (0, 0)),
                  pl.BlockSpec((1, MOE_EPS, D_MODEL, D_EXPERT), expert),
                  pl.BlockSpec((1, MOE_EPS, D_MODEL, D_EXPERT), expert),
                  pl.BlockSpec((1, MOE_EPS, D_EXPERT, D_MODEL), expert)],
        out_specs=pl.BlockSpec((MOE_TB, D_MODEL), row),
        out_shape=jax.ShapeDtypeStruct((n, D_MODEL), F32),
        scratch_shapes=[pltpu.VMEM((MOE_TBP, MOE_TB), BF16), pltpu.VMEM((MOE_TBP, D_MODEL), BF16),
                        pltpu.VMEM((MOE_TBP, AUX_LANES), F32), pltpu.VMEM((MOE_TBP, D_MODEL), F32)],
        compiler_params=_params("arbitrary", "arbitrary"),
        name="moe",
    )(h2x, gid, x1, mod, g_final, w_gate, w_up, w_down)


TM_PROJ = 1024
TM_MERGE = 512


def _layer(x, mod, seq_len, per_seq, lw, h0, ctx_kv, tables, layer):
    n = x.shape[0]
    b = n // seq_len
    emit_kv = ctx_kv is None
    outs = inproj_pallas(x, mod, lw['norm_g'][0:1], lw['w_in'], layer, seq_len, per_seq, TM_PROJ, emit_kv)
    p = outs[0]
    p3 = p.reshape(b, seq_len, D_IN)
    y_rnn, h_fin = lru_pallas(p3, h0, lw['conv_w'], lw['conv_b'], lw['wg'], lw['gb'], lw['lam'])
    if emit_kv:
        y_att = ctxatt_pallas(p3)
    else:
        y_att = natten_pallas(p3, ctx_kv[0], ctx_kv[1], tables, layer)
    x1, h2x, gid = merge_pallas(y_rnn.reshape(n, D_RNN), y_att.reshape(n, D_ATT), p, x, mod, lw['norm_g'][1:2],
                                lw['w_br_rnn'], lw['w_br_att'], lw['w_out'], lw['w_router_t'], lw['router_bias'],
                                layer, seq_len, per_seq, TM_MERGE)
    x2 = moe_pallas(h2x, gid, x1, mod, lw['g_final'], lw['w_exp_gate'], lw['w_exp_up'], lw['w_exp_down'], layer,
                    seq_len, per_seq, layer == DEPTH - 1)
    kv = (outs[1], outs[2]) if emit_kv else None
    return x2, kv, h_fin


def kernel(x_prompt, x_sample, cache_k, cache_v, state_lru, c, c_ctx, w_ada, b_ada, norm_g, w_in, conv_w,
           conv_b, lru_wa, lru_ba, lru_wx, lru_bx, lru_lam, rpb, w_br_rnn, w_br_att, w_out, w_router,
           router_bias, w_exp_gate, w_exp_up, w_exp_down, final_norm_g):
    cvecs = jnp.concatenate([c, c_ctx[None, :], jnp.zeros((V7X_SUBLANES - DEC_BATCH - 1, D_MODEL), F32)], axis=0)
    mods = adaln_pallas(cvecs, w_ada, b_ada).reshape(DEPTH, V7X_SUBLANES, N_MOD, D_MODEL)
    tables = _natten_tables(rpb)
    w_router_t = w_router.T
    xp = x_prompt.reshape(BATCH * SEQ, D_MODEL)
    xs = x_sample.reshape(DEC_BATCH * DEC_SEQ, D_MODEL)
    zeros_h0 = jnp.zeros((BATCH, 2, D_RNN), F32)
    w_exp = [w.astype(BF16) for w in (w_exp_gate, w_exp_up, w_exp_down)]
    w_proj = [w.astype(BF16) for w in (w_in, w_br_rnn, w_br_att, w_out)]
    ks, vs, hs = [], [], []
    for l in range(DEPTH):
        lw = dict(
            norm_g=norm_g[l], w_in=w_proj[0], conv_w=conv_w[l], conv_b=conv_b[l][None, :],
            wg=_lru_gate_weights(lru_wa[l], lru_wx[l]),
            gb=jnp.stack([lru_ba[l, 0], lru_bx[l, 0], lru_ba[l, 1], lru_bx[l, 1]], axis=0), lam=lru_lam[l],
            w_br_rnn=w_proj[1], w_br_att=w_proj[2], w_out=w_proj[3],
            w_router_t=w_router_t, router_bias=router_bias, g_final=final_norm_g[None, :],
            w_exp_gate=w_exp[0], w_exp_up=w_exp[1], w_exp_down=w_exp[2])
        xp, kv, h_l = _layer(xp, mods[l, DEC_BATCH:DEC_BATCH + 1], SEQ, False, lw, zeros_h0, None, None, l)
        ks.append(kv[0].reshape(BATCH, SEQ, N_HEADS, HEAD_DIM))
        vs.append(kv[1].reshape(BATCH, SEQ, N_HEADS, HEAD_DIM))
        hs.append(h_l)
        ctx_kv = (cache_k[:, l].reshape(DEC_BATCH, PAST_LEN, D_ATT).astype(BF16),
                  cache_v[:, l].reshape(DEC_BATCH, PAST_LEN, D_ATT).astype(BF16))
        xs, _, _ = _layer(xs, mods[l, :DEC_BATCH], DEC_SEQ, True, lw, state_lru[:, l], ctx_kv, tables, l)
    y_prompt = xp.reshape(BATCH, SEQ, D_MODEL)
    y_sample = xs.reshape(DEC_BATCH, DEC_SEQ, D_MODEL)
    return (y_prompt, y_sample, jnp.stack(ks, axis=1), jnp.stack(vs, axis=1), jnp.stack(hs, axis=1))
```

```python
import functools

import jax
import jax.numpy as jnp
import numpy as np
from jax import lax
from jax.experimental import pallas as pl
from jax.experimental.pallas import tpu as pltpu

D_MODEL = 1024
BATCH = 16
SEQ = 256
DEPTH = 2
DEC_BATCH = 4
DEC_SEQ = 4096
PAST_LEN = 256

GRID_W = 64
D_RNN = 1024
N_LRU_BLOCKS = 16
LRU_BLOCK = D_RNN // N_LRU_BLOCKS
CONV_W = 4
LRU_C = 8.0
N_HEADS = 16
HEAD_DIM = 64
D_ATT = N_HEADS * HEAD_DIM
WIN_H = 8
WIN_W = 16
N_EXPERTS = 16
N_GROUPS = 4
EXPERTS_PER_GROUP = N_EXPERTS // N_GROUPS
D_EXPERT = 512
N_MOD = 6
D_IN = 2 * D_RNN + 3 * D_ATT + 2 * D_MODEL
EPS = 1e-6
NEG_INF = -1e30

BF16 = jnp.bfloat16
F32 = jnp.float32

V7X_LANES = 128
V7X_SUBLANES = 8
V7X_MXU_DIM = 256
V7X_VMEM_BYTES = 64 * 1024 * 1024
VMEM_LIMIT = V7X_VMEM_BYTES - 8 * 1024 * 1024

CHUNK = D_MODEL
N_CHUNKS = D_IN // CHUNK
C_XRNN, C_GRNN, C_Q, C_K, C_V, C_GATE_R, C_GATE_A = range(N_CHUNKS)

NT_DIMS = (((1,), (1,)), ((), ()))


def _params(*sem):
    return pltpu.CompilerParams(dimension_semantics=sem, vmem_limit_bytes=VMEM_LIMIT)


def _adaln_kernel(c_ref, w_ref, b_ref, o_ref):
    cv = c_ref[...]
    s = cv * jax.nn.sigmoid(cv)
    o_ref[0] = jnp.dot(s.astype(BF16), w_ref[0].astype(BF16), preferred_element_type=F32) + b_ref[0]


def adaln_pallas(cvecs, w_ada, b_ada):
    r = cvecs.shape[0]
    return pl.pallas_call(
        _adaln_kernel,
        grid=(DEPTH, N_MOD),
        in_specs=[pl.BlockSpec((r, D_MODEL), lambda l, j: (0, 0)),
                  pl.BlockSpec((1, D_MODEL, D_MODEL), lambda l, j: (l, 0, j)),
                  pl.BlockSpec((1, 1, D_MODEL), lambda l, j: (l, 0, j))],
        out_specs=pl.BlockSpec((1, r, D_MODEL), lambda l, j: (l, 0, j)),
        out_shape=jax.ShapeDtypeStruct((DEPTH, r, N_MOD * D_MODEL), F32),
        compiler_params=_params("arbitrary", "arbitrary"),
        name="adaln",
    )(cvecs, w_ada, b_ada.reshape(DEPTH, 1, N_MOD * D_MODEL))


def _mod_spec(tm, seq_len, per_seq):
    if per_seq:
        return pl.BlockSpec((1, N_MOD, D_MODEL), lambda i, *_: (i * tm // seq_len, 0, 0))
    return pl.BlockSpec((1, N_MOD, D_MODEL), lambda i, *_: (0, 0, 0))


def _rms(x, g):
    return x * lax.rsqrt(jnp.mean(x * x, axis=-1, keepdims=True) + EPS) * g


def _inproj_kernel(x_ref, mod_ref, g_ref, w_ref, *refs, emit_kv):
    if emit_kv:
        p_ref, k32_ref, v32_ref, h_scr = refs
    else:
        p_ref, h_scr = refs
    j = pl.program_id(1)

    @pl.when(j == 0)
    def _():
        y = _rms(x_ref[...], g_ref[...])
        h_scr[...] = (y * (1.0 + mod_ref[0, 1:2, :]) + mod_ref[0, 0:1, :]).astype(BF16)

    w = w_ref[0, :, pl.ds(pl.multiple_of(j * CHUNK, CHUNK), CHUNK)]
    acc = jnp.dot(h_scr[...], w, preferred_element_type=F32)
    p_ref[...] = (acc * jnp.where(j == C_Q, HEAD_DIM ** -0.5, 1.0)).astype(BF16)
    if emit_kv:
        @pl.when(j == C_K)
        def _():
            k32_ref[...] = acc

        @pl.when(j == C_V)
        def _():
            v32_ref[...] = acc


def inproj_pallas(x, mod, g, w_in, layer, seq_len, per_seq, tm, emit_kv):
    n = x.shape[0]
    row = lambda i, j: (i, 0)
    out_shape = [jax.ShapeDtypeStruct((n, D_IN), BF16)]
    out_specs = [pl.BlockSpec((tm, CHUNK), lambda i, j: (i, j))]
    if emit_kv:
        out_shape += [jax.ShapeDtypeStruct((n, D_ATT), F32)] * 2
        out_specs += [pl.BlockSpec((tm, D_ATT), row)] * 2
    return pl.pallas_call(
        functools.partial(_inproj_kernel, emit_kv=emit_kv),
        grid=(n // tm, N_CHUNKS),
        in_specs=[pl.BlockSpec((tm, D_MODEL), row), _mod_spec(tm, seq_len, per_seq),
                  pl.BlockSpec((1, D_MODEL), lambda i, j: (0, 0)),
                  pl.BlockSpec((1, D_MODEL, D_IN), lambda i, j: (layer, 0, 0), pipeline_mode=pl.Buffered(1))],
        out_specs=out_specs,
        out_shape=out_shape,
        scratch_shapes=[pltpu.VMEM((tm, D_MODEL), BF16)],
        compiler_params=_params("arbitrary", "arbitrary"),
        name="inproj",
    )(x, mod, g, w_in)


LRU_CB = 512
LRU_TC = 256
LRU_SUB = V7X_MXU_DIM
LRU_HALO = 16


def _scan_groups(a, b, carry, reverse):
    tc, c = a.shape
    ng = tc // V7X_SUBLANES
    a3 = a.reshape(ng, V7X_SUBLANES, c)
    b3 = b.reshape(ng, V7X_SUBLANES, c)
    sub = lax.broadcasted_iota(jnp.int32, (1, V7X_SUBLANES, 1), 1)
    s = 1
    while s < V7X_SUBLANES:
        shift = V7X_SUBLANES - s if reverse else s
        ok = (sub < V7X_SUBLANES - s) if reverse else (sub >= s)
        a_sh = pltpu.roll(a3, shift, axis=1)
        b_sh = pltpu.roll(b3, shift, axis=1)
        b3 = jnp.where(ok, a3 * b_sh + b3, b3)
        a3 = jnp.where(ok, a3 * a_sh, a3)
        s *= 2
    hs = [None] * ng
    order = range(ng - 1, -1, -1) if reverse else range(ng)
    edge = 0 if reverse else V7X_SUBLANES - 1
    for gi in order:
        h = a3[gi] * carry + b3[gi]
        carry = h[edge:edge + 1, :]
        hs[gi] = h
    return jnp.concatenate(hs, axis=0), carry


def _sigmoid(x):
    return 0.5 * jnp.tanh(0.5 * x) + 0.5


def _lru_kernel(x_ref, gate_ref, h0_ref, cw_ref, cb_ref, wg_ref, gb_ref, lam_ref, y_ref, fin_ref, hf_scr, u_scr):
    t_len = x_ref.shape[1]
    cb = x_ref.shape[2]
    n_chunks = t_len // LRU_TC

    def conv_chunk(c):
        t0 = pl.multiple_of(c * LRU_TC, LRU_TC)
        cur = x_ref[0, pl.ds(t0, LRU_TC), :].astype(F32)
        lo = pl.multiple_of(jnp.maximum(t0 - LRU_HALO, 0), LRU_HALO)
        hi = pl.multiple_of(jnp.minimum(t0 + LRU_TC, t_len - LRU_HALO), LRU_HALO)
        prev = jnp.where(c > 0, x_ref[0, pl.ds(lo, LRU_HALO), :].astype(F32), 0.0)
        nxt = jnp.where(c < n_chunks - 1, x_ref[0, pl.ds(hi, LRU_HALO), :].astype(F32), 0.0)
        ext = jnp.concatenate([prev, cur, nxt], axis=0)
        n_ext = LRU_TC + 2 * LRU_HALO
        u = cb_ref[...] + jnp.zeros((LRU_TC, cb), F32)
        for j in range(CONV_W):
            off = j - CONV_W // 2
            sh = ext if off == 0 else pltpu.roll(ext, (-off) % n_ext, axis=0)
            u = u + sh[LRU_HALO:LRU_HALO + LRU_TC, :] * cw_ref[j:j + 1, :]
        return t0, u

    def gates(u, d):
        ub = u.astype(BF16)
        pre = [jnp.dot(ub[:, LRU_SUB * s:LRU_SUB * (s + 1)], wg_ref[d, s], preferred_element_type=F32)
               for s in range(cb // LRU_SUB)]
        pre_a = jnp.concatenate([p[:, :LRU_SUB] for p in pre], axis=1)
        pre_x = jnp.concatenate([p[:, LRU_SUB:] for p in pre], axis=1)
        r = _sigmoid(pre_a + gb_ref[2 * d:2 * d + 1, :])
        i = _sigmoid(pre_x + gb_ref[2 * d + 1:2 * d + 2, :])
        log_a = (-LRU_C * jax.nn.softplus(-lam_ref[d:d + 1, :])) * r
        a = jnp.exp(log_a)
        th = jnp.tanh(log_a)
        inp = (jnp.sqrt(-2.0 * th) * lax.rsqrt(1.0 - th)) * (i * u)
        return a, inp

    def fwd(c, carry):
        t0, u = conv_chunk(c)
        u_scr[pl.ds(t0, LRU_TC), :] = u
        a, inp = gates(u, 0)
        h, carry = _scan_groups(a, inp, carry, reverse=False)
        hf_scr[pl.ds(t0, LRU_TC), :] = h
        return carry

    fin_f = lax.fori_loop(0, n_chunks, fwd, h0_ref[0, 0:1, :])

    def bwd(k, carry):
        t0 = pl.multiple_of((n_chunks - 1 - k) * LRU_TC, LRU_TC)
        a, inp = gates(u_scr[pl.ds(t0, LRU_TC), :], 1)
        h, carry = _scan_groups(a, inp, carry, reverse=True)
        g = gate_ref[0, pl.ds(t0, LRU_TC), :].astype(F32)
        y_ref[0, pl.ds(t0, LRU_TC), :] = ((hf_scr[pl.ds(t0, LRU_TC), :] + h) * jax.nn.gelu(g)).astype(y_ref.dtype)
        return carry

    fin_b = lax.fori_loop(0, n_chunks, bwd, h0_ref[0, 1:2, :])
    fin_ref[0, 0:1, :] = fin_f
    fin_ref[0, 1:2, :] = fin_b


def _lru_gate_weights(lru_wa, lru_wx):
    per = LRU_SUB // LRU_BLOCK
    eye = jnp.eye(per, dtype=F32)

    def dense(w):
        w = w.reshape(2, D_RNN // LRU_SUB, per, LRU_BLOCK, LRU_BLOCK)
        full = w[:, :, :, :, None, :] * eye[None, None, :, None, :, None]
        return full.reshape(2, D_RNN // LRU_SUB, LRU_SUB, LRU_SUB)

    return jnp.concatenate([dense(lru_wa), dense(lru_wx)], axis=-1).astype(BF16)


def lru_pallas(p, h0, conv_w, conv_b, wg, gb, lam):
    b, t, _ = p.shape
    n_cb = D_RNN // LRU_CB
    return pl.pallas_call(
        _lru_kernel,
        grid=(b, n_cb),
        in_specs=[pl.BlockSpec((1, t, LRU_CB), lambda bi, ci: (bi, 0, C_XRNN * n_cb + ci)),
                  pl.BlockSpec((1, t, LRU_CB), lambda bi, ci: (bi, 0, C_GRNN * n_cb + ci)),
                  pl.BlockSpec((1, 2, LRU_CB), lambda bi, ci: (bi, 0, ci)),
                  pl.BlockSpec((CONV_W, LRU_CB), lambda bi, ci: (0, ci)),
                  pl.BlockSpec((1, LRU_CB), lambda bi, ci: (0, ci)),
                  pl.BlockSpec((2, LRU_CB // LRU_SUB, LRU_SUB, 2 * LRU_SUB), lambda bi, ci: (0, ci, 0, 0)),
                  pl.BlockSpec((4, LRU_CB), lambda bi, ci: (0, ci)),
                  pl.BlockSpec((2, LRU_CB), lambda bi, ci: (0, ci))],
        out_specs=[pl.BlockSpec((1, t, LRU_CB), lambda bi, ci: (bi, 0, ci)),
                   pl.BlockSpec((1, 2, LRU_CB), lambda bi, ci: (bi, 0, ci))],
        out_shape=[jax.ShapeDtypeStruct((b, t, D_RNN), BF16), jax.ShapeDtypeStruct((b, 2, D_RNN), F32)],
        scratch_shapes=[pltpu.VMEM((t, LRU_CB), F32), pltpu.VMEM((t, LRU_CB), F32)],
        compiler_params=_params("arbitrary", "arbitrary"),
        name="lru",
    )(p, p, h0, conv_w, conv_b, wg, gb, lam)


Q_ROWS = 4
Q_TILE = Q_ROWS * GRID_W
KEY_ROWS = 12
KEY_TILE = KEY_ROWS * GRID_W
N_Q_TILES = DEC_SEQ // Q_TILE
ATT_LANES = 256
HEADS_PER_STEP = ATT_LANES // HEAD_DIM
PAIR = 2 * HEAD_DIM


def _window_block(t):
    return jnp.clip(t - 1, 0, N_Q_TILES - KEY_ROWS // Q_ROWS)


N_DR_PAIRS = 2 * WIN_H


def _natten_tables(rpb):
    c = np.arange(GRID_W)
    q_start = np.clip(c - WIN_W // 2, 0, GRID_W - WIN_W)
    col_valid = (c[None, :] >= q_start[:, None]) & (c[None, :] < q_start[:, None] + WIN_W)
    dc = c[None, :] - c[:, None] + WIN_W - 1
    sel_c = ((dc[None] == np.arange(2 * WIN_W - 1)[:, None, None]) & col_valid[None]).astype(np.float32)
    t = jnp.einsum('lhrd,dck->lhrck', rpb.astype(F32), jnp.asarray(sel_c), precision=lax.Precision.HIGHEST)
    t = jnp.where(col_valid[None, None, None], t, NEG_INF)
    t = jnp.pad(t, ((0, 0), (0, 0), (1, 1), (0, 0), (0, 0)), constant_values=NEG_INF)
    return jnp.concatenate([t[:, :, :-1], t[:, :, 1:]], axis=-1)


def _attend(q2, keys, vals, biases):
    lane_head = lax.broadcasted_iota(jnp.int32, (1, PAIR), 1) // HEAD_DIM
    acc = jnp.zeros((q2.shape[0], PAIR), F32)
    for j in range(2):
        mine = lane_head == j
        qh = jnp.where(mine, q2, jnp.zeros_like(q2))
        s = []
        for kb, bb in zip(keys, biases[j]):
            sd = lax.dot_general(qh, kb, NT_DIMS, preferred_element_type=F32)
            s.append(sd if bb is None else sd + bb)
        m = s[0]
        for sd in s[1:]:
            m = jnp.maximum(m, sd)
        m = m.max(axis=-1, keepdims=True)
        o = None
        for sd, vb in zip(s, vals):
            pv = jnp.dot(jnp.exp(sd - m).astype(BF16), jnp.where(mine, vb, jnp.ones_like(vb)),
                         preferred_element_type=F32)
            o = pv if o is None else o + pv
        acc = acc + jnp.where(mine, o / pltpu.roll(o, HEAD_DIM, axis=1), 0.0)
    return acc


def _natten_kernel(q_ref, k_ref, v_ref, kc_ref, vc_ref, tt_ref, o_ref):
    ti = pl.program_id(1)
    rows = DEC_SEQ // GRID_W
    wstart = _window_block(ti) * Q_ROWS
    key_lane = lax.broadcasted_iota(jnp.int32, (1, Q_TILE), 1)
    entry, valid = {}, {}
    for a in range(Q_ROWS):
        r = ti * Q_ROWS + a
        first_key = (jnp.clip(r - WIN_H // 2, 0, rows - WIN_H) - wstart) * GRID_W
        for d in range(KEY_ROWS // Q_ROWS):
            lane = key_lane + d * Q_TILE
            valid[a, d] = (lane >= first_key) & (lane < first_key + WIN_H * GRID_W)
        for i in range(0, KEY_ROWS, 2):
            entry[a, i] = jnp.clip(wstart - r + WIN_H + i, 0, N_DR_PAIRS - 1)

    def bias_block(h, d):
        return jnp.concatenate(
            [jnp.where(valid[a, d],
                       jnp.concatenate([tt_ref[0, h, entry[a, i]] for i in range(Q_ROWS * d, Q_ROWS * (d + 1), 2)],
                                       axis=1), NEG_INF)
             for a in range(Q_ROWS)], axis=0)

    n_blocks = KEY_ROWS // Q_ROWS
    outs = []
    for hp in range(ATT_LANES // PAIR):
        sl = slice(PAIR * hp, PAIR * (hp + 1))
        biases = [[bias_block(2 * hp + j, d) for d in range(n_blocks)] + [None] for j in range(2)]
        blocks = lambda ref, ctx_ref: ([ref[0, Q_TILE * d:Q_TILE * (d + 1), sl] for d in range(n_blocks)]
                                       + [ctx_ref[0, :, sl]])
        outs.append(_attend(q_ref[0, :, sl], blocks(k_ref, kc_ref), blocks(v_ref, vc_ref), biases))
    o_ref[0] = jnp.concatenate(outs, axis=1).astype(o_ref.dtype)


def natten_pallas(p, k_ctx, v_ctx, tables, layer):
    b, t, _ = p.shape
    n_lb = D_ATT // ATT_LANES
    q_spec = pl.BlockSpec((1, Q_TILE, ATT_LANES), lambda lb, ti, bi: (bi, ti, C_Q * n_lb + lb))
    win = lambda ch: pl.BlockSpec((pl.Element(1), pl.Element(KEY_TILE), pl.Element(ATT_LANES)),
                                  lambda lb, ti, bi: (bi, _window_block(ti) * Q_TILE, (ch * n_lb + lb) * ATT_LANES))
    ctx = pl.BlockSpec((1, PAST_LEN, ATT_LANES), lambda lb, ti, bi: (bi, 0, lb))
    return pl.pallas_call(
        _natten_kernel,
        grid=(n_lb, N_Q_TILES, b),
        in_specs=[q_spec, win(C_K), win(C_V), ctx, ctx,
                  pl.BlockSpec((1, HEADS_PER_STEP, N_DR_PAIRS, GRID_W, 2 * GRID_W),
                               lambda lb, ti, bi: (layer, lb, 0, 0, 0))],
        out_specs=pl.BlockSpec((1, Q_TILE, ATT_LANES), lambda lb, ti, bi: (bi, ti, lb)),
        out_shape=jax.ShapeDtypeStruct((b, t, D_ATT), BF16),
        compiler_params=_params("arbitrary", "arbitrary", "arbitrary"),
        name="natten",
    )(p, p, p, k_ctx, v_ctx, tables)


def _ctxatt_kernel(q_ref, k_ref, v_ref, o_ref):
    outs = []
    for hp in range(ATT_LANES // PAIR):
        sl = slice(PAIR * hp, PAIR * (hp + 1))
        outs.append(_attend(q_ref[0, :, sl], [k_ref[0, :, sl]], [v_ref[0, :, sl]], [[None], [None]]))
    o_ref[0] = jnp.concatenate(outs, axis=1).astype(o_ref.dtype)


def ctxatt_pallas(p):
    b, t, _ = p.shape
    n_lb = D_ATT // ATT_LANES
    blk = lambda ch: pl.BlockSpec((1, t, ATT_LANES), lambda lb, bi: (bi, 0, ch * n_lb + lb))
    return pl.pallas_call(
        _ctxatt_kernel,
        grid=(n_lb, b),
        in_specs=[blk(C_Q), blk(C_K), blk(C_V)],
        out_specs=pl.BlockSpec((1, t, ATT_LANES), lambda lb, bi: (bi, 0, lb)),
        out_shape=jax.ShapeDtypeStruct((b, t, D_ATT), BF16),
        compiler_params=_params("arbitrary", "arbitrary"),
        name="ctxatt",
    )(p, p, p)


def _split_bf16(x):
    hi = x.astype(BF16)
    return hi, (x - hi.astype(F32)).astype(BF16)


def _route(logits_t, rb_ref):
    score = [jax.nn.sigmoid(logits_t[e:e + 1, :]) for e in range(N_EXPERTS)]
    sel = [score[e] + rb_ref[e] for e in range(N_EXPERTS)]
    best_g = None
    for g in range(N_GROUPS):
        v = sel[EXPERTS_PER_GROUP * g:EXPERTS_PER_GROUP * (g + 1)]
        top2 = None
        for i in range(EXPERTS_PER_GROUP):
            for j in range(i + 1, EXPERTS_PER_GROUP):
                pair = v[i] + v[j]
                top2 = pair if top2 is None else jnp.maximum(top2, pair)
        if best_g is None:
            best_g, best_v = jnp.zeros_like(top2, dtype=jnp.int32), top2
        else:
            upd = top2 > best_v
            best_g = jnp.where(upd, g, best_g)
            best_v = jnp.where(upd, top2, best_v)

    def in_best(vals, j):
        out = vals[j]
        for g in range(1, N_GROUPS):
            out = jnp.where(best_g == g, vals[EXPERTS_PER_GROUP * g + j], out)
        return out

    v = [in_best(sel, j) for j in range(EXPERTS_PER_GROUP)]
    sc = [in_best(score, j) for j in range(EXPERTS_PER_GROUP)]

    def first_argmax(vals):
        idx, top = jnp.zeros_like(best_g), vals[0]
        for j in range(1, EXPERTS_PER_GROUP):
            upd = vals[j] > top
            idx = jnp.where(upd, j, idx)
            top = jnp.where(upd, vals[j], top)
        return idx

    i1 = first_argmax(v)
    i2 = first_argmax([jnp.where(i1 == j, -jnp.inf, v[j]) for j in range(EXPERTS_PER_GROUP)])
    pick = lambda idx: sum(jnp.where(idx == j, sc[j], 0.0) for j in range(EXPERTS_PER_GROUP))
    w1, w2 = pick(i1), pick(i2)
    den = w1 + w2
    c1, c2 = w1 / den, w2 / den
    rows = []
    for e in range(N_EXPERTS):
        g, j = divmod(e, EXPERTS_PER_GROUP)
        rows.append(jnp.where(best_g == g, jnp.where(i1 == j, c1, 0.0) + jnp.where(i2 == j, c2, 0.0), 0.0))
    return jnp.concatenate(rows, axis=0), best_g


AUX_LANES = V7X_LANES
AUX_MID = N_EXPERTS
AUX_LO = 2 * N_EXPERTS
AUX_GROUP = 3 * N_EXPERTS
D_MOE_IN = D_MODEL + AUX_LANES
MERGE_CHAINS = 2


def _merge_kernel(rb_ref, yr_ref, ya_ref, gr_ref, ga_ref, x_ref, mod_ref, g_ref, wr_ref, wa_ref, wo_ref, wrt_ref,
                  x1_ref, h2_ref, gid_ref):
    w_hi, w_lo = _split_bf16(wrt_ref[...])
    dot_nt = lambda a, b: lax.dot_general(a, b, NT_DIMS, preferred_element_type=F32)
    tm = x_ref.shape[0]
    half = tm // MERGE_CHAINS
    for ci in range(MERGE_CHAINS):
        rows = slice(ci * half, (ci + 1) * half)
        m = (_sigmoid(gr_ref[rows, :].astype(F32)) * jnp.dot(yr_ref[rows, :], wr_ref[0], preferred_element_type=F32)
             + _sigmoid(ga_ref[rows, :].astype(F32)) * jnp.dot(ya_ref[rows, :], wa_ref[0],
                                                              preferred_element_type=F32))
        o = jnp.dot(m.astype(BF16), wo_ref[0], preferred_element_type=F32)
        x1 = x_ref[rows, :] + mod_ref[0, 2:3, :] * o
        x1_ref[rows, :] = x1
        h2 = _rms(x1, g_ref[...]) * (1.0 + mod_ref[0, 4:5, :]) + mod_ref[0, 3:4, :]
        h_hi, h_lo = _split_bf16(h2)
        logits_t = dot_nt(w_hi, h_hi) + (dot_nt(w_hi, h_lo) + dot_nt(w_lo, h_hi))
        comb_t, group = _route(logits_t, rb_ref)
        group = group.astype(F32)
        gid_ref[:, rows] = jnp.concatenate([group, jnp.zeros((V7X_SUBLANES - 1, half), F32)], axis=0)
        padded = jnp.concatenate([comb_t, jnp.zeros((AUX_GROUP - N_EXPERTS, half), F32), group,
                                  jnp.zeros((AUX_LANES - AUX_GROUP - 1, half), F32)], axis=0)
        c = padded.T
        c_hi = c.astype(BF16).astype(F32)
        c_mid = (c - c_hi).astype(BF16).astype(F32)
        c_lo = c - c_hi - c_mid
        aux = c_hi + pltpu.roll(c_mid, AUX_MID, axis=1) + pltpu.roll(c_lo, AUX_LO, axis=1)
        h2_ref[rows, :] = jnp.concatenate([h2.astype(BF16), aux.astype(BF16)], axis=1)


def merge_pallas(y_rnn, y_att, p, x, mod, g, w_br_rnn, w_br_att, w_out, w_router_t, router_bias, layer, seq_len,
                 per_seq, tm):
    n = x.shape[0]
    row = lambda i: (i, 0)
    full = lambda shape: pl.BlockSpec(shape, lambda i: (0, 0))
    weight = pl.BlockSpec((1, D_MODEL, D_MODEL), lambda i: (layer, 0, 0))
    return pl.pallas_call(
        _merge_kernel,
        grid=(n // tm,),
        in_specs=[pl.BlockSpec(memory_space=pltpu.SMEM),
                  pl.BlockSpec((tm, D_RNN), row), pl.BlockSpec((tm, D_ATT), row),
                  pl.BlockSpec((tm, CHUNK), lambda i: (i, C_GATE_R)), pl.BlockSpec((tm, CHUNK), lambda i: (i, C_GATE_A)),
                  pl.BlockSpec((tm, D_MODEL), row), _mod_spec(tm, seq_len, per_seq), full((1, D_MODEL)),
                  weight, weight, weight, full((N_EXPERTS, D_MODEL))],
        out_specs=[pl.BlockSpec((tm, D_MODEL), row), pl.BlockSpec((tm, D_MOE_IN), row),
                   pl.BlockSpec((V7X_SUBLANES, tm), lambda i: (0, i))],
        out_shape=[jax.ShapeDtypeStruct((n, D_MODEL), F32), jax.ShapeDtypeStruct((n, D_MOE_IN), BF16),
                   jax.ShapeDtypeStruct((V7X_SUBLANES, n), F32)],
        compiler_params=_params("arbitrary"),
        name="merge",
    )(router_bias, y_rnn, y_att, p, p, x, mod, g, w_br_rnn, w_br_att, w_out, w_router_t)


MOE_TB = 1024
MOE_RT = 128
MOE_TBP = MOE_TB + N_GROUPS * MOE_RT
MOE_EPS = 2


def _group_segments(gid_row):
    sub = lax.broadcasted_iota(jnp.int32, (V7X_SUBLANES, 1), 0).astype(F32)
    onehot = (gid_row == sub).astype(F32)
    cnt = jnp.sum(onehot, axis=1, keepdims=True)
    padded = jnp.floor((cnt + (MOE_RT - 1)) * (1.0 / MOE_RT)) * MOE_RT
    starts, run = [], jnp.zeros((1, 1), F32)
    for g in range(N_GROUPS):
        starts.append(run)
        run = run + padded[g:g + 1, :]
    return onehot, starts, [padded[g:g + 1, :] for g in range(N_GROUPS)], run


def _to_int(v):
    return v[0, 0].astype(jnp.int32)


def _moe_kernel(h_ref, gid_ref, x_ref, mod_ref, gf_ref, wg_ref, wu_ref, wd_ref, o_ref,
                p_scr, xs_scr, cs_scr, ys_scr, *, final_norm):
    step = pl.program_id(1)
    onehot, starts, sizes, used = _group_segments(gid_ref[0:1, :])
    lane = lax.broadcasted_iota(jnp.int32, (1, AUX_LANES), 1)

    @pl.when(step == 0)
    def _():
        t_row = lax.broadcasted_iota(jnp.int32, (MOE_TB, MOE_TB), 0)
        t_col = lax.broadcasted_iota(jnp.int32, (MOE_TB, MOE_TB), 1)
        earlier = (t_row < t_col).astype(BF16)
        rank = jnp.dot(onehot.astype(BF16), earlier, preferred_element_type=F32)
        pos = jnp.zeros((1, MOE_TB), F32)
        for g in range(N_GROUPS):
            pos = pos + onehot[g:g + 1, :] * (starts[g] + rank[g:g + 1, :])
        dest = lax.broadcasted_iota(jnp.int32, (MOE_TBP, 1), 0).astype(F32)
        p_scr[...] = (dest == pos).astype(BF16)
        for r0 in range(0, MOE_TBP, V7X_MXU_DIM):
            rows = pl.ds(r0, V7X_MXU_DIM)
            sorted_rows = jnp.dot(p_scr[rows, :], h_ref[...], preferred_element_type=F32)
            xs_scr[rows, :] = sorted_rows[:, :D_MODEL].astype(BF16)
            aux = sorted_rows[:, D_MODEL:]
            cs_scr[rows, :] = (aux + pltpu.roll(aux, AUX_LANES - AUX_MID, axis=1)
                               + pltpu.roll(aux, AUX_LANES - AUX_LO, axis=1))
        first_free = pl.multiple_of(_to_int(used), MOE_RT)

        def clear(k, carry):
            ys_scr[pl.ds(pl.multiple_of(first_free + k * MOE_RT, MOE_RT), MOE_RT), :] = jnp.zeros(
                (MOE_RT, D_MODEL), F32)
            return carry

        lax.fori_loop(0, (MOE_TBP - first_free) // MOE_RT, clear, 0)

    steps_per_group = EXPERTS_PER_GROUP // MOE_EPS
    group = step // steps_per_group
    first_of_group = step % steps_per_group == 0
    start_v, size_v = starts[0], sizes[0]
    for g in range(1, N_GROUPS):
        start_v = jnp.where(group == g, starts[g], start_v)
        size_v = jnp.where(group == g, sizes[g], size_v)
    seg_start = _to_int(start_v)
    n_tiles = _to_int(size_v) // MOE_RT

    def run_expert(r0, n_rows):
        rows = pl.ds(pl.multiple_of(r0, MOE_RT), n_rows)
        x = xs_scr[rows, :]
        cs = cs_scr[rows, :]
        y = None
        for k in range(MOE_EPS):
            gate = jnp.dot(x, wg_ref[0, k], preferred_element_type=F32)
            up = jnp.dot(x, wu_ref[0, k], preferred_element_type=F32)
            act = (gate * _sigmoid(gate)) * up
            yk = jnp.dot(act.astype(BF16), wd_ref[0, k], preferred_element_type=F32)
            yk = jnp.sum(jnp.where(lane == step * MOE_EPS + k, cs, 0.0), axis=-1, keepdims=True) * yk
            y = yk if y is None else y + yk

        @pl.when(first_of_group)
        def _():
            ys_scr[rows, :] = y

        @pl.when(jnp.logical_not(first_of_group))
        def _():
            ys_scr[rows, :] += y

    def pair(k, carry):
        run_expert(seg_start + k * (2 * MOE_RT), 2 * MOE_RT)
        return carry

    lax.fori_loop(0, n_tiles // 2, pair, 0)

    @pl.when(n_tiles % 2 == 1)
    def _():
        run_expert(seg_start + (n_tiles - 1) * MOE_RT, MOE_RT)

    @pl.when(step == N_EXPERTS // MOE_EPS - 1)
    def _():
        y = lax.dot_general(p_scr[...], ys_scr[...].astype(BF16), (((0,), (0,)), ((), ())),
                            preferred_element_type=F32)
        x2 = x_ref[...] + mod_ref[0, 5:6, :] * y
        o_ref[...] = _rms(x2, gf_ref[...]) if final_norm else x2


def moe_pallas(h2x, gid, x1, mod, g_final, w_gate, w_up, w_down, layer, seq_len, per_seq, final_norm):
    n = x1.shape[0]
    row = lambda i, e: (i, 0)
    expert = lambda i, e: (layer, e, 0, 0)
    return pl.pallas_call(
        functools.partial(_moe_kernel, final_norm=final_norm),
        grid=(n // MOE_TB, N_EXPERTS // MOE_EPS),
        in_specs=[pl.BlockSpec((MOE_TB, D_MOE_IN), row), pl.BlockSpec((V7X_SUBLANES, MOE_TB), lambda i, e: (0, i)),
                  pl.BlockSpec((MOE_TB, D_MODEL), row), _mod_spec(MOE_TB, seq_len, per_seq),
                  pl.BlockSpec((1, D_MODEL), lambda i, e: (0, 0)),
                  pl.BlockSpec((1, MOE_EPS, D_MODEL, D_EXPERT), expert),
                  pl.BlockSpec((1, MOE_EPS, D_MODEL, D_EXPERT), expert),
                  pl.BlockSpec((1, MOE_EPS, D_EXPERT, D_MODEL), expert)],
        out_specs=pl.BlockSpec((MOE_TB, D_MODEL), row),
        out_shape=jax.ShapeDtypeStruct((n, D_MODEL), F32),
        scratch_shapes=[pltpu.VMEM((MOE_TBP, MOE_TB), BF16), pltpu.VMEM((MOE_TBP, D_MODEL), BF16),
                        pltpu.VMEM((MOE_TBP, AUX_LANES), F32), pltpu.VMEM((MOE_TBP, D_MODEL), F32)],
        compiler_params=_params("arbitrary", "arbitrary"),
        name="moe",
    )(h2x, gid, x1, mod, g_final, w_gate, w_up, w_down)


TM_PROJ = 1024
TM_MERGE = 512


def _layer(x, mod, seq_len, per_seq, lw, h0, ctx_kv, tables, layer):
    n = x.shape[0]
    b = n // seq_len
    emit_kv = ctx_kv is None
    outs = inproj_pallas(x, mod, lw['norm_g'][0:1], lw['w_in'], layer, seq_len, per_seq, TM_PROJ, emit_kv)
    p = outs[0]
    p3 = p.reshape(b, seq_len, D_IN)
    y_rnn, h_fin = lru_pallas(p3, h0, lw['conv_w'], lw['conv_b'], lw['wg'], lw['gb'], lw['lam'])
    if emit_kv:
        y_att = ctxatt_pallas(p3)
    else:
        y_att = natten_pallas(p3, ctx_kv[0], ctx_kv[1], tables, layer)
    x1, h2x, gid = merge_pallas(y_rnn.reshape(n, D_RNN), y_att.reshape(n, D_ATT), p, x, mod, lw['norm_g'][1:2],
                                lw['w_br_rnn'], lw['w_br_att'], lw['w_out'], lw['w_router_t'], lw['router_bias'],
                                layer, seq_len, per_seq, TM_MERGE)
    x2 = moe_pallas(h2x, gid, x1, mod, lw['g_final'], lw['w_exp_gate'], lw['w_exp_up'], lw['w_exp_down'], layer,
                    seq_len, per_seq, layer == DEPTH - 1)
    kv = (outs[1], outs[2]) if emit_kv else None
    return x2, kv, h_fin


def kernel(x_prompt, x_sample, cache_k, cache_v, state_lru, c, c_ctx, w_ada, b_ada, norm_g, w_in, conv_w,
           conv_b, lru_wa, lru_ba, lru_wx, lru_bx, lru_lam, rpb, w_br_rnn, w_br_att, w_out, w_router,
           router_bias, w_exp_gate, w_exp_up, w_exp_down, final_norm_g):
    cvecs = jnp.concatenate([c, c_ctx[None, :], jnp.zeros((V7X_SUBLANES - DEC_BATCH - 1, D_MODEL), F32)], axis=0)
    mods = adaln_pallas(cvecs, w_ada, b_ada).reshape(DEPTH, V7X_SUBLANES, N_MOD, D_MODEL)
    tables = _natten_tables(rpb)
    w_router_t = w_router.T
    xp = x_prompt.reshape(BATCH * SEQ, D_MODEL)
    xs = x_sample.reshape(DEC_BATCH * DEC_SEQ, D_MODEL)
    zeros_h0 = jnp.zeros((BATCH, 2, D_RNN), F32)
    w_exp = [w.astype(BF16) for w in (w_exp_gate, w_exp_up, w_exp_down)]
    w_proj = [w.astype(BF16) for w in (w_in, w_br_rnn, w_br_att, w_out)]
    ks, vs, hs = [], [], []
    for l in range(DEPTH):
        lw = dict(
            norm_g=norm_g[l], w_in=w_proj[0], conv_w=conv_w[l], conv_b=conv_b[l][None, :],
            wg=_lru_gate_weights(lru_wa[l], lru_wx[l]),
            gb=jnp.stack([lru_ba[l, 0], lru_bx[l, 0], lru_ba[l, 1], lru_bx[l, 1]], axis=0), lam=lru_lam[l],
            w_br_rnn=w_proj[1], w_br_att=w_proj[2], w_out=w_proj[3],
            w_router_t=w_router_t, router_bias=router_bias, g_final=final_norm_g[None, :],
            w_exp_gate=w_exp[0], w_exp_up=w_exp[1], w_exp_down=w_exp[2])
        xp, kv, h_l = _layer(xp, mods[l, DEC_BATCH:DEC_BATCH + 1], SEQ, False, lw, zeros_h0, None, None, l)
        ks.append(kv[0].reshape(BATCH, SEQ, N_HEADS, HEAD_DIM))
        vs.append(kv[1].reshape(BATCH, SEQ, N_HEADS, HEAD_DIM))
        hs.append(h_l)
        ctx_kv = (cache_k[:, l].reshape(DEC_BATCH, PAST_LEN, D_ATT).astype(BF16),
                  cache_v[:, l].reshape(DEC_BATCH, PAST_LEN, D_ATT).astype(BF16))
        xs, _, _ = _layer(xs, mods[l, :DEC_BATCH], DEC_SEQ, True, lw, state_lru[:, l], ctx_kv, tables, l)
    y_prompt = xp.reshape(BATCH, SEQ, D_MODEL)
    y_sample = xs.reshape(DEC_BATCH, DEC_SEQ, D_MODEL)
    return (y_prompt, y_sample, jnp.stack(ks, axis=1), jnp.stack(vs, axis=1), jnp.stack(hs, axis=1))
```

```python
import functools

import jax
import jax.numpy as jnp
import numpy as np
from jax import lax
from jax.experimental import pallas as pl
from jax.experimental.pallas import tpu as pltpu

D_MODEL = 1024
BATCH = 16
SEQ = 256
DEPTH = 2
DEC_BATCH = 4
DEC_SEQ = 4096
PAST_LEN = 256

GRID_W = 64
D_RNN = 1024
N_LRU_BLOCKS = 16
LRU_BLOCK = D_RNN // N_LRU_BLOCKS
CONV_W = 4
LRU_C = 8.0
N_HEADS = 16
HEAD_DIM = 64
D_ATT = N_HEADS * HEAD_DIM
WIN_H = 8
WIN_W = 16
N_EXPERTS = 16
N_GROUPS = 4
EXPERTS_PER_GROUP = N_EXPERTS // N_GROUPS
D_EXPERT = 512
N_MOD = 6
D_IN = 2 * D_RNN + 3 * D_ATT + 2 * D_MODEL
EPS = 1e-6
NEG_INF = -1e30

BF16 = jnp.bfloat16
F32 = jnp.float32

V7X_LANES = 128
V7X_SUBLANES = 8
V7X_MXU_DIM = 256
V7X_VMEM_BYTES = 64 * 1024 * 1024
VMEM_LIMIT = V7X_VMEM_BYTES - 8 * 1024 * 1024

CHUNK = D_MODEL
N_CHUNKS = D_IN // CHUNK
C_XRNN, C_GRNN, C_Q, C_K, C_V, C_GATE_R, C_GATE_A = range(N_CHUNKS)

NT_DIMS = (((1,), (1,)), ((), ()))


def _params(*sem):
    return pltpu.CompilerParams(dimension_semantics=sem, vmem_limit_bytes=VMEM_LIMIT)


def _adaln_kernel(c_ref, w_ref, b_ref, o_ref):
    cv = c_ref[...]
    s = cv * jax.nn.sigmoid(cv)
    o_ref[0] = jnp.dot(s.astype(BF16), w_ref[0].astype(BF16), preferred_element_type=F32) + b_ref[0]


def adaln_pallas(cvecs, w_ada, b_ada):
    r = cvecs.shape[0]
    return pl.pallas_call(
        _adaln_kernel,
        grid=(DEPTH, N_MOD),
        in_specs=[pl.BlockSpec((r, D_MODEL), lambda l, j: (0, 0)),
                  pl.BlockSpec((1, D_MODEL, D_MODEL), lambda l, j: (l, 0, j)),
                  pl.BlockSpec((1, 1, D_MODEL), lambda l, j: (l, 0, j))],
        out_specs=pl.BlockSpec((1, r, D_MODEL), lambda l, j: (l, 0, j)),
        out_shape=jax.ShapeDtypeStruct((DEPTH, r, N_MOD * D_MODEL), F32),
        compiler_params=_params("arbitrary", "arbitrary"),
        name="adaln",
    )(cvecs, w_ada, b_ada.reshape(DEPTH, 1, N_MOD * D_MODEL))


def _mod_spec(tm, seq_len, per_seq):
    if per_seq:
        return pl.BlockSpec((1, N_MOD, D_MODEL), lambda i, *_: (i * tm // seq_len, 0, 0))
    return pl.BlockSpec((1, N_MOD, D_MODEL), lambda i, *_: (0, 0, 0))


def _rms(x, g):
    return x * lax.rsqrt(jnp.mean(x * x, axis=-1, keepdims=True) + EPS) * g


def _inproj_kernel(x_ref, mod_ref, g_ref, w_ref, *refs, emit_kv):
    if emit_kv:
        p_ref, k32_ref, v32_ref, h_scr = refs
    else:
        p_ref, h_scr = refs
    j = pl.program_id(1)

    @pl.when(j == 0)
    def _():
        y = _rms(x_ref[...], g_ref[...])
        h_scr[...] = (y * (1.0 + mod_ref[0, 1:2, :]) + mod_ref[0, 0:1, :]).astype(BF16)

    acc = jnp.dot(h_scr[...], w_ref[0], preferred_element_type=F32)
    p_ref[...] = (acc * jnp.where(j == C_Q, HEAD_DIM ** -0.5, 1.0)).astype(BF16)
    if emit_kv:
        @pl.when(j == C_K)
        def _():
            k32_ref[...] = acc

        @pl.when(j == C_V)
        def _():
            v32_ref[...] = acc


def inproj_pallas(x, mod, g, w_in, layer, seq_len, per_seq, tm, emit_kv):
    n = x.shape[0]
    row = lambda i, j: (i, 0)
    out_shape = [jax.ShapeDtypeStruct((n, D_IN), BF16)]
    out_specs = [pl.BlockSpec((tm, CHUNK), lambda i, j: (i, j))]
    if emit_kv:
        out_shape += [jax.ShapeDtypeStruct((n, D_ATT), F32)] * 2
        out_specs += [pl.BlockSpec((tm, D_ATT), row)] * 2
    return pl.pallas_call(
        functools.partial(_inproj_kernel, emit_kv=emit_kv),
        grid=(n // tm, N_CHUNKS),
        in_specs=[pl.BlockSpec((tm, D_MODEL), row), _mod_spec(tm, seq_len, per_seq),
                  pl.BlockSpec((1, D_MODEL), lambda i, j: (0, 0)),
                  pl.BlockSpec((1, D_MODEL, CHUNK), lambda i, j: (layer, 0, j))],
        out_specs=out_specs,
        out_shape=out_shape,
        scratch_shapes=[pltpu.VMEM((tm, D_MODEL), BF16)],
        compiler_params=_params("arbitrary", "arbitrary"),
        name="inproj",
    )(x, mod, g, w_in)


LRU_CB = 512
LRU_TC = 256
LRU_SUB = V7X_MXU_DIM
LRU_HALO = 16


def _scan_groups(a, b, carry, reverse):
    tc, c = a.shape
    ng = tc // V7X_SUBLANES
    a3 = a.reshape(ng, V7X_SUBLANES, c)
    b3 = b.reshape(ng, V7X_SUBLANES, c)
    sub = lax.broadcasted_iota(jnp.int32, (1, V7X_SUBLANES, 1), 1)
    s = 1
    while s < V7X_SUBLANES:
        shift = V7X_SUBLANES - s if reverse else s
        ok = (sub < V7X_SUBLANES - s) if reverse else (sub >= s)
        a_sh = pltpu.roll(a3, shift, axis=1)
        b_sh = pltpu.roll(b3, shift, axis=1)
        b3 = jnp.where(ok, a3 * b_sh + b3, b3)
        a3 = jnp.where(ok, a3 * a_sh, a3)
        s *= 2
    hs = [None] * ng
    order = range(ng - 1, -1, -1) if reverse else range(ng)
    edge = 0 if reverse else V7X_SUBLANES - 1
    for gi in order:
        h = a3[gi] * carry + b3[gi]
        carry = h[edge:edge + 1, :]
        hs[gi] = h
    return jnp.concatenate(hs, axis=0), carry


def _sigmoid(x):
    return 0.5 * jnp.tanh(0.5 * x) + 0.5


def _lru_kernel(x_ref, gate_ref, h0_ref, cw_ref, cb_ref, wg_ref, gb_ref, lam_ref, y_ref, fin_ref, hf_scr, u_scr):
    t_len = x_ref.shape[1]
    cb = x_ref.shape[2]
    n_chunks = t_len // LRU_TC

    def conv_chunk(c):
        t0 = pl.multiple_of(c * LRU_TC, LRU_TC)
        cur = x_ref[0, pl.ds(t0, LRU_TC), :].astype(F32)
        lo = pl.multiple_of(jnp.maximum(t0 - LRU_HALO, 0), LRU_HALO)
        hi = pl.multiple_of(jnp.minimum(t0 + LRU_TC, t_len - LRU_HALO), LRU_HALO)
        prev = jnp.where(c > 0, x_ref[0, pl.ds(lo, LRU_HALO), :].astype(F32), 0.0)
        nxt = jnp.where(c < n_chunks - 1, x_ref[0, pl.ds(hi, LRU_HALO), :].astype(F32), 0.0)
        ext = jnp.concatenate([prev, cur, nxt], axis=0)
        n_ext = LRU_TC + 2 * LRU_HALO
        u = cb_ref[...] + jnp.zeros((LRU_TC, cb), F32)
        for j in range(CONV_W):
            off = j - CONV_W // 2
            sh = ext if off == 0 else pltpu.roll(ext, (-off) % n_ext, axis=0)
            u = u + sh[LRU_HALO:LRU_HALO + LRU_TC, :] * cw_ref[j:j + 1, :]
        return t0, u

    def gates(u, d):
        ub = u.astype(BF16)
        pre = [jnp.dot(ub[:, LRU_SUB * s:LRU_SUB * (s + 1)], wg_ref[d, s], preferred_element_type=F32)
               for s in range(cb // LRU_SUB)]
        pre_a = jnp.concatenate([p[:, :LRU_SUB] for p in pre], axis=1)
        pre_x = jnp.concatenate([p[:, LRU_SUB:] for p in pre], axis=1)
        r = _sigmoid(pre_a + gb_ref[2 * d:2 * d + 1, :])
        i = _sigmoid(pre_x + gb_ref[2 * d + 1:2 * d + 2, :])
        log_a = (-LRU_C * jax.nn.softplus(-lam_ref[d:d + 1, :])) * r
        a = jnp.exp(log_a)
        th = jnp.tanh(log_a)
        num = -2.0 * th
        scale = jnp.where(num > 0.0, num * lax.rsqrt(num * (1.0 - th)), 0.0)
        inp = scale * (i * u)
        return a, inp

    def fwd(c, carry):
        t0, u = conv_chunk(c)
        u_scr[pl.ds(t0, LRU_TC), :] = u
        a, inp = gates(u, 0)
        h, carry = _scan_groups(a, inp, carry, reverse=False)
        hf_scr[pl.ds(t0, LRU_TC), :] = h
        return carry

    fin_f = lax.fori_loop(0, n_chunks, fwd, h0_ref[0, 0:1, :])

    def bwd(k, carry):
        t0 = pl.multiple_of((n_chunks - 1 - k) * LRU_TC, LRU_TC)
        a, inp = gates(u_scr[pl.ds(t0, LRU_TC), :], 1)
        h, carry = _scan_groups(a, inp, carry, reverse=True)
        g = gate_ref[0, pl.ds(t0, LRU_TC), :].astype(F32)
        y_ref[0, pl.ds(t0, LRU_TC), :] = ((hf_scr[pl.ds(t0, LRU_TC), :] + h) * jax.nn.gelu(g)).astype(y_ref.dtype)
        return carry

    fin_b = lax.fori_loop(0, n_chunks, bwd, h0_ref[0, 1:2, :])
    fin_ref[0, 0:1, :] = fin_f
    fin_ref[0, 1:2, :] = fin_b


def _lru_gate_weights(lru_wa, lru_wx):
    per = LRU_SUB // LRU_BLOCK
    eye = jnp.eye(per, dtype=F32)

    def dense(w):
        w = w.reshape(2, D_RNN // LRU_SUB, per, LRU_BLOCK, LRU_BLOCK)
        full = w[:, :, :, :, None, :] * eye[None, None, :, None, :, None]
        return full.reshape(2, D_RNN // LRU_SUB, LRU_SUB, LRU_SUB)

    return jnp.concatenate([dense(lru_wa), dense(lru_wx)], axis=-1).astype(BF16)


def lru_pallas(p, h0, conv_w, conv_b, wg, gb, lam):
    b, t, _ = p.shape
    n_cb = D_RNN // LRU_CB
    return pl.pallas_call(
        _lru_kernel,
        grid=(b, n_cb),
        in_specs=[pl.BlockSpec((1, t, LRU_CB), lambda bi, ci: (bi, 0, C_XRNN * n_cb + ci)),
                  pl.BlockSpec((1, t, LRU_CB), lambda bi, ci: (bi, 0, C_GRNN * n_cb + ci)),
                  pl.BlockSpec((1, 2, LRU_CB), lambda bi, ci: (bi, 0, ci)),
                  pl.BlockSpec((CONV_W, LRU_CB), lambda bi, ci: (0, ci)),
                  pl.BlockSpec((1, LRU_CB), lambda bi, ci: (0, ci)),
                  pl.BlockSpec((2, LRU_CB // LRU_SUB, LRU_SUB, 2 * LRU_SUB), lambda bi, ci: (0, ci, 0, 0)),
                  pl.BlockSpec((4, LRU_CB), lambda bi, ci: (0, ci)),
                  pl.BlockSpec((2, LRU_CB), lambda bi, ci: (0, ci))],
        out_specs=[pl.BlockSpec((1, t, LRU_CB), lambda bi, ci: (bi, 0, ci)),
                   pl.BlockSpec((1, 2, LRU_CB), lambda bi, ci: (bi, 0, ci))],
        out_shape=[jax.ShapeDtypeStruct((b, t, D_RNN), BF16), jax.ShapeDtypeStruct((b, 2, D_RNN), F32)],
        scratch_shapes=[pltpu.VMEM((t, LRU_CB), F32), pltpu.VMEM((t, LRU_CB), F32)],
        compiler_params=_params("arbitrary", "arbitrary"),
        name="lru",
    )(p, p, h0, conv_w, conv_b, wg, gb, lam)


Q_ROWS = 4
Q_TILE = Q_ROWS * GRID_W
KEY_ROWS = 12
KEY_TILE = KEY_ROWS * GRID_W
N_Q_TILES = DEC_SEQ // Q_TILE
ATT_LANES = 1024
HEADS_PER_STEP = ATT_LANES // HEAD_DIM
PAIR = 2 * HEAD_DIM


def _window_block(t):
    return jnp.clip(t - 1, 0, N_Q_TILES - KEY_ROWS // Q_ROWS)


N_DR_PAIRS = 2 * WIN_H


def _natten_tables(rpb):
    c = np.arange(GRID_W)
    q_start = np.clip(c - WIN_W // 2, 0, GRID_W - WIN_W)
    col_valid = (c[None, :] >= q_start[:, None]) & (c[None, :] < q_start[:, None] + WIN_W)
    dc = c[None, :] - c[:, None] + WIN_W - 1
    sel_c = ((dc[None] == np.arange(2 * WIN_W - 1)[:, None, None]) & col_valid[None]).astype(np.float32)
    t = jnp.einsum('lhrd,dck->lhrck', rpb.astype(F32), jnp.asarray(sel_c), precision=lax.Precision.HIGHEST)
    t = jnp.where(col_valid[None, None, None], t, NEG_INF)
    t = jnp.pad(t, ((0, 0), (0, 0), (1, 1), (0, 0), (0, 0)), constant_values=NEG_INF)
    return jnp.concatenate([t[:, :, :-1], t[:, :, 1:]], axis=-1)


def _attend(q2, keys, vals, biases):
    lane_head = lax.broadcasted_iota(jnp.int32, (1, PAIR), 1) // HEAD_DIM
    acc = jnp.zeros((q2.shape[0], PAIR), F32)
    for j in range(2):
        mine = lane_head == j
        qh = jnp.where(mine, q2, jnp.zeros_like(q2))
        s = []
        for kb, bb in zip(keys, biases[j]):
            sd = lax.dot_general(qh, kb, NT_DIMS, preferred_element_type=F32)
            s.append(sd if bb is None else sd + bb)
        m = s[0].max(axis=-1, keepdims=True)
        for sd in s[1:]:
            m = jnp.maximum(m, sd.max(axis=-1, keepdims=True))
        o = None
        for sd, vb in zip(s, vals):
            pv = jnp.dot(jnp.exp(sd - m).astype(BF16), jnp.where(mine, vb, jnp.ones_like(vb)),
                         preferred_element_type=F32)
            o = pv if o is None else o + pv
        acc = acc + jnp.where(mine, o / pltpu.roll(o, HEAD_DIM, axis=1), 0.0)
    return acc


def _natten_kernel(q_ref, k0_ref, k1_ref, k2_ref, v0_ref, v1_ref, v2_ref, kc_ref, vc_ref, tt_ref, o_ref):
    ti = pl.program_id(1)
    rows = DEC_SEQ // GRID_W
    wstart = _window_block(ti) * Q_ROWS
    first_row = lax.broadcasted_iota(jnp.int32, (1, 2 * GRID_W), 1) < GRID_W
    pieces = {}
    for a in range(Q_ROWS):
        r = ti * Q_ROWS + a
        start_r = jnp.clip(r - WIN_H // 2, 0, rows - WIN_H)
        for i in range(0, KEY_ROWS, 2):
            kr = wstart + i
            ok = [((kr + e >= start_r) & (kr + e < start_r + WIN_H)).astype(jnp.int32) for e in range(2)]
            pieces[a, i] = (jnp.clip(kr - r + WIN_H, 0, N_DR_PAIRS - 1), jnp.where(first_row, ok[0], ok[1]) > 0)

    def bias_block(h, d):
        return jnp.concatenate(
            [jnp.concatenate([jnp.where(pieces[a, i][1], tt_ref[0, h, pieces[a, i][0]], NEG_INF)
                              for i in range(Q_ROWS * d, Q_ROWS * (d + 1), 2)], axis=1)
             for a in range(Q_ROWS)], axis=0)

    k_refs = (k0_ref, k1_ref, k2_ref, kc_ref)
    v_refs = (v0_ref, v1_ref, v2_ref, vc_ref)
    outs = []
    for hp in range(ATT_LANES // PAIR):
        sl = slice(PAIR * hp, PAIR * (hp + 1))
        biases = [[bias_block(2 * hp + j, d) for d in range(3)] + [None] for j in range(2)]
        outs.append(_attend(q_ref[0, :, sl], [r[0, :, sl] for r in k_refs], [r[0, :, sl] for r in v_refs], biases))
    o_ref[0] = jnp.concatenate(outs, axis=1).astype(o_ref.dtype)


def natten_pallas(p, k_ctx, v_ctx, tables, layer):
    b, t, _ = p.shape
    n_lb = D_ATT // ATT_LANES
    q_spec = pl.BlockSpec((1, Q_TILE, ATT_LANES), lambda lb, ti, bi: (bi, ti, C_Q * n_lb + lb))
    win = lambda ch, d: pl.BlockSpec((1, Q_TILE, ATT_LANES),
                                     lambda lb, ti, bi: (bi, _window_block(ti) + d, ch * n_lb + lb))
    ctx = pl.BlockSpec((1, PAST_LEN, ATT_LANES), lambda lb, ti, bi: (bi, 0, lb))
    return pl.pallas_call(
        _natten_kernel,
        grid=(n_lb, N_Q_TILES, b),
        in_specs=[q_spec, win(C_K, 0), win(C_K, 1), win(C_K, 2), win(C_V, 0), win(C_V, 1), win(C_V, 2), ctx, ctx,
                  pl.BlockSpec((1, HEADS_PER_STEP, N_DR_PAIRS, GRID_W, 2 * GRID_W),
                               lambda lb, ti, bi: (layer, lb, 0, 0, 0))],
        out_specs=pl.BlockSpec((1, Q_TILE, ATT_LANES), lambda lb, ti, bi: (bi, ti, lb)),
        out_shape=jax.ShapeDtypeStruct((b, t, D_ATT), BF16),
        compiler_params=_params("arbitrary", "arbitrary", "arbitrary"),
        name="natten",
    )(p, p, p, p, p, p, p, k_ctx, v_ctx, tables)


def _ctxatt_kernel(q_ref, k_ref, v_ref, o_ref):
    outs = []
    for hp in range(ATT_LANES // PAIR):
        sl = slice(PAIR * hp, PAIR * (hp + 1))
        outs.append(_attend(q_ref[0, :, sl], [k_ref[0, :, sl]], [v_ref[0, :, sl]], [[None], [None]]))
    o_ref[0] = jnp.concatenate(outs, axis=1).astype(o_ref.dtype)


def ctxatt_pallas(p):
    b, t, _ = p.shape
    n_lb = D_ATT // ATT_LANES
    blk = lambda ch: pl.BlockSpec((1, t, ATT_LANES), lambda lb, bi: (bi, 0, ch * n_lb + lb))
    return pl.pallas_call(
        _ctxatt_kernel,
        grid=(n_lb, b),
        in_specs=[blk(C_Q), blk(C_K), blk(C_V)],
        out_specs=pl.BlockSpec((1, t, ATT_LANES), lambda lb, bi: (bi, 0, lb)),
        out_shape=jax.ShapeDtypeStruct((b, t, D_ATT), BF16),
        compiler_params=_params("arbitrary", "arbitrary"),
        name="ctxatt",
    )(p, p, p)


def _split_bf16(x):
    hi = x.astype(BF16)
    return hi, (x - hi.astype(F32)).astype(BF16)


def _route(logits_t, rb_ref):
    score = [jax.nn.sigmoid(logits_t[e:e + 1, :]) for e in range(N_EXPERTS)]
    sel = [score[e] + rb_ref[e] for e in range(N_EXPERTS)]
    best_g = None
    for g in range(N_GROUPS):
        v = sel[EXPERTS_PER_GROUP * g:EXPERTS_PER_GROUP * (g + 1)]
        top2 = None
        for i in range(EXPERTS_PER_GROUP):
            for j in range(i + 1, EXPERTS_PER_GROUP):
                pair = v[i] + v[j]
                top2 = pair if top2 is None else jnp.maximum(top2, pair)
        if best_g is None:
            best_g, best_v = jnp.zeros_like(top2, dtype=jnp.int32), top2
        else:
            upd = top2 > best_v
            best_g = jnp.where(upd, g, best_g)
            best_v = jnp.where(upd, top2, best_v)

    def in_best(vals, j):
        out = vals[j]
        for g in range(1, N_GROUPS):
            out = jnp.where(best_g == g, vals[EXPERTS_PER_GROUP * g + j], out)
        return out

    v = [in_best(sel, j) for j in range(EXPERTS_PER_GROUP)]
    sc = [in_best(score, j) for j in range(EXPERTS_PER_GROUP)]

    def first_argmax(vals):
        idx, top = jnp.zeros_like(best_g), vals[0]
        for j in range(1, EXPERTS_PER_GROUP):
            upd = vals[j] > top
            idx = jnp.where(upd, j, idx)
            top = jnp.where(upd, vals[j], top)
        return idx

    i1 = first_argmax(v)
    i2 = first_argmax([jnp.where(i1 == j, -jnp.inf, v[j]) for j in range(EXPERTS_PER_GROUP)])
    pick = lambda idx: sum(jnp.where(idx == j, sc[j], 0.0) for j in range(EXPERTS_PER_GROUP))
    w1, w2 = pick(i1), pick(i2)
    den = w1 + w2
    c1, c2 = w1 / den, w2 / den
    rows = []
    for e in range(N_EXPERTS):
        g, j = divmod(e, EXPERTS_PER_GROUP)
        rows.append(jnp.where(best_g == g, jnp.where(i1 == j, c1, 0.0) + jnp.where(i2 == j, c2, 0.0), 0.0))
    return jnp.concatenate(rows, axis=0), best_g


AUX_LANES = V7X_LANES
AUX_MID = N_EXPERTS
AUX_LO = 2 * N_EXPERTS
AUX_GROUP = 3 * N_EXPERTS
D_MOE_IN = D_MODEL + AUX_LANES


def _merge_kernel(rb_ref, yr_ref, ya_ref, gr_ref, ga_ref, x_ref, mod_ref, g_ref, wr_ref, wa_ref, wo_ref, wrt_ref,
                  x1_ref, h2_ref, gid_ref):
    m = (_sigmoid(gr_ref[...].astype(F32)) * jnp.dot(yr_ref[...], wr_ref[0], preferred_element_type=F32)
         + _sigmoid(ga_ref[...].astype(F32)) * jnp.dot(ya_ref[...], wa_ref[0], preferred_element_type=F32))
    o = jnp.dot(m.astype(BF16), wo_ref[0], preferred_element_type=F32)
    x1 = x_ref[...] + mod_ref[0, 2:3, :] * o
    x1_ref[...] = x1
    h2 = _rms(x1, g_ref[...]) * (1.0 + mod_ref[0, 4:5, :]) + mod_ref[0, 3:4, :]
    h_hi, h_lo = _split_bf16(h2)
    w_hi, w_lo = _split_bf16(wrt_ref[...])
    dot_nt = lambda a, b: lax.dot_general(a, b, NT_DIMS, preferred_element_type=F32)
    logits_t = dot_nt(w_hi, h_hi) + (dot_nt(w_hi, h_lo) + dot_nt(w_lo, h_hi))
    comb_t, group = _route(logits_t, rb_ref)
    tm = comb_t.shape[1]
    group = group.astype(F32)
    gid_ref[...] = jnp.concatenate([group, jnp.zeros((V7X_SUBLANES - 1, tm), F32)], axis=0)
    padded = jnp.concatenate([comb_t, jnp.zeros((AUX_GROUP - N_EXPERTS, tm), F32), group,
                              jnp.zeros((AUX_LANES - AUX_GROUP - 1, tm), F32)], axis=0)
    c = padded.T
    c_hi = c.astype(BF16).astype(F32)
    c_mid = (c - c_hi).astype(BF16).astype(F32)
    c_lo = c - c_hi - c_mid
    aux = c_hi + pltpu.roll(c_mid, AUX_MID, axis=1) + pltpu.roll(c_lo, AUX_LO, axis=1)
    h2_ref[...] = jnp.concatenate([h2.astype(BF16), aux.astype(BF16)], axis=1)


def merge_pallas(y_rnn, y_att, p, x, mod, g, w_br_rnn, w_br_att, w_out, w_router_t, router_bias, layer, seq_len,
                 per_seq, tm):
    n = x.shape[0]
    row = lambda i: (i, 0)
    full = lambda shape: pl.BlockSpec(shape, lambda i: (0, 0))
    weight = pl.BlockSpec((1, D_MODEL, D_MODEL), lambda i: (layer, 0, 0))
    return pl.pallas_call(
        _merge_kernel,
        grid=(n // tm,),
        in_specs=[pl.BlockSpec(memory_space=pltpu.SMEM),
                  pl.BlockSpec((tm, D_RNN), row), pl.BlockSpec((tm, D_ATT), row),
                  pl.BlockSpec((tm, CHUNK), lambda i: (i, C_GATE_R)), pl.BlockSpec((tm, CHUNK), lambda i: (i, C_GATE_A)),
                  pl.BlockSpec((tm, D_MODEL), row), _mod_spec(tm, seq_len, per_seq), full((1, D_MODEL)),
                  weight, weight, weight, full((N_EXPERTS, D_MODEL))],
        out_specs=[pl.BlockSpec((tm, D_MODEL), row), pl.BlockSpec((tm, D_MOE_IN), row),
                   pl.BlockSpec((V7X_SUBLANES, tm), lambda i: (0, i))],
        out_shape=[jax.ShapeDtypeStruct((n, D_MODEL), F32), jax.ShapeDtypeStruct((n, D_MOE_IN), BF16),
                   jax.ShapeDtypeStruct((V7X_SUBLANES, n), F32)],
        compiler_params=_params("arbitrary"),
        name="merge",
    )(router_bias, y_rnn, y_att, p, p, x, mod, g, w_br_rnn, w_br_att, w_out, w_router_t)


MOE_TB = 1024
MOE_RT = 128
MOE_TBP = MOE_TB + N_GROUPS * MOE_RT
MOE_EPS = 2


def _group_segments(gid_row):
    sub = lax.broadcasted_iota(jnp.int32, (V7X_SUBLANES, 1), 0).astype(F32)
    onehot = (gid_row == sub).astype(F32)
    cnt = jnp.sum(onehot, axis=1, keepdims=True)
    padded = jnp.floor((cnt + (MOE_RT - 1)) * (1.0 / MOE_RT)) * MOE_RT
    starts, run = [], jnp.zeros((1, 1), F32)
    for g in range(N_GROUPS):
        starts.append(run)
        run = run + padded[g:g + 1, :]
    return onehot, starts, [padded[g:g + 1, :] for g in range(N_GROUPS)], run


def _to_int(v):
    return v[0, 0].astype(jnp.int32)


def _moe_kernel(h_ref, gid_ref, x_ref, mod_ref, gf_ref, wg_ref, wu_ref, wd_ref, o_ref,
                p_scr, xs_scr, cs_scr, ys_scr, *, final_norm):
    step = pl.program_id(1)
    onehot, starts, sizes, used = _group_segments(gid_ref[0:1, :])
    lane = lax.broadcasted_iota(jnp.int32, (1, AUX_LANES), 1)

    @pl.when(step == 0)
    def _():
        t_row = lax.broadcasted_iota(jnp.int32, (MOE_TB, MOE_TB), 0)
        t_col = lax.broadcasted_iota(jnp.int32, (MOE_TB, MOE_TB), 1)
        earlier = (t_row < t_col).astype(BF16)
        rank = jnp.dot(onehot.astype(BF16), earlier, preferred_element_type=F32)
        pos = jnp.zeros((1, MOE_TB), F32)
        for g in range(N_GROUPS):
            pos = pos + onehot[g:g + 1, :] * (starts[g] + rank[g:g + 1, :])
        dest = lax.broadcasted_iota(jnp.int32, (MOE_TBP, 1), 0).astype(F32)
        p_scr[...] = (dest == pos).astype(BF16)
        for r0 in range(0, MOE_TBP, V7X_MXU_DIM):
            rows = pl.ds(r0, V7X_MXU_DIM)
            sorted_rows = jnp.dot(p_scr[rows, :], h_ref[...], preferred_element_type=F32)
            xs_scr[rows, :] = sorted_rows[:, :D_MODEL].astype(BF16)
            aux = sorted_rows[:, D_MODEL:]
            cs_scr[rows, :] = (aux + pltpu.roll(aux, AUX_LANES - AUX_MID, axis=1)
                               + pltpu.roll(aux, AUX_LANES - AUX_LO, axis=1))
        first_free = pl.multiple_of(_to_int(used), MOE_RT)

        def clear(k, carry):
            ys_scr[pl.ds(pl.multiple_of(first_free + k * MOE_RT, MOE_RT), MOE_RT), :] = jnp.zeros(
                (MOE_RT, D_MODEL), F32)
            return carry

        lax.fori_loop(0, (MOE_TBP - first_free) // MOE_RT, clear, 0)

    steps_per_group = EXPERTS_PER_GROUP // MOE_EPS
    group = step // steps_per_group
    first_of_group = step % steps_per_group == 0
    start_v, size_v = starts[0], sizes[0]
    for g in range(1, N_GROUPS):
        start_v = jnp.where(group == g, starts[g], start_v)
        size_v = jnp.where(group == g, sizes[g], size_v)
    seg_start = _to_int(start_v)
    n_tiles = _to_int(size_v) // MOE_RT

    def run_expert(r0, n_rows):
        rows = pl.ds(pl.multiple_of(r0, MOE_RT), n_rows)
        x = xs_scr[rows, :]
        cs = cs_scr[rows, :]
        y = None
        for k in range(MOE_EPS):
            gate = jnp.dot(x, wg_ref[0, k], preferred_element_type=F32)
            up = jnp.dot(x, wu_ref[0, k], preferred_element_type=F32)
            act = (gate * _sigmoid(gate)) * up
            yk = jnp.dot(act.astype(BF16), wd_ref[0, k], preferred_element_type=F32)
            yk = jnp.sum(jnp.where(lane == step * MOE_EPS + k, cs, 0.0), axis=-1, keepdims=True) * yk
            y = yk if y is None else y + yk

        @pl.when(first_of_group)
        def _():
            ys_scr[rows, :] = y

        @pl.when(jnp.logical_not(first_of_group))
        def _():
            ys_scr[rows, :] += y

    def pair(k, carry):
        run_expert(seg_start + k * (2 * MOE_RT), 2 * MOE_RT)
        return carry

    lax.fori_loop(0, n_tiles // 2, pair, 0)

    @pl.when(n_tiles % 2 == 1)
    def _():
        run_expert(seg_start + (n_tiles - 1) * MOE_RT, MOE_RT)

    @pl.when(step == N_EXPERTS // MOE_EPS - 1)
    def _():
        y = lax.dot_general(p_scr[...], ys_scr[...].astype(BF16), (((0,), (0,)), ((), ())),
                            preferred_element_type=F32)
        x2 = x_ref[...] + mod_ref[0, 5:6, :] * y
        o_ref[...] = _rms(x2, gf_ref[...]) if final_norm else x2


def moe_pallas(h2x, gid, x1, mod, g_final, w_gate, w_up, w_down, layer, seq_len, per_seq, final_norm):
    n = x1.shape[0]
    row = lambda i, e: (i, 0)
    expert = lambda i, e: (layer, e, 0, 0)
    return pl.pallas_call(
        functools.partial(_moe_kernel, final_norm=final_norm),
        grid=(n // MOE_TB, N_EXPERTS // MOE_EPS),
        in_specs=[pl.BlockSpec((MOE_TB, D_MOE_IN), row), pl.BlockSpec((V7X_SUBLANES, MOE_TB), lambda i, e: (0, i)),
                  pl.BlockSpec((MOE_TB, D_MODEL), row), _mod_spec(MOE_TB, seq_len, per_seq),
                  pl.BlockSpec((1, D_MODEL), lambda i, e: (0, 0)),
                  pl.BlockSpec((1, MOE_EPS, D_MODEL, D_EXPERT), expert),
                  pl.BlockSpec((1, MOE_EPS, D_MODEL, D_EXPERT), expert),
                  pl.BlockSpec((1, MOE_EPS, D_EXPERT, D_MODEL), expert)],
        out_specs=pl.BlockSpec((MOE_TB, D_MODEL), row),
        out_shape=jax.ShapeDtypeStruct((n, D_MODEL), F32),
        scratch_shapes=[pltpu.VMEM((MOE_TBP, MOE_TB), BF16), pltpu.VMEM((MOE_TBP, D_MODEL), BF16),
                        pltpu.VMEM((MOE_TBP, AUX_LANES), F32), pltpu.VMEM((MOE_TBP, D_MODEL), F32)],
        compiler_params=_params("arbitrary", "arbitrary"),
        name="moe",
    )(h2x, gid, x1, mod, g_final, w_gate, w_up, w_down)


TM_PROJ = 2048
TM_MERGE = 512


def _layer(x, mod, seq_len, per_seq, lw, h0, ctx_kv, tables, layer):
    n = x.shape[0]
    b = n // seq_len
    emit_kv = ctx_kv is None
    outs = inproj_pallas(x, mod, lw['norm_g'][0:1], lw['w_in'], layer, seq_len, per_seq,
                         TM_PROJ // 2 if emit_kv else TM_PROJ, emit_kv)
    p = outs[0]
    p3 = p.reshape(b, seq_len, D_IN)
    y_rnn, h_fin = lru_pallas(p3, h0, lw['conv_w'], lw['conv_b'], lw['wg'], lw['gb'], lw['lam'])
    if emit_kv:
        y_att = ctxatt_pallas(p3)
    else:
        y_att = natten_pallas(p3, ctx_kv[0], ctx_kv[1], tables, layer)
    x1, h2x, gid = merge_pallas(y_rnn.reshape(n, D_RNN), y_att.reshape(n, D_ATT), p, x, mod, lw['norm_g'][1:2],
                                lw['w_br_rnn'], lw['w_br_att'], lw['w_out'], lw['w_router_t'], lw['router_bias'],
                                layer, seq_len, per_seq, TM_MERGE)
    x2 = moe_pallas(h2x, gid, x1, mod, lw['g_final'], lw['w_exp_gate'], lw['w_exp_up'], lw['w_exp_down'], layer,
                    seq_len, per_seq, layer == DEPTH - 1)
    kv = (outs[1], outs[2]) if emit_kv else None
    return x2, kv, h_fin


def kernel(x_prompt, x_sample, cache_k, cache_v, state_lru, c, c_ctx, w_ada, b_ada, norm_g, w_in, conv_w,
           conv_b, lru_wa, lru_ba, lru_wx, lru_bx, lru_lam, rpb, w_br_rnn, w_br_att, w_out, w_router,
           router_bias, w_exp_gate, w_exp_up, w_exp_down, final_norm_g):
    cvecs = jnp.concatenate([c, c_ctx[None, :], jnp.zeros((V7X_SUBLANES - DEC_BATCH - 1, D_MODEL), F32)], axis=0)
    mods = adaln_pallas(cvecs, w_ada, b_ada).reshape(DEPTH, V7X_SUBLANES, N_MOD, D_MODEL)
    tables = _natten_tables(rpb)
    w_router_t = w_router.T
    xp = x_prompt.reshape(BATCH * SEQ, D_MODEL)
    xs = x_sample.reshape(DEC_BATCH * DEC_SEQ, D_MODEL)
    zeros_h0 = jnp.zeros((BATCH, 2, D_RNN), F32)
    w_exp = [w.astype(BF16) for w in (w_exp_gate, w_exp_up, w_exp_down)]
    w_proj = [w.astype(BF16) for w in (w_in, w_br_rnn, w_br_att, w_out)]
    ks, vs, hs = [], [], []
    for l in range(DEPTH):
        lw = dict(
            norm_g=norm_g[l], w_in=w_proj[0], conv_w=conv_w[l], conv_b=conv_b[l][None, :],
            wg=_lru_gate_weights(lru_wa[l], lru_wx[l]),
            gb=jnp.stack([lru_ba[l, 0], lru_bx[l, 0], lru_ba[l, 1], lru_bx[l, 1]], axis=0), lam=lru_lam[l],
            w_br_rnn=w_proj[1], w_br_att=w_proj[2], w_out=w_proj[3],
            w_router_t=w_router_t, router_bias=router_bias, g_final=final_norm_g[None, :],
            w_exp_gate=w_exp[0], w_exp_up=w_exp[1], w_exp_down=w_exp[2])
        xp, kv, h_l = _layer(xp, mods[l, DEC_BATCH:DEC_BATCH + 1], SEQ, False, lw, zeros_h0, None, None, l)
        ks.append(kv[0].reshape(BATCH, SEQ, N_HEADS, HEAD_DIM))
        vs.append(kv[1].reshape(BATCH, SEQ, N_HEADS, HEAD_DIM))
        hs.append(h_l)
        ctx_kv = (cache_k[:, l].reshape(DEC_BATCH, PAST_LEN, D_ATT).astype(BF16),
                  cache_v[:, l].reshape(DEC_BATCH, PAST_LEN, D_ATT).astype(BF16))
        xs, _, _ = _layer(xs, mods[l, :DEC_BATCH], DEC_SEQ, True, lw, state_lru[:, l], ctx_kv, tables, l)
    y_prompt = xp.reshape(BATCH, SEQ, D_MODEL)
    y_sample = xs.reshape(DEC_BATCH, DEC_SEQ, D_MODEL)
    return (y_prompt, y_sample, jnp.stack(ks, axis=1), jnp.stack(vs, axis=1), jnp.stack(hs, axis=1))
```

```python
import functools

import jax
import jax.numpy as jnp
import numpy as np
from jax import lax
from jax.experimental import pallas as pl
from jax.experimental.pallas import tpu as pltpu

D_MODEL = 1024
BATCH = 16
SEQ = 256
DEPTH = 2
DEC_BATCH = 4
DEC_SEQ = 4096
PAST_LEN = 256

GRID_W = 64
D_RNN = 1024
N_LRU_BLOCKS = 16
LRU_BLOCK = D_RNN // N_LRU_BLOCKS
CONV_W = 4
LRU_C = 8.0
N_HEADS = 16
HEAD_DIM = 64
D_ATT = N_HEADS * HEAD_DIM
WIN_H = 8
WIN_W = 16
N_EXPERTS = 16
N_GROUPS = 4
EXPERTS_PER_GROUP = N_EXPERTS // N_GROUPS
D_EXPERT = 512
N_MOD = 6
D_IN = 2 * D_RNN + 3 * D_ATT + 2 * D_MODEL
EPS = 1e-6
NEG_INF = -1e30

BF16 = jnp.bfloat16
F32 = jnp.float32

V7X_LANES = 128
V7X_SUBLANES = 8
V7X_MXU_DIM = 256
V7X_VMEM_BYTES = 64 * 1024 * 1024
VMEM_LIMIT = V7X_VMEM_BYTES - 8 * 1024 * 1024

CHUNK = D_MODEL
N_CHUNKS = D_IN // CHUNK
C_XRNN, C_GRNN, C_Q, C_K, C_V, C_GATE_R, C_GATE_A = range(N_CHUNKS)

NT_DIMS = (((1,), (1,)), ((), ()))


def _params(*sem):
    return pltpu.CompilerParams(dimension_semantics=sem, vmem_limit_bytes=VMEM_LIMIT)


def _adaln_kernel(c_ref, w_ref, b_ref, o_ref):
    cv = c_ref[...]
    s = cv * jax.nn.sigmoid(cv)
    o_ref[0] = jnp.dot(s.astype(BF16), w_ref[0].astype(BF16), preferred_element_type=F32) + b_ref[0]


def adaln_pallas(cvecs, w_ada, b_ada):
    r = cvecs.shape[0]
    return pl.pallas_call(
        _adaln_kernel,
        grid=(DEPTH, N_MOD),
        in_specs=[pl.BlockSpec((r, D_MODEL), lambda l, j: (0, 0)),
                  pl.BlockSpec((1, D_MODEL, D_MODEL), lambda l, j: (l, 0, j)),
                  pl.BlockSpec((1, 1, D_MODEL), lambda l, j: (l, 0, j))],
        out_specs=pl.BlockSpec((1, r, D_MODEL), lambda l, j: (l, 0, j)),
        out_shape=jax.ShapeDtypeStruct((DEPTH, r, N_MOD * D_MODEL), F32),
        compiler_params=_params("arbitrary", "arbitrary"),
        name="adaln",
    )(cvecs, w_ada, b_ada.reshape(DEPTH, 1, N_MOD * D_MODEL))


def _mod_spec(tm, seq_len, per_seq):
    if per_seq:
        return pl.BlockSpec((1, N_MOD, D_MODEL), lambda i, *_: (i * tm // seq_len, 0, 0))
    return pl.BlockSpec((1, N_MOD, D_MODEL), lambda i, *_: (0, 0, 0))


def _rms(x, g):
    return x * lax.rsqrt(jnp.mean(x * x, axis=-1, keepdims=True) + EPS) * g


def _inproj_kernel(x_ref, mod_ref, g_ref, w_ref, *refs, emit_kv):
    if emit_kv:
        p_ref, k32_ref, v32_ref, h_scr = refs
    else:
        p_ref, h_scr = refs
    j = pl.program_id(1)

    @pl.when(j == 0)
    def _():
        y = _rms(x_ref[...], g_ref[...])
        h_scr[...] = (y * (1.0 + mod_ref[0, 1:2, :]) + mod_ref[0, 0:1, :]).astype(BF16)

    acc = jnp.dot(h_scr[...], w_ref[0], preferred_element_type=F32)
    p_ref[...] = (acc * jnp.where(j == C_Q, HEAD_DIM ** -0.5, 1.0)).astype(BF16)
    if emit_kv:
        @pl.when(j == C_K)
        def _():
            k32_ref[...] = acc

        @pl.when(j == C_V)
        def _():
            v32_ref[...] = acc


def inproj_pallas(x, mod, g, w_in, layer, seq_len, per_seq, tm, emit_kv):
    n = x.shape[0]
    row = lambda i, j: (i, 0)
    out_shape = [jax.ShapeDtypeStruct((n, D_IN), BF16)]
    out_specs = [pl.BlockSpec((tm, CHUNK), lambda i, j: (i, j))]
    if emit_kv:
        out_shape += [jax.ShapeDtypeStruct((n, D_ATT), F32)] * 2
        out_specs += [pl.BlockSpec((tm, D_ATT), row)] * 2
    return pl.pallas_call(
        functools.partial(_inproj_kernel, emit_kv=emit_kv),
        grid=(n // tm, N_CHUNKS),
        in_specs=[pl.BlockSpec((tm, D_MODEL), row), _mod_spec(tm, seq_len, per_seq),
                  pl.BlockSpec((1, D_MODEL), lambda i, j: (0, 0)),
                  pl.BlockSpec((1, D_MODEL, CHUNK), lambda i, j: (layer, 0, j))],
        out_specs=out_specs,
        out_shape=out_shape,
        scratch_shapes=[pltpu.VMEM((tm, D_MODEL), BF16)],
        compiler_params=_params("arbitrary", "arbitrary"),
        name="inproj",
    )(x, mod, g, w_in)


LRU_CB = 512
LRU_TC = 256
LRU_SUB = V7X_MXU_DIM
LRU_HALO = 16


LRU_SEG = LRU_TC // V7X_SUBLANES
LRU_NSLAB = LRU_SEG + CONV_W


def _lru_row_maps():
    s = np.arange(V7X_SUBLANES)[None, :]
    src = (LRU_HALO - CONV_W // 2) + LRU_SEG * s + np.arange(LRU_NSLAB)[:, None]
    sel = np.zeros((LRU_NSLAB * V7X_SUBLANES, LRU_TC + 2 * LRU_HALO), np.float32)
    sel[np.arange(sel.shape[0]), src.reshape(-1)] = 1.0
    tok = (LRU_SEG * s + np.arange(LRU_SEG)[:, None]).reshape(-1)
    perm = np.zeros((LRU_TC, LRU_TC), np.float32)
    perm[np.arange(LRU_TC), tok] = 1.0
    return jnp.asarray(sel, BF16), jnp.asarray(perm, BF16), jnp.asarray(perm.T, BF16)


def _slab_scan(a3, b3, carry, reverse):
    n = a3.shape[0]
    hs, cum = [None] * n, [None] * n
    h = cp = None
    for t in (range(n - 1, -1, -1) if reverse else range(n)):
        h = b3[t] if h is None else a3[t] * h + b3[t]
        cp = a3[t] if cp is None else cp * a3[t]
        hs[t], cum[t] = h, cp
    sub = lax.broadcasted_iota(jnp.int32, (V7X_SUBLANES, 1), 0)
    pa, pb = cp, h
    s = 1
    while s < V7X_SUBLANES:
        shift = V7X_SUBLANES - s if reverse else s
        ok = (sub < V7X_SUBLANES - s) if reverse else (sub >= s)
        a_sh = pltpu.roll(pa, shift, axis=0)
        b_sh = pltpu.roll(pb, shift, axis=0)
        pb = jnp.where(ok, pa * b_sh + pb, pb)
        pa = jnp.where(ok, pa * a_sh, pa)
        s *= 2
    leaving = pb + pa * carry
    first, last = (V7X_SUBLANES - 1, 0) if reverse else (0, V7X_SUBLANES - 1)
    entering = jnp.where(sub == first, carry, pltpu.roll(leaving, V7X_SUBLANES - 1 if reverse else 1, axis=0))
    out = jnp.concatenate([hs[t] + cum[t] * entering for t in range(n)], axis=0)
    return out, leaving[last:last + 1, :]


def _sigmoid(x):
    return 0.5 * jnp.tanh(0.5 * x) + 0.5


def _lru_kernel(x_ref, gate_ref, h0_ref, cw_ref, cb_ref, wg_ref, gb_ref, lam_ref, sel_ref, perm_ref, permt_ref,
                y_ref, fin_ref, hf_scr, u_scr):
    t_len = x_ref.shape[1]
    cb = x_ref.shape[2]
    n_chunks = t_len // LRU_TC

    def conv_chunk(c):
        t0 = pl.multiple_of(c * LRU_TC, LRU_TC)
        cur = x_ref[0, pl.ds(t0, LRU_TC), :]
        lo = pl.multiple_of(jnp.maximum(t0 - LRU_HALO, 0), LRU_HALO)
        hi = pl.multiple_of(jnp.minimum(t0 + LRU_TC, t_len - LRU_HALO), LRU_HALO)
        prev = x_ref[0, pl.ds(lo, LRU_HALO), :]
        nxt = x_ref[0, pl.ds(hi, LRU_HALO), :]
        prev = jnp.where(c > 0, prev, jnp.zeros_like(prev))
        nxt = jnp.where(c < n_chunks - 1, nxt, jnp.zeros_like(nxt))
        ext = jnp.concatenate([prev, cur, nxt], axis=0)
        xs = jnp.dot(sel_ref[...], ext, preferred_element_type=F32).reshape(LRU_NSLAB, V7X_SUBLANES, cb)
        u = cb_ref[...][None] + jnp.zeros((LRU_SEG, V7X_SUBLANES, cb), F32)
        for j in range(CONV_W):
            u = u + xs[j:j + LRU_SEG] * cw_ref[j:j + 1, :][None]
        return t0, u.reshape(LRU_TC, cb)

    def slabs(v):
        return v.reshape(LRU_SEG, V7X_SUBLANES, cb)

    def gates(u, d):
        ub = u.astype(BF16)
        pre = [jnp.dot(ub[:, LRU_SUB * s:LRU_SUB * (s + 1)], wg_ref[d, s], preferred_element_type=F32)
               for s in range(cb // LRU_SUB)]
        pre_a = jnp.concatenate([p[:, :LRU_SUB] for p in pre], axis=1)
        pre_x = jnp.concatenate([p[:, LRU_SUB:] for p in pre], axis=1)
        r = _sigmoid(pre_a + gb_ref[2 * d:2 * d + 1, :])
        i = _sigmoid(pre_x + gb_ref[2 * d + 1:2 * d + 2, :])
        log_a = (-LRU_C * jax.nn.softplus(-lam_ref[d:d + 1, :])) * r
        a = jnp.exp(log_a)
        th = jnp.tanh(log_a)
        num = -2.0 * th
        scale = jnp.where(num > 0.0, num * lax.rsqrt(num * (1.0 - th)), 0.0)
        inp = scale * (i * u)
        return a, inp

    def fwd(c, carry):
        t0, u = conv_chunk(c)
        u_scr[pl.ds(t0, LRU_TC), :] = u
        a, inp = gates(u, 0)
        h, carry = _slab_scan(slabs(a), slabs(inp), carry, reverse=False)
        hf_scr[pl.ds(t0, LRU_TC), :] = h
        return carry

    fin_f = lax.fori_loop(0, n_chunks, fwd, h0_ref[0, 0:1, :])

    def bwd(k, carry):
        t0 = pl.multiple_of((n_chunks - 1 - k) * LRU_TC, LRU_TC)
        a, inp = gates(u_scr[pl.ds(t0, LRU_TC), :], 1)
        h, carry = _slab_scan(slabs(a), slabs(inp), carry, reverse=True)
        g = jnp.dot(perm_ref[...], gate_ref[0, pl.ds(t0, LRU_TC), :], preferred_element_type=F32)
        y = ((hf_scr[pl.ds(t0, LRU_TC), :] + h) * jax.nn.gelu(g)).astype(BF16)
        y_ref[0, pl.ds(t0, LRU_TC), :] = jnp.dot(permt_ref[...], y, preferred_element_type=F32).astype(y_ref.dtype)
        return carry

    fin_b = lax.fori_loop(0, n_chunks, bwd, h0_ref[0, 1:2, :])
    fin_ref[0, 0:1, :] = fin_f
    fin_ref[0, 1:2, :] = fin_b


def _lru_gate_weights(lru_wa, lru_wx):
    per = LRU_SUB // LRU_BLOCK
    eye = jnp.eye(per, dtype=F32)

    def dense(w):
        w = w.reshape(2, D_RNN // LRU_SUB, per, LRU_BLOCK, LRU_BLOCK)
        full = w[:, :, :, :, None, :] * eye[None, None, :, None, :, None]
        return full.reshape(2, D_RNN // LRU_SUB, LRU_SUB, LRU_SUB)

    return jnp.concatenate([dense(lru_wa), dense(lru_wx)], axis=-1).astype(BF16)


def lru_pallas(p, h0, conv_w, conv_b, wg, gb, lam):
    b, t, _ = p.shape
    n_cb = D_RNN // LRU_CB
    maps = _lru_row_maps()
    whole = lambda m: pl.BlockSpec(m.shape, lambda bi, ci: (0, 0))
    return pl.pallas_call(
        _lru_kernel,
        grid=(b, n_cb),
        in_specs=[pl.BlockSpec((1, t, LRU_CB), lambda bi, ci: (bi, 0, C_XRNN * n_cb + ci)),
                  pl.BlockSpec((1, t, LRU_CB), lambda bi, ci: (bi, 0, C_GRNN * n_cb + ci)),
                  pl.BlockSpec((1, 2, LRU_CB), lambda bi, ci: (bi, 0, ci)),
                  pl.BlockSpec((CONV_W, LRU_CB), lambda bi, ci: (0, ci)),
                  pl.BlockSpec((1, LRU_CB), lambda bi, ci: (0, ci)),
                  pl.BlockSpec((2, LRU_CB // LRU_SUB, LRU_SUB, 2 * LRU_SUB), lambda bi, ci: (0, ci, 0, 0)),
                  pl.BlockSpec((4, LRU_CB), lambda bi, ci: (0, ci)),
                  pl.BlockSpec((2, LRU_CB), lambda bi, ci: (0, ci))] + [whole(m) for m in maps],
        out_specs=[pl.BlockSpec((1, t, LRU_CB), lambda bi, ci: (bi, 0, ci)),
                   pl.BlockSpec((1, 2, LRU_CB), lambda bi, ci: (bi, 0, ci))],
        out_shape=[jax.ShapeDtypeStruct((b, t, D_RNN), BF16), jax.ShapeDtypeStruct((b, 2, D_RNN), F32)],
        scratch_shapes=[pltpu.VMEM((t, LRU_CB), F32), pltpu.VMEM((t, LRU_CB), F32)],
        compiler_params=_params("arbitrary", "arbitrary"),
        name="lru",
    )(p, p, h0, conv_w, conv_b, wg, gb, lam, *maps)


Q_ROWS = 4
Q_TILE = Q_ROWS * GRID_W
KEY_ROWS = 12
KEY_TILE = KEY_ROWS * GRID_W
N_Q_TILES = DEC_SEQ // Q_TILE
ATT_LANES = 1024
HEADS_PER_STEP = ATT_LANES // HEAD_DIM
PAIR = 2 * HEAD_DIM


def _window_block(t):
    return jnp.clip(t - 1, 0, N_Q_TILES - KEY_ROWS // Q_ROWS)


N_DR_PAIRS = 2 * WIN_H


def _natten_tables(rpb):
    c = np.arange(GRID_W)
    q_start = np.clip(c - WIN_W // 2, 0, GRID_W - WIN_W)
    col_valid = (c[None, :] >= q_start[:, None]) & (c[None, :] < q_start[:, None] + WIN_W)
    dc = c[None, :] - c[:, None] + WIN_W - 1
    sel_c = ((dc[None] == np.arange(2 * WIN_W - 1)[:, None, None]) & col_valid[None]).astype(np.float32)
    t = jnp.einsum('lhrd,dck->lhrck', rpb.astype(F32), jnp.asarray(sel_c), precision=lax.Precision.HIGHEST)
    t = jnp.where(col_valid[None, None, None], t, NEG_INF)
    t = jnp.pad(t, ((0, 0), (0, 0), (1, 1), (0, 0), (0, 0)), constant_values=NEG_INF)
    return jnp.concatenate([t[:, :, :-1], t[:, :, 1:]], axis=-1)


def _attend(q2, keys, vals, biases):
    lane_head = lax.broadcasted_iota(jnp.int32, (1, PAIR), 1) // HEAD_DIM
    acc = jnp.zeros((q2.shape[0], PAIR), F32)
    for j in range(2):
        mine = lane_head == j
        qh = jnp.where(mine, q2, jnp.zeros_like(q2))
        s = []
        for kb, bb in zip(keys, biases[j]):
            sd = lax.dot_general(qh, kb, NT_DIMS, preferred_element_type=F32)
            s.append(sd if bb is None else sd + bb)
        m = s[0].max(axis=-1, keepdims=True)
        for sd in s[1:]:
            m = jnp.maximum(m, sd.max(axis=-1, keepdims=True))
        o = None
        for sd, vb in zip(s, vals):
            pv = jnp.dot(jnp.exp(sd - m).astype(BF16), jnp.where(mine, vb, jnp.ones_like(vb)),
                         preferred_element_type=F32)
            o = pv if o is None else o + pv
        acc = acc + jnp.where(mine, o / pltpu.roll(o, HEAD_DIM, axis=1), 0.0)
    return acc


def _natten_kernel(q_ref, k0_ref, k1_ref, k2_ref, v0_ref, v1_ref, v2_ref, kc_ref, vc_ref, tt_ref, o_ref):
    ti = pl.program_id(1)
    rows = DEC_SEQ // GRID_W
    wstart = _window_block(ti) * Q_ROWS
    first_row = lax.broadcasted_iota(jnp.int32, (1, 2 * GRID_W), 1) < GRID_W
    pieces = {}
    for a in range(Q_ROWS):
        r = ti * Q_ROWS + a
        start_r = jnp.clip(r - WIN_H // 2, 0, rows - WIN_H)
        for i in range(0, KEY_ROWS, 2):
            kr = wstart + i
            ok = [((kr + e >= start_r) & (kr + e < start_r + WIN_H)).astype(jnp.int32) for e in range(2)]
            pieces[a, i] = (jnp.clip(kr - r + WIN_H, 0, N_DR_PAIRS - 1), jnp.where(first_row, ok[0], ok[1]) > 0)

    def bias_block(h, d):
        return jnp.concatenate(
            [jnp.concatenate([jnp.where(pieces[a, i][1], tt_ref[0, h, pieces[a, i][0]], NEG_INF)
                              for i in range(Q_ROWS * d, Q_ROWS * (d + 1), 2)], axis=1)
             for a in range(Q_ROWS)], axis=0)

    k_refs = (k0_ref, k1_ref, k2_ref, kc_ref)
    v_refs = (v0_ref, v1_ref, v2_ref, vc_ref)
    outs = []
    for hp in range(ATT_LANES // PAIR):
        sl = slice(PAIR * hp, PAIR * (hp + 1))
        biases = [[bias_block(2 * hp + j, d) for d in range(3)] + [None] for j in range(2)]
        outs.append(_attend(q_ref[0, :, sl], [r[0, :, sl] for r in k_refs], [r[0, :, sl] for r in v_refs], biases))
    o_ref[0] = jnp.concatenate(outs, axis=1).astype(o_ref.dtype)


def natten_pallas(p, k_ctx, v_ctx, tables, layer):
    b, t, _ = p.shape
    n_lb = D_ATT // ATT_LANES
    q_spec = pl.BlockSpec((1, Q_TILE, ATT_LANES), lambda lb, ti, bi: (bi, ti, C_Q * n_lb + lb))
    win = lambda ch, d: pl.BlockSpec((1, Q_TILE, ATT_LANES),
                                     lambda lb, ti, bi: (bi, _window_block(ti) + d, ch * n_lb + lb))
    ctx = pl.BlockSpec((1, PAST_LEN, ATT_LANES), lambda lb, ti, bi: (bi, 0, lb))
    return pl.pallas_call(
        _natten_kernel,
        grid=(n_lb, N_Q_TILES, b),
        in_specs=[q_spec, win(C_K, 0), win(C_K, 1), win(C_K, 2), win(C_V, 0), win(C_V, 1), win(C_V, 2), ctx, ctx,
                  pl.BlockSpec((1, HEADS_PER_STEP, N_DR_PAIRS, GRID_W, 2 * GRID_W),
                               lambda lb, ti, bi: (layer, lb, 0, 0, 0))],
        out_specs=pl.BlockSpec((1, Q_TILE, ATT_LANES), lambda lb, ti, bi: (bi, ti, lb)),
        out_shape=jax.ShapeDtypeStruct((b, t, D_ATT), BF16),
        compiler_params=_params("arbitrary", "arbitrary", "arbitrary"),
        name="natten",
    )(p, p, p, p, p, p, p, k_ctx, v_ctx, tables)


def _ctxatt_kernel(q_ref, k_ref, v_ref, o_ref):
    outs = []
    for hp in range(ATT_LANES // PAIR):
        sl = slice(PAIR * hp, PAIR * (hp + 1))
        outs.append(_attend(q_ref[0, :, sl], [k_ref[0, :, sl]], [v_ref[0, :, sl]], [[None], [None]]))
    o_ref[0] = jnp.concatenate(outs, axis=1).astype(o_ref.dtype)


def ctxatt_pallas(p):
    b, t, _ = p.shape
    n_lb = D_ATT // ATT_LANES
    blk = lambda ch: pl.BlockSpec((1, t, ATT_LANES), lambda lb, bi: (bi, 0, ch * n_lb + lb))
    return pl.pallas_call(
        _ctxatt_kernel,
        grid=(n_lb, b),
        in_specs=[blk(C_Q), blk(C_K), blk(C_V)],
        out_specs=pl.BlockSpec((1, t, ATT_LANES), lambda lb, bi: (bi, 0, lb)),
        out_shape=jax.ShapeDtypeStruct((b, t, D_ATT), BF16),
        compiler_params=_params("arbitrary", "arbitrary"),
        name="ctxatt",
    )(p, p, p)


def _split_bf16(x):
    hi = x.astype(BF16)
    return hi, (x - hi.astype(F32)).astype(BF16)


def _route(logits_t, rb_ref):
    score = [jax.nn.sigmoid(logits_t[e:e + 1, :]) for e in range(N_EXPERTS)]
    sel = [score[e] + rb_ref[e] for e in range(N_EXPERTS)]
    best_g = None
    for g in range(N_GROUPS):
        v = sel[EXPERTS_PER_GROUP * g:EXPERTS_PER_GROUP * (g + 1)]
        top2 = None
        for i in range(EXPERTS_PER_GROUP):
            for j in range(i + 1, EXPERTS_PER_GROUP):
                pair = v[i] + v[j]
                top2 = pair if top2 is None else jnp.maximum(top2, pair)
        if best_g is None:
            best_g, best_v = jnp.zeros_like(top2, dtype=jnp.int32), top2
        else:
            upd = top2 > best_v
            best_g = jnp.where(upd, g, best_g)
            best_v = jnp.where(upd, top2, best_v)

    def in_best(vals, j):
        out = vals[j]
        for g in range(1, N_GROUPS):
            out = jnp.where(best_g == g, vals[EXPERTS_PER_GROUP * g + j], out)
        return out

    v = [in_best(sel, j) for j in range(EXPERTS_PER_GROUP)]
    sc = [in_best(score, j) for j in range(EXPERTS_PER_GROUP)]

    def first_argmax(vals):
        idx, top = jnp.zeros_like(best_g), vals[0]
        for j in range(1, EXPERTS_PER_GROUP):
            upd = vals[j] > top
            idx = jnp.where(upd, j, idx)
            top = jnp.where(upd, vals[j], top)
        return idx

    i1 = first_argmax(v)
    i2 = first_argmax([jnp.where(i1 == j, -jnp.inf, v[j]) for j in range(EXPERTS_PER_GROUP)])
    pick = lambda idx: sum(jnp.where(idx == j, sc[j], 0.0) for j in range(EXPERTS_PER_GROUP))
    w1, w2 = pick(i1), pick(i2)
    den = w1 + w2
    c1, c2 = w1 / den, w2 / den
    rows = []
    for e in range(N_EXPERTS):
        g, j = divmod(e, EXPERTS_PER_GROUP)
        rows.append(jnp.where(best_g == g, jnp.where(i1 == j, c1, 0.0) + jnp.where(i2 == j, c2, 0.0), 0.0))
    return jnp.concatenate(rows, axis=0), best_g


AUX_LANES = V7X_LANES
AUX_MID = N_EXPERTS
AUX_LO = 2 * N_EXPERTS
AUX_GROUP = 3 * N_EXPERTS
D_MOE_IN = D_MODEL + AUX_LANES


def _merge_kernel(rb_ref, yr_ref, ya_ref, gr_ref, ga_ref, x_ref, mod_ref, g_ref, wr_ref, wa_ref, wo_ref, wrt_ref,
                  x1_ref, h2_ref, gid_ref):
    m = (_sigmoid(gr_ref[...].astype(F32)) * jnp.dot(yr_ref[...], wr_ref[0], preferred_element_type=F32)
         + _sigmoid(ga_ref[...].astype(F32)) * jnp.dot(ya_ref[...], wa_ref[0], preferred_element_type=F32))
    o = jnp.dot(m.astype(BF16), wo_ref[0], preferred_element_type=F32)
    x1 = x_ref[...] + mod_ref[0, 2:3, :] * o
    x1_ref[...] = x1
    h2 = _rms(x1, g_ref[...]) * (1.0 + mod_ref[0, 4:5, :]) + mod_ref[0, 3:4, :]
    h_hi, h_lo = _split_bf16(h2)
    w_hi, w_lo = _split_bf16(wrt_ref[...])
    dot_nt = lambda a, b: lax.dot_general(a, b, NT_DIMS, preferred_element_type=F32)
    by_h_hi = dot_nt(jnp.concatenate([w_hi, w_lo], axis=0), h_hi)
    logits_t = by_h_hi[:N_EXPERTS] + (dot_nt(w_hi, h_lo) + by_h_hi[N_EXPERTS:])
    comb_t, group = _route(logits_t, rb_ref)
    tm = comb_t.shape[1]
    group = group.astype(F32)
    gid_ref[...] = jnp.concatenate([group, jnp.zeros((V7X_SUBLANES - 1, tm), F32)], axis=0)
    padded = jnp.concatenate([comb_t, jnp.zeros((AUX_GROUP - N_EXPERTS, tm), F32), group,
                              jnp.zeros((AUX_LANES - AUX_GROUP - 1, tm), F32)], axis=0)
    c = padded.T
    c_hi = c.astype(BF16).astype(F32)
    c_mid = (c - c_hi).astype(BF16).astype(F32)
    c_lo = c - c_hi - c_mid
    aux = c_hi + pltpu.roll(c_mid, AUX_MID, axis=1) + pltpu.roll(c_lo, AUX_LO, axis=1)
    h2_ref[...] = jnp.concatenate([h2.astype(BF16), aux.astype(BF16)], axis=1)


def merge_pallas(y_rnn, y_att, p, x, mod, g, w_br_rnn, w_br_att, w_out, w_router_t, router_bias, layer, seq_len,
                 per_seq, tm):
    n = x.shape[0]
    row = lambda i: (i, 0)
    full = lambda shape: pl.BlockSpec(shape, lambda i: (0, 0))
    weight = pl.BlockSpec((1, D_MODEL, D_MODEL), lambda i: (layer, 0, 0))
    return pl.pallas_call(
        _merge_kernel,
        grid=(n // tm,),
        in_specs=[pl.BlockSpec(memory_space=pltpu.SMEM),
                  pl.BlockSpec((tm, D_RNN), row), pl.BlockSpec((tm, D_ATT), row),
                  pl.BlockSpec((tm, CHUNK), lambda i: (i, C_GATE_R)), pl.BlockSpec((tm, CHUNK), lambda i: (i, C_GATE_A)),
                  pl.BlockSpec((tm, D_MODEL), row), _mod_spec(tm, seq_len, per_seq), full((1, D_MODEL)),
                  weight, weight, weight, full((N_EXPERTS, D_MODEL))],
        out_specs=[pl.BlockSpec((tm, D_MODEL), row), pl.BlockSpec((tm, D_MOE_IN), row),
                   pl.BlockSpec((V7X_SUBLANES, tm), lambda i: (0, i))],
        out_shape=[jax.ShapeDtypeStruct((n, D_MODEL), F32), jax.ShapeDtypeStruct((n, D_MOE_IN), BF16),
                   jax.ShapeDtypeStruct((V7X_SUBLANES, n), F32)],
        compiler_params=_params("arbitrary"),
        name="merge",
    )(router_bias, y_rnn, y_att, p, p, x, mod, g, w_br_rnn, w_br_att, w_out, w_router_t)


MOE_TB = 1024
MOE_RT = 128
MOE_TBP = MOE_TB + N_GROUPS * MOE_RT
MOE_EPS = 2


def _group_segments(gid_row):
    sub = lax.broadcasted_iota(jnp.int32, (V7X_SUBLANES, 1), 0).astype(F32)
    onehot = (gid_row == sub).astype(F32)
    cnt = jnp.sum(onehot, axis=1, keepdims=True)
    padded = jnp.floor((cnt + (MOE_RT - 1)) * (1.0 / MOE_RT)) * MOE_RT
    starts, run = [], jnp.zeros((1, 1), F32)
    for g in range(N_GROUPS):
        starts.append(run)
        run = run + padded[g:g + 1, :]
    return onehot, starts, [padded[g:g + 1, :] for g in range(N_GROUPS)], run


def _to_int(v):
    return v[0, 0].astype(jnp.int32)


def _moe_kernel(h_ref, gid_ref, x_ref, mod_ref, gf_ref, wg_ref, wu_ref, wd_ref, o_ref,
                p_scr, xs_scr, cs_scr, ys_scr, *, final_norm):
    step = pl.program_id(1)
    onehot, starts, sizes, used = _group_segments(gid_ref[0:1, :])
    lane = lax.broadcasted_iota(jnp.int32, (1, AUX_LANES), 1)

    @pl.when(step == 0)
    def _():
        t_row = lax.broadcasted_iota(jnp.int32, (MOE_TB, MOE_TB), 0)
        t_col = lax.broadcasted_iota(jnp.int32, (MOE_TB, MOE_TB), 1)
        earlier = (t_row < t_col).astype(BF16)
        rank = jnp.dot(onehot.astype(BF16), earlier, preferred_element_type=F32)
        pos = jnp.zeros((1, MOE_TB), F32)
        for g in range(N_GROUPS):
            pos = pos + onehot[g:g + 1, :] * (starts[g] + rank[g:g + 1, :])
        dest = lax.broadcasted_iota(jnp.int32, (MOE_TBP, 1), 0).astype(F32)
        p_scr[...] = (dest == pos).astype(BF16)
        for r0 in range(0, MOE_TBP, V7X_MXU_DIM):
            rows = pl.ds(r0, V7X_MXU_DIM)
            sorted_rows = jnp.dot(p_scr[rows, :], h_ref[...], preferred_element_type=F32)
            xs_scr[rows, :] = sorted_rows[:, :D_MODEL].astype(BF16)
            aux = sorted_rows[:, D_MODEL:]
            cs_scr[rows, :] = (aux + pltpu.roll(aux, AUX_LANES - AUX_MID, axis=1)
                               + pltpu.roll(aux, AUX_LANES - AUX_LO, axis=1))
        first_free = pl.multiple_of(_to_int(used), MOE_RT)

        def clear(k, carry):
            ys_scr[pl.ds(pl.multiple_of(first_free + k * MOE_RT, MOE_RT), MOE_RT), :] = jnp.zeros(
                (MOE_RT, D_MODEL), F32)
            return carry

        lax.fori_loop(0, (MOE_TBP - first_free) // MOE_RT, clear, 0)

    steps_per_group = EXPERTS_PER_GROUP // MOE_EPS
    group = step // steps_per_group
    first_of_group = step % steps_per_group == 0
    start_v, size_v = starts[0], sizes[0]
    for g in range(1, N_GROUPS):
        start_v = jnp.where(group == g, starts[g], start_v)
        size_v = jnp.where(group == g, sizes[g], size_v)
    seg_start = _to_int(start_v)
    n_tiles = _to_int(size_v) // MOE_RT

    def run_expert(r0, n_rows):
        rows = pl.ds(pl.multiple_of(r0, MOE_RT), n_rows)
        x = xs_scr[rows, :]
        cs = cs_scr[rows, :]
        y = None
        for k in range(MOE_EPS):
            gate = jnp.dot(x, wg_ref[0, k], preferred_element_type=F32)
            up = jnp.dot(x, wu_ref[0, k], preferred_element_type=F32)
            act = (gate * _sigmoid(gate)) * up
            yk = jnp.dot(act.astype(BF16), wd_ref[0, k], preferred_element_type=F32)
            yk = jnp.sum(jnp.where(lane == step * MOE_EPS + k, cs, 0.0), axis=-1, keepdims=True) * yk
            y = yk if y is None else y + yk

        @pl.when(first_of_group)
        def _():
            ys_scr[rows, :] = y

        @pl.when(jnp.logical_not(first_of_group))
        def _():
            ys_scr[rows, :] += y

    def pair(k, carry):
        run_expert(seg_start + k * (2 * MOE_RT), 2 * MOE_RT)
        return carry

    lax.fori_loop(0, n_tiles // 2, pair, 0)

    @pl.when(n_tiles % 2 == 1)
    def _():
        run_expert(seg_start + (n_tiles - 1) * MOE_RT, MOE_RT)

    @pl.when(step == N_EXPERTS // MOE_EPS - 1)
    def _():
        y = lax.dot_general(p_scr[...], ys_scr[...].astype(BF16), (((0,), (0,)), ((), ())),
                            preferred_element_type=F32)
        x2 = x_ref[...] + mod_ref[0, 5:6, :] * y
        o_ref[...] = _rms(x2, gf_ref[...]) if final_norm else x2


def moe_pallas(h2x, gid, x1, mod, g_final, w_gate, w_up, w_down, layer, seq_len, per_seq, final_norm):
    n = x1.shape[0]
    row = lambda i, e: (i, 0)
    expert = lambda i, e: (layer, e, 0, 0)
    return pl.pallas_call(
        functools.partial(_moe_kernel, final_norm=final_norm),
        grid=(n // MOE_TB, N_EXPERTS // MOE_EPS),
        in_specs=[pl.BlockSpec((MOE_TB, D_MOE_IN), row), pl.BlockSpec((V7X_SUBLANES, MOE_TB), lambda i, e: (0, i)),
                  pl.BlockSpec((MOE_TB, D_MODEL), row), _mod_spec(MOE_TB, seq_len, per_seq),
                  pl.BlockSpec((1, D_MODEL), lambda i, e: (0, 0)),
                  pl.BlockSpec((1, MOE_EPS, D_MODEL, D_EXPERT), expert),
                  pl.BlockSpec((1, MOE_EPS, D_MODEL, D_EXPERT), expert),
                  pl.BlockSpec((1, MOE_EPS, D_EXPERT, D_MODEL), expert)],
        out_specs=pl.BlockSpec((MOE_TB, D_MODEL), row),
        out_shape=jax.ShapeDtypeStruct((n, D_MODEL), F32),
        scratch_shapes=[pltpu.VMEM((MOE_TBP, MOE_TB), BF16), pltpu.VMEM((MOE_TBP, D_MODEL), BF16),
                        pltpu.VMEM((MOE_TBP, AUX_LANES), F32), pltpu.VMEM((MOE_TBP, D_MODEL), F32)],
        compiler_params=_params("arbitrary", "arbitrary"),
        name="moe",
    )(h2x, gid, x1, mod, g_final, w_gate, w_up, w_down)


TM_PROJ = 2048
TM_MERGE = 512


def _layer(x, mod, seq_len, per_seq, lw, h0, ctx_kv, tables, layer):
    n = x.shape[0]
    b = n // seq_len
    emit_kv = ctx_kv is None
    outs = inproj_pallas(x, mod, lw['norm_g'][0:1], lw['w_in'], layer, seq_len, per_seq,
                         TM_PROJ // 2 if emit_kv else TM_PROJ, emit_kv)
    p = outs[0]
    p3 = p.reshape(b, seq_len, D_IN)
    y_rnn, h_fin = lru_pallas(p3, h0, lw['conv_w'], lw['conv_b'], lw['wg'], lw['gb'], lw['lam'])
    if emit_kv:
        y_att = ctxatt_pallas(p3)
    else:
        y_att = natten_pallas(p3, ctx_kv[0], ctx_kv[1], tables, layer)
    x1, h2x, gid = merge_pallas(y_rnn.reshape(n, D_RNN), y_att.reshape(n, D_ATT), p, x, mod, lw['norm_g'][1:2],
                                lw['w_br_rnn'], lw['w_br_att'], lw['w_out'], lw['w_router_t'], lw['router_bias'],
                                layer, seq_len, per_seq, TM_MERGE)
    x2 = moe_pallas(h2x, gid, x1, mod, lw['g_final'], lw['w_exp_gate'], lw['w_exp_up'], lw['w_exp_down'], layer,
                    seq_len, per_seq, layer == DEPTH - 1)
    kv = (outs[1], outs[2]) if emit_kv else None
    return x2, kv, h_fin


def kernel(x_prompt, x_sample, cache_k, cache_v, state_lru, c, c_ctx, w_ada, b_ada, norm_g, w_in, conv_w,
           conv_b, lru_wa, lru_ba, lru_wx, lru_bx, lru_lam, rpb, w_br_rnn, w_br_att, w_out, w_router,
           router_bias, w_exp_gate, w_exp_up, w_exp_down, final_norm_g):
    cvecs = jnp.concatenate([c, c_ctx[None, :], jnp.zeros((V7X_SUBLANES - DEC_BATCH - 1, D_MODEL), F32)], axis=0)
    mods = adaln_pallas(cvecs, w_ada, b_ada).reshape(DEPTH, V7X_SUBLANES, N_MOD, D_MODEL)
    tables = _natten_tables(rpb)
    w_router_t = w_router.T
    xp = x_prompt.reshape(BATCH * SEQ, D_MODEL)
    xs = x_sample.reshape(DEC_BATCH * DEC_SEQ, D_MODEL)
    zeros_h0 = jnp.zeros((BATCH, 2, D_RNN), F32)
    w_exp = [w.astype(BF16) for w in (w_exp_gate, w_exp_up, w_exp_down)]
    w_proj = [w.astype(BF16) for w in (w_in, w_br_rnn, w_br_att, w_out)]
    ks, vs, hs = [], [], []
    for l in range(DEPTH):
        lw = dict(
            norm_g=norm_g[l], w_in=w_proj[0], conv_w=conv_w[l], conv_b=conv_b[l][None, :],
            wg=_lru_gate_weights(lru_wa[l], lru_wx[l]),
            gb=jnp.stack([lru_ba[l, 0], lru_bx[l, 0], lru_ba[l, 1], lru_bx[l, 1]], axis=0), lam=lru_lam[l],
            w_br_rnn=w_proj[1], w_br_att=w_proj[2], w_out=w_proj[3],
            w_router_t=w_router_t, router_bias=router_bias, g_final=final_norm_g[None, :],
            w_exp_gate=w_exp[0], w_exp_up=w_exp[1], w_exp_down=w_exp[2])
        xp, kv, h_l = _layer(xp, mods[l, DEC_BATCH:DEC_BATCH + 1], SEQ, False, lw, zeros_h0, None, None, l)
        ks.append(kv[0].reshape(BATCH, SEQ, N_HEADS, HEAD_DIM))
        vs.append(kv[1].reshape(BATCH, SEQ, N_HEADS, HEAD_DIM))
        hs.append(h_l)
        ctx_kv = (cache_k[:, l].reshape(DEC_BATCH, PAST_LEN, D_ATT).astype(BF16),
                  cache_v[:, l].reshape(DEC_BATCH, PAST_LEN, D_ATT).astype(BF16))
        xs, _, _ = _layer(xs, mods[l, :DEC_BATCH], DEC_SEQ, True, lw, state_lru[:, l], ctx_kv, tables, l)
    y_prompt = xp.reshape(BATCH, SEQ, D_MODEL)
    y_sample = xs.reshape(DEC_BATCH, DEC_SEQ, D_MODEL)
    return (y_prompt, y_sample, jnp.stack(ks, axis=1), jnp.stack(vs, axis=1), jnp.stack(hs, axis=1))
```

```python
import functools

import jax
import jax.numpy as jnp
import numpy as np
from jax import lax
from jax.experimental import pallas as pl
from jax.experimental.pallas import tpu as pltpu

D_MODEL = 1024
BATCH = 16
SEQ = 256
DEPTH = 2
DEC_BATCH = 4
DEC_SEQ = 4096
PAST_LEN = 256

GRID_W = 64
D_RNN = 1024
N_LRU_BLOCKS = 16
LRU_BLOCK = D_RNN // N_LRU_BLOCKS
CONV_W = 4
LRU_C = 8.0
N_HEADS = 16
HEAD_DIM = 64
D_ATT = N_HEADS * HEAD_DIM
WIN_H = 8
WIN_W = 16
N_EXPERTS = 16
N_GROUPS = 4
EXPERTS_PER_GROUP = N_EXPERTS // N_GROUPS
D_EXPERT = 512
N_MOD = 6
D_IN = 2 * D_RNN + 3 * D_ATT + 2 * D_MODEL
EPS = 1e-6
NEG_INF = -1e30

BF16 = jnp.bfloat16
F32 = jnp.float32

V7X_LANES = 128
V7X_SUBLANES = 8
V7X_MXU_DIM = 256
V7X_VMEM_BYTES = 64 * 1024 * 1024
VMEM_LIMIT = V7X_VMEM_BYTES - 8 * 1024 * 1024

CHUNK = D_MODEL
N_CHUNKS = D_IN // CHUNK
C_XRNN, C_GRNN, C_Q, C_K, C_V, C_GATE_R, C_GATE_A = range(N_CHUNKS)

NT_DIMS = (((1,), (1,)), ((), ()))


def _params(*sem):
    return pltpu.CompilerParams(dimension_semantics=sem, vmem_limit_bytes=VMEM_LIMIT)


def _adaln_kernel(c_ref, w_ref, b_ref, o_ref):
    cv = c_ref[...]
    s = cv * jax.nn.sigmoid(cv)
    o_ref[0] = jnp.dot(s.astype(BF16), w_ref[0].astype(BF16), preferred_element_type=F32) + b_ref[0]


def adaln_pallas(cvecs, w_ada, b_ada):
    r = cvecs.shape[0]
    return pl.pallas_call(
        _adaln_kernel,
        grid=(DEPTH, N_MOD),
        in_specs=[pl.BlockSpec((r, D_MODEL), lambda l, j: (0, 0)),
                  pl.BlockSpec((1, D_MODEL, D_MODEL), lambda l, j: (l, 0, j)),
                  pl.BlockSpec((1, 1, D_MODEL), lambda l, j: (l, 0, j))],
        out_specs=pl.BlockSpec((1, r, D_MODEL), lambda l, j: (l, 0, j)),
        out_shape=jax.ShapeDtypeStruct((DEPTH, r, N_MOD * D_MODEL), F32),
        compiler_params=_params("arbitrary", "arbitrary"),
        name="adaln",
    )(cvecs, w_ada, b_ada.reshape(DEPTH, 1, N_MOD * D_MODEL))


def _mod_spec(tm, seq_len, per_seq):
    if per_seq:
        return pl.BlockSpec((1, N_MOD, D_MODEL), lambda i, *_: (i * tm // seq_len, 0, 0))
    return pl.BlockSpec((1, N_MOD, D_MODEL), lambda i, *_: (0, 0, 0))


def _rms(x, g):
    return x * lax.rsqrt(jnp.mean(x * x, axis=-1, keepdims=True) + EPS) * g


def _inproj_kernel(x_ref, mod_ref, g_ref, w_ref, *refs, emit_kv):
    if emit_kv:
        p_ref, k32_ref, v32_ref, h_scr = refs
    else:
        p_ref, h_scr = refs
    j = pl.program_id(1)

    @pl.when(j == 0)
    def _():
        y = _rms(x_ref[...], g_ref[...])
        h_scr[...] = (y * (1.0 + mod_ref[0, 1:2, :]) + mod_ref[0, 0:1, :]).astype(BF16)

    acc = jnp.dot(h_scr[...], w_ref[0], preferred_element_type=F32)
    p_ref[...] = (acc * jnp.where(j == C_Q, HEAD_DIM ** -0.5, 1.0)).astype(BF16)
    if emit_kv:
        @pl.when(j == C_K)
        def _():
            k32_ref[...] = acc

        @pl.when(j == C_V)
        def _():
            v32_ref[...] = acc


def inproj_pallas(x, mod, g, w_in, layer, seq_len, per_seq, tm, emit_kv):
    n = x.shape[0]
    row = lambda i, j: (i, 0)
    out_shape = [jax.ShapeDtypeStruct((n, D_IN), BF16)]
    out_specs = [pl.BlockSpec((tm, CHUNK), lambda i, j: (i, j))]
    if emit_kv:
        out_shape += [jax.ShapeDtypeStruct((n, D_ATT), F32)] * 2
        out_specs += [pl.BlockSpec((tm, D_ATT), row)] * 2
    return pl.pallas_call(
        functools.partial(_inproj_kernel, emit_kv=emit_kv),
        grid=(n // tm, N_CHUNKS),
        in_specs=[pl.BlockSpec((tm, D_MODEL), row), _mod_spec(tm, seq_len, per_seq),
                  pl.BlockSpec((1, D_MODEL), lambda i, j: (0, 0)),
                  pl.BlockSpec((1, D_MODEL, CHUNK), lambda i, j: (layer, 0, j))],
        out_specs=out_specs,
        out_shape=out_shape,
        scratch_shapes=[pltpu.VMEM((tm, D_MODEL), BF16)],
        compiler_params=_params("arbitrary", "arbitrary"),
        name="inproj",
    )(x, mod, g, w_in)


LRU_CB = 512
LRU_TC = 256
LRU_SUB = V7X_MXU_DIM
LRU_HALO = 16


LRU_SEG = LRU_TC // V7X_SUBLANES
LRU_NSLAB = LRU_SEG + CONV_W


def _lru_row_maps():
    s = np.arange(V7X_SUBLANES)[None, :]
    src = (LRU_HALO - CONV_W // 2) + LRU_SEG * s + np.arange(LRU_NSLAB)[:, None]
    sel = np.zeros((LRU_NSLAB * V7X_SUBLANES, LRU_TC + 2 * LRU_HALO), np.float32)
    sel[np.arange(sel.shape[0]), src.reshape(-1)] = 1.0
    tok = (LRU_SEG * s + np.arange(LRU_SEG)[:, None]).reshape(-1)
    perm = np.zeros((LRU_TC, LRU_TC), np.float32)
    perm[np.arange(LRU_TC), tok] = 1.0
    return jnp.asarray(sel, BF16), jnp.asarray(perm, BF16), jnp.asarray(perm.T, BF16)


def _slab_scan(a3, b3, carry, reverse):
    n = a3.shape[0]
    hs, cum = [None] * n, [None] * n
    h = cp = None
    for t in (range(n - 1, -1, -1) if reverse else range(n)):
        h = b3[t] if h is None else a3[t] * h + b3[t]
        cp = a3[t] if cp is None else cp * a3[t]
        hs[t], cum[t] = h, cp
    sub = lax.broadcasted_iota(jnp.int32, (V7X_SUBLANES, 1), 0)
    pa, pb = cp, h
    s = 1
    while s < V7X_SUBLANES:
        shift = V7X_SUBLANES - s if reverse else s
        ok = (sub < V7X_SUBLANES - s) if reverse else (sub >= s)
        a_sh = pltpu.roll(pa, shift, axis=0)
        b_sh = pltpu.roll(pb, shift, axis=0)
        pb = jnp.where(ok, pa * b_sh + pb, pb)
        pa = jnp.where(ok, pa * a_sh, pa)
        s *= 2
    leaving = pb + pa * carry
    first, last = (V7X_SUBLANES - 1, 0) if reverse else (0, V7X_SUBLANES - 1)
    entering = jnp.where(sub == first, carry, pltpu.roll(leaving, V7X_SUBLANES - 1 if reverse else 1, axis=0))
    out = jnp.concatenate([hs[t] + cum[t] * entering for t in range(n)], axis=0)
    return out, leaving[last:last + 1, :]


def _sigmoid(x):
    return 0.5 * jnp.tanh(0.5 * x) + 0.5


def _lru_kernel(x_ref, gate_ref, h0_ref, cw_ref, cb_ref, wg_ref, gb_ref, lam_ref, sel_ref, perm_ref, permt_ref,
                y_ref, fin_ref, hf_scr, u_scr):
    t_len = x_ref.shape[1]
    cb = x_ref.shape[2]
    n_chunks = t_len // LRU_TC

    def conv_chunk(c):
        t0 = pl.multiple_of(c * LRU_TC, LRU_TC)
        cur = x_ref[0, pl.ds(t0, LRU_TC), :]
        lo = pl.multiple_of(jnp.maximum(t0 - LRU_HALO, 0), LRU_HALO)
        hi = pl.multiple_of(jnp.minimum(t0 + LRU_TC, t_len - LRU_HALO), LRU_HALO)
        prev = x_ref[0, pl.ds(lo, LRU_HALO), :]
        nxt = x_ref[0, pl.ds(hi, LRU_HALO), :]
        prev = jnp.where(c > 0, prev, jnp.zeros_like(prev))
        nxt = jnp.where(c < n_chunks - 1, nxt, jnp.zeros_like(nxt))
        ext = jnp.concatenate([prev, cur, nxt], axis=0)
        xs = jnp.dot(sel_ref[...], ext, preferred_element_type=F32).reshape(LRU_NSLAB, V7X_SUBLANES, cb)
        u = cb_ref[...][None] + jnp.zeros((LRU_SEG, V7X_SUBLANES, cb), F32)
        for j in range(CONV_W):
            u = u + xs[j:j + LRU_SEG] * cw_ref[j:j + 1, :][None]
        return t0, u.reshape(LRU_TC, cb)

    def slabs(v):
        return v.reshape(LRU_SEG, V7X_SUBLANES, cb)

    def gates(u, d):
        ub = u.astype(BF16)
        pre = [jnp.dot(ub[:, LRU_SUB * s:LRU_SUB * (s + 1)], wg_ref[d, s], preferred_element_type=F32)
               for s in range(cb // LRU_SUB)]
        pre_a = jnp.concatenate([p[:, :LRU_SUB] for p in pre], axis=1)
        pre_x = jnp.concatenate([p[:, LRU_SUB:] for p in pre], axis=1)
        r = _sigmoid(pre_a + gb_ref[2 * d:2 * d + 1, :])
        i = _sigmoid(pre_x + gb_ref[2 * d + 1:2 * d + 2, :])
        log_a = (-LRU_C * jax.nn.softplus(-lam_ref[d:d + 1, :])) * r
        a = jnp.exp(log_a)
        th = jnp.tanh(log_a)
        num = -2.0 * th
        scale = jnp.where(num > 0.0, num * lax.rsqrt(num * (1.0 - th)), 0.0)
        inp = scale * (i * u)
        return a, inp

    def fwd(c, carry):
        t0, u = conv_chunk(c)
        u_scr[pl.ds(t0, LRU_TC), :] = u
        a, inp = gates(u, 0)
        h, carry = _slab_scan(slabs(a), slabs(inp), carry, reverse=False)
        hf_scr[pl.ds(t0, LRU_TC), :] = h
        return carry

    unroll = 4 if n_chunks % 4 == 0 else 1
    fin_f = lax.fori_loop(0, n_chunks, fwd, h0_ref[0, 0:1, :], unroll=unroll)

    def bwd(k, carry):
        t0 = pl.multiple_of((n_chunks - 1 - k) * LRU_TC, LRU_TC)
        a, inp = gates(u_scr[pl.ds(t0, LRU_TC), :], 1)
        h, carry = _slab_scan(slabs(a), slabs(inp), carry, reverse=True)
        g = jnp.dot(perm_ref[...], gate_ref[0, pl.ds(t0, LRU_TC), :], preferred_element_type=F32)
        y = ((hf_scr[pl.ds(t0, LRU_TC), :] + h) * jax.nn.gelu(g)).astype(BF16)
        y_ref[0, pl.ds(t0, LRU_TC), :] = jnp.dot(permt_ref[...], y, preferred_element_type=F32).astype(y_ref.dtype)
        return carry

    fin_b = lax.fori_loop(0, n_chunks, bwd, h0_ref[0, 1:2, :], unroll=unroll)
    fin_ref[0, 0:1, :] = fin_f
    fin_ref[0, 1:2, :] = fin_b


def _lru_gate_weights(lru_wa, lru_wx):
    per = LRU_SUB // LRU_BLOCK
    eye = jnp.eye(per, dtype=F32)

    def dense(w):
        w = w.reshape(2, D_RNN // LRU_SUB, per, LRU_BLOCK, LRU_BLOCK)
        full = w[:, :, :, :, None, :] * eye[None, None, :, None, :, None]
        return full.reshape(2, D_RNN // LRU_SUB, LRU_SUB, LRU_SUB)

    return jnp.concatenate([dense(lru_wa), dense(lru_wx)], axis=-1).astype(BF16)


def lru_pallas(p, h0, conv_w, conv_b, wg, gb, lam):
    b, t, _ = p.shape
    n_cb = D_RNN // LRU_CB
    maps = _lru_row_maps()
    whole = lambda m: pl.BlockSpec(m.shape, lambda bi, ci: (0, 0))
    return pl.pallas_call(
        _lru_kernel,
        grid=(b, n_cb),
        in_specs=[pl.BlockSpec((1, t, LRU_CB), lambda bi, ci: (bi, 0, C_XRNN * n_cb + ci)),
                  pl.BlockSpec((1, t, LRU_CB), lambda bi, ci: (bi, 0, C_GRNN * n_cb + ci)),
                  pl.BlockSpec((1, 2, LRU_CB), lambda bi, ci: (bi, 0, ci)),
                  pl.BlockSpec((CONV_W, LRU_CB), lambda bi, ci: (0, ci)),
                  pl.BlockSpec((1, LRU_CB), lambda bi, ci: (0, ci)),
                  pl.BlockSpec((2, LRU_CB // LRU_SUB, LRU_SUB, 2 * LRU_SUB), lambda bi, ci: (0, ci, 0, 0)),
                  pl.BlockSpec((4, LRU_CB), lambda bi, ci: (0, ci)),
                  pl.BlockSpec((2, LRU_CB), lambda bi, ci: (0, ci))] + [whole(m) for m in maps],
        out_specs=[pl.BlockSpec((1, t, LRU_CB), lambda bi, ci: (bi, 0, ci)),
                   pl.BlockSpec((1, 2, LRU_CB), lambda bi, ci: (bi, 0, ci))],
        out_shape=[jax.ShapeDtypeStruct((b, t, D_RNN), BF16), jax.ShapeDtypeStruct((b, 2, D_RNN), F32)],
        scratch_shapes=[pltpu.VMEM((t, LRU_CB), F32), pltpu.VMEM((t, LRU_CB), F32)],
        compiler_params=_params("arbitrary", "arbitrary"),
        name="lru",
    )(p, p, h0, conv_w, conv_b, wg, gb, lam, *maps)


Q_ROWS = 4
Q_TILE = Q_ROWS * GRID_W
KEY_ROWS = 12
KEY_TILE = KEY_ROWS * GRID_W
N_Q_TILES = DEC_SEQ // Q_TILE
ATT_LANES = 1024
HEADS_PER_STEP = ATT_LANES // HEAD_DIM
PAIR = 2 * HEAD_DIM


def _window_block(t):
    return jnp.clip(t - 1, 0, N_Q_TILES - KEY_ROWS // Q_ROWS)


N_DR_PAIRS = 2 * WIN_H


def _natten_tables(rpb):
    c = np.arange(GRID_W)
    q_start = np.clip(c - WIN_W // 2, 0, GRID_W - WIN_W)
    col_valid = (c[None, :] >= q_start[:, None]) & (c[None, :] < q_start[:, None] + WIN_W)
    dc = c[None, :] - c[:, None] + WIN_W - 1
    sel_c = ((dc[None] == np.arange(2 * WIN_W - 1)[:, None, None]) & col_valid[None]).astype(np.float32)
    t = jnp.einsum('lhrd,dck->lhrck', rpb.astype(F32), jnp.asarray(sel_c), precision=lax.Precision.HIGHEST)
    t = jnp.where(col_valid[None, None, None], t, NEG_INF)
    t = jnp.pad(t, ((0, 0), (0, 0), (1, 1), (0, 0), (0, 0)), constant_values=NEG_INF)
    return jnp.concatenate([t[:, :, :-1], t[:, :, 1:]], axis=-1)


def _attend(q2, keys, vals, biases):
    lane_head = lax.broadcasted_iota(jnp.int32, (1, PAIR), 1) // HEAD_DIM
    acc = jnp.zeros((q2.shape[0], PAIR), F32)
    for j in range(2):
        mine = lane_head == j
        qh = jnp.where(mine, q2, jnp.zeros_like(q2))
        s = []
        for kb, bb in zip(keys, biases[j]):
            sd = lax.dot_general(qh, kb, NT_DIMS, preferred_element_type=F32)
            s.append(sd if bb is None else sd + bb)
        m = s[0].max(axis=-1, keepdims=True)
        for sd in s[1:]:
            m = jnp.maximum(m, sd.max(axis=-1, keepdims=True))
        o = None
        for sd, vb in zip(s, vals):
            pv = jnp.dot(jnp.exp(sd - m).astype(BF16), jnp.where(mine, vb, jnp.ones_like(vb)),
                         preferred_element_type=F32)
            o = pv if o is None else o + pv
        acc = acc + jnp.where(mine, o / pltpu.roll(o, HEAD_DIM, axis=1), 0.0)
    return acc


def _natten_kernel(q_ref, k0_ref, k1_ref, k2_ref, v0_ref, v1_ref, v2_ref, kc_ref, vc_ref, tt_ref, o_ref):
    ti = pl.program_id(1)
    rows = DEC_SEQ // GRID_W
    wstart = _window_block(ti) * Q_ROWS
    first_row = lax.broadcasted_iota(jnp.int32, (1, 2 * GRID_W), 1) < GRID_W
    pieces = {}
    for a in range(Q_ROWS):
        r = ti * Q_ROWS + a
        start_r = jnp.clip(r - WIN_H // 2, 0, rows - WIN_H)
        for i in range(0, KEY_ROWS, 2):
            kr = wstart + i
            ok = [((kr + e >= start_r) & (kr + e < start_r + WIN_H)).astype(jnp.int32) for e in range(2)]
            pieces[a, i] = (jnp.clip(kr - r + WIN_H, 0, N_DR_PAIRS - 1), jnp.where(first_row, ok[0], ok[1]) > 0)

    def bias_block(h, d):
        return jnp.concatenate(
            [jnp.concatenate([jnp.where(pieces[a, i][1], tt_ref[0, h, pieces[a, i][0]], NEG_INF)
                              for i in range(Q_ROWS * d, Q_ROWS * (d + 1), 2)], axis=1)
             for a in range(Q_ROWS)], axis=0)

    k_refs = (k0_ref, k1_ref, k2_ref, kc_ref)
    v_refs = (v0_ref, v1_ref, v2_ref, vc_ref)
    outs = []
    for hp in range(ATT_LANES // PAIR):
        sl = slice(PAIR * hp, PAIR * (hp + 1))
        biases = [[bias_block(2 * hp + j, d) for d in range(3)] + [None] for j in range(2)]
        outs.append(_attend(q_ref[0, :, sl], [r[0, :, sl] for r in k_refs], [r[0, :, sl] for r in v_refs], biases))
    o_ref[0] = jnp.concatenate(outs, axis=1).astype(o_ref.dtype)


def natten_pallas(p, k_ctx, v_ctx, tables, layer):
    b, t, _ = p.shape
    n_lb = D_ATT // ATT_LANES
    q_spec = pl.BlockSpec((1, Q_TILE, ATT_LANES), lambda lb, ti, bi: (bi, ti, C_Q * n_lb + lb))
    win = lambda ch, d: pl.BlockSpec((1, Q_TILE, ATT_LANES),
                                     lambda lb, ti, bi: (bi, _window_block(ti) + d, ch * n_lb + lb))
    ctx = pl.BlockSpec((1, PAST_LEN, ATT_LANES), lambda lb, ti, bi: (bi, 0, lb))
    return pl.pallas_call(
        _natten_kernel,
        grid=(n_lb, N_Q_TILES, b),
        in_specs=[q_spec, win(C_K, 0), win(C_K, 1), win(C_K, 2), win(C_V, 0), win(C_V, 1), win(C_V, 2), ctx, ctx,
                  pl.BlockSpec((1, HEADS_PER_STEP, N_DR_PAIRS, GRID_W, 2 * GRID_W),
                               lambda lb, ti, bi: (layer, lb, 0, 0, 0))],
        out_specs=pl.BlockSpec((1, Q_TILE, ATT_LANES), lambda lb, ti, bi: (bi, ti, lb)),
        out_shape=jax.ShapeDtypeStruct((b, t, D_ATT), BF16),
        compiler_params=_params("arbitrary", "arbitrary", "arbitrary"),
        name="natten",
    )(p, p, p, p, p, p, p, k_ctx, v_ctx, tables)


def _ctxatt_kernel(q_ref, k_ref, v_ref, o_ref):
    outs = []
    for hp in range(ATT_LANES // PAIR):
        sl = slice(PAIR * hp, PAIR * (hp + 1))
        outs.append(_attend(q_ref[0, :, sl], [k_ref[0, :, sl]], [v_ref[0, :, sl]], [[None], [None]]))
    o_ref[0] = jnp.concatenate(outs, axis=1).astype(o_ref.dtype)


def ctxatt_pallas(p):
    b, t, _ = p.shape
    n_lb = D_ATT // ATT_LANES
    blk = lambda ch: pl.BlockSpec((1, t, ATT_LANES), lambda lb, bi: (bi, 0, ch * n_lb + lb))
    return pl.pallas_call(
        _ctxatt_kernel,
        grid=(n_lb, b),
        in_specs=[blk(C_Q), blk(C_K), blk(C_V)],
        out_specs=pl.BlockSpec((1, t, ATT_LANES), lambda lb, bi: (bi, 0, lb)),
        out_shape=jax.ShapeDtypeStruct((b, t, D_ATT), BF16),
        compiler_params=_params("arbitrary", "arbitrary"),
        name="ctxatt",
    )(p, p, p)


def _split_bf16(x):
    hi = x.astype(BF16)
    return hi, (x - hi.astype(F32)).astype(BF16)


def _route(logits_t, rb_ref):
    score = [jax.nn.sigmoid(logits_t[e:e + 1, :]) for e in range(N_EXPERTS)]
    sel = [score[e] + rb_ref[e] for e in range(N_EXPERTS)]
    best_g = None
    for g in range(N_GROUPS):
        v = sel[EXPERTS_PER_GROUP * g:EXPERTS_PER_GROUP * (g + 1)]
        top2 = None
        for i in range(EXPERTS_PER_GROUP):
            for j in range(i + 1, EXPERTS_PER_GROUP):
                pair = v[i] + v[j]
                top2 = pair if top2 is None else jnp.maximum(top2, pair)
        if best_g is None:
            best_g, best_v = jnp.zeros_like(top2, dtype=jnp.int32), top2
        else:
            upd = top2 > best_v
            best_g = jnp.where(upd, g, best_g)
            best_v = jnp.where(upd, top2, best_v)

    def in_best(vals, j):
        out = vals[j]
        for g in range(1, N_GROUPS):
            out = jnp.where(best_g == g, vals[EXPERTS_PER_GROUP * g + j], out)
        return out

    v = [in_best(sel, j) for j in range(EXPERTS_PER_GROUP)]
    sc = [in_best(score, j) for j in range(EXPERTS_PER_GROUP)]

    def first_argmax(vals):
        idx, top = jnp.zeros_like(best_g), vals[0]
        for j in range(1, EXPERTS_PER_GROUP):
            upd = vals[j] > top
            idx = jnp.where(upd, j, idx)
            top = jnp.where(upd, vals[j], top)
        return idx

    i1 = first_argmax(v)
    i2 = first_argmax([jnp.where(i1 == j, -jnp.inf, v[j]) for j in range(EXPERTS_PER_GROUP)])
    pick = lambda idx: sum(jnp.where(idx == j, sc[j], 0.0) for j in range(EXPERTS_PER_GROUP))
    w1, w2 = pick(i1), pick(i2)
    den = w1 + w2
    c1, c2 = w1 / den, w2 / den
    rows = []
    for e in range(N_EXPERTS):
        g, j = divmod(e, EXPERTS_PER_GROUP)
        rows.append(jnp.where(best_g == g, jnp.where(i1 == j, c1, 0.0) + jnp.where(i2 == j, c2, 0.0), 0.0))
    return jnp.concatenate(rows, axis=0), best_g


AUX_LANES = V7X_LANES
AUX_MID = N_EXPERTS
AUX_LO = 2 * N_EXPERTS
AUX_GROUP = 3 * N_EXPERTS
D_MOE_IN = D_MODEL + AUX_LANES


def _merge_kernel(rb_ref, yr_ref, ya_ref, gr_ref, ga_ref, x_ref, mod_ref, g_ref, wr_ref, wa_ref, wo_ref, wrt_ref,
                  x1_ref, h2_ref, gid_ref):
    m = (_sigmoid(gr_ref[...].astype(F32)) * jnp.dot(yr_ref[...], wr_ref[0], preferred_element_type=F32)
         + _sigmoid(ga_ref[...].astype(F32)) * jnp.dot(ya_ref[...], wa_ref[0], preferred_element_type=F32))
    o = jnp.dot(m.astype(BF16), wo_ref[0], preferred_element_type=F32)
    x1 = x_ref[...] + mod_ref[0, 2:3, :] * o
    x1_ref[...] = x1
    h2 = _rms(x1, g_ref[...]) * (1.0 + mod_ref[0, 4:5, :]) + mod_ref[0, 3:4, :]
    h_hi, h_lo = _split_bf16(h2)
    w_hi, w_lo = _split_bf16(wrt_ref[...])
    dot_nt = lambda a, b: lax.dot_general(a, b, NT_DIMS, preferred_element_type=F32)
    by_h_hi = dot_nt(jnp.concatenate([w_hi, w_lo], axis=0), h_hi)
    logits_t = by_h_hi[:N_EXPERTS] + (dot_nt(w_hi, h_lo) + by_h_hi[N_EXPERTS:])
    comb_t, group = _route(logits_t, rb_ref)
    tm = comb_t.shape[1]
    group = group.astype(F32)
    gid_ref[...] = jnp.concatenate([group, jnp.zeros((V7X_SUBLANES - 1, tm), F32)], axis=0)
    padded = jnp.concatenate([comb_t, jnp.zeros((AUX_GROUP - N_EXPERTS, tm), F32), group,
                              jnp.zeros((AUX_LANES - AUX_GROUP - 1, tm), F32)], axis=0)
    c = padded.T
    c_hi = c.astype(BF16).astype(F32)
    c_mid = (c - c_hi).astype(BF16).astype(F32)
    c_lo = c - c_hi - c_mid
    aux = c_hi + pltpu.roll(c_mid, AUX_MID, axis=1) + pltpu.roll(c_lo, AUX_LO, axis=1)
    h2_ref[...] = jnp.concatenate([h2.astype(BF16), aux.astype(BF16)], axis=1)


def merge_pallas(y_rnn, y_att, p, x, mod, g, w_br_rnn, w_br_att, w_out, w_router_t, router_bias, layer, seq_len,
                 per_seq, tm):
    n = x.shape[0]
    row = lambda i: (i, 0)
    full = lambda shape: pl.BlockSpec(shape, lambda i: (0, 0))
    weight = pl.BlockSpec((1, D_MODEL, D_MODEL), lambda i: (layer, 0, 0))
    return pl.pallas_call(
        _merge_kernel,
        grid=(n // tm,),
        in_specs=[pl.BlockSpec(memory_space=pltpu.SMEM),
                  pl.BlockSpec((tm, D_RNN), row), pl.BlockSpec((tm, D_ATT), row),
                  pl.BlockSpec((tm, CHUNK), lambda i: (i, C_GATE_R)), pl.BlockSpec((tm, CHUNK), lambda i: (i, C_GATE_A)),
                  pl.BlockSpec((tm, D_MODEL), row), _mod_spec(tm, seq_len, per_seq), full((1, D_MODEL)),
                  weight, weight, weight, full((N_EXPERTS, D_MODEL))],
        out_specs=[pl.BlockSpec((tm, D_MODEL), row), pl.BlockSpec((tm, D_MOE_IN), row),
                   pl.BlockSpec((V7X_SUBLANES, tm), lambda i: (0, i))],
        out_shape=[jax.ShapeDtypeStruct((n, D_MODEL), F32), jax.ShapeDtypeStruct((n, D_MOE_IN), BF16),
                   jax.ShapeDtypeStruct((V7X_SUBLANES, n), F32)],
        compiler_params=_params("arbitrary"),
        name="merge",
    )(router_bias, y_rnn, y_att, p, p, x, mod, g, w_br_rnn, w_br_att, w_out, w_router_t)


MOE_TB = 1024
MOE_RT = 128
MOE_TBP = MOE_TB + N_GROUPS * MOE_RT
MOE_EPS = 2


def _group_segments(gid_row):
    sub = lax.broadcasted_iota(jnp.int32, (V7X_SUBLANES, 1), 0).astype(F32)
    onehot = (gid_row == sub).astype(F32)
    cnt = jnp.sum(onehot, axis=1, keepdims=True)
    padded = jnp.floor((cnt + (MOE_RT - 1)) * (1.0 / MOE_RT)) * MOE_RT
    starts, run = [], jnp.zeros((1, 1), F32)
    for g in range(N_GROUPS):
        starts.append(run)
        run = run + padded[g:g + 1, :]
    return onehot, starts, [padded[g:g + 1, :] for g in range(N_GROUPS)], run


def _to_int(v):
    return v[0, 0].astype(jnp.int32)


def _moe_kernel(h_ref, gid_ref, x_ref, mod_ref, gf_ref, wg_ref, wu_ref, wd_ref, o_ref,
                p_scr, xs_scr, cs_scr, ys_scr, *, final_norm):
    step = pl.program_id(1)
    onehot, starts, sizes, used = _group_segments(gid_ref[0:1, :])
    lane = lax.broadcasted_iota(jnp.int32, (1, AUX_LANES), 1)

    @pl.when(step == 0)
    def _():
        t_row = lax.broadcasted_iota(jnp.int32, (MOE_TB, MOE_TB), 0)
        t_col = lax.broadcasted_iota(jnp.int32, (MOE_TB, MOE_TB), 1)
        earlier = (t_row < t_col).astype(BF16)
        rank = jnp.dot(onehot.astype(BF16), earlier, preferred_element_type=F32)
        pos = jnp.zeros((1, MOE_TB), F32)
        for g in range(N_GROUPS):
            pos = pos + onehot[g:g + 1, :] * (starts[g] + rank[g:g + 1, :])
        dest = lax.broadcasted_iota(jnp.int32, (MOE_TBP, 1), 0).astype(F32)
        p_scr[...] = (dest == pos).astype(BF16)
        for r0 in range(0, MOE_TBP, V7X_MXU_DIM):
            rows = pl.ds(r0, V7X_MXU_DIM)
            sorted_rows = jnp.dot(p_scr[rows, :], h_ref[...], preferred_element_type=F32)
            xs_scr[rows, :] = sorted_rows[:, :D_MODEL].astype(BF16)
            aux = sorted_rows[:, D_MODEL:]
            cs_scr[rows, :] = (aux + pltpu.roll(aux, AUX_LANES - AUX_MID, axis=1)
                               + pltpu.roll(aux, AUX_LANES - AUX_LO, axis=1))
        first_free = pl.multiple_of(_to_int(used), MOE_RT)

        def clear(k, carry):
            ys_scr[pl.ds(pl.multiple_of(first_free + k * MOE_RT, MOE_RT), MOE_RT), :] = jnp.zeros(
                (MOE_RT, D_MODEL), F32)
            return carry

        lax.fori_loop(0, (MOE_TBP - first_free) // MOE_RT, clear, 0)

    steps_per_group = EXPERTS_PER_GROUP // MOE_EPS
    group = step // steps_per_group
    first_of_group = step % steps_per_group == 0
    start_v, size_v = starts[0], sizes[0]
    for g in range(1, N_GROUPS):
        start_v = jnp.where(group == g, starts[g], start_v)
        size_v = jnp.where(group == g, sizes[g], size_v)
    seg_start = _to_int(start_v)
    n_tiles = _to_int(size_v) // MOE_RT

    def run_expert(r0, n_rows):
        rows = pl.ds(pl.multiple_of(r0, MOE_RT), n_rows)
        x = xs_scr[rows, :]
        cs = cs_scr[rows, :]
        y = None
        for k in range(MOE_EPS):
            gate = jnp.dot(x, wg_ref[0, k], preferred_element_type=F32)
            up = jnp.dot(x, wu_ref[0, k], preferred_element_type=F32)
            act = (gate * _sigmoid(gate)) * up
            yk = jnp.dot(act.astype(BF16), wd_ref[0, k], preferred_element_type=F32)
            yk = jnp.sum(jnp.where(lane == step * MOE_EPS + k, cs, 0.0), axis=-1, keepdims=True) * yk
            y = yk if y is None else y + yk

        @pl.when(first_of_group)
        def _():
            ys_scr[rows, :] = y

        @pl.when(jnp.logical_not(first_of_group))
        def _():
            ys_scr[rows, :] += y

    def pair(k, carry):
        run_expert(seg_start + k * (2 * MOE_RT), 2 * MOE_RT)
        return carry

    lax.fori_loop(0, n_tiles // 2, pair, 0)

    @pl.when(n_tiles % 2 == 1)
    def _():
        run_expert(seg_start + (n_tiles - 1) * MOE_RT, MOE_RT)

    @pl.when(step == N_EXPERTS // MOE_EPS - 1)
    def _():
        y = lax.dot_general(p_scr[...], ys_scr[...].astype(BF16), (((0,), (0,)), ((), ())),
                            preferred_element_type=F32)
        x2 = x_ref[...] + mod_ref[0, 5:6, :] * y
        o_ref[...] = _rms(x2, gf_ref[...]) if final_norm else x2


def moe_pallas(h2x, gid, x1, mod, g_final, w_gate, w_up, w_down, layer, seq_len, per_seq, final_norm):
    n = x1.shape[0]
    row = lambda i, e: (i, 0)
    expert = lambda i, e: (layer, e, 0, 0)
    return pl.pallas_call(
        functools.partial(_moe_kernel, final_norm=final_norm),
        grid=(n // MOE_TB, N_EXPERTS // MOE_EPS),
        in_specs=[pl.BlockSpec((MOE_TB, D_MOE_IN), row), pl.BlockSpec((V7X_SUBLANES, MOE_TB), lambda i, e: (0, i)),
                  pl.BlockSpec((MOE_TB, D_MODEL), row), _mod_spec(MOE_TB, seq_len, per_seq),
                  pl.BlockSpec((1, D_MODEL), lambda i, e: (0, 0)),
                  pl.BlockSpec((1, MOE_EPS, D_MODEL, D_EXPERT), expert),
                  pl.BlockSpec((1, MOE_EPS, D_MODEL, D_EXPERT), expert),
                  pl.BlockSpec((1, MOE_EPS, D_EXPERT, D_MODEL), expert)],
        out_specs=pl.BlockSpec((MOE_TB, D_MODEL), row),
        out_shape=jax.ShapeDtypeStruct((n, D_MODEL), F32),
        scratch_shapes=[pltpu.VMEM((MOE_TBP, MOE_TB), BF16), pltpu.VMEM((MOE_TBP, D_MODEL), BF16),
                        pltpu.VMEM((MOE_TBP, AUX_LANES), F32), pltpu.VMEM((MOE_TBP, D_MODEL), F32)],
        compiler_params=_params("arbitrary", "arbitrary"),
        name="moe",
    )(h2x, gid, x1, mod, g_final, w_gate, w_up, w_down)


TM_PROJ = 2048
TM_MERGE = 1024


def _layer(x, mod, seq_len, per_seq, lw, h0, ctx_kv, tables, layer):
    n = x.shape[0]
    b = n // seq_len
    emit_kv = ctx_kv is None
    outs = inproj_pallas(x, mod, lw['norm_g'][0:1], lw['w_in'], layer, seq_len, per_seq,
                         TM_PROJ // 2 if emit_kv else TM_PROJ, emit_kv)
    p = outs[0]
    p3 = p.reshape(b, seq_len, D_IN)
    y_rnn, h_fin = lru_pallas(p3, h0, lw['conv_w'], lw['conv_b'], lw['wg'], lw['gb'], lw['lam'])
    if emit_kv:
        y_att = ctxatt_pallas(p3)
    else:
        y_att = natten_pallas(p3, ctx_kv[0], ctx_kv[1], tables, layer)
    x1, h2x, gid = merge_pallas(y_rnn.reshape(n, D_RNN), y_att.reshape(n, D_ATT), p, x, mod, lw['norm_g'][1:2],
                                lw['w_br_rnn'], lw['w_br_att'], lw['w_out'], lw['w_router_t'], lw['router_bias'],
                                layer, seq_len, per_seq, TM_MERGE)
    x2 = moe_pallas(h2x, gid, x1, mod, lw['g_final'], lw['w_exp_gate'], lw['w_exp_up'], lw['w_exp_down'], layer,
                    seq_len, per_seq, layer == DEPTH - 1)
    kv = (outs[1], outs[2]) if emit_kv else None
    return x2, kv, h_fin


def kernel(x_prompt, x_sample, cache_k, cache_v, state_lru, c, c_ctx, w_ada, b_ada, norm_g, w_in, conv_w,
           conv_b, lru_wa, lru_ba, lru_wx, lru_bx, lru_lam, rpb, w_br_rnn, w_br_att, w_out, w_router,
           router_bias, w_exp_gate, w_exp_up, w_exp_down, final_norm_g):
    cvecs = jnp.concatenate([c, c_ctx[None, :], jnp.zeros((V7X_SUBLANES - DEC_BATCH - 1, D_MODEL), F32)], axis=0)
    mods = adaln_pallas(cvecs, w_ada, b_ada).reshape(DEPTH, V7X_SUBLANES, N_MOD, D_MODEL)
    tables = _natten_tables(rpb)
    w_router_t = w_router.T
    xp = x_prompt.reshape(BATCH * SEQ, D_MODEL)
    xs = x_sample.reshape(DEC_BATCH * DEC_SEQ, D_MODEL)
    zeros_h0 = jnp.zeros((BATCH, 2, D_RNN), F32)
    w_exp = [w.astype(BF16) for w in (w_exp_gate, w_exp_up, w_exp_down)]
    w_proj = [w.astype(BF16) for w in (w_in, w_br_rnn, w_br_att, w_out)]
    ks, vs, hs = [], [], []
    for l in range(DEPTH):
        lw = dict(
            norm_g=norm_g[l], w_in=w_proj[0], conv_w=conv_w[l], conv_b=conv_b[l][None, :],
            wg=_lru_gate_weights(lru_wa[l], lru_wx[l]),
            gb=jnp.stack([lru_ba[l, 0], lru_bx[l, 0], lru_ba[l, 1], lru_bx[l, 1]], axis=0), lam=lru_lam[l],
            w_br_rnn=w_proj[1], w_br_att=w_proj[2], w_out=w_proj[3],
            w_router_t=w_router_t, router_bias=router_bias, g_final=final_norm_g[None, :],
            w_exp_gate=w_exp[0], w_exp_up=w_exp[1], w_exp_down=w_exp[2])
        xp, kv, h_l = _layer(xp, mods[l, DEC_BATCH:DEC_BATCH + 1], SEQ, False, lw, zeros_h0, None, None, l)
        ks.append(kv[0].reshape(BATCH, SEQ, D_ATT))
        vs.append(kv[1].reshape(BATCH, SEQ, D_ATT))
        hs.append(h_l)
        ctx_kv = (cache_k[:, l].reshape(DEC_BATCH, PAST_LEN, D_ATT).astype(BF16),
                  cache_v[:, l].reshape(DEC_BATCH, PAST_LEN, D_ATT).astype(BF16))
        xs, _, _ = _layer(xs, mods[l, :DEC_BATCH], DEC_SEQ, True, lw, state_lru[:, l], ctx_kv, tables, l)
    y_prompt = xp.reshape(BATCH, SEQ, D_MODEL)
    y_sample = xs.reshape(DEC_BATCH, DEC_SEQ, D_MODEL)
    heads = lambda z: jnp.stack(z, axis=1).reshape(BATCH, DEPTH, SEQ, N_HEADS, HEAD_DIM)
    return (y_prompt, y_sample, heads(ks), heads(vs), jnp.stack(hs, axis=1))
```

```python
import functools

import jax
import jax.numpy as jnp
import numpy as np
from jax import lax
from jax.experimental import pallas as pl
from jax.experimental.pallas import tpu as pltpu

D_MODEL = 1024
BATCH = 16
SEQ = 256
DEPTH = 2
DEC_BATCH = 4
DEC_SEQ = 4096
PAST_LEN = 256

GRID_W = 64
D_RNN = 1024
N_LRU_BLOCKS = 16
LRU_BLOCK = D_RNN // N_LRU_BLOCKS
CONV_W = 4
LRU_C = 8.0
N_HEADS = 16
HEAD_DIM = 64
D_ATT = N_HEADS * HEAD_DIM
WIN_H = 8
WIN_W = 16
N_EXPERTS = 16
N_GROUPS = 4
EXPERTS_PER_GROUP = N_EXPERTS // N_GROUPS
D_EXPERT = 512
N_MOD = 6
D_IN = 2 * D_RNN + 3 * D_ATT + 2 * D_MODEL
EPS = 1e-6
NEG_INF = -1e30

BF16 = jnp.bfloat16
F32 = jnp.float32

V7X_LANES = 128
V7X_SUBLANES = 8
V7X_MXU_DIM = 256
V7X_VMEM_BYTES = 64 * 1024 * 1024
VMEM_LIMIT = V7X_VMEM_BYTES - 8 * 1024 * 1024

CHUNK = D_MODEL
N_CHUNKS = D_IN // CHUNK
C_XRNN, C_GRNN, C_Q, C_K, C_V, C_GATE_R, C_GATE_A = range(N_CHUNKS)

NT_DIMS = (((1,), (1,)), ((), ()))


def _params(*sem):
    return pltpu.CompilerParams(dimension_semantics=sem, vmem_limit_bytes=VMEM_LIMIT)


def _adaln_kernel(c_ref, w_ref, b_ref, o_ref):
    cv = c_ref[...]
    s = cv * jax.nn.sigmoid(cv)
    o_ref[0] = jnp.dot(s.astype(BF16), w_ref[0].astype(BF16), preferred_element_type=F32) + b_ref[0]


def adaln_pallas(cvecs, w_ada, b_ada):
    r = cvecs.shape[0]
    return pl.pallas_call(
        _adaln_kernel,
        grid=(DEPTH, N_MOD),
        in_specs=[pl.BlockSpec((r, D_MODEL), lambda l, j: (0, 0)),
                  pl.BlockSpec((1, D_MODEL, D_MODEL), lambda l, j: (l, 0, j)),
                  pl.BlockSpec((1, 1, D_MODEL), lambda l, j: (l, 0, j))],
        out_specs=pl.BlockSpec((1, r, D_MODEL), lambda l, j: (l, 0, j)),
        out_shape=jax.ShapeDtypeStruct((DEPTH, r, N_MOD * D_MODEL), F32),
        compiler_params=_params("arbitrary", "arbitrary"),
        name="adaln",
    )(cvecs, w_ada, b_ada.reshape(DEPTH, 1, N_MOD * D_MODEL))


def _mod_spec(tm, seq_len, per_seq):
    if per_seq:
        return pl.BlockSpec((1, N_MOD, D_MODEL), lambda i, *_: (i * tm // seq_len, 0, 0))
    return pl.BlockSpec((1, N_MOD, D_MODEL), lambda i, *_: (0, 0, 0))


def _rms(x, g):
    return x * lax.rsqrt(jnp.mean(x * x, axis=-1, keepdims=True) + EPS) * g


def _inproj_kernel(x_ref, mod_ref, g_ref, w_ref, *refs, emit_kv):
    if emit_kv:
        p_ref, k32_ref, v32_ref, h_scr = refs
    else:
        p_ref, h_scr = refs
    j = pl.program_id(1)

    @pl.when(j == 0)
    def _():
        y = _rms(x_ref[...], g_ref[...])
        h_scr[...] = (y * (1.0 + mod_ref[0, 1:2, :]) + mod_ref[0, 0:1, :]).astype(BF16)

    acc = jnp.dot(h_scr[...], w_ref[0], preferred_element_type=F32)
    p_ref[...] = (acc * jnp.where(j == C_Q, HEAD_DIM ** -0.5, 1.0)).astype(BF16)
    if emit_kv:
        @pl.when(j == C_K)
        def _():
            k32_ref[...] = acc.reshape(k32_ref.shape)

        @pl.when(j == C_V)
        def _():
            v32_ref[...] = acc.reshape(v32_ref.shape)


def inproj_pallas(x, mod, g, w_in, layer, seq_len, per_seq, tm, emit_kv):
    n = x.shape[0]
    row = lambda i, j: (i, 0)
    out_shape = [jax.ShapeDtypeStruct((n, D_IN), BF16)]
    out_specs = [pl.BlockSpec((tm, CHUNK), lambda i, j: (i, j))]
    if emit_kv:
        out_shape += [jax.ShapeDtypeStruct((n // seq_len, seq_len, N_HEADS, HEAD_DIM), F32)] * 2
        out_specs += [pl.BlockSpec((tm // seq_len, seq_len, N_HEADS, HEAD_DIM), lambda i, j: (i, 0, 0, 0))] * 2
    return pl.pallas_call(
        functools.partial(_inproj_kernel, emit_kv=emit_kv),
        grid=(n // tm, N_CHUNKS),
        in_specs=[pl.BlockSpec((tm, D_MODEL), row), _mod_spec(tm, seq_len, per_seq),
                  pl.BlockSpec((1, D_MODEL), lambda i, j: (0, 0)),
                  pl.BlockSpec((1, D_MODEL, CHUNK), lambda i, j: (layer, 0, j))],
        out_specs=out_specs,
        out_shape=out_shape,
        scratch_shapes=[pltpu.VMEM((tm, D_MODEL), BF16)],
        compiler_params=_params("arbitrary", "arbitrary"),
        name="inproj",
    )(x, mod, g, w_in)


LRU_CB = 512
LRU_TC = 256
LRU_SUB = V7X_MXU_DIM
LRU_HALO = 16


LRU_SEG = LRU_TC // V7X_SUBLANES
LRU_NSLAB = LRU_SEG + CONV_W


def _lru_row_maps():
    s = np.arange(V7X_SUBLANES)[None, :]
    src = (LRU_HALO - CONV_W // 2) + LRU_SEG * s + np.arange(LRU_NSLAB)[:, None]
    sel = np.zeros((LRU_NSLAB * V7X_SUBLANES, LRU_TC + 2 * LRU_HALO), np.float32)
    sel[np.arange(sel.shape[0]), src.reshape(-1)] = 1.0
    tok = (LRU_SEG * s + np.arange(LRU_SEG)[:, None]).reshape(-1)
    perm = np.zeros((LRU_TC, LRU_TC), np.float32)
    perm[np.arange(LRU_TC), tok] = 1.0
    return jnp.asarray(sel, BF16), jnp.asarray(perm, BF16), jnp.asarray(perm.T, BF16)


def _slab_scan(a3, b3, carry, reverse):
    n = a3.shape[0]
    hs, cum = [None] * n, [None] * n
    h = cp = None
    for t in (range(n - 1, -1, -1) if reverse else range(n)):
        h = b3[t] if h is None else a3[t] * h + b3[t]
        cp = a3[t] if cp is None else cp * a3[t]
        hs[t], cum[t] = h, cp
    sub = lax.broadcasted_iota(jnp.int32, (V7X_SUBLANES, 1), 0)
    pa, pb = cp, h
    s = 1
    while s < V7X_SUBLANES:
        shift = V7X_SUBLANES - s if reverse else s
        ok = (sub < V7X_SUBLANES - s) if reverse else (sub >= s)
        a_sh = pltpu.roll(pa, shift, axis=0)
        b_sh = pltpu.roll(pb, shift, axis=0)
        pb = jnp.where(ok, pa * b_sh + pb, pb)
        pa = jnp.where(ok, pa * a_sh, pa)
        s *= 2
    leaving = pb + pa * carry
    first, last = (V7X_SUBLANES - 1, 0) if reverse else (0, V7X_SUBLANES - 1)
    entering = jnp.where(sub == first, carry, pltpu.roll(leaving, V7X_SUBLANES - 1 if reverse else 1, axis=0))
    out = jnp.concatenate([hs[t] + cum[t] * entering for t in range(n)], axis=0)
    return out, leaving[last:last + 1, :]


def _sigmoid(x):
    return 0.5 * jnp.tanh(0.5 * x) + 0.5


def _lru_kernel(x_ref, gate_ref, h0_ref, cw_ref, cb_ref, wg_ref, gb_ref, lam_ref, sel_ref, perm_ref, permt_ref,
                y_ref, fin_ref, hf_scr, u_scr):
    t_len = x_ref.shape[1]
    cb = x_ref.shape[2]
    n_chunks = t_len // LRU_TC

    def conv_chunk(c):
        t0 = pl.multiple_of(c * LRU_TC, LRU_TC)
        cur = x_ref[0, pl.ds(t0, LRU_TC), :]
        lo = pl.multiple_of(jnp.maximum(t0 - LRU_HALO, 0), LRU_HALO)
        hi = pl.multiple_of(jnp.minimum(t0 + LRU_TC, t_len - LRU_HALO), LRU_HALO)
        prev = x_ref[0, pl.ds(lo, LRU_HALO), :]
        nxt = x_ref[0, pl.ds(hi, LRU_HALO), :]
        prev = jnp.where(c > 0, prev, jnp.zeros_like(prev))
        nxt = jnp.where(c < n_chunks - 1, nxt, jnp.zeros_like(nxt))
        ext = jnp.concatenate([prev, cur, nxt], axis=0)
        xs = jnp.dot(sel_ref[...], ext, preferred_element_type=F32).reshape(LRU_NSLAB, V7X_SUBLANES, cb)
        u = cb_ref[...][None] + jnp.zeros((LRU_SEG, V7X_SUBLANES, cb), F32)
        for j in range(CONV_W):
            u = u + xs[j:j + LRU_SEG] * cw_ref[j:j + 1, :][None]
        return t0, u.reshape(LRU_TC, cb)

    def slabs(v):
        return v.reshape(LRU_SEG, V7X_SUBLANES, cb)

    def gates(u, d):
        ub = u.astype(BF16)
        pre = [jnp.dot(ub[:, LRU_SUB * s:LRU_SUB * (s + 1)], wg_ref[d, s], preferred_element_type=F32)
               for s in range(cb // LRU_SUB)]
        pre_a = jnp.concatenate([p[:, :LRU_SUB] for p in pre], axis=1)
        pre_x = jnp.concatenate([p[:, LRU_SUB:] for p in pre], axis=1)
        r = _sigmoid(pre_a + gb_ref[2 * d:2 * d + 1, :])
        i = _sigmoid(pre_x + gb_ref[2 * d + 1:2 * d + 2, :])
        log_a = (-LRU_C * jax.nn.softplus(-lam_ref[d:d + 1, :])) * r
        a = jnp.exp(log_a)
        th = jnp.tanh(log_a)
        num = -2.0 * th
        scale = jnp.where(num > 0.0, num * lax.rsqrt(num * (1.0 - th)), 0.0)
        inp = scale * (i * u)
        return a, inp

    def fwd(c, carry):
        t0, u = conv_chunk(c)
        u_scr[pl.ds(t0, LRU_TC), :] = u
        a, inp = gates(u, 0)
        h, carry = _slab_scan(slabs(a), slabs(inp), carry, reverse=False)
        hf_scr[pl.ds(t0, LRU_TC), :] = h
        return carry

    unroll = 4 if n_chunks % 4 == 0 else 1
    fin_f = lax.fori_loop(0, n_chunks, fwd, h0_ref[0, 0:1, :], unroll=unroll)

    def bwd(k, carry):
        t0 = pl.multiple_of((n_chunks - 1 - k) * LRU_TC, LRU_TC)
        a, inp = gates(u_scr[pl.ds(t0, LRU_TC), :], 1)
        h, carry = _slab_scan(slabs(a), slabs(inp), carry, reverse=True)
        g = jnp.dot(perm_ref[...], gate_ref[0, pl.ds(t0, LRU_TC), :], preferred_element_type=F32)
        y = ((hf_scr[pl.ds(t0, LRU_TC), :] + h) * jax.nn.gelu(g)).astype(BF16)
        y_ref[0, pl.ds(t0, LRU_TC), :] = jnp.dot(permt_ref[...], y, preferred_element_type=F32).astype(y_ref.dtype)
        return carry

    fin_b = lax.fori_loop(0, n_chunks, bwd, h0_ref[0, 1:2, :], unroll=unroll)
    fin_ref[0, 0:1, :] = fin_f
    fin_ref[0, 1:2, :] = fin_b


def _lru_gate_weights(lru_wa, lru_wx):
    per = LRU_SUB // LRU_BLOCK
    eye = jnp.eye(per, dtype=F32)

    def dense(w):
        w = w.reshape(2, D_RNN // LRU_SUB, per, LRU_BLOCK, LRU_BLOCK)
        full = w[:, :, :, :, None, :] * eye[None, None, :, None, :, None]
        return full.reshape(2, D_RNN // LRU_SUB, LRU_SUB, LRU_SUB)

    return jnp.concatenate([dense(lru_wa), dense(lru_wx)], axis=-1).astype(BF16)


def lru_pallas(p, h0, conv_w, conv_b, wg, gb, lam):
    b, t, _ = p.shape
    n_cb = D_RNN // LRU_CB
    maps = _lru_row_maps()
    whole = lambda m: pl.BlockSpec(m.shape, lambda bi, ci: (0, 0))
    return pl.pallas_call(
        _lru_kernel,
        grid=(b, n_cb),
        in_specs=[pl.BlockSpec((1, t, LRU_CB), lambda bi, ci: (bi, 0, C_XRNN * n_cb + ci)),
                  pl.BlockSpec((1, t, LRU_CB), lambda bi, ci: (bi, 0, C_GRNN * n_cb + ci)),
                  pl.BlockSpec((1, 2, LRU_CB), lambda bi, ci: (bi, 0, ci)),
                  pl.BlockSpec((CONV_W, LRU_CB), lambda bi, ci: (0, ci)),
                  pl.BlockSpec((1, LRU_CB), lambda bi, ci: (0, ci)),
                  pl.BlockSpec((2, LRU_CB // LRU_SUB, LRU_SUB, 2 * LRU_SUB), lambda bi, ci: (0, ci, 0, 0)),
                  pl.BlockSpec((4, LRU_CB), lambda bi, ci: (0, ci)),
                  pl.BlockSpec((2, LRU_CB), lambda bi, ci: (0, ci))] + [whole(m) for m in maps],
        out_specs=[pl.BlockSpec((1, t, LRU_CB), lambda bi, ci: (bi, 0, ci)),
                   pl.BlockSpec((1, 2, LRU_CB), lambda bi, ci: (bi, 0, ci))],
        out_shape=[jax.ShapeDtypeStruct((b, t, D_RNN), BF16), jax.ShapeDtypeStruct((b, 2, D_RNN), F32)],
        scratch_shapes=[pltpu.VMEM((t, LRU_CB), F32), pltpu.VMEM((t, LRU_CB), F32)],
        compiler_params=_params("arbitrary", "arbitrary"),
        name="lru",
    )(p, p, h0, conv_w, conv_b, wg, gb, lam, *maps)


Q_ROWS = 4
Q_TILE = Q_ROWS * GRID_W
KEY_ROWS = 12
KEY_TILE = KEY_ROWS * GRID_W
N_Q_TILES = DEC_SEQ // Q_TILE
ATT_LANES = 1024
HEADS_PER_STEP = ATT_LANES // HEAD_DIM
PAIR = 2 * HEAD_DIM


def _window_block(t):
    return jnp.clip(t - 1, 0, N_Q_TILES - KEY_ROWS // Q_ROWS)


N_DR_PAIRS = 2 * WIN_H


def _natten_tables(rpb):
    c = np.arange(GRID_W)
    q_start = np.clip(c - WIN_W // 2, 0, GRID_W - WIN_W)
    col_valid = (c[None, :] >= q_start[:, None]) & (c[None, :] < q_start[:, None] + WIN_W)
    dc = c[None, :] - c[:, None] + WIN_W - 1
    sel_c = ((dc[None] == np.arange(2 * WIN_W - 1)[:, None, None]) & col_valid[None]).astype(np.float32)
    t = jnp.einsum('lhrd,dck->lhrck', rpb.astype(F32), jnp.asarray(sel_c), precision=lax.Precision.HIGHEST)
    t = jnp.where(col_valid[None, None, None], t, NEG_INF)
    t = jnp.pad(t, ((0, 0), (0, 0), (1, 1), (0, 0), (0, 0)), constant_values=NEG_INF)
    return jnp.concatenate([t[:, :, :-1], t[:, :, 1:]], axis=-1)


def _attend(q2, keys, vals, biases):
    lane_head = lax.broadcasted_iota(jnp.int32, (1, PAIR), 1) // HEAD_DIM
    acc = jnp.zeros((q2.shape[0], PAIR), F32)
    for j in range(2):
        mine = lane_head == j
        qh = jnp.where(mine, q2, jnp.zeros_like(q2))
        s = []
        for kb, bb in zip(keys, biases[j]):
            sd = lax.dot_general(qh, kb, NT_DIMS, preferred_element_type=F32)
            s.append(sd if bb is None else sd + bb)
        m = s[0].max(axis=-1, keepdims=True)
        for sd in s[1:]:
            m = jnp.maximum(m, sd.max(axis=-1, keepdims=True))
        o = None
        for sd, vb in zip(s, vals):
            pv = jnp.dot(jnp.exp(sd - m).astype(BF16), jnp.where(mine, vb, jnp.ones_like(vb)),
                         preferred_element_type=F32)
            o = pv if o is None else o + pv
        acc = acc + jnp.where(mine, o / pltpu.roll(o, HEAD_DIM, axis=1), 0.0)
    return acc


def _natten_kernel(q_ref, k0_ref, k1_ref, k2_ref, v0_ref, v1_ref, v2_ref, kc_ref, vc_ref, tt_ref, o_ref):
    ti = pl.program_id(1)
    rows = DEC_SEQ // GRID_W
    wstart = _window_block(ti) * Q_ROWS
    first_row = lax.broadcasted_iota(jnp.int32, (1, 2 * GRID_W), 1) < GRID_W
    pieces = {}
    for a in range(Q_ROWS):
        r = ti * Q_ROWS + a
        start_r = jnp.clip(r - WIN_H // 2, 0, rows - WIN_H)
        for i in range(0, KEY_ROWS, 2):
            kr = wstart + i
            ok = [((kr + e >= start_r) & (kr + e < start_r + WIN_H)).astype(jnp.int32) for e in range(2)]
            pieces[a, i] = (jnp.clip(kr - r + WIN_H, 0, N_DR_PAIRS - 1), jnp.where(first_row, ok[0], ok[1]) > 0)

    def bias_block(h, d):
        return jnp.concatenate(
            [jnp.concatenate([jnp.where(pieces[a, i][1], tt_ref[0, h, pieces[a, i][0]], NEG_INF)
                              for i in range(Q_ROWS * d, Q_ROWS * (d + 1), 2)], axis=1)
             for a in range(Q_ROWS)], axis=0)

    k_refs = (k0_ref, k1_ref, k2_ref, kc_ref)
    v_refs = (v0_ref, v1_ref, v2_ref, vc_ref)
    outs = []
    for hp in range(ATT_LANES // PAIR):
        sl = slice(PAIR * hp, PAIR * (hp + 1))
        biases = [[bias_block(2 * hp + j, d) for d in range(3)] + [None] for j in range(2)]
        outs.append(_attend(q_ref[0, :, sl], [r[0, :, sl] for r in k_refs], [r[0, :, sl] for r in v_refs], biases))
    o_ref[0] = jnp.concatenate(outs, axis=1).astype(o_ref.dtype)


def natten_pallas(p, k_ctx, v_ctx, tables, layer):
    b, t, _ = p.shape
    n_lb = D_ATT // ATT_LANES
    q_spec = pl.BlockSpec((1, Q_TILE, ATT_LANES), lambda lb, ti, bi: (bi, ti, C_Q * n_lb + lb))
    win = lambda ch, d: pl.BlockSpec((1, Q_TILE, ATT_LANES),
                                     lambda lb, ti, bi: (bi, _window_block(ti) + d, ch * n_lb + lb))
    ctx = pl.BlockSpec((1, PAST_LEN, ATT_LANES), lambda lb, ti, bi: (bi, 0, lb))
    return pl.pallas_call(
        _natten_kernel,
        grid=(n_lb, N_Q_TILES, b),
        in_specs=[q_spec, win(C_K, 0), win(C_K, 1), win(C_K, 2), win(C_V, 0), win(C_V, 1), win(C_V, 2), ctx, ctx,
                  pl.BlockSpec((1, HEADS_PER_STEP, N_DR_PAIRS, GRID_W, 2 * GRID_W),
                               lambda lb, ti, bi: (layer, lb, 0, 0, 0))],
        out_specs=pl.BlockSpec((1, Q_TILE, ATT_LANES), lambda lb, ti, bi: (bi, ti, lb)),
        out_shape=jax.ShapeDtypeStruct((b, t, D_ATT), BF16),
        compiler_params=_params("arbitrary", "arbitrary", "arbitrary"),
        name="natten",
    )(p, p, p, p, p, p, p, k_ctx, v_ctx, tables)


def _ctxatt_kernel(q_ref, k_ref, v_ref, o_ref):
    outs = []
    for hp in range(ATT_LANES // PAIR):
        sl = slice(PAIR * hp, PAIR * (hp + 1))
        outs.append(_attend(q_ref[0, :, sl], [k_ref[0, :, sl]], [v_ref[0, :, sl]], [[None], [None]]))
    o_ref[0] = jnp.concatenate(outs, axis=1).astype(o_ref.dtype)


def ctxatt_pallas(p):
    b, t, _ = p.shape
    n_lb = D_ATT // ATT_LANES
    blk = lambda ch: pl.BlockSpec((1, t, ATT_LANES), lambda lb, bi: (bi, 0, ch * n_lb + lb))
    return pl.pallas_call(
        _ctxatt_kernel,
        grid=(n_lb, b),
        in_specs=[blk(C_Q), blk(C_K), blk(C_V)],
        out_specs=pl.BlockSpec((1, t, ATT_LANES), lambda lb, bi: (bi, 0, lb)),
        out_shape=jax.ShapeDtypeStruct((b, t, D_ATT), BF16),
        compiler_params=_params("arbitrary", "arbitrary"),
        name="ctxatt",
    )(p, p, p)


def _split_bf16(x):
    hi = x.astype(BF16)
    return hi, (x - hi.astype(F32)).astype(BF16)


def _route(logits_t, rb_ref):
    score = [jax.nn.sigmoid(logits_t[e:e + 1, :]) for e in range(N_EXPERTS)]
    sel = [score[e] + rb_ref[e] for e in range(N_EXPERTS)]
    best_g = None
    for g in range(N_GROUPS):
        v = sel[EXPERTS_PER_GROUP * g:EXPERTS_PER_GROUP * (g + 1)]
        top2 = None
        for i in range(EXPERTS_PER_GROUP):
            for j in range(i + 1, EXPERTS_PER_GROUP):
                pair = v[i] + v[j]
                top2 = pair if top2 is None else jnp.maximum(top2, pair)
        if best_g is None:
            best_g, best_v = jnp.zeros_like(top2, dtype=jnp.int32), top2
        else:
            upd = top2 > best_v
            best_g = jnp.where(upd, g, best_g)
            best_v = jnp.where(upd, top2, best_v)

    def in_best(vals, j):
        out = vals[j]
        for g in range(1, N_GROUPS):
            out = jnp.where(best_g == g, vals[EXPERTS_PER_GROUP * g + j], out)
        return out

    v = [in_best(sel, j) for j in range(EXPERTS_PER_GROUP)]
    sc = [in_best(score, j) for j in range(EXPERTS_PER_GROUP)]

    def first_argmax(vals):
        idx, top = jnp.zeros_like(best_g), vals[0]
        for j in range(1, EXPERTS_PER_GROUP):
            upd = vals[j] > top
            idx = jnp.where(upd, j, idx)
            top = jnp.where(upd, vals[j], top)
        return idx

    i1 = first_argmax(v)
    i2 = first_argmax([jnp.where(i1 == j, -jnp.inf, v[j]) for j in range(EXPERTS_PER_GROUP)])
    pick = lambda idx: sum(jnp.where(idx == j, sc[j], 0.0) for j in range(EXPERTS_PER_GROUP))
    w1, w2 = pick(i1), pick(i2)
    den = w1 + w2
    c1, c2 = w1 / den, w2 / den
    rows = []
    for e in range(N_EXPERTS):
        g, j = divmod(e, EXPERTS_PER_GROUP)
        rows.append(jnp.where(best_g == g, jnp.where(i1 == j, c1, 0.0) + jnp.where(i2 == j, c2, 0.0), 0.0))
    return jnp.concatenate(rows, axis=0), best_g


AUX_LANES = V7X_LANES
AUX_MID = N_EXPERTS
AUX_LO = 2 * N_EXPERTS
AUX_GROUP = 3 * N_EXPERTS
D_MOE_IN = D_MODEL + AUX_LANES


def _merge_kernel(rb_ref, yr_ref, ya_ref, gr_ref, ga_ref, x_ref, mod_ref, g_ref, wr_ref, wa_ref, wo_ref, wrt_ref,
                  x1_ref, h2_ref, gid_ref):
    m = (_sigmoid(gr_ref[...].astype(F32)) * jnp.dot(yr_ref[...], wr_ref[0], preferred_element_type=F32)
         + _sigmoid(ga_ref[...].astype(F32)) * jnp.dot(ya_ref[...], wa_ref[0], preferred_element_type=F32))
    o = jnp.dot(m.astype(BF16), wo_ref[0], preferred_element_type=F32)
    x1 = x_ref[...] + mod_ref[0, 2:3, :] * o
    x1_ref[...] = x1
    h2 = _rms(x1, g_ref[...]) * (1.0 + mod_ref[0, 4:5, :]) + mod_ref[0, 3:4, :]
    h_hi, h_lo = _split_bf16(h2)
    w_hi, w_lo = _split_bf16(wrt_ref[...])
    dot_nt = lambda a, b: lax.dot_general(a, b, NT_DIMS, preferred_element_type=F32)
    by_h_hi = dot_nt(jnp.concatenate([w_hi, w_lo], axis=0), h_hi)
    logits_t = by_h_hi[:N_EXPERTS] + (dot_nt(w_hi, h_lo) + by_h_hi[N_EXPERTS:])
    comb_t, group = _route(logits_t, rb_ref)
    tm = comb_t.shape[1]
    group = group.astype(F32)
    gid_ref[...] = jnp.concatenate([group, jnp.zeros((V7X_SUBLANES - 1, tm), F32)], axis=0)
    padded = jnp.concatenate([comb_t, jnp.zeros((AUX_GROUP - N_EXPERTS, tm), F32), group,
                              jnp.zeros((AUX_LANES - AUX_GROUP - 1, tm), F32)], axis=0)
    c = padded.T
    c_hi = c.astype(BF16).astype(F32)
    c_mid = (c - c_hi).astype(BF16).astype(F32)
    c_lo = c - c_hi - c_mid
    aux = c_hi + pltpu.roll(c_mid, AUX_MID, axis=1) + pltpu.roll(c_lo, AUX_LO, axis=1)
    h2_ref[...] = jnp.concatenate([h2.astype(BF16), aux.astype(BF16)], axis=1)


def merge_pallas(y_rnn, y_att, p, x, mod, g, w_br_rnn, w_br_att, w_out, w_router_t, router_bias, layer, seq_len,
                 per_seq, tm):
    n = x.shape[0]
    row = lambda i: (i, 0)
    full = lambda shape: pl.BlockSpec(shape, lambda i: (0, 0))
    weight = pl.BlockSpec((1, D_MODEL, D_MODEL), lambda i: (layer, 0, 0))
    return pl.pallas_call(
        _merge_kernel,
        grid=(n // tm,),
        in_specs=[pl.BlockSpec(memory_space=pltpu.SMEM),
                  pl.BlockSpec((tm, D_RNN), row), pl.BlockSpec((tm, D_ATT), row),
                  pl.BlockSpec((tm, CHUNK), lambda i: (i, C_GATE_R)), pl.BlockSpec((tm, CHUNK), lambda i: (i, C_GATE_A)),
                  pl.BlockSpec((tm, D_MODEL), row), _mod_spec(tm, seq_len, per_seq), full((1, D_MODEL)),
                  weight, weight, weight, full((N_EXPERTS, D_MODEL))],
        out_specs=[pl.BlockSpec((tm, D_MODEL), row), pl.BlockSpec((tm, D_MOE_IN), row),
                   pl.BlockSpec((V7X_SUBLANES, tm), lambda i: (0, i))],
        out_shape=[jax.ShapeDtypeStruct((n, D_MODEL), F32), jax.ShapeDtypeStruct((n, D_MOE_IN), BF16),
                   jax.ShapeDtypeStruct((V7X_SUBLANES, n), F32)],
        compiler_params=_params("arbitrary"),
        name="merge",
    )(router_bias, y_rnn, y_att, p, p, x, mod, g, w_br_rnn, w_br_att, w_out, w_router_t)


MOE_TB = 1024
MOE_RT = 128
MOE_TBP = MOE_TB + N_GROUPS * MOE_RT
MOE_EPS = 2


def _group_segments(gid_row):
    sub = lax.broadcasted_iota(jnp.int32, (V7X_SUBLANES, 1), 0).astype(F32)
    onehot = (gid_row == sub).astype(F32)
    cnt = jnp.sum(onehot, axis=1, keepdims=True)
    padded = jnp.floor((cnt + (MOE_RT - 1)) * (1.0 / MOE_RT)) * MOE_RT
    starts, run = [], jnp.zeros((1, 1), F32)
    for g in range(N_GROUPS):
        starts.append(run)
        run = run + padded[g:g + 1, :]
    return onehot, starts, [padded[g:g + 1, :] for g in range(N_GROUPS)], run


def _to_int(v):
    return v[0, 0].astype(jnp.int32)


def _moe_kernel(h_ref, gid_ref, x_ref, mod_ref, gf_ref, wg_ref, wu_ref, wd_ref, o_ref,
                p_scr, xs_scr, cs_scr, ys_scr, *, final_norm):
    step = pl.program_id(1)
    onehot, starts, sizes, used = _group_segments(gid_ref[0:1, :])
    lane = lax.broadcasted_iota(jnp.int32, (1, AUX_LANES), 1)

    @pl.when(step == 0)
    def _():
        t_row = lax.broadcasted_iota(jnp.int32, (MOE_TB, MOE_TB), 0)
        t_col = lax.broadcasted_iota(jnp.int32, (MOE_TB, MOE_TB), 1)
        earlier = (t_row < t_col).astype(BF16)
        rank = jnp.dot(onehot.astype(BF16), earlier, preferred_element_type=F32)
        pos = jnp.zeros((1, MOE_TB), F32)
        for g in range(N_GROUPS):
            pos = pos + onehot[g:g + 1, :] * (starts[g] + rank[g:g + 1, :])
        dest = lax.broadcasted_iota(jnp.int32, (MOE_TBP, 1), 0).astype(F32)
        p_scr[...] = (dest == pos).astype(BF16)
        for r0 in range(0, MOE_TBP, V7X_MXU_DIM):
            rows = pl.ds(r0, V7X_MXU_DIM)
            sorted_rows = jnp.dot(p_scr[rows, :], h_ref[...], preferred_element_type=F32)
            xs_scr[rows, :] = sorted_rows[:, :D_MODEL].astype(BF16)
            aux = sorted_rows[:, D_MODEL:]
            cs_scr[rows, :] = (aux + pltpu.roll(aux, AUX_LANES - AUX_MID, axis=1)
                               + pltpu.roll(aux, AUX_LANES - AUX_LO, axis=1))
        first_free = pl.multiple_of(_to_int(used), MOE_RT)

        def clear(k, carry):
            ys_scr[pl.ds(pl.multiple_of(first_free + k * MOE_RT, MOE_RT), MOE_RT), :] = jnp.zeros(
                (MOE_RT, D_MODEL), F32)
            return carry

        lax.fori_loop(0, (MOE_TBP - first_free) // MOE_RT, clear, 0)

    steps_per_group = EXPERTS_PER_GROUP // MOE_EPS
    group = step // steps_per_group
    first_of_group = step % steps_per_group == 0
    start_v, size_v = starts[0], sizes[0]
    for g in range(1, N_GROUPS):
        start_v = jnp.where(group == g, starts[g], start_v)
        size_v = jnp.where(group == g, sizes[g], size_v)
    seg_start = _to_int(start_v)
    n_tiles = _to_int(size_v) // MOE_RT

    def run_expert(r0, n_rows):
        rows = pl.ds(pl.multiple_of(r0, MOE_RT), n_rows)
        x = xs_scr[rows, :]
        cs = cs_scr[rows, :]
        y = None
        for k in range(MOE_EPS):
            gate = jnp.dot(x, wg_ref[0, k], preferred_element_type=F32)
            up = jnp.dot(x, wu_ref[0, k], preferred_element_type=F32)
            act = (gate * _sigmoid(gate)) * up
            yk = jnp.dot(act.astype(BF16), wd_ref[0, k], preferred_element_type=F32)
            yk = jnp.sum(jnp.where(lane == step * MOE_EPS + k, cs, 0.0), axis=-1, keepdims=True) * yk
            y = yk if y is None else y + yk

        @pl.when(first_of_group)
        def _():
            ys_scr[rows, :] = y

        @pl.when(jnp.logical_not(first_of_group))
        def _():
            ys_scr[rows, :] += y

    def pair(k, carry):
        run_expert(seg_start + k * (2 * MOE_RT), 2 * MOE_RT)
        return carry

    lax.fori_loop(0, n_tiles // 2, pair, 0)

    @pl.when(n_tiles % 2 == 1)
    def _():
        run_expert(seg_start + (n_tiles - 1) * MOE_RT, MOE_RT)

    @pl.when(step == N_EXPERTS // MOE_EPS - 1)
    def _():
        y = lax.dot_general(p_scr[...], ys_scr[...].astype(BF16), (((0,), (0,)), ((), ())),
                            preferred_element_type=F32)
        x2 = x_ref[...] + mod_ref[0, 5:6, :] * y
        o_ref[...] = _rms(x2, gf_ref[...]) if final_norm else x2


def moe_pallas(h2x, gid, x1, mod, g_final, w_gate, w_up, w_down, layer, seq_len, per_seq, final_norm):
    n = x1.shape[0]
    row = lambda i, e: (i, 0)
    expert = lambda i, e: (layer, e, 0, 0)
    return pl.pallas_call(
        functools.partial(_moe_kernel, final_norm=final_norm),
        grid=(n // MOE_TB, N_EXPERTS // MOE_EPS),
        in_specs=[pl.BlockSpec((MOE_TB, D_MOE_IN), row), pl.BlockSpec((V7X_SUBLANES, MOE_TB), lambda i, e: (0, i)),
                  pl.BlockSpec((MOE_TB, D_MODEL), row), _mod_spec(MOE_TB, seq_len, per_seq),
                  pl.BlockSpec((1, D_MODEL), lambda i, e: (0, 0)),
                  pl.BlockSpec((1, MOE_EPS, D_MODEL, D_EXPERT), expert),
                  pl.BlockSpec((1, MOE_EPS, D_MODEL, D_EXPERT), expert),
                  pl.BlockSpec((1, MOE_EPS, D_EXPERT, D_MODEL), expert)],
        out_specs=pl.BlockSpec((MOE_TB, D_MODEL), row),
        out_shape=jax.ShapeDtypeStruct((n, D_MODEL), F32),
        scratch_shapes=[pltpu.VMEM((MOE_TBP, MOE_TB), BF16), pltpu.VMEM((MOE_TBP, D_MODEL), BF16),
                        pltpu.VMEM((MOE_TBP, AUX_LANES), F32), pltpu.VMEM((MOE_TBP, D_MODEL), F32)],
        compiler_params=_params("arbitrary", "arbitrary"),
        name="moe",
    )(h2x, gid, x1, mod, g_final, w_gate, w_up, w_down)


TM_PROJ = 2048
TM_MERGE = 1024


def _layer(x, mod, seq_len, per_seq, lw, h0, ctx_kv, tables, layer):
    n = x.shape[0]
    b = n // seq_len
    emit_kv = ctx_kv is None
    outs = inproj_pallas(x, mod, lw['norm_g'][0:1], lw['w_in'], layer, seq_len, per_seq,
                         TM_PROJ // 4 if emit_kv else TM_PROJ, emit_kv)
    p = outs[0]
    p3 = p.reshape(b, seq_len, D_IN)
    y_rnn, h_fin = lru_pallas(p3, h0, lw['conv_w'], lw['conv_b'], lw['wg'], lw['gb'], lw['lam'])
    if emit_kv:
        y_att = ctxatt_pallas(p3)
    else:
        y_att = natten_pallas(p3, ctx_kv[0], ctx_kv[1], tables, layer)
    x1, h2x, gid = merge_pallas(y_rnn.reshape(n, D_RNN), y_att.reshape(n, D_ATT), p, x, mod, lw['norm_g'][1:2],
                                lw['w_br_rnn'], lw['w_br_att'], lw['w_out'], lw['w_router_t'], lw['router_bias'],
                                layer, seq_len, per_seq, TM_MERGE)
    x2 = moe_pallas(h2x, gid, x1, mod, lw['g_final'], lw['w_exp_gate'], lw['w_exp_up'], lw['w_exp_down'], layer,
                    seq_len, per_seq, layer == DEPTH - 1)
    kv = (outs[1], outs[2]) if emit_kv else None
    return x2, kv, h_fin


def kernel(x_prompt, x_sample, cache_k, cache_v, state_lru, c, c_ctx, w_ada, b_ada, norm_g, w_in, conv_w,
           conv_b, lru_wa, lru_ba, lru_wx, lru_bx, lru_lam, rpb, w_br_rnn, w_br_att, w_out, w_router,
           router_bias, w_exp_gate, w_exp_up, w_exp_down, final_norm_g):
    cvecs = jnp.concatenate([c, c_ctx[None, :], jnp.zeros((V7X_SUBLANES - DEC_BATCH - 1, D_MODEL), F32)], axis=0)
    mods = adaln_pallas(cvecs, w_ada, b_ada).reshape(DEPTH, V7X_SUBLANES, N_MOD, D_MODEL)
    tables = _natten_tables(rpb)
    w_router_t = w_router.T
    xp = x_prompt.reshape(BATCH * SEQ, D_MODEL)
    xs = x_sample.reshape(DEC_BATCH * DEC_SEQ, D_MODEL)
    zeros_h0 = jnp.zeros((BATCH, 2, D_RNN), F32)
    w_exp = [w.astype(BF16) for w in (w_exp_gate, w_exp_up, w_exp_down)]
    w_proj = [w.astype(BF16) for w in (w_in, w_br_rnn, w_br_att, w_out)]
    ks, vs, hs = [], [], []
    for l in range(DEPTH):
        lw = dict(
            norm_g=norm_g[l], w_in=w_proj[0], conv_w=conv_w[l], conv_b=conv_b[l][None, :],
            wg=_lru_gate_weights(lru_wa[l], lru_wx[l]),
            gb=jnp.stack([lru_ba[l, 0], lru_bx[l, 0], lru_ba[l, 1], lru_bx[l, 1]], axis=0), lam=lru_lam[l],
            w_br_rnn=w_proj[1], w_br_att=w_proj[2], w_out=w_proj[3],
            w_router_t=w_router_t, router_bias=router_bias, g_final=final_norm_g[None, :],
            w_exp_gate=w_exp[0], w_exp_up=w_exp[1], w_exp_down=w_exp[2])
        xp, kv, h_l = _layer(xp, mods[l, DEC_BATCH:DEC_BATCH + 1], SEQ, False, lw, zeros_h0, None, None, l)
        ks.append(kv[0])
        vs.append(kv[1])
        hs.append(h_l)
        ctx_kv = (cache_k[:, l].reshape(DEC_BATCH, PAST_LEN, D_ATT).astype(BF16),
                  cache_v[:, l].reshape(DEC_BATCH, PAST_LEN, D_ATT).astype(BF16))
        xs, _, _ = _layer(xs, mods[l, :DEC_BATCH], DEC_SEQ, True, lw, state_lru[:, l], ctx_kv, tables, l)
    y_prompt = xp.reshape(BATCH, SEQ, D_MODEL)
    y_sample = xs.reshape(DEC_BATCH, DEC_SEQ, D_MODEL)
    return (y_prompt, y_sample, jnp.stack(ks, axis=1), jnp.stack(vs, axis=1), jnp.stack(hs, axis=1))
```

```python
import functools

import jax
import jax.numpy as jnp
import numpy as np
from jax import lax
from jax.experimental import pallas as pl
from jax.experimental.pallas import tpu as pltpu

D_MODEL = 1024
BATCH = 16
SEQ = 256
DEPTH = 2
DEC_BATCH = 4
DEC_SEQ = 4096
PAST_LEN = 256

GRID_W = 64
D_RNN = 1024
N_LRU_BLOCKS = 16
LRU_BLOCK = D_RNN // N_LRU_BLOCKS
CONV_W = 4
LRU_C = 8.0
N_HEADS = 16
HEAD_DIM = 64
D_ATT = N_HEADS * HEAD_DIM
WIN_H = 8
WIN_W = 16
N_EXPERTS = 16
N_GROUPS = 4
EXPERTS_PER_GROUP = N_EXPERTS // N_GROUPS
D_EXPERT = 512
N_MOD = 6
D_IN = 2 * D_RNN + 3 * D_ATT + 2 * D_MODEL
EPS = 1e-6
NEG_INF = -1e30

BF16 = jnp.bfloat16
F32 = jnp.float32

V7X_LANES = 128
V7X_SUBLANES = 8
V7X_MXU_DIM = 256
V7X_VMEM_BYTES = 64 * 1024 * 1024
VMEM_LIMIT = V7X_VMEM_BYTES - 8 * 1024 * 1024

CHUNK = D_MODEL
N_CHUNKS = D_IN // CHUNK
C_XRNN, C_GRNN, C_Q, C_K, C_V, C_GATE_R, C_GATE_A = range(N_CHUNKS)

NT_DIMS = (((1,), (1,)), ((), ()))


def _params(*sem):
    return pltpu.CompilerParams(dimension_semantics=sem, vmem_limit_bytes=VMEM_LIMIT)


def _adaln_kernel(c_ref, w_ref, b_ref, o_ref):
    cv = c_ref[...]
    s = cv * jax.nn.sigmoid(cv)
    o_ref[0] = jnp.dot(s.astype(BF16), w_ref[0].astype(BF16), preferred_element_type=F32) + b_ref[0]


def adaln_pallas(cvecs, w_ada, b_ada):
    r = cvecs.shape[0]
    return pl.pallas_call(
        _adaln_kernel,
        grid=(DEPTH, N_MOD),
        in_specs=[pl.BlockSpec((r, D_MODEL), lambda l, j: (0, 0)),
                  pl.BlockSpec((1, D_MODEL, D_MODEL), lambda l, j: (l, 0, j)),
                  pl.BlockSpec((1, 1, D_MODEL), lambda l, j: (l, 0, j))],
        out_specs=pl.BlockSpec((1, r, D_MODEL), lambda l, j: (l, 0, j)),
        out_shape=jax.ShapeDtypeStruct((DEPTH, r, N_MOD * D_MODEL), F32),
        compiler_params=_params("arbitrary", "arbitrary"),
        name="adaln",
    )(cvecs, w_ada, b_ada.reshape(DEPTH, 1, N_MOD * D_MODEL))


def _mod_spec(tm, seq_len, per_seq):
    if per_seq:
        return pl.BlockSpec((1, N_MOD, D_MODEL), lambda i, *_: (i * tm // seq_len, 0, 0))
    return pl.BlockSpec((1, N_MOD, D_MODEL), lambda i, *_: (0, 0, 0))


def _rms(x, g):
    return x * lax.rsqrt(jnp.mean(x * x, axis=-1, keepdims=True) + EPS) * g


def _inproj_kernel(x_ref, mod_ref, g_ref, w_ref, *refs, emit_kv):
    if emit_kv:
        p_ref, k32_ref, v32_ref, h_scr = refs[-4:]
    else:
        p_ref, h_scr = refs
    j = pl.program_id(1)

    @pl.when(j == 0)
    def _():
        y = _rms(x_ref[...], g_ref[...])
        h_scr[...] = (y * (1.0 + mod_ref[0, 1:2, :]) + mod_ref[0, 0:1, :]).astype(BF16)

    acc = jnp.dot(h_scr[...], w_ref[0], preferred_element_type=F32)
    p_ref[...] = (acc * jnp.where(j == C_Q, HEAD_DIM ** -0.5, 1.0)).astype(BF16)
    if emit_kv:
        @pl.when(j == C_K)
        def _():
            k32_ref[...] = acc.reshape(k32_ref.shape)

        @pl.when(j == C_V)
        def _():
            v32_ref[...] = acc.reshape(v32_ref.shape)


def inproj_pallas(x, mod, g, w_in, layer, seq_len, per_seq, tm, emit_kv, caches=()):
    n = x.shape[0]
    row = lambda i, j: (i, 0)
    in_specs = [pl.BlockSpec((tm, D_MODEL), row), _mod_spec(tm, seq_len, per_seq),
                pl.BlockSpec((1, D_MODEL), lambda i, j: (0, 0)),
                pl.BlockSpec((1, D_MODEL, CHUNK), lambda i, j: (layer, 0, j))]
    out_shape = [jax.ShapeDtypeStruct((n, D_IN), BF16)]
    out_specs = [pl.BlockSpec((tm, CHUNK), lambda i, j: (i, j))]
    aliases = {}
    if emit_kv:
        out_shape += [jax.ShapeDtypeStruct((n // seq_len, DEPTH, seq_len, N_HEADS, HEAD_DIM), F32)] * 2
        out_specs += [pl.BlockSpec((tm // seq_len, 1, seq_len, N_HEADS, HEAD_DIM),
                                   lambda i, j: (i, layer, 0, 0, 0))] * 2
        aliases = {len(in_specs) + k: 1 + k for k in range(len(caches))}
        in_specs += [pl.BlockSpec(memory_space=pl.ANY)] * len(caches)
    return pl.pallas_call(
        functools.partial(_inproj_kernel, emit_kv=emit_kv),
        grid=(n // tm, N_CHUNKS),
        in_specs=in_specs,
        out_specs=out_specs,
        out_shape=out_shape,
        input_output_aliases=aliases,
        scratch_shapes=[pltpu.VMEM((tm, D_MODEL), BF16)],
        compiler_params=_params("arbitrary", "arbitrary"),
        name="inproj",
    )(x, mod, g, w_in, *caches)


LRU_CB = 512
LRU_TC = 256
LRU_SUB = V7X_MXU_DIM
LRU_HALO = 16


LRU_SEG = LRU_TC // V7X_SUBLANES
LRU_NSLAB = LRU_SEG + CONV_W


def _lru_row_maps():
    s = np.arange(V7X_SUBLANES)[None, :]
    src = (LRU_HALO - CONV_W // 2) + LRU_SEG * s + np.arange(LRU_NSLAB)[:, None]
    sel = np.zeros((LRU_NSLAB * V7X_SUBLANES, LRU_TC + 2 * LRU_HALO), np.float32)
    sel[np.arange(sel.shape[0]), src.reshape(-1)] = 1.0
    tok = (LRU_SEG * s + np.arange(LRU_SEG)[:, None]).reshape(-1)
    perm = np.zeros((LRU_TC, LRU_TC), np.float32)
    perm[np.arange(LRU_TC), tok] = 1.0
    return jnp.asarray(sel, BF16), jnp.asarray(perm, BF16), jnp.asarray(perm.T, BF16)


def _slab_scan(a3, b3, carry, reverse):
    n = a3.shape[0]
    hs, cum = [None] * n, [None] * n
    h = cp = None
    for t in (range(n - 1, -1, -1) if reverse else range(n)):
        h = b3[t] if h is None else a3[t] * h + b3[t]
        cp = a3[t] if cp is None else cp * a3[t]
        hs[t], cum[t] = h, cp
    sub = lax.broadcasted_iota(jnp.int32, (V7X_SUBLANES, 1), 0)
    pa, pb = cp, h
    s = 1
    while s < V7X_SUBLANES:
        shift = V7X_SUBLANES - s if reverse else s
        ok = (sub < V7X_SUBLANES - s) if reverse else (sub >= s)
        a_sh = pltpu.roll(pa, shift, axis=0)
        b_sh = pltpu.roll(pb, shift, axis=0)
        pb = jnp.where(ok, pa * b_sh + pb, pb)
        pa = jnp.where(ok, pa * a_sh, pa)
        s *= 2
    leaving = pb + pa * carry
    first, last = (V7X_SUBLANES - 1, 0) if reverse else (0, V7X_SUBLANES - 1)
    entering = jnp.where(sub == first, carry, pltpu.roll(leaving, V7X_SUBLANES - 1 if reverse else 1, axis=0))
    out = jnp.concatenate([hs[t] + cum[t] * entering for t in range(n)], axis=0)
    return out, leaving[last:last + 1, :]


def _sigmoid(x):
    return 0.5 * jnp.tanh(0.5 * x) + 0.5


def _lru_kernel(x_ref, gate_ref, h0_ref, cw_ref, cb_ref, wg_ref, gb_ref, lam_ref, sel_ref, perm_ref, permt_ref,
                y_ref, fin_ref, hf_scr, u_scr):
    t_len = x_ref.shape[1]
    cb = x_ref.shape[2]
    n_chunks = t_len // LRU_TC

    def conv_chunk(c):
        t0 = pl.multiple_of(c * LRU_TC, LRU_TC)
        cur = x_ref[0, pl.ds(t0, LRU_TC), :]
        lo = pl.multiple_of(jnp.maximum(t0 - LRU_HALO, 0), LRU_HALO)
        hi = pl.multiple_of(jnp.minimum(t0 + LRU_TC, t_len - LRU_HALO), LRU_HALO)
        prev = x_ref[0, pl.ds(lo, LRU_HALO), :]
        nxt = x_ref[0, pl.ds(hi, LRU_HALO), :]
        prev = jnp.where(c > 0, prev, jnp.zeros_like(prev))
        nxt = jnp.where(c < n_chunks - 1, nxt, jnp.zeros_like(nxt))
        ext = jnp.concatenate([prev, cur, nxt], axis=0)
        xs = jnp.dot(sel_ref[...], ext, preferred_element_type=F32).reshape(LRU_NSLAB, V7X_SUBLANES, cb)
        u = cb_ref[...][None] + jnp.zeros((LRU_SEG, V7X_SUBLANES, cb), F32)
        for j in range(CONV_W):
            u = u + xs[j:j + LRU_SEG] * cw_ref[j:j + 1, :][None]
        return t0, u.reshape(LRU_TC, cb)

    def slabs(v):
        return v.reshape(LRU_SEG, V7X_SUBLANES, cb)

    def gates(u, d):
        ub = u.astype(BF16)
        pre = [jnp.dot(ub[:, LRU_SUB * s:LRU_SUB * (s + 1)], wg_ref[d, s], preferred_element_type=F32)
               for s in range(cb // LRU_SUB)]
        pre_a = jnp.concatenate([p[:, :LRU_SUB] for p in pre], axis=1)
        pre_x = jnp.concatenate([p[:, LRU_SUB:] for p in pre], axis=1)
        r = _sigmoid(pre_a + gb_ref[2 * d:2 * d + 1, :])
        i = _sigmoid(pre_x + gb_ref[2 * d + 1:2 * d + 2, :])
        log_a = (-LRU_C * jax.nn.softplus(-lam_ref[d:d + 1, :])) * r
        a = jnp.exp(log_a)
        th = jnp.tanh(log_a)
        num = -2.0 * th
        scale = jnp.where(num > 0.0, num * lax.rsqrt(num * (1.0 - th)), 0.0)
        inp = scale * (i * u)
        return a, inp

    def fwd(c, carry):
        t0, u = conv_chunk(c)
        u_scr[pl.ds(t0, LRU_TC), :] = u
        a, inp = gates(u, 0)
        h, carry = _slab_scan(slabs(a), slabs(inp), carry, reverse=False)
        hf_scr[pl.ds(t0, LRU_TC), :] = h
        return carry

    unroll = 4 if n_chunks % 4 == 0 else 1
    fin_f = lax.fori_loop(0, n_chunks, fwd, h0_ref[0, 0:1, :], unroll=unroll)

    def bwd(k, carry):
        t0 = pl.multiple_of((n_chunks - 1 - k) * LRU_TC, LRU_TC)
        a, inp = gates(u_scr[pl.ds(t0, LRU_TC), :], 1)
        h, carry = _slab_scan(slabs(a), slabs(inp), carry, reverse=True)
        g = jnp.dot(perm_ref[...], gate_ref[0, pl.ds(t0, LRU_TC), :], preferred_element_type=F32)
        y = ((hf_scr[pl.ds(t0, LRU_TC), :] + h) * jax.nn.gelu(g)).astype(BF16)
        y_ref[0, pl.ds(t0, LRU_TC), :] = jnp.dot(permt_ref[...], y, preferred_element_type=F32).astype(y_ref.dtype)
        return carry

    fin_b = lax.fori_loop(0, n_chunks, bwd, h0_ref[0, 1:2, :], unroll=unroll)
    fin_ref[0, 0:1, :] = fin_f
    fin_ref[0, 1:2, :] = fin_b


def _lru_gate_weights(lru_wa, lru_wx):
    per = LRU_SUB // LRU_BLOCK
    eye = jnp.eye(per, dtype=F32)

    def dense(w):
        w = w.reshape(2, D_RNN // LRU_SUB, per, LRU_BLOCK, LRU_BLOCK)
        full = w[:, :, :, :, None, :] * eye[None, None, :, None, :, None]
        return full.reshape(2, D_RNN // LRU_SUB, LRU_SUB, LRU_SUB)

    return jnp.concatenate([dense(lru_wa), dense(lru_wx)], axis=-1).astype(BF16)


def lru_pallas(p, h0, conv_w, conv_b, wg, gb, lam):
    b, t, _ = p.shape
    n_cb = D_RNN // LRU_CB
    maps = _lru_row_maps()
    whole = lambda m: pl.BlockSpec(m.shape, lambda bi, ci: (0, 0))
    return pl.pallas_call(
        _lru_kernel,
        grid=(b, n_cb),
        in_specs=[pl.BlockSpec((1, t, LRU_CB), lambda bi, ci: (bi, 0, C_XRNN * n_cb + ci)),
                  pl.BlockSpec((1, t, LRU_CB), lambda bi, ci: (bi, 0, C_GRNN * n_cb + ci)),
                  pl.BlockSpec((1, 2, LRU_CB), lambda bi, ci: (bi, 0, ci)),
                  pl.BlockSpec((CONV_W, LRU_CB), lambda bi, ci: (0, ci)),
                  pl.BlockSpec((1, LRU_CB), lambda bi, ci: (0, ci)),
                  pl.BlockSpec((2, LRU_CB // LRU_SUB, LRU_SUB, 2 * LRU_SUB), lambda bi, ci: (0, ci, 0, 0)),
                  pl.BlockSpec((4, LRU_CB), lambda bi, ci: (0, ci)),
                  pl.BlockSpec((2, LRU_CB), lambda bi, ci: (0, ci))] + [whole(m) for m in maps],
        out_specs=[pl.BlockSpec((1, t, LRU_CB), lambda bi, ci: (bi, 0, ci)),
                   pl.BlockSpec((1, 2, LRU_CB), lambda bi, ci: (bi, 0, ci))],
        out_shape=[jax.ShapeDtypeStruct((b, t, D_RNN), BF16), jax.ShapeDtypeStruct((b, 2, D_RNN), F32)],
        scratch_shapes=[pltpu.VMEM((t, LRU_CB), F32), pltpu.VMEM((t, LRU_CB), F32)],
        compiler_params=_params("arbitrary", "arbitrary"),
        name="lru",
    )(p, p, h0, conv_w, conv_b, wg, gb, lam, *maps)


Q_ROWS = 4
Q_TILE = Q_ROWS * GRID_W
KEY_ROWS = 12
KEY_TILE = KEY_ROWS * GRID_W
N_Q_TILES = DEC_SEQ // Q_TILE
ATT_LANES = 1024
HEADS_PER_STEP = ATT_LANES // HEAD_DIM
PAIR = 2 * HEAD_DIM


def _window_block(t):
    return jnp.clip(t - 1, 0, N_Q_TILES - KEY_ROWS // Q_ROWS)


N_DR_PAIRS = 2 * WIN_H


def _natten_tables(rpb):
    c = np.arange(GRID_W)
    q_start = np.clip(c - WIN_W // 2, 0, GRID_W - WIN_W)
    col_valid = (c[None, :] >= q_start[:, None]) & (c[None, :] < q_start[:, None] + WIN_W)
    dc = c[None, :] - c[:, None] + WIN_W - 1
    sel_c = ((dc[None] == np.arange(2 * WIN_W - 1)[:, None, None]) & col_valid[None]).astype(np.float32)
    t = jnp.einsum('lhrd,dck->lhrck', rpb.astype(F32), jnp.asarray(sel_c), precision=lax.Precision.HIGHEST)
    t = jnp.where(col_valid[None, None, None], t, NEG_INF)
    t = jnp.pad(t, ((0, 0), (0, 0), (1, 1), (0, 0), (0, 0)), constant_values=NEG_INF)
    return jnp.concatenate([t[:, :, :-1], t[:, :, 1:]], axis=-1)


def _attend(q2, keys, vals, biases):
    lane_head = lax.broadcasted_iota(jnp.int32, (1, PAIR), 1) // HEAD_DIM
    acc = jnp.zeros((q2.shape[0], PAIR), F32)
    for j in range(2):
        mine = lane_head == j
        qh = jnp.where(mine, q2, jnp.zeros_like(q2))
        s = []
        for kb, bb in zip(keys, biases[j]):
            sd = lax.dot_general(qh, kb, NT_DIMS, preferred_element_type=F32)
            s.append(sd if bb is None else sd + bb)
        m = s[0].max(axis=-1, keepdims=True)
        for sd in s[1:]:
            m = jnp.maximum(m, sd.max(axis=-1, keepdims=True))
        o = None
        for sd, vb in zip(s, vals):
            pv = jnp.dot(jnp.exp(sd - m).astype(BF16), jnp.where(mine, vb, jnp.ones_like(vb)),
                         preferred_element_type=F32)
            o = pv if o is None else o + pv
        acc = acc + jnp.where(mine, o / pltpu.roll(o, HEAD_DIM, axis=1), 0.0)
    return acc


def _natten_kernel(q_ref, k0_ref, k1_ref, k2_ref, v0_ref, v1_ref, v2_ref, kc_ref, vc_ref, tt_ref, o_ref):
    ti = pl.program_id(1)
    rows = DEC_SEQ // GRID_W
    wstart = _window_block(ti) * Q_ROWS
    first_row = lax.broadcasted_iota(jnp.int32, (1, 2 * GRID_W), 1) < GRID_W
    pieces = {}
    for a in range(Q_ROWS):
        r = ti * Q_ROWS + a
        start_r = jnp.clip(r - WIN_H // 2, 0, rows - WIN_H)
        for i in range(0, KEY_ROWS, 2):
            kr = wstart + i
            ok = [((kr + e >= start_r) & (kr + e < start_r + WIN_H)).astype(jnp.int32) for e in range(2)]
            pieces[a, i] = (jnp.clip(kr - r + WIN_H, 0, N_DR_PAIRS - 1), jnp.where(first_row, ok[0], ok[1]) > 0)

    def bias_block(h, d):
        return jnp.concatenate(
            [jnp.concatenate([jnp.where(pieces[a, i][1], tt_ref[0, h, pieces[a, i][0]], NEG_INF)
                              for i in range(Q_ROWS * d, Q_ROWS * (d + 1), 2)], axis=1)
             for a in range(Q_ROWS)], axis=0)

    k_refs = (k0_ref, k1_ref, k2_ref, kc_ref)
    v_refs = (v0_ref, v1_ref, v2_ref, vc_ref)
    outs = []
    for hp in range(ATT_LANES // PAIR):
        sl = slice(PAIR * hp, PAIR * (hp + 1))
        biases = [[bias_block(2 * hp + j, d) for d in range(3)] + [None] for j in range(2)]
        outs.append(_attend(q_ref[0, :, sl], [r[0, :, sl] for r in k_refs], [r[0, :, sl] for r in v_refs], biases))
    o_ref[0] = jnp.concatenate(outs, axis=1).astype(o_ref.dtype)


def natten_pallas(p, k_ctx, v_ctx, tables, layer):
    b, t, _ = p.shape
    n_lb = D_ATT // ATT_LANES
    q_spec = pl.BlockSpec((1, Q_TILE, ATT_LANES), lambda lb, ti, bi: (bi, ti, C_Q * n_lb + lb))
    win = lambda ch, d: pl.BlockSpec((1, Q_TILE, ATT_LANES),
                                     lambda lb, ti, bi: (bi, _window_block(ti) + d, ch * n_lb + lb))
    ctx = pl.BlockSpec((1, PAST_LEN, ATT_LANES), lambda lb, ti, bi: (bi, 0, lb))
    return pl.pallas_call(
        _natten_kernel,
        grid=(n_lb, N_Q_TILES, b),
        in_specs=[q_spec, win(C_K, 0), win(C_K, 1), win(C_K, 2), win(C_V, 0), win(C_V, 1), win(C_V, 2), ctx, ctx,
                  pl.BlockSpec((1, HEADS_PER_STEP, N_DR_PAIRS, GRID_W, 2 * GRID_W),
                               lambda lb, ti, bi: (layer, lb, 0, 0, 0))],
        out_specs=pl.BlockSpec((1, Q_TILE, ATT_LANES), lambda lb, ti, bi: (bi, ti, lb)),
        out_shape=jax.ShapeDtypeStruct((b, t, D_ATT), BF16),
        compiler_params=_params("arbitrary", "arbitrary", "arbitrary"),
        name="natten",
    )(p, p, p, p, p, p, p, k_ctx, v_ctx, tables)


def _ctxatt_kernel(q_ref, k_ref, v_ref, o_ref):
    outs = []
    for hp in range(ATT_LANES // PAIR):
        sl = slice(PAIR * hp, PAIR * (hp + 1))
        outs.append(_attend(q_ref[0, :, sl], [k_ref[0, :, sl]], [v_ref[0, :, sl]], [[None], [None]]))
    o_ref[0] = jnp.concatenate(outs, axis=1).astype(o_ref.dtype)


def ctxatt_pallas(p):
    b, t, _ = p.shape
    n_lb = D_ATT // ATT_LANES
    blk = lambda ch: pl.BlockSpec((1, t, ATT_LANES), lambda lb, bi: (bi, 0, ch * n_lb + lb))
    return pl.pallas_call(
        _ctxatt_kernel,
        grid=(n_lb, b),
        in_specs=[blk(C_Q), blk(C_K), blk(C_V)],
        out_specs=pl.BlockSpec((1, t, ATT_LANES), lambda lb, bi: (bi, 0, lb)),
        out_shape=jax.ShapeDtypeStruct((b, t, D_ATT), BF16),
        compiler_params=_params("arbitrary", "arbitrary"),
        name="ctxatt",
    )(p, p, p)


def _split_bf16(x):
    hi = x.astype(BF16)
    return hi, (x - hi.astype(F32)).astype(BF16)


def _route(logits_t, rb_ref):
    score = [jax.nn.sigmoid(logits_t[e:e + 1, :]) for e in range(N_EXPERTS)]
    sel = [score[e] + rb_ref[e] for e in range(N_EXPERTS)]
    best_g = None
    for g in range(N_GROUPS):
        v = sel[EXPERTS_PER_GROUP * g:EXPERTS_PER_GROUP * (g + 1)]
        top2 = None
        for i in range(EXPERTS_PER_GROUP):
            for j in range(i + 1, EXPERTS_PER_GROUP):
                pair = v[i] + v[j]
                top2 = pair if top2 is None else jnp.maximum(top2, pair)
        if best_g is None:
            best_g, best_v = jnp.zeros_like(top2, dtype=jnp.int32), top2
        else:
            upd = top2 > best_v
            best_g = jnp.where(upd, g, best_g)
            best_v = jnp.where(upd, top2, best_v)

    def in_best(vals, j):
        out = vals[j]
        for g in range(1, N_GROUPS):
            out = jnp.where(best_g == g, vals[EXPERTS_PER_GROUP * g + j], out)
        return out

    v = [in_best(sel, j) for j in range(EXPERTS_PER_GROUP)]
    sc = [in_best(score, j) for j in range(EXPERTS_PER_GROUP)]

    def first_argmax(vals):
        idx, top = jnp.zeros_like(best_g), vals[0]
        for j in range(1, EXPERTS_PER_GROUP):
            upd = vals[j] > top
            idx = jnp.where(upd, j, idx)
            top = jnp.where(upd, vals[j], top)
        return idx

    i1 = first_argmax(v)
    i2 = first_argmax([jnp.where(i1 == j, -jnp.inf, v[j]) for j in range(EXPERTS_PER_GROUP)])
    pick = lambda idx: sum(jnp.where(idx == j, sc[j], 0.0) for j in range(EXPERTS_PER_GROUP))
    w1, w2 = pick(i1), pick(i2)
    den = w1 + w2
    c1, c2 = w1 / den, w2 / den
    rows = []
    for e in range(N_EXPERTS):
        g, j = divmod(e, EXPERTS_PER_GROUP)
        rows.append(jnp.where(best_g == g, jnp.where(i1 == j, c1, 0.0) + jnp.where(i2 == j, c2, 0.0), 0.0))
    return jnp.concatenate(rows, axis=0), best_g


AUX_LANES = V7X_LANES
AUX_MID = N_EXPERTS
AUX_LO = 2 * N_EXPERTS
AUX_GROUP = 3 * N_EXPERTS
D_MOE_IN = D_MODEL + AUX_LANES


def _merge_kernel(rb_ref, yr_ref, ya_ref, gr_ref, ga_ref, x_ref, mod_ref, g_ref, wr_ref, wa_ref, wo_ref, wrt_ref,
                  x1_ref, h2_ref, gid_ref):
    m = (_sigmoid(gr_ref[...].astype(F32)) * jnp.dot(yr_ref[...], wr_ref[0], preferred_element_type=F32)
         + _sigmoid(ga_ref[...].astype(F32)) * jnp.dot(ya_ref[...], wa_ref[0], preferred_element_type=F32))
    o = jnp.dot(m.astype(BF16), wo_ref[0], preferred_element_type=F32)
    x1 = x_ref[...] + mod_ref[0, 2:3, :] * o
    x1_ref[...] = x1
    h2 = _rms(x1, g_ref[...]) * (1.0 + mod_ref[0, 4:5, :]) + mod_ref[0, 3:4, :]
    h_hi, h_lo = _split_bf16(h2)
    w_hi, w_lo = _split_bf16(wrt_ref[...])
    dot_nt = lambda a, b: lax.dot_general(a, b, NT_DIMS, preferred_element_type=F32)
    by_h_hi = dot_nt(jnp.concatenate([w_hi, w_lo], axis=0), h_hi)
    logits_t = by_h_hi[:N_EXPERTS] + (dot_nt(w_hi, h_lo) + by_h_hi[N_EXPERTS:])
    comb_t, group = _route(logits_t, rb_ref)
    tm = comb_t.shape[1]
    group = group.astype(F32)
    gid_ref[...] = jnp.concatenate([group, jnp.zeros((V7X_SUBLANES - 1, tm), F32)], axis=0)
    padded = jnp.concatenate([comb_t, jnp.zeros((AUX_GROUP - N_EXPERTS, tm), F32), group,
                              jnp.zeros((AUX_LANES - AUX_GROUP - 1, tm), F32)], axis=0)
    c = padded.T
    c_hi = c.astype(BF16).astype(F32)
    c_mid = (c - c_hi).astype(BF16).astype(F32)
    c_lo = c - c_hi - c_mid
    aux = c_hi + pltpu.roll(c_mid, AUX_MID, axis=1) + pltpu.roll(c_lo, AUX_LO, axis=1)
    h2_ref[...] = jnp.concatenate([h2.astype(BF16), aux.astype(BF16)], axis=1)


def merge_pallas(y_rnn, y_att, p, x, mod, g, w_br_rnn, w_br_att, w_out, w_router_t, router_bias, layer, seq_len,
                 per_seq, tm):
    n = x.shape[0]
    row = lambda i: (i, 0)
    full = lambda shape: pl.BlockSpec(shape, lambda i: (0, 0))
    weight = pl.BlockSpec((1, D_MODEL, D_MODEL), lambda i: (layer, 0, 0))
    return pl.pallas_call(
        _merge_kernel,
        grid=(n // tm,),
        in_specs=[pl.BlockSpec(memory_space=pltpu.SMEM),
                  pl.BlockSpec((tm, D_RNN), row), pl.BlockSpec((tm, D_ATT), row),
                  pl.BlockSpec((tm, CHUNK), lambda i: (i, C_GATE_R)), pl.BlockSpec((tm, CHUNK), lambda i: (i, C_GATE_A)),
                  pl.BlockSpec((tm, D_MODEL), row), _mod_spec(tm, seq_len, per_seq), full((1, D_MODEL)),
                  weight, weight, weight, full((N_EXPERTS, D_MODEL))],
        out_specs=[pl.BlockSpec((tm, D_MODEL), row), pl.BlockSpec((tm, D_MOE_IN), row),
                   pl.BlockSpec((V7X_SUBLANES, tm), lambda i: (0, i))],
        out_shape=[jax.ShapeDtypeStruct((n, D_MODEL), F32), jax.ShapeDtypeStruct((n, D_MOE_IN), BF16),
                   jax.ShapeDtypeStruct((V7X_SUBLANES, n), F32)],
        compiler_params=_params("arbitrary"),
        name="merge",
    )(router_bias, y_rnn, y_att, p, p, x, mod, g, w_br_rnn, w_br_att, w_out, w_router_t)


MOE_TB = 1024
MOE_RT = 128
MOE_TBP = MOE_TB + N_GROUPS * MOE_RT
MOE_EPS = 2


def _group_segments(gid_row):
    sub = lax.broadcasted_iota(jnp.int32, (V7X_SUBLANES, 1), 0).astype(F32)
    onehot = (gid_row == sub).astype(F32)
    cnt = jnp.sum(onehot, axis=1, keepdims=True)
    padded = jnp.floor((cnt + (MOE_RT - 1)) * (1.0 / MOE_RT)) * MOE_RT
    starts, run = [], jnp.zeros((1, 1), F32)
    for g in range(N_GROUPS):
        starts.append(run)
        run = run + padded[g:g + 1, :]
    return onehot, starts, [padded[g:g + 1, :] for g in range(N_GROUPS)], run


def _to_int(v):
    return v[0, 0].astype(jnp.int32)


def _moe_kernel(h_ref, gid_ref, x_ref, mod_ref, gf_ref, wg_ref, wu_ref, wd_ref, o_ref,
                p_scr, xs_scr, cs_scr, ys_scr, *, final_norm):
    step = pl.program_id(1)
    onehot, starts, sizes, used = _group_segments(gid_ref[0:1, :])
    lane = lax.broadcasted_iota(jnp.int32, (1, AUX_LANES), 1)

    @pl.when(step == 0)
    def _():
        t_row = lax.broadcasted_iota(jnp.int32, (MOE_TB, MOE_TB), 0)
        t_col = lax.broadcasted_iota(jnp.int32, (MOE_TB, MOE_TB), 1)
        earlier = (t_row < t_col).astype(BF16)
        rank = jnp.dot(onehot.astype(BF16), earlier, preferred_element_type=F32)
        pos = jnp.zeros((1, MOE_TB), F32)
        for g in range(N_GROUPS):
            pos = pos + onehot[g:g + 1, :] * (starts[g] + rank[g:g + 1, :])
        dest = lax.broadcasted_iota(jnp.int32, (MOE_TBP, 1), 0).astype(F32)
        p_scr[...] = (dest == pos).astype(BF16)
        for r0 in range(0, MOE_TBP, V7X_MXU_DIM):
            rows = pl.ds(r0, V7X_MXU_DIM)
            sorted_rows = jnp.dot(p_scr[rows, :], h_ref[...], preferred_element_type=F32)
            xs_scr[rows, :] = sorted_rows[:, :D_MODEL].astype(BF16)
            aux = sorted_rows[:, D_MODEL:]
            cs_scr[rows, :] = (aux + pltpu.roll(aux, AUX_LANES - AUX_MID, axis=1)
                               + pltpu.roll(aux, AUX_LANES - AUX_LO, axis=1))
        first_free = pl.multiple_of(_to_int(used), MOE_RT)

        def clear(k, carry):
            ys_scr[pl.ds(pl.multiple_of(first_free + k * MOE_RT, MOE_RT), MOE_RT), :] = jnp.zeros(
                (MOE_RT, D_MODEL), F32)
            return carry

        lax.fori_loop(0, (MOE_TBP - first_free) // MOE_RT, clear, 0)

    steps_per_group = EXPERTS_PER_GROUP // MOE_EPS
    group = step // steps_per_group
    first_of_group = step % steps_per_group == 0
    start_v, size_v = starts[0], sizes[0]
    for g in range(1, N_GROUPS):
        start_v = jnp.where(group == g, starts[g], start_v)
        size_v = jnp.where(group == g, sizes[g], size_v)
    seg_start = _to_int(start_v)
    n_tiles = _to_int(size_v) // MOE_RT

    def run_expert(r0, n_rows):
        rows = pl.ds(pl.multiple_of(r0, MOE_RT), n_rows)
        x = xs_scr[rows, :]
        cs = cs_scr[rows, :]
        y = None
        for k in range(MOE_EPS):
            gate = jnp.dot(x, wg_ref[0, k], preferred_element_type=F32)
            up = jnp.dot(x, wu_ref[0, k], preferred_element_type=F32)
            act = (gate * _sigmoid(gate)) * up
            yk = jnp.dot(act.astype(BF16), wd_ref[0, k], preferred_element_type=F32)
            yk = jnp.sum(jnp.where(lane == step * MOE_EPS + k, cs, 0.0), axis=-1, keepdims=True) * yk
            y = yk if y is None else y + yk

        @pl.when(first_of_group)
        def _():
            ys_scr[rows, :] = y

        @pl.when(jnp.logical_not(first_of_group))
        def _():
            ys_scr[rows, :] += y

    def pair(k, carry):
        run_expert(seg_start + k * (2 * MOE_RT), 2 * MOE_RT)
        return carry

    lax.fori_loop(0, n_tiles // 2, pair, 0)

    @pl.when(n_tiles % 2 == 1)
    def _():
        run_expert(seg_start + (n_tiles - 1) * MOE_RT, MOE_RT)

    @pl.when(step == N_EXPERTS // MOE_EPS - 1)
    def _():
        y = lax.dot_general(p_scr[...], ys_scr[...].astype(BF16), (((0,), (0,)), ((), ())),
                            preferred_element_type=F32)
        x2 = x_ref[...] + mod_ref[0, 5:6, :] * y
        o_ref[...] = _rms(x2, gf_ref[...]) if final_norm else x2


def moe_pallas(h2x, gid, x1, mod, g_final, w_gate, w_up, w_down, layer, seq_len, per_seq, final_norm):
    n = x1.shape[0]
    row = lambda i, e: (i, 0)
    expert = lambda i, e: (layer, e, 0, 0)
    return pl.pallas_call(
        functools.partial(_moe_kernel, final_norm=final_norm),
        grid=(n // MOE_TB, N_EXPERTS // MOE_EPS),
        in_specs=[pl.BlockSpec((MOE_TB, D_MOE_IN), row), pl.BlockSpec((V7X_SUBLANES, MOE_TB), lambda i, e: (0, i)),
                  pl.BlockSpec((MOE_TB, D_MODEL), row), _mod_spec(MOE_TB, seq_len, per_seq),
                  pl.BlockSpec((1, D_MODEL), lambda i, e: (0, 0)),
                  pl.BlockSpec((1, MOE_EPS, D_MODEL, D_EXPERT), expert),
                  pl.BlockSpec((1, MOE_EPS, D_MODEL, D_EXPERT), expert),
                  pl.BlockSpec((1, MOE_EPS, D_EXPERT, D_MODEL), expert)],
        out_specs=pl.BlockSpec((MOE_TB, D_MODEL), row),
        out_shape=jax.ShapeDtypeStruct((n, D_MODEL), F32),
        scratch_shapes=[pltpu.VMEM((MOE_TBP, MOE_TB), BF16), pltpu.VMEM((MOE_TBP, D_MODEL), BF16),
                        pltpu.VMEM((MOE_TBP, AUX_LANES), F32), pltpu.VMEM((MOE_TBP, D_MODEL), F32)],
        compiler_params=_params("arbitrary", "arbitrary"),
        name="moe",
    )(h2x, gid, x1, mod, g_final, w_gate, w_up, w_down)


TM_PROJ = 2048
TM_MERGE = 1024


def _layer(x, mod, seq_len, per_seq, lw, h0, ctx_kv, tables, layer, caches=()):
    n = x.shape[0]
    b = n // seq_len
    emit_kv = ctx_kv is None
    outs = inproj_pallas(x, mod, lw['norm_g'][0:1], lw['w_in'], layer, seq_len, per_seq,
                         TM_PROJ // 4 if emit_kv else TM_PROJ, emit_kv, caches)
    p = outs[0]
    p3 = p.reshape(b, seq_len, D_IN)
    y_rnn, h_fin = lru_pallas(p3, h0, lw['conv_w'], lw['conv_b'], lw['wg'], lw['gb'], lw['lam'])
    if emit_kv:
        y_att = ctxatt_pallas(p3)
    else:
        y_att = natten_pallas(p3, ctx_kv[0], ctx_kv[1], tables, layer)
    x1, h2x, gid = merge_pallas(y_rnn.reshape(n, D_RNN), y_att.reshape(n, D_ATT), p, x, mod, lw['norm_g'][1:2],
                                lw['w_br_rnn'], lw['w_br_att'], lw['w_out'], lw['w_router_t'], lw['router_bias'],
                                layer, seq_len, per_seq, TM_MERGE)
    x2 = moe_pallas(h2x, gid, x1, mod, lw['g_final'], lw['w_exp_gate'], lw['w_exp_up'], lw['w_exp_down'], layer,
                    seq_len, per_seq, layer == DEPTH - 1)
    kv = (outs[1], outs[2]) if emit_kv else None
    return x2, kv, h_fin


def kernel(x_prompt, x_sample, cache_k, cache_v, state_lru, c, c_ctx, w_ada, b_ada, norm_g, w_in, conv_w,
           conv_b, lru_wa, lru_ba, lru_wx, lru_bx, lru_lam, rpb, w_br_rnn, w_br_att, w_out, w_router,
           router_bias, w_exp_gate, w_exp_up, w_exp_down, final_norm_g):
    cvecs = jnp.concatenate([c, c_ctx[None, :], jnp.zeros((V7X_SUBLANES - DEC_BATCH - 1, D_MODEL), F32)], axis=0)
    mods = adaln_pallas(cvecs, w_ada, b_ada).reshape(DEPTH, V7X_SUBLANES, N_MOD, D_MODEL)
    tables = _natten_tables(rpb)
    w_router_t = w_router.T
    xp = x_prompt.reshape(BATCH * SEQ, D_MODEL)
    xs = x_sample.reshape(DEC_BATCH * DEC_SEQ, D_MODEL)
    zeros_h0 = jnp.zeros((BATCH, 2, D_RNN), F32)
    w_exp = [w.astype(BF16) for w in (w_exp_gate, w_exp_up, w_exp_down)]
    w_proj = [w.astype(BF16) for w in (w_in, w_br_rnn, w_br_att, w_out)]
    caches, hs = (), []
    for l in range(DEPTH):
        lw = dict(
            norm_g=norm_g[l], w_in=w_proj[0], conv_w=conv_w[l], conv_b=conv_b[l][None, :],
            wg=_lru_gate_weights(lru_wa[l], lru_wx[l]),
            gb=jnp.stack([lru_ba[l, 0], lru_bx[l, 0], lru_ba[l, 1], lru_bx[l, 1]], axis=0), lam=lru_lam[l],
            w_br_rnn=w_proj[1], w_br_att=w_proj[2], w_out=w_proj[3],
            w_router_t=w_router_t, router_bias=router_bias, g_final=final_norm_g[None, :],
            w_exp_gate=w_exp[0], w_exp_up=w_exp[1], w_exp_down=w_exp[2])
        xp, caches, h_l = _layer(xp, mods[l, DEC_BATCH:DEC_BATCH + 1], SEQ, False, lw, zeros_h0, None, None, l,
                                 caches)
        hs.append(h_l)
        ctx_kv = (cache_k[:, l].reshape(DEC_BATCH, PAST_LEN, D_ATT).astype(BF16),
                  cache_v[:, l].reshape(DEC_BATCH, PAST_LEN, D_ATT).astype(BF16))
        xs, _, _ = _layer(xs, mods[l, :DEC_BATCH], DEC_SEQ, True, lw, state_lru[:, l], ctx_kv, tables, l)
    y_prompt = xp.reshape(BATCH, SEQ, D_MODEL)
    y_sample = xs.reshape(DEC_BATCH, DEC_SEQ, D_MODEL)
    return (y_prompt, y_sample, caches[0], caches[1], jnp.stack(hs, axis=1))
```

```python
import functools

import jax
import jax.numpy as jnp
import numpy as np
from jax import lax
from jax.experimental import pallas as pl
from jax.experimental.pallas import tpu as pltpu

D_MODEL = 1024
BATCH = 16
SEQ = 256
DEPTH = 2
DEC_BATCH = 4
DEC_SEQ = 4096
PAST_LEN = 256

GRID_W = 64
D_RNN = 1024
N_LRU_BLOCKS = 16
LRU_BLOCK = D_RNN // N_LRU_BLOCKS
CONV_W = 4
LRU_C = 8.0
N_HEADS = 16
HEAD_DIM = 64
D_ATT = N_HEADS * HEAD_DIM
WIN_H = 8
WIN_W = 16
N_EXPERTS = 16
N_GROUPS = 4
EXPERTS_PER_GROUP = N_EXPERTS // N_GROUPS
D_EXPERT = 512
N_MOD = 6
D_IN = 2 * D_RNN + 3 * D_ATT + 2 * D_MODEL
EPS = 1e-6
NEG_INF = -1e30
LOG2_E = 1.4426950408889634

BF16 = jnp.bfloat16
F32 = jnp.float32

V7X_LANES = 128
V7X_SUBLANES = 8
V7X_MXU_DIM = 256
V7X_VMEM_BYTES = 64 * 1024 * 1024
VMEM_LIMIT = V7X_VMEM_BYTES - 8 * 1024 * 1024

CHUNK = D_MODEL
N_CHUNKS = D_IN // CHUNK
C_XRNN, C_GRNN, C_Q, C_K, C_V, C_GATE_R, C_GATE_A = range(N_CHUNKS)

NT_DIMS = (((1,), (1,)), ((), ()))


def _params(*sem):
    return pltpu.CompilerParams(dimension_semantics=sem, vmem_limit_bytes=VMEM_LIMIT)


def _adaln_kernel(c_ref, w_ref, b_ref, o_ref):
    cv = c_ref[...]
    s = cv * jax.nn.sigmoid(cv)
    o_ref[0] = jnp.dot(s.astype(BF16), w_ref[0].astype(BF16), preferred_element_type=F32) + b_ref[0]


def adaln_pallas(cvecs, w_ada, b_ada):
    r = cvecs.shape[0]
    return pl.pallas_call(
        _adaln_kernel,
        grid=(DEPTH, N_MOD),
        in_specs=[pl.BlockSpec((r, D_MODEL), lambda l, j: (0, 0)),
                  pl.BlockSpec((1, D_MODEL, D_MODEL), lambda l, j: (l, 0, j)),
                  pl.BlockSpec((1, 1, D_MODEL), lambda l, j: (l, 0, j))],
        out_specs=pl.BlockSpec((1, r, D_MODEL), lambda l, j: (l, 0, j)),
        out_shape=jax.ShapeDtypeStruct((DEPTH, r, N_MOD * D_MODEL), F32),
        compiler_params=_params("arbitrary", "arbitrary"),
        name="adaln",
    )(cvecs, w_ada, b_ada.reshape(DEPTH, 1, N_MOD * D_MODEL))


def _mod_spec(tm, seq_len, per_seq):
    if per_seq:
        return pl.BlockSpec((1, N_MOD, D_MODEL), lambda i, *_: (i * tm // seq_len, 0, 0))
    return pl.BlockSpec((1, N_MOD, D_MODEL), lambda i, *_: (0, 0, 0))


def _rms(x, g):
    return x * lax.rsqrt(jnp.mean(x * x, axis=-1, keepdims=True) + EPS) * g


def _inproj_kernel(x_ref, mod_ref, g_ref, w_ref, *refs, emit_kv):
    if emit_kv:
        p_ref, k32_ref, v32_ref, h_scr = refs[-4:]
    else:
        p_ref, h_scr = refs
    j = pl.program_id(1)

    @pl.when(j == 0)
    def _():
        y = _rms(x_ref[...], g_ref[...])
        h_scr[...] = (y * (1.0 + mod_ref[0, 1:2, :]) + mod_ref[0, 0:1, :]).astype(BF16)

    acc = jnp.dot(h_scr[...], w_ref[0], preferred_element_type=F32)
    p_ref[...] = (acc * jnp.where(j == C_Q, HEAD_DIM ** -0.5 * LOG2_E, 1.0)).astype(BF16)
    if emit_kv:
        @pl.when(j == C_K)
        def _():
            k32_ref[...] = acc.reshape(k32_ref.shape)

        @pl.when(j == C_V)
        def _():
            v32_ref[...] = acc.reshape(v32_ref.shape)


def inproj_pallas(x, mod, g, w_in, layer, seq_len, per_seq, tm, emit_kv, caches=()):
    n = x.shape[0]
    row = lambda i, j: (i, 0)
    in_specs = [pl.BlockSpec((tm, D_MODEL), row), _mod_spec(tm, seq_len, per_seq),
                pl.BlockSpec((1, D_MODEL), lambda i, j: (0, 0)),
                pl.BlockSpec((1, D_MODEL, CHUNK), lambda i, j: (layer, 0, j))]
    out_shape = [jax.ShapeDtypeStruct((n, D_IN), BF16)]
    out_specs = [pl.BlockSpec((tm, CHUNK), lambda i, j: (i, j))]
    aliases = {}
    if emit_kv:
        out_shape += [jax.ShapeDtypeStruct((n // seq_len, DEPTH, seq_len, N_HEADS, HEAD_DIM), F32)] * 2
        out_specs += [pl.BlockSpec((tm // seq_len, 1, seq_len, N_HEADS, HEAD_DIM),
                                   lambda i, j: (i, layer, 0, 0, 0))] * 2
        aliases = {len(in_specs) + k: 1 + k for k in range(len(caches))}
        in_specs += [pl.BlockSpec(memory_space=pl.ANY)] * len(caches)
    return pl.pallas_call(
        functools.partial(_inproj_kernel, emit_kv=emit_kv),
        grid=(n // tm, N_CHUNKS),
        in_specs=in_specs,
        out_specs=out_specs,
        out_shape=out_shape,
        input_output_aliases=aliases,
        scratch_shapes=[pltpu.VMEM((tm, D_MODEL), BF16)],
        compiler_params=_params("arbitrary", "arbitrary"),
        name="inproj",
    )(x, mod, g, w_in, *caches)


LRU_CB = 512
LRU_TC = 256
LRU_SUB = V7X_MXU_DIM
LRU_HALO = 16


LRU_SEG = LRU_TC // V7X_SUBLANES
LRU_NSLAB = LRU_SEG + CONV_W


def _lru_row_maps():
    s = np.arange(V7X_SUBLANES)[None, :]
    src = (LRU_HALO - CONV_W // 2) + LRU_SEG * s + np.arange(LRU_NSLAB)[:, None]
    sel = np.zeros((LRU_NSLAB * V7X_SUBLANES, LRU_TC + 2 * LRU_HALO), np.float32)
    sel[np.arange(sel.shape[0]), src.reshape(-1)] = 1.0
    tok = (LRU_SEG * s + np.arange(LRU_SEG)[:, None]).reshape(-1)
    perm = np.zeros((LRU_TC, LRU_TC), np.float32)
    perm[np.arange(LRU_TC), tok] = 1.0
    return jnp.asarray(sel, BF16), jnp.asarray(perm, BF16), jnp.asarray(perm.T, BF16)


def _slab_scan(a3, b3, carry, reverse):
    n = a3.shape[0]
    hs, cum = [None] * n, [None] * n
    h = cp = None
    for t in (range(n - 1, -1, -1) if reverse else range(n)):
        h = b3[t] if h is None else a3[t] * h + b3[t]
        cp = a3[t] if cp is None else cp * a3[t]
        hs[t], cum[t] = h, cp
    sub = lax.broadcasted_iota(jnp.int32, (V7X_SUBLANES, 1), 0)
    pa, pb = cp, h
    s = 1
    while s < V7X_SUBLANES:
        shift = V7X_SUBLANES - s if reverse else s
        ok = (sub < V7X_SUBLANES - s) if reverse else (sub >= s)
        a_sh = pltpu.roll(pa, shift, axis=0)
        b_sh = pltpu.roll(pb, shift, axis=0)
        pb = jnp.where(ok, pa * b_sh + pb, pb)
        pa = jnp.where(ok, pa * a_sh, pa)
        s *= 2
    leaving = pb + pa * carry
    first, last = (V7X_SUBLANES - 1, 0) if reverse else (0, V7X_SUBLANES - 1)
    entering = jnp.where(sub == first, carry, pltpu.roll(leaving, V7X_SUBLANES - 1 if reverse else 1, axis=0))
    out = jnp.concatenate([hs[t] + cum[t] * entering for t in range(n)], axis=0)
    return out, leaving[last:last + 1, :]


def _sigmoid(x):
    return 0.5 * jnp.tanh(0.5 * x) + 0.5


def _lru_kernel(x_ref, gate_ref, h0_ref, cw_ref, cb_ref, wg_ref, gb_ref, lam_ref, sel_ref, perm_ref, permt_ref,
                y_ref, fin_ref, hf_scr, u_scr):
    t_len = x_ref.shape[1]
    cb = x_ref.shape[2]
    n_chunks = t_len // LRU_TC

    def conv_chunk(c):
        t0 = pl.multiple_of(c * LRU_TC, LRU_TC)
        cur = x_ref[0, pl.ds(t0, LRU_TC), :]
        lo = pl.multiple_of(jnp.maximum(t0 - LRU_HALO, 0), LRU_HALO)
        hi = pl.multiple_of(jnp.minimum(t0 + LRU_TC, t_len - LRU_HALO), LRU_HALO)
        prev = x_ref[0, pl.ds(lo, LRU_HALO), :]
        nxt = x_ref[0, pl.ds(hi, LRU_HALO), :]
        prev = jnp.where(c > 0, prev, jnp.zeros_like(prev))
        nxt = jnp.where(c < n_chunks - 1, nxt, jnp.zeros_like(nxt))
        ext = jnp.concatenate([prev, cur, nxt], axis=0)
        xs = jnp.dot(sel_ref[...], ext, preferred_element_type=F32).reshape(LRU_NSLAB, V7X_SUBLANES, cb)
        u = cb_ref[...][None] + jnp.zeros((LRU_SEG, V7X_SUBLANES, cb), F32)
        for j in range(CONV_W):
            u = u + xs[j:j + LRU_SEG] * cw_ref[j:j + 1, :][None]
        return t0, u.reshape(LRU_TC, cb)

    def slabs(v):
        return v.reshape(LRU_SEG, V7X_SUBLANES, cb)

    def gates(u, d):
        ub = u.astype(BF16)
        pre = [jnp.dot(ub[:, LRU_SUB * s:LRU_SUB * (s + 1)], wg_ref[d, s], preferred_element_type=F32)
               for s in range(cb // LRU_SUB)]
        pre_a = jnp.concatenate([p[:, :LRU_SUB] for p in pre], axis=1)
        pre_x = jnp.concatenate([p[:, LRU_SUB:] for p in pre], axis=1)
        r = _sigmoid(pre_a + gb_ref[2 * d:2 * d + 1, :])
        i = _sigmoid(pre_x + gb_ref[2 * d + 1:2 * d + 2, :])
        log_a = (-LRU_C * jax.nn.softplus(-lam_ref[d:d + 1, :])) * r
        a = jnp.exp(log_a)
        th = jnp.tanh(log_a)
        num = -2.0 * th
        scale = jnp.where(num > 0.0, num * lax.rsqrt(num * (1.0 - th)), 0.0)
        inp = scale * (i * u)
        return a, inp

    def fwd(c, carry):
        t0, u = conv_chunk(c)
        u_scr[pl.ds(t0, LRU_TC), :] = u
        a, inp = gates(u, 0)
        h, carry = _slab_scan(slabs(a), slabs(inp), carry, reverse=False)
        hf_scr[pl.ds(t0, LRU_TC), :] = h
        return carry

    unroll = 4 if n_chunks % 4 == 0 else 1
    fin_f = lax.fori_loop(0, n_chunks, fwd, h0_ref[0, 0:1, :], unroll=unroll)

    def bwd(k, carry):
        t0 = pl.multiple_of((n_chunks - 1 - k) * LRU_TC, LRU_TC)
        a, inp = gates(u_scr[pl.ds(t0, LRU_TC), :], 1)
        h, carry = _slab_scan(slabs(a), slabs(inp), carry, reverse=True)
        g = jnp.dot(perm_ref[...], gate_ref[0, pl.ds(t0, LRU_TC), :], preferred_element_type=F32)
        y = ((hf_scr[pl.ds(t0, LRU_TC), :] + h) * jax.nn.gelu(g)).astype(BF16)
        y_ref[0, pl.ds(t0, LRU_TC), :] = jnp.dot(permt_ref[...], y, preferred_element_type=F32).astype(y_ref.dtype)
        return carry

    fin_b = lax.fori_loop(0, n_chunks, bwd, h0_ref[0, 1:2, :], unroll=unroll)
    fin_ref[0, 0:1, :] = fin_f
    fin_ref[0, 1:2, :] = fin_b


def _lru_gate_weights(lru_wa, lru_wx):
    per = LRU_SUB // LRU_BLOCK
    eye = jnp.eye(per, dtype=F32)

    def dense(w):
        w = w.reshape(2, D_RNN // LRU_SUB, per, LRU_BLOCK, LRU_BLOCK)
        full = w[:, :, :, :, None, :] * eye[None, None, :, None, :, None]
        return full.reshape(2, D_RNN // LRU_SUB, LRU_SUB, LRU_SUB)

    return jnp.concatenate([dense(lru_wa), dense(lru_wx)], axis=-1).astype(BF16)


def lru_pallas(p, h0, conv_w, conv_b, wg, gb, lam):
    b, t, _ = p.shape
    n_cb = D_RNN // LRU_CB
    maps = _lru_row_maps()
    whole = lambda m: pl.BlockSpec(m.shape, lambda bi, ci: (0, 0))
    return pl.pallas_call(
        _lru_kernel,
        grid=(b, n_cb),
        in_specs=[pl.BlockSpec((1, t, LRU_CB), lambda bi, ci: (bi, 0, C_XRNN * n_cb + ci)),
                  pl.BlockSpec((1, t, LRU_CB), lambda bi, ci: (bi, 0, C_GRNN * n_cb + ci)),
                  pl.BlockSpec((1, 2, LRU_CB), lambda bi, ci: (bi, 0, ci)),
                  pl.BlockSpec((CONV_W, LRU_CB), lambda bi, ci: (0, ci)),
                  pl.BlockSpec((1, LRU_CB), lambda bi, ci: (0, ci)),
                  pl.BlockSpec((2, LRU_CB // LRU_SUB, LRU_SUB, 2 * LRU_SUB), lambda bi, ci: (0, ci, 0, 0)),
                  pl.BlockSpec((4, LRU_CB), lambda bi, ci: (0, ci)),
                  pl.BlockSpec((2, LRU_CB), lambda bi, ci: (0, ci))] + [whole(m) for m in maps],
        out_specs=[pl.BlockSpec((1, t, LRU_CB), lambda bi, ci: (bi, 0, ci)),
                   pl.BlockSpec((1, 2, LRU_CB), lambda bi, ci: (bi, 0, ci))],
        out_shape=[jax.ShapeDtypeStruct((b, t, D_RNN), BF16), jax.ShapeDtypeStruct((b, 2, D_RNN), F32)],
        scratch_shapes=[pltpu.VMEM((t, LRU_CB), F32), pltpu.VMEM((t, LRU_CB), F32)],
        compiler_params=_params("arbitrary", "arbitrary"),
        name="lru",
    )(p, p, h0, conv_w, conv_b, wg, gb, lam, *maps)


Q_ROWS = 4
Q_TILE = Q_ROWS * GRID_W
KEY_ROWS = 12
KEY_TILE = KEY_ROWS * GRID_W
N_Q_TILES = DEC_SEQ // Q_TILE
ATT_LANES = 1024
HEADS_PER_STEP = ATT_LANES // HEAD_DIM
PAIR = 2 * HEAD_DIM


def _window_block(t):
    return jnp.clip(t - 1, 0, N_Q_TILES - KEY_ROWS // Q_ROWS)


N_DR_PAIRS = 2 * WIN_H


def _natten_tables(rpb):
    c = np.arange(GRID_W)
    q_start = np.clip(c - WIN_W // 2, 0, GRID_W - WIN_W)
    col_valid = (c[None, :] >= q_start[:, None]) & (c[None, :] < q_start[:, None] + WIN_W)
    dc = c[None, :] - c[:, None] + WIN_W - 1
    sel_c = ((dc[None] == np.arange(2 * WIN_W - 1)[:, None, None]) & col_valid[None]).astype(np.float32)
    t = jnp.einsum('lhrd,dck->lhrck', rpb.astype(F32), jnp.asarray(sel_c), precision=lax.Precision.HIGHEST)
    t = jnp.where(col_valid[None, None, None], t * LOG2_E, NEG_INF)
    t = jnp.pad(t, ((0, 0), (0, 0), (1, 1), (0, 0), (0, 0)), constant_values=NEG_INF)
    return jnp.concatenate([t[:, :, :-1], t[:, :, 1:]], axis=-1)


def _attend(q2, keys, vals, biases):
    lane_head = lax.broadcasted_iota(jnp.int32, (1, PAIR), 1) // HEAD_DIM
    acc = jnp.zeros((q2.shape[0], PAIR), F32)
    for j in range(2):
        mine = lane_head == j
        qh = jnp.where(mine, q2, jnp.zeros_like(q2))
        s = []
        for kb, bb in zip(keys, biases[j]):
            sd = lax.dot_general(qh, kb, NT_DIMS, preferred_element_type=F32)
            s.append(sd if bb is None else sd + bb)
        m = s[0].max(axis=-1, keepdims=True)
        for sd in s[1:]:
            m = jnp.maximum(m, sd.max(axis=-1, keepdims=True))
        o = None
        for sd, vb in zip(s, vals):
            pv = jnp.dot(jnp.exp2(sd - m).astype(BF16), jnp.where(mine, vb, jnp.ones_like(vb)),
                         preferred_element_type=F32)
            o = pv if o is None else o + pv
        acc = acc + jnp.where(mine, o / pltpu.roll(o, HEAD_DIM, axis=1), 0.0)
    return acc


def _natten_kernel(q_ref, k0_ref, k1_ref, k2_ref, v0_ref, v1_ref, v2_ref, kc_ref, vc_ref, tt_ref, o_ref):
    ti = pl.program_id(1)
    rows = DEC_SEQ // GRID_W
    wstart = _window_block(ti) * Q_ROWS
    first_row = lax.broadcasted_iota(jnp.int32, (1, 2 * GRID_W), 1) < GRID_W
    pieces = {}
    for a in range(Q_ROWS):
        r = ti * Q_ROWS + a
        start_r = jnp.clip(r - WIN_H // 2, 0, rows - WIN_H)
        for i in range(0, KEY_ROWS, 2):
            kr = wstart + i
            ok = [((kr + e >= start_r) & (kr + e < start_r + WIN_H)).astype(jnp.int32) for e in range(2)]
            pieces[a, i] = (jnp.clip(kr - r + WIN_H, 0, N_DR_PAIRS - 1), jnp.where(first_row, ok[0], ok[1]) > 0)

    def bias_block(h, d):
        return jnp.concatenate(
            [jnp.concatenate([jnp.where(pieces[a, i][1], tt_ref[0, h, pieces[a, i][0]], NEG_INF)
                              for i in range(Q_ROWS * d, Q_ROWS * (d + 1), 2)], axis=1)
             for a in range(Q_ROWS)], axis=0)

    k_refs = (k0_ref, k1_ref, k2_ref, kc_ref)
    v_refs = (v0_ref, v1_ref, v2_ref, vc_ref)
    outs = []
    for hp in range(ATT_LANES // PAIR):
        sl = slice(PAIR * hp, PAIR * (hp + 1))
        biases = [[bias_block(2 * hp + j, d) for d in range(3)] + [None] for j in range(2)]
        outs.append(_attend(q_ref[0, :, sl], [r[0, :, sl] for r in k_refs], [r[0, :, sl] for r in v_refs], biases))
    o_ref[0] = jnp.concatenate(outs, axis=1).astype(o_ref.dtype)


def natten_pallas(p, k_ctx, v_ctx, tables, layer):
    b, t, _ = p.shape
    n_lb = D_ATT // ATT_LANES
    q_spec = pl.BlockSpec((1, Q_TILE, ATT_LANES), lambda lb, ti, bi: (bi, ti, C_Q * n_lb + lb))
    win = lambda ch, d: pl.BlockSpec((1, Q_TILE, ATT_LANES),
                                     lambda lb, ti, bi: (bi, _window_block(ti) + d, ch * n_lb + lb))
    ctx = pl.BlockSpec((1, PAST_LEN, ATT_LANES), lambda lb, ti, bi: (bi, 0, lb))
    return pl.pallas_call(
        _natten_kernel,
        grid=(n_lb, N_Q_TILES, b),
        in_specs=[q_spec, win(C_K, 0), win(C_K, 1), win(C_K, 2), win(C_V, 0), win(C_V, 1), win(C_V, 2), ctx, ctx,
                  pl.BlockSpec((1, HEADS_PER_STEP, N_DR_PAIRS, GRID_W, 2 * GRID_W),
                               lambda lb, ti, bi: (layer, lb, 0, 0, 0))],
        out_specs=pl.BlockSpec((1, Q_TILE, ATT_LANES), lambda lb, ti, bi: (bi, ti, lb)),
        out_shape=jax.ShapeDtypeStruct((b, t, D_ATT), BF16),
        compiler_params=_params("arbitrary", "arbitrary", "arbitrary"),
        name="natten",
    )(p, p, p, p, p, p, p, k_ctx, v_ctx, tables)


def _ctxatt_kernel(q_ref, k_ref, v_ref, o_ref):
    outs = []
    for hp in range(ATT_LANES // PAIR):
        sl = slice(PAIR * hp, PAIR * (hp + 1))
        outs.append(_attend(q_ref[0, :, sl], [k_ref[0, :, sl]], [v_ref[0, :, sl]], [[None], [None]]))
    o_ref[0] = jnp.concatenate(outs, axis=1).astype(o_ref.dtype)


def ctxatt_pallas(p):
    b, t, _ = p.shape
    n_lb = D_ATT // ATT_LANES
    blk = lambda ch: pl.BlockSpec((1, t, ATT_LANES), lambda lb, bi: (bi, 0, ch * n_lb + lb))
    return pl.pallas_call(
        _ctxatt_kernel,
        grid=(n_lb, b),
        in_specs=[blk(C_Q), blk(C_K), blk(C_V)],
        out_specs=pl.BlockSpec((1, t, ATT_LANES), lambda lb, bi: (bi, 0, lb)),
        out_shape=jax.ShapeDtypeStruct((b, t, D_ATT), BF16),
        compiler_params=_params("arbitrary", "arbitrary"),
        name="ctxatt",
    )(p, p, p)


def _split_bf16(x):
    hi = x.astype(BF16)
    return hi, (x - hi.astype(F32)).astype(BF16)


def _route(logits_t, rb_ref):
    score = [jax.nn.sigmoid(logits_t[e:e + 1, :]) for e in range(N_EXPERTS)]
    sel = [score[e] + rb_ref[e] for e in range(N_EXPERTS)]
    best_g = None
    for g in range(N_GROUPS):
        v = sel[EXPERTS_PER_GROUP * g:EXPERTS_PER_GROUP * (g + 1)]
        top2 = None
        for i in range(EXPERTS_PER_GROUP):
            for j in range(i + 1, EXPERTS_PER_GROUP):
                pair = v[i] + v[j]
                top2 = pair if top2 is None else jnp.maximum(top2, pair)
        if best_g is None:
            best_g, best_v = jnp.zeros_like(top2, dtype=jnp.int32), top2
        else:
            upd = top2 > best_v
            best_g = jnp.where(upd, g, best_g)
            best_v = jnp.where(upd, top2, best_v)

    def in_best(vals, j):
        out = vals[j]
        for g in range(1, N_GROUPS):
            out = jnp.where(best_g == g, vals[EXPERTS_PER_GROUP * g + j], out)
        return out

    v = [in_best(sel, j) for j in range(EXPERTS_PER_GROUP)]
    sc = [in_best(score, j) for j in range(EXPERTS_PER_GROUP)]

    def first_argmax(vals):
        idx, top = jnp.zeros_like(best_g), vals[0]
        for j in range(1, EXPERTS_PER_GROUP):
            upd = vals[j] > top
            idx = jnp.where(upd, j, idx)
            top = jnp.where(upd, vals[j], top)
        return idx

    i1 = first_argmax(v)
    i2 = first_argmax([jnp.where(i1 == j, -jnp.inf, v[j]) for j in range(EXPERTS_PER_GROUP)])
    pick = lambda idx: sum(jnp.where(idx == j, sc[j], 0.0) for j in range(EXPERTS_PER_GROUP))
    w1, w2 = pick(i1), pick(i2)
    den = w1 + w2
    c1, c2 = w1 / den, w2 / den
    rows = []
    for e in range(N_EXPERTS):
        g, j = divmod(e, EXPERTS_PER_GROUP)
        rows.append(jnp.where(best_g == g, jnp.where(i1 == j, c1, 0.0) + jnp.where(i2 == j, c2, 0.0), 0.0))
    return jnp.concatenate(rows, axis=0), best_g


AUX_LANES = V7X_LANES
AUX_MID = N_EXPERTS
AUX_LO = 2 * N_EXPERTS
AUX_GROUP = 3 * N_EXPERTS
D_MOE_IN = D_MODEL + AUX_LANES


def _merge_kernel(rb_ref, yr_ref, ya_ref, gr_ref, ga_ref, x_ref, mod_ref, g_ref, wr_ref, wa_ref, wo_ref, wrt_ref,
                  x1_ref, h2_ref, gid_ref):
    m = (_sigmoid(gr_ref[...].astype(F32)) * jnp.dot(yr_ref[...], wr_ref[0], preferred_element_type=F32)
         + _sigmoid(ga_ref[...].astype(F32)) * jnp.dot(ya_ref[...], wa_ref[0], preferred_element_type=F32))
    o = jnp.dot(m.astype(BF16), wo_ref[0], preferred_element_type=F32)
    x1 = x_ref[...] + mod_ref[0, 2:3, :] * o
    x1_ref[...] = x1
    h2 = _rms(x1, g_ref[...]) * (1.0 + mod_ref[0, 4:5, :]) + mod_ref[0, 3:4, :]
    h_hi, h_lo = _split_bf16(h2)
    w_hi, w_lo = _split_bf16(wrt_ref[...])
    dot_nt = lambda a, b: lax.dot_general(a, b, NT_DIMS, preferred_element_type=F32)
    by_h_hi = dot_nt(jnp.concatenate([w_hi, w_lo], axis=0), h_hi)
    logits_t = by_h_hi[:N_EXPERTS] + (dot_nt(w_hi, h_lo) + by_h_hi[N_EXPERTS:])
    comb_t, group = _route(logits_t, rb_ref)
    tm = comb_t.shape[1]
    group = group.astype(F32)
    gid_ref[...] = jnp.concatenate([group, jnp.zeros((V7X_SUBLANES - 1, tm), F32)], axis=0)
    padded = jnp.concatenate([comb_t, jnp.zeros((AUX_GROUP - N_EXPERTS, tm), F32), group,
                              jnp.zeros((AUX_LANES - AUX_GROUP - 1, tm), F32)], axis=0)
    c = padded.T
    c_hi = c.astype(BF16).astype(F32)
    c_mid = (c - c_hi).astype(BF16).astype(F32)
    c_lo = c - c_hi - c_mid
    aux = c_hi + pltpu.roll(c_mid, AUX_MID, axis=1) + pltpu.roll(c_lo, AUX_LO, axis=1)
    h2_ref[...] = jnp.concatenate([h2.astype(BF16), aux.astype(BF16)], axis=1)


def merge_pallas(y_rnn, y_att, p, x, mod, g, w_br_rnn, w_br_att, w_out, w_router_t, router_bias, layer, seq_len,
                 per_seq, tm):
    n = x.shape[0]
    row = lambda i: (i, 0)
    full = lambda shape: pl.BlockSpec(shape, lambda i: (0, 0))
    weight = pl.BlockSpec((1, D_MODEL, D_MODEL), lambda i: (layer, 0, 0))
    return pl.pallas_call(
        _merge_kernel,
        grid=(n // tm,),
        in_specs=[pl.BlockSpec(memory_space=pltpu.SMEM),
                  pl.BlockSpec((tm, D_RNN), row), pl.BlockSpec((tm, D_ATT), row),
                  pl.BlockSpec((tm, CHUNK), lambda i: (i, C_GATE_R)), pl.BlockSpec((tm, CHUNK), lambda i: (i, C_GATE_A)),
                  pl.BlockSpec((tm, D_MODEL), row), _mod_spec(tm, seq_len, per_seq), full((1, D_MODEL)),
                  weight, weight, weight, full((N_EXPERTS, D_MODEL))],
        out_specs=[pl.BlockSpec((tm, D_MODEL), row), pl.BlockSpec((tm, D_MOE_IN), row),
                   pl.BlockSpec((V7X_SUBLANES, tm), lambda i: (0, i))],
        out_shape=[jax.ShapeDtypeStruct((n, D_MODEL), F32), jax.ShapeDtypeStruct((n, D_MOE_IN), BF16),
                   jax.ShapeDtypeStruct((V7X_SUBLANES, n), F32)],
        compiler_params=_params("arbitrary"),
        name="merge",
    )(router_bias, y_rnn, y_att, p, p, x, mod, g, w_br_rnn, w_br_att, w_out, w_router_t)


MOE_TB = 1024
MOE_RT = 128
MOE_TBP = MOE_TB + N_GROUPS * MOE_RT
MOE_EPS = 2


def _group_segments(gid_row):
    sub = lax.broadcasted_iota(jnp.int32, (V7X_SUBLANES, 1), 0).astype(F32)
    onehot = (gid_row == sub).astype(F32)
    cnt = jnp.sum(onehot, axis=1, keepdims=True)
    padded = jnp.floor((cnt + (MOE_RT - 1)) * (1.0 / MOE_RT)) * MOE_RT
    starts, run = [], jnp.zeros((1, 1), F32)
    for g in range(N_GROUPS):
        starts.append(run)
        run = run + padded[g:g + 1, :]
    return onehot, starts, [padded[g:g + 1, :] for g in range(N_GROUPS)], run


def _to_int(v):
    return v[0, 0].astype(jnp.int32)


def _moe_kernel(h_ref, gid_ref, x_ref, mod_ref, gf_ref, wg_ref, wu_ref, wd_ref, o_ref,
                p_scr, xs_scr, cs_scr, ys_scr, *, final_norm):
    step = pl.program_id(1)
    onehot, starts, sizes, used = _group_segments(gid_ref[0:1, :])
    lane = lax.broadcasted_iota(jnp.int32, (1, AUX_LANES), 1)

    @pl.when(step == 0)
    def _():
        t_row = lax.broadcasted_iota(jnp.int32, (MOE_TB, MOE_TB), 0)
        t_col = lax.broadcasted_iota(jnp.int32, (MOE_TB, MOE_TB), 1)
        earlier = (t_row < t_col).astype(BF16)
        rank = jnp.dot(onehot.astype(BF16), earlier, preferred_element_type=F32)
        pos = jnp.zeros((1, MOE_TB), F32)
        for g in range(N_GROUPS):
            pos = pos + onehot[g:g + 1, :] * (starts[g] + rank[g:g + 1, :])
        dest = lax.broadcasted_iota(jnp.int32, (MOE_TBP, 1), 0).astype(F32)
        p_scr[...] = (dest == pos).astype(BF16)
        for r0 in range(0, MOE_TBP, V7X_MXU_DIM):
            rows = pl.ds(r0, V7X_MXU_DIM)
            sorted_rows = jnp.dot(p_scr[rows, :], h_ref[...], preferred_element_type=F32)
            xs_scr[rows, :] = sorted_rows[:, :D_MODEL].astype(BF16)
            aux = sorted_rows[:, D_MODEL:]
            cs_scr[rows, :] = (aux + pltpu.roll(aux, AUX_LANES - AUX_MID, axis=1)
                               + pltpu.roll(aux, AUX_LANES - AUX_LO, axis=1))
        first_free = pl.multiple_of(_to_int(used), MOE_RT)

        def clear(k, carry):
            ys_scr[pl.ds(pl.multiple_of(first_free + k * MOE_RT, MOE_RT), MOE_RT), :] = jnp.zeros(
                (MOE_RT, D_MODEL), F32)
            return carry

        lax.fori_loop(0, (MOE_TBP - first_free) // MOE_RT, clear, 0)

    steps_per_group = EXPERTS_PER_GROUP // MOE_EPS
    group = step // steps_per_group
    first_of_group = step % steps_per_group == 0
    start_v, size_v = starts[0], sizes[0]
    for g in range(1, N_GROUPS):
        start_v = jnp.where(group == g, starts[g], start_v)
        size_v = jnp.where(group == g, sizes[g], size_v)
    seg_start = _to_int(start_v)
    n_tiles = _to_int(size_v) // MOE_RT

    def run_expert(r0, n_rows):
        rows = pl.ds(pl.multiple_of(r0, MOE_RT), n_rows)
        x = xs_scr[rows, :]
        cs = cs_scr[rows, :]
        y = None
        for k in range(MOE_EPS):
            gate = jnp.dot(x, wg_ref[0, k], preferred_element_type=F32)
            up = jnp.dot(x, wu_ref[0, k], preferred_element_type=F32)
            act = (gate * _sigmoid(gate)) * up
            yk = jnp.dot(act.astype(BF16), wd_ref[0, k], preferred_element_type=F32)
            yk = jnp.sum(jnp.where(lane == step * MOE_EPS + k, cs, 0.0), axis=-1, keepdims=True) * yk
            y = yk if y is None else y + yk

        @pl.when(first_of_group)
        def _():
            ys_scr[rows, :] = y

        @pl.when(jnp.logical_not(first_of_group))
        def _():
            ys_scr[rows, :] += y

    def pair(k, carry):
        run_expert(seg_start + k * (2 * MOE_RT), 2 * MOE_RT)
        return carry

    lax.fori_loop(0, n_tiles // 2, pair, 0)

    @pl.when(n_tiles % 2 == 1)
    def _():
        run_expert(seg_start + (n_tiles - 1) * MOE_RT, MOE_RT)

    @pl.when(step == N_EXPERTS // MOE_EPS - 1)
    def _():
        y = lax.dot_general(p_scr[...], ys_scr[...].astype(BF16), (((0,), (0,)), ((), ())),
                            preferred_element_type=F32)
        x2 = x_ref[...] + mod_ref[0, 5:6, :] * y
        o_ref[...] = _rms(x2, gf_ref[...]) if final_norm else x2


def moe_pallas(h2x, gid, x1, mod, g_final, w_gate, w_up, w_down, layer, seq_len, per_seq, final_norm):
    n = x1.shape[0]
    row = lambda i, e: (i, 0)
    expert = lambda i, e: (layer, e, 0, 0)
    return pl.pallas_call(
        functools.partial(_moe_kernel, final_norm=final_norm),
        grid=(n // MOE_TB, N_EXPERTS // MOE_EPS),
        in_specs=[pl.BlockSpec((MOE_TB, D_MOE_IN), row), pl.BlockSpec((V7X_SUBLANES, MOE_TB), lambda i, e: (0, i)),
                  pl.BlockSpec((MOE_TB, D_MODEL), row), _mod_spec(MOE_TB, seq_len, per_seq),
                  pl.BlockSpec((1, D_MODEL), lambda i, e: (0, 0)),
                  pl.BlockSpec((1, MOE_EPS, D_MODEL, D_EXPERT), expert),
                  pl.BlockSpec((1, MOE_EPS, D_MODEL, D_EXPERT), expert),
                  pl.BlockSpec((1, MOE_EPS, D_EXPERT, D_MODEL), expert)],
        out_specs=pl.BlockSpec((MOE_TB, D_MODEL), row),
        out_shape=jax.ShapeDtypeStruct((n, D_MODEL), F32),
        scratch_shapes=[pltpu.VMEM((MOE_TBP, MOE_TB), BF16), pltpu.VMEM((MOE_TBP, D_MODEL), BF16),
                        pltpu.VMEM((MOE_TBP, AUX_LANES), F32), pltpu.VMEM((MOE_TBP, D_MODEL), F32)],
        compiler_params=_params("arbitrary", "arbitrary"),
        name="moe",
    )(h2x, gid, x1, mod, g_final, w_gate, w_up, w_down)


TM_PROJ = 2048
TM_MERGE = 1024


def _layer(x, mod, seq_len, per_seq, lw, h0, ctx_kv, tables, layer, caches=()):
    n = x.shape[0]
    b = n // seq_len
    emit_kv = ctx_kv is None
    outs = inproj_pallas(x, mod, lw['norm_g'][0:1], lw['w_in'], layer, seq_len, per_seq,
                         TM_PROJ // 4 if emit_kv else TM_PROJ, emit_kv, caches)
    p = outs[0]
    p3 = p.reshape(b, seq_len, D_IN)
    y_rnn, h_fin = lru_pallas(p3, h0, lw['conv_w'], lw['conv_b'], lw['wg'], lw['gb'], lw['lam'])
    if emit_kv:
        y_att = ctxatt_pallas(p3)
    else:
        y_att = natten_pallas(p3, ctx_kv[0], ctx_kv[1], tables, layer)
    x1, h2x, gid = merge_pallas(y_rnn.reshape(n, D_RNN), y_att.reshape(n, D_ATT), p, x, mod, lw['norm_g'][1:2],
                                lw['w_br_rnn'], lw['w_br_att'], lw['w_out'], lw['w_router_t'], lw['router_bias'],
                                layer, seq_len, per_seq, TM_MERGE)
    x2 = moe_pallas(h2x, gid, x1, mod, lw['g_final'], lw['w_exp_gate'], lw['w_exp_up'], lw['w_exp_down'], layer,
                    seq_len, per_seq, layer == DEPTH - 1)
    kv = (outs[1], outs[2]) if emit_kv else None
    return x2, kv, h_fin


def kernel(x_prompt, x_sample, cache_k, cache_v, state_lru, c, c_ctx, w_ada, b_ada, norm_g, w_in, conv_w,
           conv_b, lru_wa, lru_ba, lru_wx, lru_bx, lru_lam, rpb, w_br_rnn, w_br_att, w_out, w_router,
           router_bias, w_exp_gate, w_exp_up, w_exp_down, final_norm_g):
    cvecs = jnp.concatenate([c, c_ctx[None, :], jnp.zeros((V7X_SUBLANES - DEC_BATCH - 1, D_MODEL), F32)], axis=0)
    mods = adaln_pallas(cvecs, w_ada, b_ada).reshape(DEPTH, V7X_SUBLANES, N_MOD, D_MODEL)
    tables = _natten_tables(rpb)
    w_router_t = w_router.T
    xp = x_prompt.reshape(BATCH * SEQ, D_MODEL)
    xs = x_sample.reshape(DEC_BATCH * DEC_SEQ, D_MODEL)
    zeros_h0 = jnp.zeros((BATCH, 2, D_RNN), F32)
    w_exp = [w.astype(BF16) for w in (w_exp_gate, w_exp_up, w_exp_down)]
    w_proj = [w.astype(BF16) for w in (w_in, w_br_rnn, w_br_att, w_out)]
    caches, hs = (), []
    for l in range(DEPTH):
        lw = dict(
            norm_g=norm_g[l], w_in=w_proj[0], conv_w=conv_w[l], conv_b=conv_b[l][None, :],
            wg=_lru_gate_weights(lru_wa[l], lru_wx[l]),
            gb=jnp.stack([lru_ba[l, 0], lru_bx[l, 0], lru_ba[l, 1], lru_bx[l, 1]], axis=0), lam=lru_lam[l],
            w_br_rnn=w_proj[1], w_br_att=w_proj[2], w_out=w_proj[3],
            w_router_t=w_router_t, router_bias=router_bias, g_final=final_norm_g[None, :],
            w_exp_gate=w_exp[0], w_exp_up=w_exp[1], w_exp_down=w_exp[2])
        xp, caches, h_l = _layer(xp, mods[l, DEC_BATCH:DEC_BATCH + 1], SEQ, False, lw, zeros_h0, None, None, l,
                                 caches)
        hs.append(h_l)
        ctx_kv = (cache_k[:, l].reshape(DEC_BATCH, PAST_LEN, D_ATT).astype(BF16),
                  cache_v[:, l].reshape(DEC_BATCH, PAST_LEN, D_ATT).astype(BF16))
        xs, _, _ = _layer(xs, mods[l, :DEC_BATCH], DEC_SEQ, True, lw, state_lru[:, l], ctx_kv, tables, l)
    y_prompt = xp.reshape(BATCH, SEQ, D_MODEL)
    y_sample = xs.reshape(DEC_BATCH, DEC_SEQ, D_MODEL)
    return (y_prompt, y_sample, caches[0], caches[1], jnp.stack(hs, axis=1))
```

```python
import functools

import jax
import jax.numpy as jnp
import numpy as np
from jax import lax
from jax.experimental import pallas as pl
from jax.experimental.pallas import tpu as pltpu

D_MODEL = 1024
BATCH = 16
SEQ = 256
DEPTH = 2
DEC_BATCH = 4
DEC_SEQ = 4096
PAST_LEN = 256

GRID_W = 64
D_RNN = 1024
N_LRU_BLOCKS = 16
LRU_BLOCK = D_RNN // N_LRU_BLOCKS
CONV_W = 4
LRU_C = 8.0
N_HEADS = 16
HEAD_DIM = 64
D_ATT = N_HEADS * HEAD_DIM
WIN_H = 8
WIN_W = 16
N_EXPERTS = 16
N_GROUPS = 4
EXPERTS_PER_GROUP = N_EXPERTS // N_GROUPS
D_EXPERT = 512
N_MOD = 6
D_IN = 2 * D_RNN + 3 * D_ATT + 2 * D_MODEL
EPS = 1e-6
NEG_INF = -1e30

BF16 = jnp.bfloat16
F32 = jnp.float32

V7X_LANES = 128
V7X_SUBLANES = 8
V7X_MXU_DIM = 256
V7X_VMEM_BYTES = 64 * 1024 * 1024
VMEM_LIMIT = V7X_VMEM_BYTES - 8 * 1024 * 1024

CHUNK = D_MODEL
N_CHUNKS = D_IN // CHUNK
C_XRNN, C_GRNN, C_Q, C_K, C_V, C_GATE_R, C_GATE_A = range(N_CHUNKS)

NT_DIMS = (((1,), (1,)), ((), ()))


def _params(*sem):
    return pltpu.CompilerParams(dimension_semantics=sem, vmem_limit_bytes=VMEM_LIMIT)


def _adaln_kernel(c_ref, w_ref, b_ref, o_ref):
    cv = c_ref[...]
    s = cv * jax.nn.sigmoid(cv)
    o_ref[0] = jnp.dot(s.astype(BF16), w_ref[0].astype(BF16), preferred_element_type=F32) + b_ref[0]


def adaln_pallas(cvecs, w_ada, b_ada):
    r = cvecs.shape[0]
    return pl.pallas_call(
        _adaln_kernel,
        grid=(DEPTH, N_MOD),
        in_specs=[pl.BlockSpec((r, D_MODEL), lambda l, j: (0, 0)),
                  pl.BlockSpec((1, D_MODEL, D_MODEL), lambda l, j: (l, 0, j)),
                  pl.BlockSpec((1, 1, D_MODEL), lambda l, j: (l, 0, j))],
        out_specs=pl.BlockSpec((1, r, D_MODEL), lambda l, j: (l, 0, j)),
        out_shape=jax.ShapeDtypeStruct((DEPTH, r, N_MOD * D_MODEL), F32),
        compiler_params=_params("arbitrary", "arbitrary"),
        name="adaln",
    )(cvecs, w_ada, b_ada.reshape(DEPTH, 1, N_MOD * D_MODEL))


def _mod_spec(tm, seq_len, per_seq):
    if per_seq:
        return pl.BlockSpec((1, N_MOD, D_MODEL), lambda i, *_: (i * tm // seq_len, 0, 0))
    return pl.BlockSpec((1, N_MOD, D_MODEL), lambda i, *_: (0, 0, 0))


def _rms(x, g):
    return x * lax.rsqrt(jnp.mean(x * x, axis=-1, keepdims=True) + EPS) * g


def _inproj_kernel(x_ref, mod_ref, g_ref, w_ref, *refs, emit_kv):
    if emit_kv:
        p_ref, k32_ref, v32_ref, h_scr = refs[-4:]
    else:
        p_ref, h_scr = refs
    j = pl.program_id(1)

    @pl.when(j == 0)
    def _():
        y = _rms(x_ref[...], g_ref[...])
        h_scr[...] = (y * (1.0 + mod_ref[0, 1:2, :]) + mod_ref[0, 0:1, :]).astype(BF16)

    acc = jnp.dot(h_scr[...], w_ref[0], preferred_element_type=F32)
    p_ref[...] = (acc * jnp.where(j == C_Q, HEAD_DIM ** -0.5, 1.0)).astype(BF16)
    if emit_kv:
        @pl.when(j == C_K)
        def _():
            k32_ref[...] = acc.reshape(k32_ref.shape)

        @pl.when(j == C_V)
        def _():
            v32_ref[...] = acc.reshape(v32_ref.shape)


def inproj_pallas(x, mod, g, w_in, layer, seq_len, per_seq, tm, emit_kv, caches=()):
    n = x.shape[0]
    row = lambda i, j: (i, 0)
    in_specs = [pl.BlockSpec((tm, D_MODEL), row), _mod_spec(tm, seq_len, per_seq),
                pl.BlockSpec((1, D_MODEL), lambda i, j: (0, 0)),
                pl.BlockSpec((1, D_MODEL, CHUNK), lambda i, j: (layer, 0, j))]
    out_shape = [jax.ShapeDtypeStruct((n, D_IN), BF16)]
    out_specs = [pl.BlockSpec((tm, CHUNK), lambda i, j: (i, j))]
    aliases = {}
    if emit_kv:
        out_shape += [jax.ShapeDtypeStruct((n // seq_len, DEPTH, seq_len, N_HEADS, HEAD_DIM), F32)] * 2
        out_specs += [pl.BlockSpec((tm // seq_len, 1, seq_len, N_HEADS, HEAD_DIM),
                                   lambda i, j: (i, layer, 0, 0, 0))] * 2
        aliases = {len(in_specs) + k: 1 + k for k in range(len(caches))}
        in_specs += [pl.BlockSpec(memory_space=pl.ANY)] * len(caches)
    return pl.pallas_call(
        functools.partial(_inproj_kernel, emit_kv=emit_kv),
        grid=(n // tm, N_CHUNKS),
        in_specs=in_specs,
        out_specs=out_specs,
        out_shape=out_shape,
        input_output_aliases=aliases,
        scratch_shapes=[pltpu.VMEM((tm, D_MODEL), BF16)],
        compiler_params=_params("arbitrary", "arbitrary"),
        name="inproj",
    )(x, mod, g, w_in, *caches)


LRU_CB = 512
LRU_TC = 256
LRU_SUB = V7X_MXU_DIM
LRU_HALO = 16


LRU_SEG = LRU_TC // V7X_SUBLANES
LRU_NSLAB = LRU_SEG + CONV_W


def _lru_row_maps():
    s = np.arange(V7X_SUBLANES)[None, :]
    src = (LRU_HALO - CONV_W // 2) + LRU_SEG * s + np.arange(LRU_NSLAB)[:, None]
    sel = np.zeros((LRU_NSLAB * V7X_SUBLANES, LRU_TC + 2 * LRU_HALO), np.float32)
    sel[np.arange(sel.shape[0]), src.reshape(-1)] = 1.0
    tok = (LRU_SEG * s + np.arange(LRU_SEG)[:, None]).reshape(-1)
    perm = np.zeros((LRU_TC, LRU_TC), np.float32)
    perm[np.arange(LRU_TC), tok] = 1.0
    return jnp.asarray(sel, BF16), jnp.asarray(perm, BF16), jnp.asarray(perm.T, BF16)


def _slab_scan(a3, b3, carry, reverse):
    n = a3.shape[0]
    hs, cum = [None] * n, [None] * n
    h = cp = None
    for t in (range(n - 1, -1, -1) if reverse else range(n)):
        h = b3[t] if h is None else a3[t] * h + b3[t]
        cp = a3[t] if cp is None else cp * a3[t]
        hs[t], cum[t] = h, cp
    sub = lax.broadcasted_iota(jnp.int32, (V7X_SUBLANES, 1), 0)
    pa, pb = cp, h
    s = 1
    while s < V7X_SUBLANES:
        shift = V7X_SUBLANES - s if reverse else s
        ok = (sub < V7X_SUBLANES - s) if reverse else (sub >= s)
        a_sh = pltpu.roll(pa, shift, axis=0)
        b_sh = pltpu.roll(pb, shift, axis=0)
        pb = jnp.where(ok, pa * b_sh + pb, pb)
        pa = jnp.where(ok, pa * a_sh, pa)
        s *= 2
    leaving = pb + pa * carry
    first, last = (V7X_SUBLANES - 1, 0) if reverse else (0, V7X_SUBLANES - 1)
    entering = jnp.where(sub == first, carry, pltpu.roll(leaving, V7X_SUBLANES - 1 if reverse else 1, axis=0))
    out = jnp.concatenate([hs[t] + cum[t] * entering for t in range(n)], axis=0)
    return out, leaving[last:last + 1, :]


def _sigmoid(x):
    return 0.5 * jnp.tanh(0.5 * x) + 0.5


def _lru_kernel(x_ref, gate_ref, h0_ref, cw_ref, cb_ref, wg_ref, gb_ref, lam_ref, sel_ref, perm_ref, permt_ref,
                y_ref, fin_ref, hf_scr, u_scr):
    t_len = x_ref.shape[1]
    cb = x_ref.shape[2]
    n_chunks = t_len // LRU_TC

    def conv_chunk(c):
        t0 = pl.multiple_of(c * LRU_TC, LRU_TC)
        cur = x_ref[0, pl.ds(t0, LRU_TC), :]
        lo = pl.multiple_of(jnp.maximum(t0 - LRU_HALO, 0), LRU_HALO)
        hi = pl.multiple_of(jnp.minimum(t0 + LRU_TC, t_len - LRU_HALO), LRU_HALO)
        prev = x_ref[0, pl.ds(lo, LRU_HALO), :]
        nxt = x_ref[0, pl.ds(hi, LRU_HALO), :]
        prev = jnp.where(c > 0, prev, jnp.zeros_like(prev))
        nxt = jnp.where(c < n_chunks - 1, nxt, jnp.zeros_like(nxt))
        ext = jnp.concatenate([prev, cur, nxt], axis=0)
        xs = jnp.dot(sel_ref[...], ext, preferred_element_type=F32).reshape(LRU_NSLAB, V7X_SUBLANES, cb)
        u = cb_ref[...][None] + jnp.zeros((LRU_SEG, V7X_SUBLANES, cb), F32)
        for j in range(CONV_W):
            u = u + xs[j:j + LRU_SEG] * cw_ref[j:j + 1, :][None]
        return t0, u.reshape(LRU_TC, cb)

    def slabs(v):
        return v.reshape(LRU_SEG, V7X_SUBLANES, cb)

    def gates(u, d):
        ub = u.astype(BF16)
        pre = [jnp.dot(ub[:, LRU_SUB * s:LRU_SUB * (s + 1)], wg_ref[d, s], preferred_element_type=F32)
               for s in range(cb // LRU_SUB)]
        pre_a = jnp.concatenate([p[:, :LRU_SUB] for p in pre], axis=1)
        pre_x = jnp.concatenate([p[:, LRU_SUB:] for p in pre], axis=1)
        r = _sigmoid(pre_a + gb_ref[2 * d:2 * d + 1, :])
        i = _sigmoid(pre_x + gb_ref[2 * d + 1:2 * d + 2, :])
        log_a = (-LRU_C * jax.nn.softplus(-lam_ref[d:d + 1, :])) * r
        a = jnp.exp(log_a)
        th = jnp.tanh(log_a)
        num = -2.0 * th
        scale = jnp.where(num > 0.0, num * lax.rsqrt(num * (1.0 - th)), 0.0)
        inp = scale * (i * u)
        return a, inp

    def fwd(c, carry):
        t0, u = conv_chunk(c)
        u_scr[pl.ds(t0, LRU_TC), :] = u
        a, inp = gates(u, 0)
        h, carry = _slab_scan(slabs(a), slabs(inp), carry, reverse=False)
        hf_scr[pl.ds(t0, LRU_TC), :] = h
        return carry

    unroll = 4 if n_chunks % 4 == 0 else 1
    fin_f = lax.fori_loop(0, n_chunks, fwd, h0_ref[0, 0:1, :], unroll=unroll)

    def bwd(k, carry):
        t0 = pl.multiple_of((n_chunks - 1 - k) * LRU_TC, LRU_TC)
        a, inp = gates(u_scr[pl.ds(t0, LRU_TC), :], 1)
        h, carry = _slab_scan(slabs(a), slabs(inp), carry, reverse=True)
        g = jnp.dot(perm_ref[...], gate_ref[0, pl.ds(t0, LRU_TC), :], preferred_element_type=F32)
        y = ((hf_scr[pl.ds(t0, LRU_TC), :] + h) * jax.nn.gelu(g)).astype(BF16)
        y_ref[0, pl.ds(t0, LRU_TC), :] = jnp.dot(permt_ref[...], y, preferred_element_type=F32).astype(y_ref.dtype)
        return carry

    fin_b = lax.fori_loop(0, n_chunks, bwd, h0_ref[0, 1:2, :], unroll=unroll)
    fin_ref[0, 0:1, :] = fin_f
    fin_ref[0, 1:2, :] = fin_b


def _lru_gate_weights(lru_wa, lru_wx):
    per = LRU_SUB // LRU_BLOCK
    eye = jnp.eye(per, dtype=F32)

    def dense(w):
        w = w.reshape(2, D_RNN // LRU_SUB, per, LRU_BLOCK, LRU_BLOCK)
        full = w[:, :, :, :, None, :] * eye[None, None, :, None, :, None]
        return full.reshape(2, D_RNN // LRU_SUB, LRU_SUB, LRU_SUB)

    return jnp.concatenate([dense(lru_wa), dense(lru_wx)], axis=-1).astype(BF16)


def lru_pallas(p, h0, conv_w, conv_b, wg, gb, lam):
    b, t, _ = p.shape
    n_cb = D_RNN // LRU_CB
    maps = _lru_row_maps()
    whole = lambda m: pl.BlockSpec(m.shape, lambda bi, ci: (0, 0))
    return pl.pallas_call(
        _lru_kernel,
        grid=(b, n_cb),
        in_specs=[pl.BlockSpec((1, t, LRU_CB), lambda bi, ci: (bi, 0, C_XRNN * n_cb + ci)),
                  pl.BlockSpec((1, t, LRU_CB), lambda bi, ci: (bi, 0, C_GRNN * n_cb + ci)),
                  pl.BlockSpec((1, 2, LRU_CB), lambda bi, ci: (bi, 0, ci)),
                  pl.BlockSpec((CONV_W, LRU_CB), lambda bi, ci: (0, ci)),
                  pl.BlockSpec((1, LRU_CB), lambda bi, ci: (0, ci)),
                  pl.BlockSpec((2, LRU_CB // LRU_SUB, LRU_SUB, 2 * LRU_SUB), lambda bi, ci: (0, ci, 0, 0)),
                  pl.BlockSpec((4, LRU_CB), lambda bi, ci: (0, ci)),
                  pl.BlockSpec((2, LRU_CB), lambda bi, ci: (0, ci))] + [whole(m) for m in maps],
        out_specs=[pl.BlockSpec((1, t, LRU_CB), lambda bi, ci: (bi, 0, ci)),
                   pl.BlockSpec((1, 2, LRU_CB), lambda bi, ci: (bi, 0, ci))],
        out_shape=[jax.ShapeDtypeStruct((b, t, D_RNN), BF16), jax.ShapeDtypeStruct((b, 2, D_RNN), F32)],
        scratch_shapes=[pltpu.VMEM((t, LRU_CB), F32), pltpu.VMEM((t, LRU_CB), F32)],
        compiler_params=_params("arbitrary", "arbitrary"),
        name="lru",
    )(p, p, h0, conv_w, conv_b, wg, gb, lam, *maps)


Q_ROWS = 4
Q_TILE = Q_ROWS * GRID_W
KEY_ROWS = 12
KEY_TILE = KEY_ROWS * GRID_W
N_Q_TILES = DEC_SEQ // Q_TILE
ATT_LANES = 1024
HEADS_PER_STEP = ATT_LANES // HEAD_DIM
PAIR = 2 * HEAD_DIM


def _window_block(t):
    return jnp.clip(t - 1, 0, N_Q_TILES - KEY_ROWS // Q_ROWS)


N_DR_PAIRS = 2 * WIN_H


def _natten_tables(rpb):
    c = np.arange(GRID_W)
    q_start = np.clip(c - WIN_W // 2, 0, GRID_W - WIN_W)
    col_valid = (c[None, :] >= q_start[:, None]) & (c[None, :] < q_start[:, None] + WIN_W)
    dc = c[None, :] - c[:, None] + WIN_W - 1
    sel_c = ((dc[None] == np.arange(2 * WIN_W - 1)[:, None, None]) & col_valid[None]).astype(np.float32)
    t = jnp.einsum('lhrd,dck->lhrck', rpb.astype(F32), jnp.asarray(sel_c), precision=lax.Precision.HIGHEST)
    t = jnp.where(col_valid[None, None, None], t, NEG_INF)
    t = jnp.pad(t, ((0, 0), (0, 0), (1, 1), (0, 0), (0, 0)), constant_values=NEG_INF)
    return jnp.concatenate([t[:, :, :-1], t[:, :, 1:]], axis=-1)


def _attend(q2, keys, vals, biases):
    lane_head = lax.broadcasted_iota(jnp.int32, (1, PAIR), 1) // HEAD_DIM
    acc = jnp.zeros((q2.shape[0], PAIR), F32)
    for j in range(2):
        mine = lane_head == j
        qh = jnp.where(mine, q2, jnp.zeros_like(q2))
        s = []
        for kb, bb in zip(keys, biases[j]):
            sd = lax.dot_general(qh, kb, NT_DIMS, preferred_element_type=F32)
            s.append(sd if bb is None else sd + bb)
        m = s[0].max(axis=-1, keepdims=True)
        for sd in s[1:]:
            m = jnp.maximum(m, sd.max(axis=-1, keepdims=True))
        o = None
        for sd, vb in zip(s, vals):
            pv = jnp.dot(jnp.exp(sd - m).astype(BF16), jnp.where(mine, vb, jnp.ones_like(vb)),
                         preferred_element_type=F32)
            o = pv if o is None else o + pv
        acc = acc + jnp.where(mine, o / pltpu.roll(o, HEAD_DIM, axis=1), 0.0)
    return acc


def _natten_kernel(q_ref, k0_ref, k1_ref, k2_ref, v0_ref, v1_ref, v2_ref, kc_ref, vc_ref, tt_ref, o_ref):
    ti = pl.program_id(1)
    rows = DEC_SEQ // GRID_W
    wstart = _window_block(ti) * Q_ROWS
    first_row = lax.broadcasted_iota(jnp.int32, (1, 2 * GRID_W), 1) < GRID_W
    pieces = {}
    for a in range(Q_ROWS):
        r = ti * Q_ROWS + a
        start_r = jnp.clip(r - WIN_H // 2, 0, rows - WIN_H)
        for i in range(0, KEY_ROWS, 2):
            kr = wstart + i
            ok = [((kr + e >= start_r) & (kr + e < start_r + WIN_H)).astype(jnp.int32) for e in range(2)]
            pieces[a, i] = (jnp.clip(kr - r + WIN_H, 0, N_DR_PAIRS - 1), jnp.where(first_row, ok[0], ok[1]) > 0)

    def bias_block(h, d):
        return jnp.concatenate(
            [jnp.concatenate([jnp.where(pieces[a, i][1], tt_ref[0, h, pieces[a, i][0]], NEG_INF)
                              for i in range(Q_ROWS * d, Q_ROWS * (d + 1), 2)], axis=1)
             for a in range(Q_ROWS)], axis=0)

    k_refs = (k0_ref, k1_ref, k2_ref, kc_ref)
    v_refs = (v0_ref, v1_ref, v2_ref, vc_ref)
    outs = []
    for hp in range(ATT_LANES // PAIR):
        sl = slice(PAIR * hp, PAIR * (hp + 1))
        biases = [[bias_block(2 * hp + j, d) for d in range(3)] + [None] for j in range(2)]
        outs.append(_attend(q_ref[0, :, sl], [r[0, :, sl] for r in k_refs], [r[0, :, sl] for r in v_refs], biases))
    o_ref[0] = jnp.concatenate(outs, axis=1).astype(o_ref.dtype)


def natten_pallas(p, k_ctx, v_ctx, tables, layer):
    b, t, _ = p.shape
    n_lb = D_ATT // ATT_LANES
    q_spec = pl.BlockSpec((1, Q_TILE, ATT_LANES), lambda lb, ti, bi: (bi, ti, C_Q * n_lb + lb))
    win = lambda ch, d: pl.BlockSpec((1, Q_TILE, ATT_LANES),
                                     lambda lb, ti, bi: (bi, _window_block(ti) + d, ch * n_lb + lb))
    ctx = pl.BlockSpec((1, PAST_LEN, ATT_LANES), lambda lb, ti, bi: (bi, 0, lb))
    return pl.pallas_call(
        _natten_kernel,
        grid=(n_lb, N_Q_TILES, b),
        in_specs=[q_spec, win(C_K, 0), win(C_K, 1), win(C_K, 2), win(C_V, 0), win(C_V, 1), win(C_V, 2), ctx, ctx,
                  pl.BlockSpec((1, HEADS_PER_STEP, N_DR_PAIRS, GRID_W, 2 * GRID_W),
                               lambda lb, ti, bi: (layer, lb, 0, 0, 0))],
        out_specs=pl.BlockSpec((1, Q_TILE, ATT_LANES), lambda lb, ti, bi: (bi, ti, lb)),
        out_shape=jax.ShapeDtypeStruct((b, t, D_ATT), BF16),
        compiler_params=_params("arbitrary", "arbitrary", "arbitrary"),
        name="natten",
    )(p, p, p, p, p, p, p, k_ctx, v_ctx, tables)


def _ctxatt_kernel(q_ref, k_ref, v_ref, o_ref):
    outs = []
    for hp in range(ATT_LANES // PAIR):
        sl = slice(PAIR * hp, PAIR * (hp + 1))
        outs.append(_attend(q_ref[0, :, sl], [k_ref[0, :, sl]], [v_ref[0, :, sl]], [[None], [None]]))
    o_ref[0] = jnp.concatenate(outs, axis=1).astype(o_ref.dtype)


def ctxatt_pallas(p):
    b, t, _ = p.shape
    n_lb = D_ATT // ATT_LANES
    blk = lambda ch: pl.BlockSpec((1, t, ATT_LANES), lambda lb, bi: (bi, 0, ch * n_lb + lb))
    return pl.pallas_call(
        _ctxatt_kernel,
        grid=(n_lb, b),
        in_specs=[blk(C_Q), blk(C_K), blk(C_V)],
        out_specs=pl.BlockSpec((1, t, ATT_LANES), lambda lb, bi: (bi, 0, lb)),
        out_shape=jax.ShapeDtypeStruct((b, t, D_ATT), BF16),
        compiler_params=_params("arbitrary", "arbitrary"),
        name="ctxatt",
    )(p, p, p)


def _split_bf16(x):
    hi = x.astype(BF16)
    return hi, (x - hi.astype(F32)).astype(BF16)


def _route(logits_t, rb_ref):
    score = [jax.nn.sigmoid(logits_t[e:e + 1, :]) for e in range(N_EXPERTS)]
    sel = [score[e] + rb_ref[e] for e in range(N_EXPERTS)]
    best_g = None
    for g in range(N_GROUPS):
        v = sel[EXPERTS_PER_GROUP * g:EXPERTS_PER_GROUP * (g + 1)]
        top2 = None
        for i in range(EXPERTS_PER_GROUP):
            for j in range(i + 1, EXPERTS_PER_GROUP):
                pair = v[i] + v[j]
                top2 = pair if top2 is None else jnp.maximum(top2, pair)
        if best_g is None:
            best_g, best_v = jnp.zeros_like(top2, dtype=jnp.int32), top2
        else:
            upd = top2 > best_v
            best_g = jnp.where(upd, g, best_g)
            best_v = jnp.where(upd, top2, best_v)

    def in_best(vals, j):
        out = vals[j]
        for g in range(1, N_GROUPS):
            out = jnp.where(best_g == g, vals[EXPERTS_PER_GROUP * g + j], out)
        return out

    v = [in_best(sel, j) for j in range(EXPERTS_PER_GROUP)]
    sc = [in_best(score, j) for j in range(EXPERTS_PER_GROUP)]

    def first_argmax(vals):
        idx, top = jnp.zeros_like(best_g), vals[0]
        for j in range(1, EXPERTS_PER_GROUP):
            upd = vals[j] > top
            idx = jnp.where(upd, j, idx)
            top = jnp.where(upd, vals[j], top)
        return idx

    i1 = first_argmax(v)
    i2 = first_argmax([jnp.where(i1 == j, -jnp.inf, v[j]) for j in range(EXPERTS_PER_GROUP)])
    pick = lambda idx: sum(jnp.where(idx == j, sc[j], 0.0) for j in range(EXPERTS_PER_GROUP))
    w1, w2 = pick(i1), pick(i2)
    den = w1 + w2
    c1, c2 = w1 / den, w2 / den
    rows = []
    for e in range(N_EXPERTS):
        g, j = divmod(e, EXPERTS_PER_GROUP)
        rows.append(jnp.where(best_g == g, jnp.where(i1 == j, c1, 0.0) + jnp.where(i2 == j, c2, 0.0), 0.0))
    return jnp.concatenate(rows, axis=0), best_g


AUX_LANES = V7X_LANES
AUX_MID = N_EXPERTS
AUX_LO = 2 * N_EXPERTS
AUX_GROUP = 3 * N_EXPERTS
D_MOE_IN = D_MODEL + AUX_LANES


def _merge_kernel(rb_ref, yr_ref, ya_ref, gr_ref, ga_ref, x_ref, mod_ref, g_ref, wr_ref, wa_ref, wo_ref, wrt_ref,
                  x1_ref, h2_ref, gid_ref):
    m = (_sigmoid(gr_ref[...].astype(F32)) * jnp.dot(yr_ref[...], wr_ref[0], preferred_element_type=F32)
         + _sigmoid(ga_ref[...].astype(F32)) * jnp.dot(ya_ref[...], wa_ref[0], preferred_element_type=F32))
    o = jnp.dot(m.astype(BF16), wo_ref[0], preferred_element_type=F32)
    x1 = x_ref[...] + mod_ref[0, 2:3, :] * o
    x1_ref[...] = x1
    h2 = _rms(x1, g_ref[...]) * (1.0 + mod_ref[0, 4:5, :]) + mod_ref[0, 3:4, :]
    h_hi, h_lo = _split_bf16(h2)
    w_hi, w_lo = _split_bf16(wrt_ref[...])
    dot_nt = lambda a, b: lax.dot_general(a, b, NT_DIMS, preferred_element_type=F32)
    by_h_hi = dot_nt(jnp.concatenate([w_hi, w_lo], axis=0), h_hi)
    logits_t = by_h_hi[:N_EXPERTS] + (dot_nt(w_hi, h_lo) + by_h_hi[N_EXPERTS:])
    comb_t, group = _route(logits_t, rb_ref)
    tm = comb_t.shape[1]
    group = group.astype(F32)
    gid_ref[...] = jnp.concatenate([group, jnp.zeros((V7X_SUBLANES - 1, tm), F32)], axis=0)
    padded = jnp.concatenate([comb_t, jnp.zeros((AUX_GROUP - N_EXPERTS, tm), F32), group,
                              jnp.zeros((AUX_LANES - AUX_GROUP - 1, tm), F32)], axis=0)
    c = padded.T
    c_hi = c.astype(BF16).astype(F32)
    c_mid = (c - c_hi).astype(BF16).astype(F32)
    c_lo = c - c_hi - c_mid
    aux = c_hi + pltpu.roll(c_mid, AUX_MID, axis=1) + pltpu.roll(c_lo, AUX_LO, axis=1)
    h2_ref[...] = jnp.concatenate([h2.astype(BF16), aux.astype(BF16)], axis=1)


def merge_pallas(y_rnn, y_att, p, x, mod, g, w_br_rnn, w_br_att, w_out, w_router_t, router_bias, layer, seq_len,
                 per_seq, tm):
    n = x.shape[0]
    row = lambda i: (i, 0)
    full = lambda shape: pl.BlockSpec(shape, lambda i: (0, 0))
    weight = pl.BlockSpec((1, D_MODEL, D_MODEL), lambda i: (layer, 0, 0))
    return pl.pallas_call(
        _merge_kernel,
        grid=(n // tm,),
        in_specs=[pl.BlockSpec(memory_space=pltpu.SMEM),
                  pl.BlockSpec((tm, D_RNN), row), pl.BlockSpec((tm, D_ATT), row),
                  pl.BlockSpec((tm, CHUNK), lambda i: (i, C_GATE_R)), pl.BlockSpec((tm, CHUNK), lambda i: (i, C_GATE_A)),
                  pl.BlockSpec((tm, D_MODEL), row), _mod_spec(tm, seq_len, per_seq), full((1, D_MODEL)),
                  weight, weight, weight, full((N_EXPERTS, D_MODEL))],
        out_specs=[pl.BlockSpec((tm, D_MODEL), row), pl.BlockSpec((tm, D_MOE_IN), row),
                   pl.BlockSpec((V7X_SUBLANES, tm), lambda i: (0, i))],
        out_shape=[jax.ShapeDtypeStruct((n, D_MODEL), F32), jax.ShapeDtypeStruct((n, D_MOE_IN), BF16),
                   jax.ShapeDtypeStruct((V7X_SUBLANES, n), F32)],
        compiler_params=_params("arbitrary"),
        name="merge",
    )(router_bias, y_rnn, y_att, p, p, x, mod, g, w_br_rnn, w_br_att, w_out, w_router_t)


MOE_TB = 1024
MOE_RT = 128
MOE_TBP = MOE_TB + N_GROUPS * MOE_RT


def _group_segments(gid_row):
    sub = lax.broadcasted_iota(jnp.int32, (V7X_SUBLANES, 1), 0).astype(F32)
    onehot = (gid_row == sub).astype(F32)
    cnt = jnp.sum(onehot, axis=1, keepdims=True)
    padded = jnp.floor((cnt + (MOE_RT - 1)) * (1.0 / MOE_RT)) * MOE_RT
    starts, run = [], jnp.zeros((1, 1), F32)
    for g in range(N_GROUPS):
        starts.append(run)
        run = run + padded[g:g + 1, :]
    return onehot, starts, [padded[g:g + 1, :] for g in range(N_GROUPS)], run


def _to_int(v):
    return v[0, 0].astype(jnp.int32)


def _moe_kernel(h_ref, gid_ref, x_ref, mod_ref, gf_ref, wg_ref, wu_ref, wd_ref, o_ref,
                p_scr, xs_scr, cs_scr, ys_scr, *, final_norm):
    step = pl.program_id(1)
    onehot, starts, sizes, used = _group_segments(gid_ref[0:1, :])
    lane = lax.broadcasted_iota(jnp.int32, (1, AUX_LANES), 1)

    @pl.when(step == 0)
    def _():
        t_row = lax.broadcasted_iota(jnp.int32, (MOE_TB, MOE_TB), 0)
        t_col = lax.broadcasted_iota(jnp.int32, (MOE_TB, MOE_TB), 1)
        earlier = (t_row < t_col).astype(BF16)
        rank = jnp.dot(onehot.astype(BF16), earlier, preferred_element_type=F32)
        pos = jnp.zeros((1, MOE_TB), F32)
        for g in range(N_GROUPS):
            pos = pos + onehot[g:g + 1, :] * (starts[g] + rank[g:g + 1, :])
        dest = lax.broadcasted_iota(jnp.int32, (MOE_TBP, 1), 0).astype(F32)
        p_scr[...] = (dest == pos).astype(BF16)
        for r0 in range(0, MOE_TBP, V7X_MXU_DIM):
            rows = pl.ds(r0, V7X_MXU_DIM)
            sorted_rows = jnp.dot(p_scr[rows, :], h_ref[...], preferred_element_type=F32)
            xs_scr[rows, :] = sorted_rows[:, :D_MODEL].astype(BF16)
            aux = sorted_rows[:, D_MODEL:]
            cs_scr[rows, :] = (aux + pltpu.roll(aux, AUX_LANES - AUX_MID, axis=1)
                               + pltpu.roll(aux, AUX_LANES - AUX_LO, axis=1))
        first_free = pl.multiple_of(_to_int(used), MOE_RT)

        def clear(k, carry):
            ys_scr[pl.ds(pl.multiple_of(first_free + k * MOE_RT, MOE_RT), MOE_RT), :] = jnp.zeros(
                (MOE_RT, D_MODEL), BF16)
            return carry

        lax.fori_loop(0, (MOE_TBP - first_free) // MOE_RT, clear, 0)

    start_v, size_v = starts[0], sizes[0]
    for g in range(1, N_GROUPS):
        start_v = jnp.where(step == g, starts[g], start_v)
        size_v = jnp.where(step == g, sizes[g], size_v)
    seg_start = _to_int(start_v)
    n_tiles = _to_int(size_v) // MOE_RT

    def run_expert(r0, n_rows):
        rows = pl.ds(pl.multiple_of(r0, MOE_RT), n_rows)
        x = xs_scr[rows, :]
        cs = cs_scr[rows, :]
        y = None
        for k in range(EXPERTS_PER_GROUP):
            gate = jnp.dot(x, wg_ref[0, k], preferred_element_type=F32)
            up = jnp.dot(x, wu_ref[0, k], preferred_element_type=F32)
            act = (gate * _sigmoid(gate)) * up
            yk = jnp.dot(act.astype(BF16), wd_ref[0, k], preferred_element_type=F32)
            yk = jnp.sum(jnp.where(lane == step * EXPERTS_PER_GROUP + k, cs, 0.0), axis=-1, keepdims=True) * yk
            y = yk if y is None else y + yk
        ys_scr[rows, :] = y.astype(BF16)

    def pair(k, carry):
        run_expert(seg_start + k * (2 * MOE_RT), 2 * MOE_RT)
        return carry

    lax.fori_loop(0, n_tiles // 2, pair, 0)

    @pl.when(n_tiles % 2 == 1)
    def _():
        run_expert(seg_start + (n_tiles - 1) * MOE_RT, MOE_RT)

    @pl.when(step == N_GROUPS - 1)
    def _():
        y = lax.dot_general(p_scr[...], ys_scr[...], (((0,), (0,)), ((), ())),
                            preferred_element_type=F32)
        x2 = x_ref[...] + mod_ref[0, 5:6, :] * y
        o_ref[...] = _rms(x2, gf_ref[...]) if final_norm else x2


def moe_pallas(h2x, gid, x1, mod, g_final, w_gate, w_up, w_down, layer, seq_len, per_seq, final_norm):
    n = x1.shape[0]
    row = lambda i, g: (i, 0)
    group = lambda i, g: (layer, g, 0, 0)
    return pl.pallas_call(
        functools.partial(_moe_kernel, final_norm=final_norm),
        grid=(n // MOE_TB, N_GROUPS),
        in_specs=[pl.BlockSpec((MOE_TB, D_MOE_IN), row),
                  pl.BlockSpec((V7X_SUBLANES, MOE_TB), lambda i, g: (0, i)),
                  pl.BlockSpec((MOE_TB, D_MODEL), row, pipeline_mode=pl.Buffered(1)),
                  _mod_spec(MOE_TB, seq_len, per_seq),
                  pl.BlockSpec((1, D_MODEL), lambda i, g: (0, 0)),
                  pl.BlockSpec((1, EXPERTS_PER_GROUP, D_MODEL, D_EXPERT), group),
                  pl.BlockSpec((1, EXPERTS_PER_GROUP, D_MODEL, D_EXPERT), group),
                  pl.BlockSpec((1, EXPERTS_PER_GROUP, D_EXPERT, D_MODEL), group)],
        out_specs=pl.BlockSpec((MOE_TB, D_MODEL), row),
        out_shape=jax.ShapeDtypeStruct((n, D_MODEL), F32),
        scratch_shapes=[pltpu.VMEM((MOE_TBP, MOE_TB), BF16), pltpu.VMEM((MOE_TBP, D_MODEL), BF16),
                        pltpu.VMEM((MOE_TBP, AUX_LANES), F32), pltpu.VMEM((MOE_TBP, D_MODEL), BF16)],
        compiler_params=_params("arbitrary", "arbitrary"),
        name="moe",
    )(h2x, gid, x1, mod, g_final, w_gate, w_up, w_down)


TM_PROJ = 2048
TM_MERGE = 1024


def _layer(x, mod, seq_len, per_seq, lw, h0, ctx_kv, tables, layer, caches=()):
    n = x.shape[0]
    b = n // seq_len
    emit_kv = ctx_kv is None
    outs = inproj_pallas(x, mod, lw['norm_g'][0:1], lw['w_in'], layer, seq_len, per_seq,
                         TM_PROJ // 4 if emit_kv else TM_PROJ, emit_kv, caches)
    p = outs[0]
    p3 = p.reshape(b, seq_len, D_IN)
    y_rnn, h_fin = lru_pallas(p3, h0, lw['conv_w'], lw['conv_b'], lw['wg'], lw['gb'], lw['lam'])
    if emit_kv:
        y_att = ctxatt_pallas(p3)
    else:
        y_att = natten_pallas(p3, ctx_kv[0], ctx_kv[1], tables, layer)
    x1, h2x, gid = merge_pallas(y_rnn.reshape(n, D_RNN), y_att.reshape(n, D_ATT), p, x, mod, lw['norm_g'][1:2],
                                lw['w_br_rnn'], lw['w_br_att'], lw['w_out'], lw['w_router_t'], lw['router_bias'],
                                layer, seq_len, per_seq, TM_MERGE)
    x2 = moe_pallas(h2x, gid, x1, mod, lw['g_final'], lw['w_exp_gate'], lw['w_exp_up'], lw['w_exp_down'], layer,
                    seq_len, per_seq, layer == DEPTH - 1)
    kv = (outs[1], outs[2]) if emit_kv else None
    return x2, kv, h_fin


def kernel(x_prompt, x_sample, cache_k, cache_v, state_lru, c, c_ctx, w_ada, b_ada, norm_g, w_in, conv_w,
           conv_b, lru_wa, lru_ba, lru_wx, lru_bx, lru_lam, rpb, w_br_rnn, w_br_att, w_out, w_router,
           router_bias, w_exp_gate, w_exp_up, w_exp_down, final_norm_g):
    cvecs = jnp.concatenate([c, c_ctx[None, :], jnp.zeros((V7X_SUBLANES - DEC_BATCH - 1, D_MODEL), F32)], axis=0)
    mods = adaln_pallas(cvecs, w_ada, b_ada).reshape(DEPTH, V7X_SUBLANES, N_MOD, D_MODEL)
    tables = _natten_tables(rpb)
    w_router_t = w_router.T
    xp = x_prompt.reshape(BATCH * SEQ, D_MODEL)
    xs = x_sample.reshape(DEC_BATCH * DEC_SEQ, D_MODEL)
    zeros_h0 = jnp.zeros((BATCH, 2, D_RNN), F32)
    w_exp = [w.astype(BF16) for w in (w_exp_gate, w_exp_up, w_exp_down)]
    w_proj = [w.astype(BF16) for w in (w_in, w_br_rnn, w_br_att, w_out)]
    caches, hs = (), []
    for l in range(DEPTH):
        lw = dict(
            norm_g=norm_g[l], w_in=w_proj[0], conv_w=conv_w[l], conv_b=conv_b[l][None, :],
            wg=_lru_gate_weights(lru_wa[l], lru_wx[l]),
            gb=jnp.stack([lru_ba[l, 0], lru_bx[l, 0], lru_ba[l, 1], lru_bx[l, 1]], axis=0), lam=lru_lam[l],
            w_br_rnn=w_proj[1], w_br_att=w_proj[2], w_out=w_proj[3],
            w_router_t=w_router_t, router_bias=router_bias, g_final=final_norm_g[None, :],
            w_exp_gate=w_exp[0], w_exp_up=w_exp[1], w_exp_down=w_exp[2])
        xp, caches, h_l = _layer(xp, mods[l, DEC_BATCH:DEC_BATCH + 1], SEQ, False, lw, zeros_h0, None, None, l,
                                 caches)
        hs.append(h_l)
        ctx_kv = (cache_k[:, l].reshape(DEC_BATCH, PAST_LEN, D_ATT).astype(BF16),
                  cache_v[:, l].reshape(DEC_BATCH, PAST_LEN, D_ATT).astype(BF16))
        xs, _, _ = _layer(xs, mods[l, :DEC_BATCH], DEC_SEQ, True, lw, state_lru[:, l], ctx_kv, tables, l)
    y_prompt = xp.reshape(BATCH, SEQ, D_MODEL)
    y_sample = xs.reshape(DEC_BATCH, DEC_SEQ, D_MODEL)
    return (y_prompt, y_sample, caches[0], caches[1], jnp.stack(hs, axis=1))
```

```python
import functools

import jax
import jax.numpy as jnp
import numpy as np
from jax import lax
from jax.experimental import pallas as pl
from jax.experimental.pallas import tpu as pltpu

D_MODEL = 1024
BATCH = 16
SEQ = 256
DEPTH = 2
DEC_BATCH = 4
DEC_SEQ = 4096
PAST_LEN = 256

GRID_W = 64
D_RNN = 1024
N_LRU_BLOCKS = 16
LRU_BLOCK = D_RNN // N_LRU_BLOCKS
CONV_W = 4
LRU_C = 8.0
N_HEADS = 16
HEAD_DIM = 64
D_ATT = N_HEADS * HEAD_DIM
WIN_H = 8
WIN_W = 16
N_EXPERTS = 16
N_GROUPS = 4
EXPERTS_PER_GROUP = N_EXPERTS // N_GROUPS
D_EXPERT = 512
N_MOD = 6
D_IN = 2 * D_RNN + 3 * D_ATT + 2 * D_MODEL
EPS = 1e-6
NEG_INF = -1e30

BF16 = jnp.bfloat16
F32 = jnp.float32

V7X_LANES = 128
V7X_SUBLANES = 8
V7X_MXU_DIM = 256
V7X_VMEM_BYTES = 64 * 1024 * 1024
VMEM_LIMIT = V7X_VMEM_BYTES - 8 * 1024 * 1024

CHUNK = D_MODEL
N_CHUNKS = D_IN // CHUNK
C_XRNN, C_GRNN, C_Q, C_K, C_V, C_GATE_R, C_GATE_A = range(N_CHUNKS)

NT_DIMS = (((1,), (1,)), ((), ()))


def _params(*sem):
    return pltpu.CompilerParams(dimension_semantics=sem, vmem_limit_bytes=VMEM_LIMIT)


def _adaln_kernel(c_ref, w_ref, b_ref, o_ref):
    cv = c_ref[...]
    s = cv * jax.nn.sigmoid(cv)
    o_ref[0] = jnp.dot(s.astype(BF16), w_ref[0].astype(BF16), preferred_element_type=F32) + b_ref[0]


def adaln_pallas(cvecs, w_ada, b_ada):
    r = cvecs.shape[0]
    return pl.pallas_call(
        _adaln_kernel,
        grid=(DEPTH, N_MOD),
        in_specs=[pl.BlockSpec((r, D_MODEL), lambda l, j: (0, 0)),
                  pl.BlockSpec((1, D_MODEL, D_MODEL), lambda l, j: (l, 0, j)),
                  pl.BlockSpec((1, 1, D_MODEL), lambda l, j: (l, 0, j))],
        out_specs=pl.BlockSpec((1, r, D_MODEL), lambda l, j: (l, 0, j)),
        out_shape=jax.ShapeDtypeStruct((DEPTH, r, N_MOD * D_MODEL), F32),
        compiler_params=_params("arbitrary", "arbitrary"),
        name="adaln",
    )(cvecs, w_ada, b_ada.reshape(DEPTH, 1, N_MOD * D_MODEL))


def _mod_spec(tm, seq_len, per_seq):
    if per_seq:
        return pl.BlockSpec((1, N_MOD, D_MODEL), lambda i, *_: (i * tm // seq_len, 0, 0))
    return pl.BlockSpec((1, N_MOD, D_MODEL), lambda i, *_: (0, 0, 0))


def _rms(x, g):
    return x * lax.rsqrt(jnp.mean(x * x, axis=-1, keepdims=True) + EPS) * g


def _inproj_kernel(x_ref, mod_ref, g_ref, w_ref, *refs, emit_kv):
    if emit_kv:
        p_ref, k32_ref, v32_ref, h_scr = refs[-4:]
    else:
        p_ref, h_scr = refs
    j = pl.program_id(1)

    @pl.when(j == 0)
    def _():
        y = _rms(x_ref[...], g_ref[...])
        h_scr[...] = (y * (1.0 + mod_ref[0, 1:2, :]) + mod_ref[0, 0:1, :]).astype(BF16)

    acc = jnp.dot(h_scr[...], w_ref[0], preferred_element_type=F32)
    p_ref[...] = (acc * jnp.where(j == C_Q, HEAD_DIM ** -0.5, 1.0)).astype(BF16)
    if emit_kv:
        @pl.when(j == C_K)
        def _():
            k32_ref[...] = acc.reshape(k32_ref.shape)

        @pl.when(j == C_V)
        def _():
            v32_ref[...] = acc.reshape(v32_ref.shape)


def inproj_pallas(x, mod, g, w_in, layer, seq_len, per_seq, tm, emit_kv, caches=()):
    n = x.shape[0]
    row = lambda i, j: (i, 0)
    in_specs = [pl.BlockSpec((tm, D_MODEL), row), _mod_spec(tm, seq_len, per_seq),
                pl.BlockSpec((1, D_MODEL), lambda i, j: (0, 0)),
                pl.BlockSpec((1, D_MODEL, CHUNK), lambda i, j: (layer, 0, j))]
    out_shape = [jax.ShapeDtypeStruct((n, D_IN), BF16)]
    out_specs = [pl.BlockSpec((tm, CHUNK), lambda i, j: (i, j))]
    aliases = {}
    if emit_kv:
        out_shape += [jax.ShapeDtypeStruct((n // seq_len, DEPTH, seq_len, N_HEADS, HEAD_DIM), F32)] * 2
        out_specs += [pl.BlockSpec((tm // seq_len, 1, seq_len, N_HEADS, HEAD_DIM),
                                   lambda i, j: (i, layer, 0, 0, 0))] * 2
        aliases = {len(in_specs) + k: 1 + k for k in range(len(caches))}
        in_specs += [pl.BlockSpec(memory_space=pl.ANY)] * len(caches)
    return pl.pallas_call(
        functools.partial(_inproj_kernel, emit_kv=emit_kv),
        grid=(n // tm, N_CHUNKS),
        in_specs=in_specs,
        out_specs=out_specs,
        out_shape=out_shape,
        input_output_aliases=aliases,
        scratch_shapes=[pltpu.VMEM((tm, D_MODEL), BF16)],
        compiler_params=_params("arbitrary", "arbitrary"),
        name="inproj",
    )(x, mod, g, w_in, *caches)


LRU_CB = 512
LRU_TC = 256
LRU_SUB = V7X_MXU_DIM
LRU_HALO = 16


LRU_SEG = LRU_TC // V7X_SUBLANES
LRU_NSLAB = LRU_SEG + CONV_W


def _lru_row_maps():
    s = np.arange(V7X_SUBLANES)[None, :]
    src = (LRU_HALO - CONV_W // 2) + LRU_SEG * s + np.arange(LRU_NSLAB)[:, None]
    sel = np.zeros((LRU_NSLAB * V7X_SUBLANES, LRU_TC + 2 * LRU_HALO), np.float32)
    sel[np.arange(sel.shape[0]), src.reshape(-1)] = 1.0
    tok = (LRU_SEG * s + np.arange(LRU_SEG)[:, None]).reshape(-1)
    perm = np.zeros((LRU_TC, LRU_TC), np.float32)
    perm[np.arange(LRU_TC), tok] = 1.0
    return jnp.asarray(sel, BF16), jnp.asarray(perm, BF16), jnp.asarray(perm.T, BF16)


def _slab_scan(a3, b3, carry, reverse):
    n = a3.shape[0]
    hs, cum = [None] * n, [None] * n
    h = cp = None
    for t in (range(n - 1, -1, -1) if reverse else range(n)):
        h = b3[t] if h is None else a3[t] * h + b3[t]
        cp = a3[t] if cp is None else cp * a3[t]
        hs[t], cum[t] = h, cp
    sub = lax.broadcasted_iota(jnp.int32, (V7X_SUBLANES, 1), 0)
    pa, pb = cp, h
    s = 1
    while s < V7X_SUBLANES:
        shift = V7X_SUBLANES - s if reverse else s
        ok = (sub < V7X_SUBLANES - s) if reverse else (sub >= s)
        a_sh = pltpu.roll(pa, shift, axis=0)
        b_sh = pltpu.roll(pb, shift, axis=0)
        pb = jnp.where(ok, pa * b_sh + pb, pb)
        pa = jnp.where(ok, pa * a_sh, pa)
        s *= 2
    leaving = pb + pa * carry
    first, last = (V7X_SUBLANES - 1, 0) if reverse else (0, V7X_SUBLANES - 1)
    entering = jnp.where(sub == first, carry, pltpu.roll(leaving, V7X_SUBLANES - 1 if reverse else 1, axis=0))
    out = jnp.concatenate([hs[t] + cum[t] * entering for t in range(n)], axis=0)
    return out, leaving[last:last + 1, :]


def _sigmoid(x):
    return 0.5 * jnp.tanh(0.5 * x) + 0.5


def _lru_kernel(x_ref, gate_ref, h0_ref, cw_ref, cb_ref, wg_ref, gb_ref, lam_ref, sel_ref, perm_ref, permt_ref,
                y_ref, fin_ref, hf_scr, u_scr):
    t_len = x_ref.shape[1]
    cb = x_ref.shape[2]
    n_chunks = t_len // LRU_TC

    def conv_chunk(c):
        t0 = pl.multiple_of(c * LRU_TC, LRU_TC)
        cur = x_ref[0, pl.ds(t0, LRU_TC), :]
        lo = pl.multiple_of(jnp.maximum(t0 - LRU_HALO, 0), LRU_HALO)
        hi = pl.multiple_of(jnp.minimum(t0 + LRU_TC, t_len - LRU_HALO), LRU_HALO)
        prev = x_ref[0, pl.ds(lo, LRU_HALO), :]
        nxt = x_ref[0, pl.ds(hi, LRU_HALO), :]
        prev = jnp.where(c > 0, prev, jnp.zeros_like(prev))
        nxt = jnp.where(c < n_chunks - 1, nxt, jnp.zeros_like(nxt))
        ext = jnp.concatenate([prev, cur, nxt], axis=0)
        xs = jnp.dot(sel_ref[...], ext, preferred_element_type=F32).reshape(LRU_NSLAB, V7X_SUBLANES, cb)
        u = cb_ref[...][None] + jnp.zeros((LRU_SEG, V7X_SUBLANES, cb), F32)
        for j in range(CONV_W):
            u = u + xs[j:j + LRU_SEG] * cw_ref[j:j + 1, :][None]
        return t0, u.reshape(LRU_TC, cb)

    def slabs(v):
        return v.reshape(LRU_SEG, V7X_SUBLANES, cb)

    def gates(u, d):
        ub = u.astype(BF16)
        pre = [jnp.dot(ub[:, LRU_SUB * s:LRU_SUB * (s + 1)], wg_ref[d, s], preferred_element_type=F32)
               for s in range(cb // LRU_SUB)]
        pre_a = jnp.concatenate([p[:, :LRU_SUB] for p in pre], axis=1)
        pre_x = jnp.concatenate([p[:, LRU_SUB:] for p in pre], axis=1)
        r = _sigmoid(pre_a + gb_ref[2 * d:2 * d + 1, :])
        i = _sigmoid(pre_x + gb_ref[2 * d + 1:2 * d + 2, :])
        log_a = (-LRU_C * jax.nn.softplus(-lam_ref[d:d + 1, :])) * r
        a = jnp.exp(log_a)
        th = jnp.tanh(log_a)
        num = -2.0 * th
        scale = jnp.where(num > 0.0, num * lax.rsqrt(num * (1.0 - th)), 0.0)
        inp = scale * (i * u)
        return a, inp

    def fwd(c, carry):
        t0, u = conv_chunk(c)
        u_scr[pl.ds(t0, LRU_TC), :] = u
        a, inp = gates(u, 0)
        h, carry = _slab_scan(slabs(a), slabs(inp), carry, reverse=False)
        hf_scr[pl.ds(t0, LRU_TC), :] = h
        return carry

    unroll = 4 if n_chunks % 4 == 0 else 1
    fin_f = lax.fori_loop(0, n_chunks, fwd, h0_ref[0, 0:1, :], unroll=unroll)

    def bwd(k, carry):
        t0 = pl.multiple_of((n_chunks - 1 - k) * LRU_TC, LRU_TC)
        a, inp = gates(u_scr[pl.ds(t0, LRU_TC), :], 1)
        h, carry = _slab_scan(slabs(a), slabs(inp), carry, reverse=True)
        g = jnp.dot(perm_ref[...], gate_ref[0, pl.ds(t0, LRU_TC), :], preferred_element_type=F32)
        y = ((hf_scr[pl.ds(t0, LRU_TC), :] + h) * jax.nn.gelu(g)).astype(BF16)
        y_ref[0, pl.ds(t0, LRU_TC), :] = jnp.dot(permt_ref[...], y, preferred_element_type=F32).astype(y_ref.dtype)
        return carry

    fin_b = lax.fori_loop(0, n_chunks, bwd, h0_ref[0, 1:2, :], unroll=unroll)
    fin_ref[0, 0:1, :] = fin_f
    fin_ref[0, 1:2, :] = fin_b


def _lru_gate_weights(lru_wa, lru_wx):
    per = LRU_SUB // LRU_BLOCK
    eye = jnp.eye(per, dtype=F32)

    def dense(w):
        w = w.reshape(2, D_RNN // LRU_SUB, per, LRU_BLOCK, LRU_BLOCK)
        full = w[:, :, :, :, None, :] * eye[None, None, :, None, :, None]
        return full.reshape(2, D_RNN // LRU_SUB, LRU_SUB, LRU_SUB)

    return jnp.concatenate([dense(lru_wa), dense(lru_wx)], axis=-1).astype(BF16)


def lru_pallas(p, h0, conv_w, conv_b, wg, gb, lam):
    b, t, _ = p.shape
    n_cb = D_RNN // LRU_CB
    maps = _lru_row_maps()
    whole = lambda m: pl.BlockSpec(m.shape, lambda bi, ci: (0, 0))
    return pl.pallas_call(
        _lru_kernel,
        grid=(b, n_cb),
        in_specs=[pl.BlockSpec((1, t, LRU_CB), lambda bi, ci: (bi, 0, C_XRNN * n_cb + ci)),
                  pl.BlockSpec((1, t, LRU_CB), lambda bi, ci: (bi, 0, C_GRNN * n_cb + ci)),
                  pl.BlockSpec((1, 2, LRU_CB), lambda bi, ci: (bi, 0, ci)),
                  pl.BlockSpec((CONV_W, LRU_CB), lambda bi, ci: (0, ci)),
                  pl.BlockSpec((1, LRU_CB), lambda bi, ci: (0, ci)),
                  pl.BlockSpec((2, LRU_CB // LRU_SUB, LRU_SUB, 2 * LRU_SUB), lambda bi, ci: (0, ci, 0, 0)),
                  pl.BlockSpec((4, LRU_CB), lambda bi, ci: (0, ci)),
                  pl.BlockSpec((2, LRU_CB), lambda bi, ci: (0, ci))] + [whole(m) for m in maps],
        out_specs=[pl.BlockSpec((1, t, LRU_CB), lambda bi, ci: (bi, 0, ci)),
                   pl.BlockSpec((1, 2, LRU_CB), lambda bi, ci: (bi, 0, ci))],
        out_shape=[jax.ShapeDtypeStruct((b, t, D_RNN), BF16), jax.ShapeDtypeStruct((b, 2, D_RNN), F32)],
        scratch_shapes=[pltpu.VMEM((t, LRU_CB), F32), pltpu.VMEM((t, LRU_CB), F32)],
        compiler_params=_params("arbitrary", "arbitrary"),
        name="lru",
    )(p, p, h0, conv_w, conv_b, wg, gb, lam, *maps)


Q_ROWS = 4
Q_TILE = Q_ROWS * GRID_W
KEY_ROWS = 12
N_Q_TILES = DEC_SEQ // Q_TILE
ATT_LANES = 1024
HEADS_PER_STEP = ATT_LANES // HEAD_DIM
PAIR = 2 * HEAD_DIM


def _window_block(t):
    return jnp.clip(t - 1, 0, N_Q_TILES - KEY_ROWS // Q_ROWS)


N_DR_PAIRS = 2 * WIN_H


def _natten_tables(rpb):
    c = np.arange(GRID_W)
    q_start = np.clip(c - WIN_W // 2, 0, GRID_W - WIN_W)
    col_valid = (c[None, :] >= q_start[:, None]) & (c[None, :] < q_start[:, None] + WIN_W)
    dc = c[None, :] - c[:, None] + WIN_W - 1
    sel_c = ((dc[None] == np.arange(2 * WIN_W - 1)[:, None, None]) & col_valid[None]).astype(np.float32)
    t = jnp.einsum('lhrd,dck->lhrck', rpb.astype(F32), jnp.asarray(sel_c), precision=lax.Precision.HIGHEST)
    t = jnp.where(col_valid[None, None, None], t, NEG_INF)
    t = jnp.pad(t, ((0, 0), (0, 0), (1, 1), (0, 0), (0, 0)), constant_values=NEG_INF)
    return jnp.concatenate([t[:, :, :-1], t[:, :, 1:]], axis=-1)


def _attend(q2, keys, vals, biases):
    lane_head = lax.broadcasted_iota(jnp.int32, (1, PAIR), 1) // HEAD_DIM
    acc = jnp.zeros((q2.shape[0], PAIR), F32)
    for j in range(2):
        mine = lane_head == j
        qh = jnp.where(mine, q2, jnp.zeros_like(q2))
        s = []
        for kb, bb in zip(keys, biases[j]):
            sd = lax.dot_general(qh, kb, NT_DIMS, preferred_element_type=F32)
            s.append(sd if bb is None else sd + bb)
        m = s[0].max(axis=-1, keepdims=True)
        for sd in s[1:]:
            m = jnp.maximum(m, sd.max(axis=-1, keepdims=True))
        o = None
        for sd, vb in zip(s, vals):
            pv = jnp.dot(jnp.exp(sd - m).astype(BF16), jnp.where(mine, vb, jnp.ones_like(vb)),
                         preferred_element_type=F32)
            o = pv if o is None else o + pv
        acc = acc + jnp.where(mine, o / pltpu.roll(o, HEAD_DIM, axis=1), 0.0)
    return acc


def _natten_kernel(q_ref, k0_ref, k1_ref, k2_ref, v0_ref, v1_ref, v2_ref, kc_ref, vc_ref, tt_ref, o_ref):
    ti = pl.program_id(1)
    rows = DEC_SEQ // GRID_W
    wstart = _window_block(ti) * Q_ROWS
    first_row = lax.broadcasted_iota(jnp.int32, (1, 2 * GRID_W), 1) < GRID_W
    pieces = {}
    for a in range(Q_ROWS):
        r = ti * Q_ROWS + a
        start_r = jnp.clip(r - WIN_H // 2, 0, rows - WIN_H)
        for i in range(0, KEY_ROWS, 2):
            kr = wstart + i
            ok = [((kr + e >= start_r) & (kr + e < start_r + WIN_H)).astype(jnp.int32) for e in range(2)]
            pieces[a, i] = (jnp.clip(kr - r + WIN_H, 0, N_DR_PAIRS - 1), jnp.where(first_row, ok[0], ok[1]) > 0)

    def bias_block(h, d):
        return jnp.concatenate(
            [jnp.concatenate([jnp.where(pieces[a, i][1], tt_ref[0, h, pieces[a, i][0]], NEG_INF)
                              for i in range(Q_ROWS * d, Q_ROWS * (d + 1), 2)], axis=1)
             for a in range(Q_ROWS)], axis=0)

    k_refs = (k0_ref, k1_ref, k2_ref, kc_ref)
    v_refs = (v0_ref, v1_ref, v2_ref, vc_ref)
    outs = []
    for hp in range(ATT_LANES // PAIR):
        sl = slice(PAIR * hp, PAIR * (hp + 1))
        biases = [[bias_block(2 * hp + j, d) for d in range(3)] + [None] for j in range(2)]
        outs.append(_attend(q_ref[0, :, sl], [r[0, :, sl] for r in k_refs], [r[0, :, sl] for r in v_refs], biases))
    o_ref[0] = jnp.concatenate(outs, axis=1).astype(o_ref.dtype)


def natten_pallas(p, k_ctx, v_ctx, tables, layer):
    b, t, _ = p.shape
    n_lb = D_ATT // ATT_LANES
    q_spec = pl.BlockSpec((1, Q_TILE, ATT_LANES), lambda lb, ti, bi: (bi, ti, C_Q * n_lb + lb))
    win = lambda ch, d: pl.BlockSpec((1, Q_TILE, ATT_LANES),
                                     lambda lb, ti, bi: (bi, _window_block(ti) + d, ch * n_lb + lb))
    ctx = pl.BlockSpec((1, PAST_LEN, ATT_LANES), lambda lb, ti, bi: (bi, 0, lb))
    return pl.pallas_call(
        _natten_kernel,
        grid=(n_lb, N_Q_TILES, b),
        in_specs=[q_spec, win(C_K, 0), win(C_K, 1), win(C_K, 2), win(C_V, 0), win(C_V, 1), win(C_V, 2), ctx, ctx,
                  pl.BlockSpec((1, HEADS_PER_STEP, N_DR_PAIRS, GRID_W, 2 * GRID_W),
                               lambda lb, ti, bi: (layer, lb, 0, 0, 0))],
        out_specs=pl.BlockSpec((1, Q_TILE, ATT_LANES), lambda lb, ti, bi: (bi, ti, lb)),
        out_shape=jax.ShapeDtypeStruct((b, t, D_ATT), BF16),
        compiler_params=_params("arbitrary", "arbitrary", "arbitrary"),
        name="natten",
    )(p, p, p, p, p, p, p, k_ctx, v_ctx, tables)


def _ctxatt_kernel(q_ref, k_ref, v_ref, o_ref):
    outs = []
    for hp in range(ATT_LANES // PAIR):
        sl = slice(PAIR * hp, PAIR * (hp + 1))
        outs.append(_attend(q_ref[0, :, sl], [k_ref[0, :, sl]], [v_ref[0, :, sl]], [[None], [None]]))
    o_ref[0] = jnp.concatenate(outs, axis=1).astype(o_ref.dtype)


def ctxatt_pallas(p):
    b, t, _ = p.shape
    n_lb = D_ATT // ATT_LANES
    blk = lambda ch: pl.BlockSpec((1, t, ATT_LANES), lambda lb, bi: (bi, 0, ch * n_lb + lb))
    return pl.pallas_call(
        _ctxatt_kernel,
        grid=(n_lb, b),
        in_specs=[blk(C_Q), blk(C_K), blk(C_V)],
        out_specs=pl.BlockSpec((1, t, ATT_LANES), lambda lb, bi: (bi, 0, lb)),
        out_shape=jax.ShapeDtypeStruct((b, t, D_ATT), BF16),
        compiler_params=_params("arbitrary", "arbitrary"),
        name="ctxatt",
    )(p, p, p)


def _split_bf16(x):
    hi = x.astype(BF16)
    return hi, (x - hi.astype(F32)).astype(BF16)


def _route(logits_t, rb_ref):
    score = [jax.nn.sigmoid(logits_t[e:e + 1, :]) for e in range(N_EXPERTS)]
    sel = [score[e] + rb_ref[e] for e in range(N_EXPERTS)]
    best_g = None
    for g in range(N_GROUPS):
        v = sel[EXPERTS_PER_GROUP * g:EXPERTS_PER_GROUP * (g + 1)]
        top2 = None
        for i in range(EXPERTS_PER_GROUP):
            for j in range(i + 1, EXPERTS_PER_GROUP):
                pair = v[i] + v[j]
                top2 = pair if top2 is None else jnp.maximum(top2, pair)
        if best_g is None:
            best_g, best_v = jnp.zeros_like(top2, dtype=jnp.int32), top2
        else:
            upd = top2 > best_v
            best_g = jnp.where(upd, g, best_g)
            best_v = jnp.where(upd, top2, best_v)

    def in_best(vals, j):
        out = vals[j]
        for g in range(1, N_GROUPS):
            out = jnp.where(best_g == g, vals[EXPERTS_PER_GROUP * g + j], out)
        return out

    v = [in_best(sel, j) for j in range(EXPERTS_PER_GROUP)]
    sc = [in_best(score, j) for j in range(EXPERTS_PER_GROUP)]

    def first_argmax(vals):
        idx, top = jnp.zeros_like(best_g), vals[0]
        for j in range(1, EXPERTS_PER_GROUP):
            upd = vals[j] > top
            idx = jnp.where(upd, j, idx)
            top = jnp.where(upd, vals[j], top)
        return idx

    i1 = first_argmax(v)
    i2 = first_argmax([jnp.where(i1 == j, -jnp.inf, v[j]) for j in range(EXPERTS_PER_GROUP)])
    pick = lambda idx: sum(jnp.where(idx == j, sc[j], 0.0) for j in range(EXPERTS_PER_GROUP))
    w1, w2 = pick(i1), pick(i2)
    den = w1 + w2
    c1, c2 = w1 / den, w2 / den
    rows = []
    for e in range(N_EXPERTS):
        g, j = divmod(e, EXPERTS_PER_GROUP)
        rows.append(jnp.where(best_g == g, jnp.where(i1 == j, c1, 0.0) + jnp.where(i2 == j, c2, 0.0), 0.0))
    return jnp.concatenate(rows, axis=0), best_g


AUX_LANES = V7X_LANES
AUX_MID = N_EXPERTS
AUX_LO = 2 * N_EXPERTS
AUX_GROUP = 3 * N_EXPERTS
D_MOE_IN = D_MODEL + AUX_LANES


def _merge_kernel(rb_ref, yr_ref, ya_ref, gr_ref, ga_ref, x_ref, mod_ref, g_ref, wr_ref, wa_ref, wo_ref, wrt_ref,
                  x1_ref, h2_ref, gid_ref):
    m = (_sigmoid(gr_ref[...].astype(F32)) * jnp.dot(yr_ref[...], wr_ref[0], preferred_element_type=F32)
         + _sigmoid(ga_ref[...].astype(F32)) * jnp.dot(ya_ref[...], wa_ref[0], preferred_element_type=F32))
    o = jnp.dot(m.astype(BF16), wo_ref[0], preferred_element_type=F32)
    x1 = x_ref[...] + mod_ref[0, 2:3, :] * o
    x1_ref[...] = x1
    h2 = _rms(x1, g_ref[...]) * (1.0 + mod_ref[0, 4:5, :]) + mod_ref[0, 3:4, :]
    h_hi, h_lo = _split_bf16(h2)
    w_hi, w_lo = _split_bf16(wrt_ref[...])
    dot_nt = lambda a, b: lax.dot_general(a, b, NT_DIMS, preferred_element_type=F32)
    by_h_hi = dot_nt(jnp.concatenate([w_hi, w_lo], axis=0), h_hi)
    logits_t = by_h_hi[:N_EXPERTS] + (dot_nt(w_hi, h_lo) + by_h_hi[N_EXPERTS:])
    comb_t, group = _route(logits_t, rb_ref)
    tm = comb_t.shape[1]
    group = group.astype(F32)
    gid_ref[...] = jnp.concatenate([group, jnp.zeros((V7X_SUBLANES - 1, tm), F32)], axis=0)
    padded = jnp.concatenate([comb_t, jnp.zeros((AUX_GROUP - N_EXPERTS, tm), F32), group,
                              jnp.zeros((AUX_LANES - AUX_GROUP - 1, tm), F32)], axis=0)
    c = padded.T
    c_hi = c.astype(BF16).astype(F32)
    c_mid = (c - c_hi).astype(BF16).astype(F32)
    c_lo = c - c_hi - c_mid
    aux = c_hi + pltpu.roll(c_mid, AUX_MID, axis=1) + pltpu.roll(c_lo, AUX_LO, axis=1)
    h2_ref[...] = jnp.concatenate([h2.astype(BF16), aux.astype(BF16)], axis=1)


def merge_pallas(y_rnn, y_att, p, x, mod, g, w_br_rnn, w_br_att, w_out, w_router_t, router_bias, layer, seq_len,
                 per_seq, tm):
    n = x.shape[0]
    row = lambda i: (i, 0)
    full = lambda shape: pl.BlockSpec(shape, lambda i: (0, 0))
    weight = pl.BlockSpec((1, D_MODEL, D_MODEL), lambda i: (layer, 0, 0))
    return pl.pallas_call(
        _merge_kernel,
        grid=(n // tm,),
        in_specs=[pl.BlockSpec(memory_space=pltpu.SMEM),
                  pl.BlockSpec((tm, D_RNN), row), pl.BlockSpec((tm, D_ATT), row),
                  pl.BlockSpec((tm, CHUNK), lambda i: (i, C_GATE_R)), pl.BlockSpec((tm, CHUNK), lambda i: (i, C_GATE_A)),
                  pl.BlockSpec((tm, D_MODEL), row), _mod_spec(tm, seq_len, per_seq), full((1, D_MODEL)),
                  weight, weight, weight, full((N_EXPERTS, D_MODEL))],
        out_specs=[pl.BlockSpec((tm, D_MODEL), row), pl.BlockSpec((tm, D_MOE_IN), row),
                   pl.BlockSpec((V7X_SUBLANES, tm), lambda i: (0, i))],
        out_shape=[jax.ShapeDtypeStruct((n, D_MODEL), F32), jax.ShapeDtypeStruct((n, D_MOE_IN), BF16),
                   jax.ShapeDtypeStruct((V7X_SUBLANES, n), F32)],
        compiler_params=_params("arbitrary"),
        name="merge",
    )(router_bias, y_rnn, y_att, p, p, x, mod, g, w_br_rnn, w_br_att, w_out, w_router_t)


MOE_TB = 1024
MOE_RT = 128
MOE_TBP = MOE_TB + N_GROUPS * MOE_RT


def _group_segments(gid_row):
    sub = lax.broadcasted_iota(jnp.int32, (V7X_SUBLANES, 1), 0).astype(F32)
    onehot = (gid_row == sub).astype(F32)
    cnt = jnp.sum(onehot, axis=1, keepdims=True)
    padded = jnp.floor((cnt + (MOE_RT - 1)) * (1.0 / MOE_RT)) * MOE_RT
    starts, run = [], jnp.zeros((1, 1), F32)
    for g in range(N_GROUPS):
        starts.append(run)
        run = run + padded[g:g + 1, :]
    return onehot, starts, [padded[g:g + 1, :] for g in range(N_GROUPS)], run


def _to_int(v):
    return v[0, 0].astype(jnp.int32)


def _moe_kernel(h_ref, gid_ref, x_ref, mod_ref, gf_ref, wg_ref, wu_ref, wd_ref, o_ref,
                p_scr, xs_scr, cs_scr, ys_scr, *, final_norm):
    step = pl.program_id(1)
    onehot, starts, sizes, used = _group_segments(gid_ref[0:1, :])
    lane = lax.broadcasted_iota(jnp.int32, (1, AUX_LANES), 1)

    @pl.when(step == 0)
    def _():
        t_row = lax.broadcasted_iota(jnp.int32, (MOE_TB, MOE_TB), 0)
        t_col = lax.broadcasted_iota(jnp.int32, (MOE_TB, MOE_TB), 1)
        earlier = (t_row < t_col).astype(BF16)
        rank = jnp.dot(onehot.astype(BF16), earlier, preferred_element_type=F32)
        pos = jnp.zeros((1, MOE_TB), F32)
        for g in range(N_GROUPS):
            pos = pos + onehot[g:g + 1, :] * (starts[g] + rank[g:g + 1, :])
        dest = lax.broadcasted_iota(jnp.int32, (MOE_TBP, 1), 0).astype(F32)
        p_scr[...] = (dest == pos).astype(BF16)
        for r0 in range(0, MOE_TBP, V7X_MXU_DIM):
            rows = pl.ds(r0, V7X_MXU_DIM)
            sorted_rows = jnp.dot(p_scr[rows, :], h_ref[...], preferred_element_type=F32)
            xs_scr[rows, :] = sorted_rows[:, :D_MODEL].astype(BF16)
            aux = sorted_rows[:, D_MODEL:]
            cs_scr[rows, :] = (aux + pltpu.roll(aux, AUX_LANES - AUX_MID, axis=1)
                               + pltpu.roll(aux, AUX_LANES - AUX_LO, axis=1))
        first_free = pl.multiple_of(_to_int(used), MOE_RT)

        def clear(k, carry):
            ys_scr[pl.ds(pl.multiple_of(first_free + k * MOE_RT, MOE_RT), MOE_RT), :] = jnp.zeros(
                (MOE_RT, D_MODEL), BF16)
            return carry

        lax.fori_loop(0, (MOE_TBP - first_free) // MOE_RT, clear, 0)

    start_v, size_v = starts[0], sizes[0]
    for g in range(1, N_GROUPS):
        start_v = jnp.where(step == g, starts[g], start_v)
        size_v = jnp.where(step == g, sizes[g], size_v)
    seg_start = _to_int(start_v)
    n_tiles = _to_int(size_v) // MOE_RT

    def run_expert(r0, n_rows):
        rows = pl.ds(pl.multiple_of(r0, MOE_RT), n_rows)
        x = xs_scr[rows, :]
        cs = cs_scr[rows, :]
        y = None
        for k in range(EXPERTS_PER_GROUP):
            gate = jnp.dot(x, wg_ref[0, k], preferred_element_type=F32)
            up = jnp.dot(x, wu_ref[0, k], preferred_element_type=F32)
            act = (gate * _sigmoid(gate)) * up
            yk = jnp.dot(act.astype(BF16), wd_ref[0, k], preferred_element_type=F32)
            yk = jnp.sum(jnp.where(lane == step * EXPERTS_PER_GROUP + k, cs, 0.0), axis=-1, keepdims=True) * yk
            y = yk if y is None else y + yk
        ys_scr[rows, :] = y.astype(BF16)

    def pair(k, carry):
        run_expert(seg_start + k * (2 * MOE_RT), 2 * MOE_RT)
        return carry

    lax.fori_loop(0, n_tiles // 2, pair, 0)

    @pl.when(n_tiles % 2 == 1)
    def _():
        run_expert(seg_start + (n_tiles - 1) * MOE_RT, MOE_RT)

    @pl.when(step == N_GROUPS - 1)
    def _():
        y = lax.dot_general(p_scr[...], ys_scr[...], (((0,), (0,)), ((), ())),
                            preferred_element_type=F32)
        x2 = x_ref[...] + mod_ref[0, 5:6, :] * y
        o_ref[...] = _rms(x2, gf_ref[...]) if final_norm else x2


def moe_pallas(h2x, gid, x1, mod, g_final, w_gate, w_up, w_down, layer, seq_len, per_seq, final_norm):
    n = x1.shape[0]
    row = lambda i, g: (i, 0)
    group = lambda i, g: (layer, g, 0, 0)
    return pl.pallas_call(
        functools.partial(_moe_kernel, final_norm=final_norm),
        grid=(n // MOE_TB, N_GROUPS),
        in_specs=[pl.BlockSpec((MOE_TB, D_MOE_IN), row),
                  pl.BlockSpec((V7X_SUBLANES, MOE_TB), lambda i, g: (0, i)),
                  pl.BlockSpec((MOE_TB, D_MODEL), row, pipeline_mode=pl.Buffered(1)),
                  _mod_spec(MOE_TB, seq_len, per_seq),
                  pl.BlockSpec((1, D_MODEL), lambda i, g: (0, 0)),
                  pl.BlockSpec((1, EXPERTS_PER_GROUP, D_MODEL, D_EXPERT), group),
                  pl.BlockSpec((1, EXPERTS_PER_GROUP, D_MODEL, D_EXPERT), group),
                  pl.BlockSpec((1, EXPERTS_PER_GROUP, D_EXPERT, D_MODEL), group)],
        out_specs=pl.BlockSpec((MOE_TB, D_MODEL), row),
        out_shape=jax.ShapeDtypeStruct((n, D_MODEL), F32),
        scratch_shapes=[pltpu.VMEM((MOE_TBP, MOE_TB), BF16), pltpu.VMEM((MOE_TBP, D_MODEL), BF16),
                        pltpu.VMEM((MOE_TBP, AUX_LANES), F32), pltpu.VMEM((MOE_TBP, D_MODEL), BF16)],
        compiler_params=_params("arbitrary", "arbitrary"),
        name="moe",
    )(h2x, gid, x1, mod, g_final, w_gate, w_up, w_down)


TM_PROJ = 2048
TM_MERGE = 1024


def _layer(x, mod, seq_len, per_seq, lw, h0, ctx_kv, tables, layer, caches=()):
    n = x.shape[0]
    b = n // seq_len
    emit_kv = ctx_kv is None
    outs = inproj_pallas(x, mod, lw['norm_g'][0:1], lw['w_in'], layer, seq_len, per_seq,
                         TM_PROJ // 4 if emit_kv else TM_PROJ, emit_kv, caches)
    p = outs[0]
    p3 = p.reshape(b, seq_len, D_IN)
    y_rnn, h_fin = lru_pallas(p3, h0, lw['conv_w'], lw['conv_b'], lw['wg'], lw['gb'], lw['lam'])
    if emit_kv:
        y_att = ctxatt_pallas(p3)
    else:
        y_att = natten_pallas(p3, ctx_kv[0], ctx_kv[1], tables, layer)
    x1, h2x, gid = merge_pallas(y_rnn.reshape(n, D_RNN), y_att.reshape(n, D_ATT), p, x, mod, lw['norm_g'][1:2],
                                lw['w_br_rnn'], lw['w_br_att'], lw['w_out'], lw['w_router_t'], lw['router_bias'],
                                layer, seq_len, per_seq, TM_MERGE)
    x2 = moe_pallas(h2x, gid, x1, mod, lw['g_final'], lw['w_exp_gate'], lw['w_exp_up'], lw['w_exp_down'], layer,
                    seq_len, per_seq, layer == DEPTH - 1)
    kv = (outs[1], outs[2]) if emit_kv else None
    return x2, kv, h_fin


def kernel(x_prompt, x_sample, cache_k, cache_v, state_lru, c, c_ctx, w_ada, b_ada, norm_g, w_in, conv_w,
           conv_b, lru_wa, lru_ba, lru_wx, lru_bx, lru_lam, rpb, w_br_rnn, w_br_att, w_out, w_router,
           router_bias, w_exp_gate, w_exp_up, w_exp_down, final_norm_g):
    cvecs = jnp.concatenate([c, c_ctx[None, :], jnp.zeros((V7X_SUBLANES - DEC_BATCH - 1, D_MODEL), F32)], axis=0)
    mods = adaln_pallas(cvecs, w_ada, b_ada).reshape(DEPTH, V7X_SUBLANES, N_MOD, D_MODEL)
    tables = _natten_tables(rpb)
    w_router_t = w_router.T
    xp = x_prompt.reshape(BATCH * SEQ, D_MODEL)
    xs = x_sample.reshape(DEC_BATCH * DEC_SEQ, D_MODEL)
    zeros_h0 = jnp.zeros((BATCH, 2, D_RNN), F32)
    w_exp = [w.astype(BF16) for w in (w_exp_gate, w_exp_up, w_exp_down)]
    w_proj = [w.astype(BF16) for w in (w_in, w_br_rnn, w_br_att, w_out)]
    caches, hs = (), []
    for l in range(DEPTH):
        lw = dict(
            norm_g=norm_g[l], w_in=w_proj[0], conv_w=conv_w[l], conv_b=conv_b[l][None, :],
            wg=_lru_gate_weights(lru_wa[l], lru_wx[l]),
            gb=jnp.stack([lru_ba[l, 0], lru_bx[l, 0], lru_ba[l, 1], lru_bx[l, 1]], axis=0), lam=lru_lam[l],
            w_br_rnn=w_proj[1], w_br_att=w_proj[2], w_out=w_proj[3],
            w_router_t=w_router_t, router_bias=router_bias, g_final=final_norm_g[None, :],
            w_exp_gate=w_exp[0], w_exp_up=w_exp[1], w_exp_down=w_exp[2])
        xp, caches, h_l = _layer(xp, mods[l, DEC_BATCH:DEC_BATCH + 1], SEQ, False, lw, zeros_h0, None, None, l,
                                 caches)
        hs.append(h_l)
        ctx_kv = (cache_k[:, l].reshape(DEC_BATCH, PAST_LEN, D_ATT).astype(BF16),
                  cache_v[:, l].reshape(DEC_BATCH, PAST_LEN, D_ATT).astype(BF16))
        xs, _, _ = _layer(xs, mods[l, :DEC_BATCH], DEC_SEQ, True, lw, state_lru[:, l], ctx_kv, tables, l)
    y_prompt = xp.reshape(BATCH, SEQ, D_MODEL)
    y_sample = xs.reshape(DEC_BATCH, DEC_SEQ, D_MODEL)
    return (y_prompt, y_sample, caches[0], caches[1], jnp.stack(hs, axis=1))
```

```python
import functools

import jax
import jax.numpy as jnp
import numpy as np
from jax import lax
from jax.experimental import pallas as pl
from jax.experimental.pallas import tpu as pltpu

D_MODEL = 1024
BATCH = 16
SEQ = 256
DEPTH = 2
DEC_BATCH = 4
DEC_SEQ = 4096
PAST_LEN = 256

GRID_W = 64
D_RNN = 1024
N_LRU_BLOCKS = 16
LRU_BLOCK = D_RNN // N_LRU_BLOCKS
CONV_W = 4
LRU_C = 8.0
N_HEADS = 16
HEAD_DIM = 64
D_ATT = N_HEADS * HEAD_DIM
WIN_H = 8
WIN_W = 16
N_EXPERTS = 16
N_GROUPS = 4
EXPERTS_PER_GROUP = N_EXPERTS // N_GROUPS
D_EXPERT = 512
N_MOD = 6
D_IN = 2 * D_RNN + 3 * D_ATT + 2 * D_MODEL
EPS = 1e-6
NEG_INF = -1e30

BF16 = jnp.bfloat16
F32 = jnp.float32

V7X_LANES = 128
V7X_SUBLANES = 8
V7X_MXU_DIM = 256
V7X_VMEM_BYTES = 64 * 1024 * 1024
VMEM_LIMIT = V7X_VMEM_BYTES - 8 * 1024 * 1024

CHUNK = D_MODEL
N_CHUNKS = D_IN // CHUNK
C_XRNN, C_GRNN, C_Q, C_K, C_V, C_GATE_R, C_GATE_A = range(N_CHUNKS)

NT_DIMS = (((1,), (1,)), ((), ()))


def _params(*sem):
    return pltpu.CompilerParams(dimension_semantics=sem, vmem_limit_bytes=VMEM_LIMIT)


def _adaln_kernel(c_ref, w_ref, b_ref, o_ref):
    cv = c_ref[...]
    s = cv * jax.nn.sigmoid(cv)
    o_ref[0] = jnp.dot(s.astype(BF16), w_ref[0].astype(BF16), preferred_element_type=F32) + b_ref[0]


def adaln_pallas(cvecs, w_ada, b_ada):
    r = cvecs.shape[0]
    return pl.pallas_call(
        _adaln_kernel,
        grid=(DEPTH, N_MOD),
        in_specs=[pl.BlockSpec((r, D_MODEL), lambda l, j: (0, 0)),
                  pl.BlockSpec((1, D_MODEL, D_MODEL), lambda l, j: (l, 0, j)),
                  pl.BlockSpec((1, 1, D_MODEL), lambda l, j: (l, 0, j))],
        out_specs=pl.BlockSpec((1, r, D_MODEL), lambda l, j: (l, 0, j)),
        out_shape=jax.ShapeDtypeStruct((DEPTH, r, N_MOD * D_MODEL), F32),
        compiler_params=_params("arbitrary", "arbitrary"),
        name="adaln",
    )(cvecs, w_ada, b_ada.reshape(DEPTH, 1, N_MOD * D_MODEL))


def _mod_spec(tm, seq_len, per_seq):
    if per_seq:
        return pl.BlockSpec((1, N_MOD, D_MODEL), lambda i, *_: (i * tm // seq_len, 0, 0))
    return pl.BlockSpec((1, N_MOD, D_MODEL), lambda i, *_: (0, 0, 0))


def _rms(x, g):
    return x * lax.rsqrt(jnp.mean(x * x, axis=-1, keepdims=True) + EPS) * g


def _inproj_kernel(x_ref, mod_ref, g_ref, w_ref, *refs, emit_kv):
    if emit_kv:
        p_ref, k32_ref, v32_ref, h_scr = refs[-4:]
    else:
        p_ref, h_scr = refs
    j = pl.program_id(1)

    @pl.when(j == 0)
    def _():
        y = _rms(x_ref[...], g_ref[...])
        h_scr[...] = (y * (1.0 + mod_ref[0, 1:2, :]) + mod_ref[0, 0:1, :]).astype(BF16)

    acc = jnp.dot(h_scr[...], w_ref[0], preferred_element_type=F32)
    p_ref[...] = (acc * jnp.where(j == C_Q, HEAD_DIM ** -0.5, 1.0)).astype(BF16)
    if emit_kv:
        @pl.when(j == C_K)
        def _():
            k32_ref[...] = acc.reshape(k32_ref.shape)

        @pl.when(j == C_V)
        def _():
            v32_ref[...] = acc.reshape(v32_ref.shape)


def inproj_pallas(x, mod, g, w_in, layer, seq_len, per_seq, tm, emit_kv, caches=()):
    n = x.shape[0]
    row = lambda i, j: (i, 0)
    in_specs = [pl.BlockSpec((tm, D_MODEL), row), _mod_spec(tm, seq_len, per_seq),
                pl.BlockSpec((1, D_MODEL), lambda i, j: (0, 0)),
                pl.BlockSpec((1, D_MODEL, CHUNK), lambda i, j: (layer, 0, j))]
    out_shape = [jax.ShapeDtypeStruct((n, D_IN), BF16)]
    out_specs = [pl.BlockSpec((tm, CHUNK), lambda i, j: (i, j))]
    aliases = {}
    if emit_kv:
        out_shape += [jax.ShapeDtypeStruct((n // seq_len, DEPTH, seq_len, N_HEADS, HEAD_DIM), F32)] * 2
        out_specs += [pl.BlockSpec((tm // seq_len, 1, seq_len, N_HEADS, HEAD_DIM),
                                   lambda i, j: (i, layer, 0, 0, 0))] * 2
        aliases = {len(in_specs) + k: 1 + k for k in range(len(caches))}
        in_specs += [pl.BlockSpec(memory_space=pl.ANY)] * len(caches)
    return pl.pallas_call(
        functools.partial(_inproj_kernel, emit_kv=emit_kv),
        grid=(n // tm, N_CHUNKS),
        in_specs=in_specs,
        out_specs=out_specs,
        out_shape=out_shape,
        input_output_aliases=aliases,
        scratch_shapes=[pltpu.VMEM((tm, D_MODEL), BF16)],
        compiler_params=_params("arbitrary", "arbitrary"),
        name="inproj",
    )(x, mod, g, w_in, *caches)


LRU_CB = 512
LRU_TC = 256
LRU_SUB = V7X_MXU_DIM
LRU_HALO = 16


LRU_SEG = LRU_TC // V7X_SUBLANES
LRU_NSLAB = LRU_SEG + CONV_W


def _lru_row_maps():
    s = np.arange(V7X_SUBLANES)[None, :]
    src = (LRU_HALO - CONV_W // 2) + LRU_SEG * s + np.arange(LRU_NSLAB)[:, None]
    sel = np.zeros((LRU_NSLAB * V7X_SUBLANES, LRU_TC + 2 * LRU_HALO), np.float32)
    sel[np.arange(sel.shape[0]), src.reshape(-1)] = 1.0
    tok = (LRU_SEG * s + np.arange(LRU_SEG)[:, None]).reshape(-1)
    perm = np.zeros((LRU_TC, LRU_TC), np.float32)
    perm[np.arange(LRU_TC), tok] = 1.0
    return jnp.asarray(sel, BF16), jnp.asarray(perm, BF16), jnp.asarray(perm.T, BF16)


def _slab_scan(a3, b3, carry, reverse):
    n = a3.shape[0]
    hs, cum = [None] * n, [None] * n
    h = cp = None
    for t in (range(n - 1, -1, -1) if reverse else range(n)):
        h = b3[t] if h is None else a3[t] * h + b3[t]
        cp = a3[t] if cp is None else cp * a3[t]
        hs[t], cum[t] = h, cp
    sub = lax.broadcasted_iota(jnp.int32, (V7X_SUBLANES, 1), 0)
    pa, pb = cp, h
    s = 1
    while s < V7X_SUBLANES:
        shift = V7X_SUBLANES - s if reverse else s
        ok = (sub < V7X_SUBLANES - s) if reverse else (sub >= s)
        a_sh = pltpu.roll(pa, shift, axis=0)
        b_sh = pltpu.roll(pb, shift, axis=0)
        pb = jnp.where(ok, pa * b_sh + pb, pb)
        pa = jnp.where(ok, pa * a_sh, pa)
        s *= 2
    leaving = pb + pa * carry
    first, last = (V7X_SUBLANES - 1, 0) if reverse else (0, V7X_SUBLANES - 1)
    entering = jnp.where(sub == first, carry, pltpu.roll(leaving, V7X_SUBLANES - 1 if reverse else 1, axis=0))
    out = jnp.concatenate([hs[t] + cum[t] * entering for t in range(n)], axis=0)
    return out, leaving[last:last + 1, :]


def _sigmoid(x):
    return 0.5 * jnp.tanh(0.5 * x) + 0.5


def _lru_kernel(x_ref, gate_ref, h0_ref, cw_ref, cb_ref, wg_ref, gb_ref, lam_ref, sel_ref, perm_ref, permt_ref,
                y_ref, fin_ref, hf_scr, u_scr):
    t_len = x_ref.shape[1]
    cb = x_ref.shape[2]
    n_chunks = t_len // LRU_TC

    def conv_chunk(c):
        t0 = pl.multiple_of(c * LRU_TC, LRU_TC)
        cur = x_ref[0, pl.ds(t0, LRU_TC), :]
        lo = pl.multiple_of(jnp.maximum(t0 - LRU_HALO, 0), LRU_HALO)
        hi = pl.multiple_of(jnp.minimum(t0 + LRU_TC, t_len - LRU_HALO), LRU_HALO)
        prev = x_ref[0, pl.ds(lo, LRU_HALO), :]
        nxt = x_ref[0, pl.ds(hi, LRU_HALO), :]
        prev = jnp.where(c > 0, prev, jnp.zeros_like(prev))
        nxt = jnp.where(c < n_chunks - 1, nxt, jnp.zeros_like(nxt))
        ext = jnp.concatenate([prev, cur, nxt], axis=0)
        xs = jnp.dot(sel_ref[...], ext, preferred_element_type=F32).reshape(LRU_NSLAB, V7X_SUBLANES, cb)
        u = cb_ref[...][None] + jnp.zeros((LRU_SEG, V7X_SUBLANES, cb), F32)
        for j in range(CONV_W):
            u = u + xs[j:j + LRU_SEG] * cw_ref[j:j + 1, :][None]
        return t0, u.reshape(LRU_TC, cb)

    def slabs(v):
        return v.reshape(LRU_SEG, V7X_SUBLANES, cb)

    def gates(u, d):
        ub = u.astype(BF16)
        pre = [jnp.dot(ub[:, LRU_SUB * s:LRU_SUB * (s + 1)], wg_ref[d, s], preferred_element_type=F32)
               for s in range(cb // LRU_SUB)]
        pre_a = jnp.concatenate([p[:, :LRU_SUB] for p in pre], axis=1)
        pre_x = jnp.concatenate([p[:, LRU_SUB:] for p in pre], axis=1)
        r = _sigmoid(pre_a + gb_ref[2 * d:2 * d + 1, :])
        i = _sigmoid(pre_x + gb_ref[2 * d + 1:2 * d + 2, :])
        log_a = (-LRU_C * jax.nn.softplus(-lam_ref[d:d + 1, :])) * r
        a = jnp.exp(log_a)
        th = jnp.tanh(log_a)
        num = -2.0 * th
        scale = jnp.where(num > 0.0, num * lax.rsqrt(num * (1.0 - th)), 0.0)
        inp = scale * (i * u)
        return a, inp

    def fwd(c, carry):
        t0, u = conv_chunk(c)
        u_scr[pl.ds(t0, LRU_TC), :] = u
        a, inp = gates(u, 0)
        h, carry = _slab_scan(slabs(a), slabs(inp), carry, reverse=False)
        hf_scr[pl.ds(t0, LRU_TC), :] = h
        return carry

    unroll = 4 if n_chunks % 4 == 0 else 1
    fin_f = lax.fori_loop(0, n_chunks, fwd, h0_ref[0, 0:1, :], unroll=unroll)

    def bwd(k, carry):
        t0 = pl.multiple_of((n_chunks - 1 - k) * LRU_TC, LRU_TC)
        a, inp = gates(u_scr[pl.ds(t0, LRU_TC), :], 1)
        h, carry = _slab_scan(slabs(a), slabs(inp), carry, reverse=True)
        g = jnp.dot(perm_ref[...], gate_ref[0, pl.ds(t0, LRU_TC), :], preferred_element_type=F32)
        y = ((hf_scr[pl.ds(t0, LRU_TC), :] + h) * jax.nn.gelu(g)).astype(BF16)
        y_ref[0, pl.ds(t0, LRU_TC), :] = jnp.dot(permt_ref[...], y, preferred_element_type=F32).astype(y_ref.dtype)
        return carry

    fin_b = lax.fori_loop(0, n_chunks, bwd, h0_ref[0, 1:2, :], unroll=unroll)
    fin_ref[0, 0:1, :] = fin_f
    fin_ref[0, 1:2, :] = fin_b


def _lru_gate_weights(lru_wa, lru_wx):
    per = LRU_SUB // LRU_BLOCK
    eye = jnp.eye(per, dtype=F32)

    def dense(w):
        w = w.reshape(2, D_RNN // LRU_SUB, per, LRU_BLOCK, LRU_BLOCK)
        full = w[:, :, :, :, None, :] * eye[None, None, :, None, :, None]
        return full.reshape(2, D_RNN // LRU_SUB, LRU_SUB, LRU_SUB)

    return jnp.concatenate([dense(lru_wa), dense(lru_wx)], axis=-1).astype(BF16)


def lru_pallas(p, h0, conv_w, conv_b, wg, gb, lam):
    b, t, _ = p.shape
    n_cb = D_RNN // LRU_CB
    maps = _lru_row_maps()
    whole = lambda m: pl.BlockSpec(m.shape, lambda bi, ci: (0, 0))
    return pl.pallas_call(
        _lru_kernel,
        grid=(b, n_cb),
        in_specs=[pl.BlockSpec((1, t, LRU_CB), lambda bi, ci: (bi, 0, C_XRNN * n_cb + ci)),
                  pl.BlockSpec((1, t, LRU_CB), lambda bi, ci: (bi, 0, C_GRNN * n_cb + ci)),
                  pl.BlockSpec((1, 2, LRU_CB), lambda bi, ci: (bi, 0, ci)),
                  pl.BlockSpec((CONV_W, LRU_CB), lambda bi, ci: (0, ci)),
                  pl.BlockSpec((1, LRU_CB), lambda bi, ci: (0, ci)),
                  pl.BlockSpec((2, LRU_CB // LRU_SUB, LRU_SUB, 2 * LRU_SUB), lambda bi, ci: (0, ci, 0, 0)),
                  pl.BlockSpec((4, LRU_CB), lambda bi, ci: (0, ci)),
                  pl.BlockSpec((2, LRU_CB), lambda bi, ci: (0, ci))] + [whole(m) for m in maps],
        out_specs=[pl.BlockSpec((1, t, LRU_CB), lambda bi, ci: (bi, 0, ci)),
                   pl.BlockSpec((1, 2, LRU_CB), lambda bi, ci: (bi, 0, ci))],
        out_shape=[jax.ShapeDtypeStruct((b, t, D_RNN), BF16), jax.ShapeDtypeStruct((b, 2, D_RNN), F32)],
        scratch_shapes=[pltpu.VMEM((t, LRU_CB), F32), pltpu.VMEM((t, LRU_CB), F32)],
        compiler_params=_params("arbitrary", "arbitrary"),
        name="lru",
    )(p, p, h0, conv_w, conv_b, wg, gb, lam, *maps)


Q_ROWS = 4
Q_TILE = Q_ROWS * GRID_W
KEY_ROWS = 12
N_Q_TILES = DEC_SEQ // Q_TILE
ATT_LANES = 1024
HEADS_PER_STEP = ATT_LANES // HEAD_DIM
PAIR = 2 * HEAD_DIM


def _window_block(t):
    return jnp.clip(t - 1, 0, N_Q_TILES - KEY_ROWS // Q_ROWS)


N_DR_PAIRS = 2 * WIN_H


def _natten_tables(rpb):
    c = np.arange(GRID_W)
    q_start = np.clip(c - WIN_W // 2, 0, GRID_W - WIN_W)
    col_valid = (c[None, :] >= q_start[:, None]) & (c[None, :] < q_start[:, None] + WIN_W)
    dc = c[None, :] - c[:, None] + WIN_W - 1
    sel_c = ((dc[None] == np.arange(2 * WIN_W - 1)[:, None, None]) & col_valid[None]).astype(np.float32)
    t = jnp.einsum('lhrd,dck->lhrck', rpb.astype(F32), jnp.asarray(sel_c), precision=lax.Precision.HIGHEST)
    t = jnp.where(col_valid[None, None, None], t, NEG_INF)
    t = jnp.pad(t, ((0, 0), (0, 0), (1, 1), (0, 0), (0, 0)), constant_values=NEG_INF)
    return jnp.concatenate([t[:, :, :-1], t[:, :, 1:]], axis=-1)


def _attend_heads(n_pairs, q_pair, key_blocks, val_blocks, bias_blocks):
    lane_head = lax.broadcasted_iota(jnp.int32, (1, PAIR), 1) // HEAD_DIM

    def scores(h):
        hp, j = divmod(h, 2)
        mine = lane_head == j
        q2 = q_pair(hp)
        qh = jnp.where(mine, q2, jnp.zeros_like(q2))
        s = []
        for kb, bb in zip(key_blocks(hp), bias_blocks(h)):
            sd = lax.dot_general(qh, kb, NT_DIMS, preferred_element_type=F32)
            s.append(sd if bb is None else sd + bb)
        m = s[0].max(axis=-1, keepdims=True)
        for sd in s[1:]:
            m = jnp.maximum(m, sd.max(axis=-1, keepdims=True))
        return hp, mine, s, m

    def weighted(hp, mine, s, m):
        o = None
        for sd, vb in zip(s, val_blocks(hp)):
            pv = jnp.dot(jnp.exp(sd - m).astype(BF16), jnp.where(mine, vb, jnp.ones_like(vb)),
                         preferred_element_type=F32)
            o = pv if o is None else o + pv
        return jnp.where(mine, o / pltpu.roll(o, HEAD_DIM, axis=1), 0.0)

    outs = [None] * n_pairs
    pending = None
    for h in range(2 * n_pairs + 1):
        upcoming = scores(h) if h < 2 * n_pairs else None
        if pending is not None:
            o = weighted(*pending)
            outs[pending[0]] = o if outs[pending[0]] is None else outs[pending[0]] + o
        pending = upcoming
    return jnp.concatenate(outs, axis=1)


def _natten_kernel(q_ref, k0_ref, k1_ref, k2_ref, v0_ref, v1_ref, v2_ref, kc_ref, vc_ref, tt_ref, o_ref):
    ti = pl.program_id(1)
    rows = DEC_SEQ // GRID_W
    wstart = _window_block(ti) * Q_ROWS
    first_row = lax.broadcasted_iota(jnp.int32, (1, 2 * GRID_W), 1) < GRID_W
    pieces = {}
    for a in range(Q_ROWS):
        r = ti * Q_ROWS + a
        start_r = jnp.clip(r - WIN_H // 2, 0, rows - WIN_H)
        for i in range(0, KEY_ROWS, 2):
            kr = wstart + i
            ok = [((kr + e >= start_r) & (kr + e < start_r + WIN_H)).astype(jnp.int32) for e in range(2)]
            pieces[a, i] = (jnp.clip(kr - r + WIN_H, 0, N_DR_PAIRS - 1), jnp.where(first_row, ok[0], ok[1]) > 0)

    def bias_block(h, d):
        return jnp.concatenate(
            [jnp.concatenate([jnp.where(pieces[a, i][1], tt_ref[0, h, pieces[a, i][0]], NEG_INF)
                              for i in range(Q_ROWS * d, Q_ROWS * (d + 1), 2)], axis=1)
             for a in range(Q_ROWS)], axis=0)

    k_refs = (k0_ref, k1_ref, k2_ref, kc_ref)
    v_refs = (v0_ref, v1_ref, v2_ref, vc_ref)
    lanes = lambda hp: slice(PAIR * hp, PAIR * (hp + 1))
    o_ref[0] = _attend_heads(
        ATT_LANES // PAIR, lambda hp: q_ref[0, :, lanes(hp)],
        lambda hp: [r[0, :, lanes(hp)] for r in k_refs], lambda hp: [r[0, :, lanes(hp)] for r in v_refs],
        lambda h: [bias_block(h, d) for d in range(KEY_ROWS // Q_ROWS)] + [None]).astype(o_ref.dtype)


def natten_pallas(p, k_ctx, v_ctx, tables, layer):
    b, t, _ = p.shape
    n_lb = D_ATT // ATT_LANES
    q_spec = pl.BlockSpec((1, Q_TILE, ATT_LANES), lambda lb, ti, bi: (bi, ti, C_Q * n_lb + lb))
    win = lambda ch, d: pl.BlockSpec((1, Q_TILE, ATT_LANES),
                                     lambda lb, ti, bi: (bi, _window_block(ti) + d, ch * n_lb + lb))
    ctx = pl.BlockSpec((1, PAST_LEN, ATT_LANES), lambda lb, ti, bi: (bi, 0, lb))
    return pl.pallas_call(
        _natten_kernel,
        grid=(n_lb, N_Q_TILES, b),
        in_specs=[q_spec, win(C_K, 0), win(C_K, 1), win(C_K, 2), win(C_V, 0), win(C_V, 1), win(C_V, 2), ctx, ctx,
                  pl.BlockSpec((1, HEADS_PER_STEP, N_DR_PAIRS, GRID_W, 2 * GRID_W),
                               lambda lb, ti, bi: (layer, lb, 0, 0, 0))],
        out_specs=pl.BlockSpec((1, Q_TILE, ATT_LANES), lambda lb, ti, bi: (bi, ti, lb)),
        out_shape=jax.ShapeDtypeStruct((b, t, D_ATT), BF16),
        compiler_params=_params("arbitrary", "arbitrary", "arbitrary"),
        name="natten",
    )(p, p, p, p, p, p, p, k_ctx, v_ctx, tables)


def _ctxatt_kernel(q_ref, k_ref, v_ref, o_ref):
    lanes = lambda hp: slice(PAIR * hp, PAIR * (hp + 1))
    o_ref[0] = _attend_heads(ATT_LANES // PAIR, lambda hp: q_ref[0, :, lanes(hp)], lambda hp: [k_ref[0, :, lanes(hp)]],
                             lambda hp: [v_ref[0, :, lanes(hp)]], lambda h: [None]).astype(o_ref.dtype)


def ctxatt_pallas(p):
    b, t, _ = p.shape
    n_lb = D_ATT // ATT_LANES
    blk = lambda ch: pl.BlockSpec((1, t, ATT_LANES), lambda lb, bi: (bi, 0, ch * n_lb + lb))
    return pl.pallas_call(
        _ctxatt_kernel,
        grid=(n_lb, b),
        in_specs=[blk(C_Q), blk(C_K), blk(C_V)],
        out_specs=pl.BlockSpec((1, t, ATT_LANES), lambda lb, bi: (bi, 0, lb)),
        out_shape=jax.ShapeDtypeStruct((b, t, D_ATT), BF16),
        compiler_params=_params("arbitrary", "arbitrary"),
        name="ctxatt",
    )(p, p, p)


def _split_bf16(x):
    hi = x.astype(BF16)
    return hi, (x - hi.astype(F32)).astype(BF16)


def _route(logits_t, rb_ref):
    score = [jax.nn.sigmoid(logits_t[e:e + 1, :]) for e in range(N_EXPERTS)]
    sel = [score[e] + rb_ref[e] for e in range(N_EXPERTS)]
    best_g = None
    for g in range(N_GROUPS):
        v = sel[EXPERTS_PER_GROUP * g:EXPERTS_PER_GROUP * (g + 1)]
        top2 = None
        for i in range(EXPERTS_PER_GROUP):
            for j in range(i + 1, EXPERTS_PER_GROUP):
                pair = v[i] + v[j]
                top2 = pair if top2 is None else jnp.maximum(top2, pair)
        if best_g is None:
            best_g, best_v = jnp.zeros_like(top2, dtype=jnp.int32), top2
        else:
            upd = top2 > best_v
            best_g = jnp.where(upd, g, best_g)
            best_v = jnp.where(upd, top2, best_v)

    def in_best(vals, j):
        out = vals[j]
        for g in range(1, N_GROUPS):
            out = jnp.where(best_g == g, vals[EXPERTS_PER_GROUP * g + j], out)
        return out

    v = [in_best(sel, j) for j in range(EXPERTS_PER_GROUP)]
    sc = [in_best(score, j) for j in range(EXPERTS_PER_GROUP)]

    def first_argmax(vals):
        idx, top = jnp.zeros_like(best_g), vals[0]
        for j in range(1, EXPERTS_PER_GROUP):
            upd = vals[j] > top
            idx = jnp.where(upd, j, idx)
            top = jnp.where(upd, vals[j], top)
        return idx

    i1 = first_argmax(v)
    i2 = first_argmax([jnp.where(i1 == j, -jnp.inf, v[j]) for j in range(EXPERTS_PER_GROUP)])
    pick = lambda idx: sum(jnp.where(idx == j, sc[j], 0.0) for j in range(EXPERTS_PER_GROUP))
    w1, w2 = pick(i1), pick(i2)
    den = w1 + w2
    c1, c2 = w1 / den, w2 / den
    rows = []
    for e in range(N_EXPERTS):
        g, j = divmod(e, EXPERTS_PER_GROUP)
        rows.append(jnp.where(best_g == g, jnp.where(i1 == j, c1, 0.0) + jnp.where(i2 == j, c2, 0.0), 0.0))
    return jnp.concatenate(rows, axis=0), best_g


AUX_LANES = V7X_LANES
AUX_MID = N_EXPERTS
AUX_LO = 2 * N_EXPERTS
AUX_GROUP = 3 * N_EXPERTS
D_MOE_IN = D_MODEL + AUX_LANES


def _merge_kernel(rb_ref, yr_ref, ya_ref, gr_ref, ga_ref, x_ref, mod_ref, g_ref, wr_ref, wa_ref, wo_ref, wrt_ref,
                  x1_ref, h2_ref, gid_ref):
    m = (_sigmoid(gr_ref[...].astype(F32)) * jnp.dot(yr_ref[...], wr_ref[0], preferred_element_type=F32)
         + _sigmoid(ga_ref[...].astype(F32)) * jnp.dot(ya_ref[...], wa_ref[0], preferred_element_type=F32))
    o = jnp.dot(m.astype(BF16), wo_ref[0], preferred_element_type=F32)
    x1 = x_ref[...] + mod_ref[0, 2:3, :] * o
    x1_ref[...] = x1
    h2 = _rms(x1, g_ref[...]) * (1.0 + mod_ref[0, 4:5, :]) + mod_ref[0, 3:4, :]
    h_hi, h_lo = _split_bf16(h2)
    w_hi, w_lo = _split_bf16(wrt_ref[...])
    dot_nt = lambda a, b: lax.dot_general(a, b, NT_DIMS, preferred_element_type=F32)
    by_h_hi = dot_nt(jnp.concatenate([w_hi, w_lo], axis=0), h_hi)
    logits_t = by_h_hi[:N_EXPERTS] + (dot_nt(w_hi, h_lo) + by_h_hi[N_EXPERTS:])
    comb_t, group = _route(logits_t, rb_ref)
    tm = comb_t.shape[1]
    group = group.astype(F32)
    gid_ref[...] = jnp.concatenate([group, jnp.zeros((V7X_SUBLANES - 1, tm), F32)], axis=0)
    padded = jnp.concatenate([comb_t, jnp.zeros((AUX_GROUP - N_EXPERTS, tm), F32), group,
                              jnp.zeros((AUX_LANES - AUX_GROUP - 1, tm), F32)], axis=0)
    c = padded.T
    c_hi = c.astype(BF16).astype(F32)
    c_mid = (c - c_hi).astype(BF16).astype(F32)
    c_lo = c - c_hi - c_mid
    aux = c_hi + pltpu.roll(c_mid, AUX_MID, axis=1) + pltpu.roll(c_lo, AUX_LO, axis=1)
    h2_ref[...] = jnp.concatenate([h2.astype(BF16), aux.astype(BF16)], axis=1)


def merge_pallas(y_rnn, y_att, p, x, mod, g, w_br_rnn, w_br_att, w_out, w_router_t, router_bias, layer, seq_len,
                 per_seq, tm):
    n = x.shape[0]
    row = lambda i: (i, 0)
    full = lambda shape: pl.BlockSpec(shape, lambda i: (0, 0))
    weight = pl.BlockSpec((1, D_MODEL, D_MODEL), lambda i: (layer, 0, 0))
    return pl.pallas_call(
        _merge_kernel,
        grid=(n // tm,),
        in_specs=[pl.BlockSpec(memory_space=pltpu.SMEM),
                  pl.BlockSpec((tm, D_RNN), row), pl.BlockSpec((tm, D_ATT), row),
                  pl.BlockSpec((tm, CHUNK), lambda i: (i, C_GATE_R)), pl.BlockSpec((tm, CHUNK), lambda i: (i, C_GATE_A)),
                  pl.BlockSpec((tm, D_MODEL), row), _mod_spec(tm, seq_len, per_seq), full((1, D_MODEL)),
                  weight, weight, weight, full((N_EXPERTS, D_MODEL))],
        out_specs=[pl.BlockSpec((tm, D_MODEL), row), pl.BlockSpec((tm, D_MOE_IN), row),
                   pl.BlockSpec((V7X_SUBLANES, tm), lambda i: (0, i))],
        out_shape=[jax.ShapeDtypeStruct((n, D_MODEL), F32), jax.ShapeDtypeStruct((n, D_MOE_IN), BF16),
                   jax.ShapeDtypeStruct((V7X_SUBLANES, n), F32)],
        compiler_params=_params("arbitrary"),
        name="merge",
    )(router_bias, y_rnn, y_att, p, p, x, mod, g, w_br_rnn, w_br_att, w_out, w_router_t)


MOE_TB = 1024
MOE_RT = 128
MOE_TBP = MOE_TB + N_GROUPS * MOE_RT


def _group_segments(gid_row):
    sub = lax.broadcasted_iota(jnp.int32, (V7X_SUBLANES, 1), 0).astype(F32)
    onehot = (gid_row == sub).astype(F32)
    cnt = jnp.sum(onehot, axis=1, keepdims=True)
    padded = jnp.floor((cnt + (MOE_RT - 1)) * (1.0 / MOE_RT)) * MOE_RT
    starts, run = [], jnp.zeros((1, 1), F32)
    for g in range(N_GROUPS):
        starts.append(run)
        run = run + padded[g:g + 1, :]
    return onehot, starts, [padded[g:g + 1, :] for g in range(N_GROUPS)], run


def _to_int(v):
    return v[0, 0].astype(jnp.int32)


def _moe_kernel(h_ref, gid_ref, x_ref, mod_ref, gf_ref, wg_ref, wu_ref, wd_ref, o_ref,
                p_scr, xs_scr, cs_scr, ys_scr, *, final_norm):
    step = pl.program_id(1)
    onehot, starts, sizes, used = _group_segments(gid_ref[0:1, :])
    lane = lax.broadcasted_iota(jnp.int32, (1, AUX_LANES), 1)

    @pl.when(step == 0)
    def _():
        t_row = lax.broadcasted_iota(jnp.int32, (MOE_TB, MOE_TB), 0)
        t_col = lax.broadcasted_iota(jnp.int32, (MOE_TB, MOE_TB), 1)
        earlier = (t_row < t_col).astype(BF16)
        rank = jnp.dot(onehot.astype(BF16), earlier, preferred_element_type=F32)
        pos = jnp.zeros((1, MOE_TB), F32)
        for g in range(N_GROUPS):
            pos = pos + onehot[g:g + 1, :] * (starts[g] + rank[g:g + 1, :])
        dest = lax.broadcasted_iota(jnp.int32, (MOE_TBP, 1), 0).astype(F32)
        p_scr[...] = (dest == pos).astype(BF16)
        for r0 in range(0, MOE_TBP, V7X_MXU_DIM):
            rows = pl.ds(r0, V7X_MXU_DIM)
            sorted_rows = jnp.dot(p_scr[rows, :], h_ref[...], preferred_element_type=F32)
            xs_scr[rows, :] = sorted_rows[:, :D_MODEL].astype(BF16)
            aux = sorted_rows[:, D_MODEL:]
            cs_scr[rows, :] = (aux + pltpu.roll(aux, AUX_LANES - AUX_MID, axis=1)
                               + pltpu.roll(aux, AUX_LANES - AUX_LO, axis=1))
        first_free = pl.multiple_of(_to_int(used), MOE_RT)

        def clear(k, carry):
            ys_scr[pl.ds(pl.multiple_of(first_free + k * MOE_RT, MOE_RT), MOE_RT), :] = jnp.zeros(
                (MOE_RT, D_MODEL), BF16)
            return carry

        lax.fori_loop(0, (MOE_TBP - first_free) // MOE_RT, clear, 0)

    start_v, size_v = starts[0], sizes[0]
    for g in range(1, N_GROUPS):
        start_v = jnp.where(step == g, starts[g], start_v)
        size_v = jnp.where(step == g, sizes[g], size_v)
    seg_start = _to_int(start_v)
    n_tiles = _to_int(size_v) // MOE_RT

    def run_expert(r0, n_rows):
        rows = pl.ds(pl.multiple_of(r0, MOE_RT), n_rows)
        x = xs_scr[rows, :]
        cs = cs_scr[rows, :]
        y = None
        for k in range(EXPERTS_PER_GROUP):
            gate = jnp.dot(x, wg_ref[0, k], preferred_element_type=F32)
            up = jnp.dot(x, wu_ref[0, k], preferred_element_type=F32)
            act = (gate * _sigmoid(gate)) * up
            yk = jnp.dot(act.astype(BF16), wd_ref[0, k], preferred_element_type=F32)
            yk = jnp.sum(jnp.where(lane == step * EXPERTS_PER_GROUP + k, cs, 0.0), axis=-1, keepdims=True) * yk
            y = yk if y is None else y + yk
        ys_scr[rows, :] = y.astype(BF16)

    def pair(k, carry):
        run_expert(seg_start + k * (2 * MOE_RT), 2 * MOE_RT)
        return carry

    lax.fori_loop(0, n_tiles // 2, pair, 0)

    @pl.when(n_tiles % 2 == 1)
    def _():
        run_expert(seg_start + (n_tiles - 1) * MOE_RT, MOE_RT)

    @pl.when(step == N_GROUPS - 1)
    def _():
        y = lax.dot_general(p_scr[...], ys_scr[...], (((0,), (0,)), ((), ())),
                            preferred_element_type=F32)
        x2 = x_ref[...] + mod_ref[0, 5:6, :] * y
        o_ref[...] = _rms(x2, gf_ref[...]) if final_norm else x2


def moe_pallas(h2x, gid, x1, mod, g_final, w_gate, w_up, w_down, layer, seq_len, per_seq, final_norm):
    n = x1.shape[0]
    row = lambda i, g: (i, 0)
    group = lambda i, g: (layer, g, 0, 0)
    return pl.pallas_call(
        functools.partial(_moe_kernel, final_norm=final_norm),
        grid=(n // MOE_TB, N_GROUPS),
        in_specs=[pl.BlockSpec((MOE_TB, D_MOE_IN), row),
                  pl.BlockSpec((V7X_SUBLANES, MOE_TB), lambda i, g: (0, i)),
                  pl.BlockSpec((MOE_TB, D_MODEL), row, pipeline_mode=pl.Buffered(1)),
                  _mod_spec(MOE_TB, seq_len, per_seq),
                  pl.BlockSpec((1, D_MODEL), lambda i, g: (0, 0)),
                  pl.BlockSpec((1, EXPERTS_PER_GROUP, D_MODEL, D_EXPERT), group),
                  pl.BlockSpec((1, EXPERTS_PER_GROUP, D_MODEL, D_EXPERT), group),
                  pl.BlockSpec((1, EXPERTS_PER_GROUP, D_EXPERT, D_MODEL), group)],
        out_specs=pl.BlockSpec((MOE_TB, D_MODEL), row),
        out_shape=jax.ShapeDtypeStruct((n, D_MODEL), F32),
        scratch_shapes=[pltpu.VMEM((MOE_TBP, MOE_TB), BF16), pltpu.VMEM((MOE_TBP, D_MODEL), BF16),
                        pltpu.VMEM((MOE_TBP, AUX_LANES), F32), pltpu.VMEM((MOE_TBP, D_MODEL), BF16)],
        compiler_params=_params("arbitrary", "arbitrary"),
        name="moe",
    )(h2x, gid, x1, mod, g_final, w_gate, w_up, w_down)


TM_PROJ = 2048
TM_MERGE = 1024


def _layer(x, mod, seq_len, per_seq, lw, h0, ctx_kv, tables, layer, caches=()):
    n = x.shape[0]
    b = n // seq_len
    emit_kv = ctx_kv is None
    outs = inproj_pallas(x, mod, lw['norm_g'][0:1], lw['w_in'], layer, seq_len, per_seq,
                         TM_PROJ // 4 if emit_kv else TM_PROJ, emit_kv, caches)
    p = outs[0]
    p3 = p.reshape(b, seq_len, D_IN)
    y_rnn, h_fin = lru_pallas(p3, h0, lw['conv_w'], lw['conv_b'], lw['wg'], lw['gb'], lw['lam'])
    if emit_kv:
        y_att = ctxatt_pallas(p3)
    else:
        y_att = natten_pallas(p3, ctx_kv[0], ctx_kv[1], tables, layer)
    x1, h2x, gid = merge_pallas(y_rnn.reshape(n, D_RNN), y_att.reshape(n, D_ATT), p, x, mod, lw['norm_g'][1:2],
                                lw['w_br_rnn'], lw['w_br_att'], lw['w_out'], lw['w_router_t'], lw['router_bias'],
                                layer, seq_len, per_seq, TM_MERGE)
    x2 = moe_pallas(h2x, gid, x1, mod, lw['g_final'], lw['w_exp_gate'], lw['w_exp_up'], lw['w_exp_down'], layer,
                    seq_len, per_seq, layer == DEPTH - 1)
    kv = (outs[1], outs[2]) if emit_kv else None
    return x2, kv, h_fin


def kernel(x_prompt, x_sample, cache_k, cache_v, state_lru, c, c_ctx, w_ada, b_ada, norm_g, w_in, conv_w,
           conv_b, lru_wa, lru_ba, lru_wx, lru_bx, lru_lam, rpb, w_br_rnn, w_br_att, w_out, w_router,
           router_bias, w_exp_gate, w_exp_up, w_exp_down, final_norm_g):
    cvecs = jnp.concatenate([c, c_ctx[None, :], jnp.zeros((V7X_SUBLANES - DEC_BATCH - 1, D_MODEL), F32)], axis=0)
    mods = adaln_pallas(cvecs, w_ada, b_ada).reshape(DEPTH, V7X_SUBLANES, N_MOD, D_MODEL)
    tables = _natten_tables(rpb)
    w_router_t = w_router.T
    xp = x_prompt.reshape(BATCH * SEQ, D_MODEL)
    xs = x_sample.reshape(DEC_BATCH * DEC_SEQ, D_MODEL)
    zeros_h0 = jnp.zeros((BATCH, 2, D_RNN), F32)
    w_exp = [w.astype(BF16) for w in (w_exp_gate, w_exp_up, w_exp_down)]
    w_proj = [w.astype(BF16) for w in (w_in, w_br_rnn, w_br_att, w_out)]
    caches, hs = (), []
    for l in range(DEPTH):
        lw = dict(
            norm_g=norm_g[l], w_in=w_proj[0], conv_w=conv_w[l], conv_b=conv_b[l][None, :],
            wg=_lru_gate_weights(lru_wa[l], lru_wx[l]),
            gb=jnp.stack([lru_ba[l, 0], lru_bx[l, 0], lru_ba[l, 1], lru_bx[l, 1]], axis=0), lam=lru_lam[l],
            w_br_rnn=w_proj[1], w_br_att=w_proj[2], w_out=w_proj[3],
            w_router_t=w_router_t, router_bias=router_bias, g_final=final_norm_g[None, :],
            w_exp_gate=w_exp[0], w_exp_up=w_exp[1], w_exp_down=w_exp[2])
        xp, caches, h_l = _layer(xp, mods[l, DEC_BATCH:DEC_BATCH + 1], SEQ, False, lw, zeros_h0, None, None, l,
                                 caches)
        hs.append(h_l)
        ctx_kv = (cache_k[:, l].reshape(DEC_BATCH, PAST_LEN, D_ATT).astype(BF16),
                  cache_v[:, l].reshape(DEC_BATCH, PAST_LEN, D_ATT).astype(BF16))
        xs, _, _ = _layer(xs, mods[l, :DEC_BATCH], DEC_SEQ, True, lw, state_lru[:, l], ctx_kv, tables, l)
    y_prompt = xp.reshape(BATCH, SEQ, D_MODEL)
    y_sample = xs.reshape(DEC_BATCH, DEC_SEQ, D_MODEL)
    return (y_prompt, y_sample, caches[0], caches[1], jnp.stack(hs, axis=1))
```

```python
import functools

import jax
import jax.numpy as jnp
import numpy as np
from jax import lax
from jax.experimental import pallas as pl
from jax.experimental.pallas import tpu as pltpu

D_MODEL = 1024
BATCH = 16
SEQ = 256
DEPTH = 2
DEC_BATCH = 4
DEC_SEQ = 4096
PAST_LEN = 256

GRID_W = 64
D_RNN = 1024
N_LRU_BLOCKS = 16
LRU_BLOCK = D_RNN // N_LRU_BLOCKS
CONV_W = 4
LRU_C = 8.0
N_HEADS = 16
HEAD_DIM = 64
D_ATT = N_HEADS * HEAD_DIM
WIN_H = 8
WIN_W = 16
N_EXPERTS = 16
N_GROUPS = 4
EXPERTS_PER_GROUP = N_EXPERTS // N_GROUPS
D_EXPERT = 512
N_MOD = 6
D_IN = 2 * D_RNN + 3 * D_ATT + 2 * D_MODEL
EPS = 1e-6
NEG_INF = -1e30

BF16 = jnp.bfloat16
F32 = jnp.float32

V7X_LANES = 128
V7X_SUBLANES = 8
V7X_MXU_DIM = 256
V7X_VMEM_BYTES = 64 * 1024 * 1024
VMEM_LIMIT = V7X_VMEM_BYTES - 8 * 1024 * 1024

CHUNK = D_MODEL
N_CHUNKS = D_IN // CHUNK
C_XRNN, C_GRNN, C_Q, C_K, C_V, C_GATE_R, C_GATE_A = range(N_CHUNKS)

NT_DIMS = (((1,), (1,)), ((), ()))


def _params(*sem):
    return pltpu.CompilerParams(dimension_semantics=sem, vmem_limit_bytes=VMEM_LIMIT)


def _adaln_kernel(c_ref, w_ref, b_ref, o_ref):
    cv = c_ref[...]
    s = cv * jax.nn.sigmoid(cv)
    o_ref[0] = jnp.dot(s.astype(BF16), w_ref[0].astype(BF16), preferred_element_type=F32) + b_ref[0]


def adaln_pallas(cvecs, w_ada, b_ada):
    r = cvecs.shape[0]
    return pl.pallas_call(
        _adaln_kernel,
        grid=(DEPTH, N_MOD),
        in_specs=[pl.BlockSpec((r, D_MODEL), lambda l, j: (0, 0)),
                  pl.BlockSpec((1, D_MODEL, D_MODEL), lambda l, j: (l, 0, j)),
                  pl.BlockSpec((1, 1, D_MODEL), lambda l, j: (l, 0, j))],
        out_specs=pl.BlockSpec((1, r, D_MODEL), lambda l, j: (l, 0, j)),
        out_shape=jax.ShapeDtypeStruct((DEPTH, r, N_MOD * D_MODEL), F32),
        compiler_params=_params("arbitrary", "arbitrary"),
        name="adaln",
    )(cvecs, w_ada, b_ada.reshape(DEPTH, 1, N_MOD * D_MODEL))


def _mod_spec(tm, seq_len, per_seq):
    if per_seq:
        return pl.BlockSpec((1, N_MOD, D_MODEL), lambda i, *_: (i * tm // seq_len, 0, 0))
    return pl.BlockSpec((1, N_MOD, D_MODEL), lambda i, *_: (0, 0, 0))


def _rms(x, g):
    return x * lax.rsqrt(jnp.mean(x * x, axis=-1, keepdims=True) + EPS) * g


def _inproj_kernel(x_ref, mod_ref, g_ref, w_ref, *refs, emit_kv):
    if emit_kv:
        p_ref, k32_ref, v32_ref, h_scr = refs[-4:]
    else:
        p_ref, h_scr = refs
    j = pl.program_id(1)

    @pl.when(j == 0)
    def _():
        y = _rms(x_ref[...], g_ref[...])
        h_scr[...] = (y * (1.0 + mod_ref[0, 1:2, :]) + mod_ref[0, 0:1, :]).astype(BF16)

    acc = jnp.dot(h_scr[...], w_ref[0], preferred_element_type=F32)
    p_ref[...] = (acc * jnp.where(j == C_Q, HEAD_DIM ** -0.5, 1.0)).astype(BF16)
    if emit_kv:
        @pl.when(j == C_K)
        def _():
            k32_ref[...] = acc.reshape(k32_ref.shape)

        @pl.when(j == C_V)
        def _():
            v32_ref[...] = acc.reshape(v32_ref.shape)


def inproj_pallas(x, mod, g, w_in, layer, seq_len, per_seq, tm, emit_kv, caches=()):
    n = x.shape[0]
    row = lambda i, j: (i, 0)
    in_specs = [pl.BlockSpec((tm, D_MODEL), row), _mod_spec(tm, seq_len, per_seq),
                pl.BlockSpec((1, D_MODEL), lambda i, j: (0, 0)),
                pl.BlockSpec((1, D_MODEL, CHUNK), lambda i, j: (layer, 0, j))]
    out_shape = [jax.ShapeDtypeStruct((n, D_IN), BF16)]
    out_specs = [pl.BlockSpec((tm, CHUNK), lambda i, j: (i, j))]
    aliases = {}
    if emit_kv:
        out_shape += [jax.ShapeDtypeStruct((n // seq_len, DEPTH, seq_len, N_HEADS, HEAD_DIM), F32)] * 2
        out_specs += [pl.BlockSpec((tm // seq_len, 1, seq_len, N_HEADS, HEAD_DIM),
                                   lambda i, j: (i, layer, 0, 0, 0))] * 2
        aliases = {len(in_specs) + k: 1 + k for k in range(len(caches))}
        in_specs += [pl.BlockSpec(memory_space=pl.ANY)] * len(caches)
    return pl.pallas_call(
        functools.partial(_inproj_kernel, emit_kv=emit_kv),
        grid=(n // tm, N_CHUNKS),
        in_specs=in_specs,
        out_specs=out_specs,
        out_shape=out_shape,
        input_output_aliases=aliases,
        scratch_shapes=[pltpu.VMEM((tm, D_MODEL), BF16)],
        compiler_params=_params("arbitrary", "arbitrary"),
        name="inproj",
    )(x, mod, g, w_in, *caches)


LRU_CB = 512
LRU_TC = 256
LRU_SUB = V7X_MXU_DIM
LRU_HALO = 16


LRU_SEG = LRU_TC // V7X_SUBLANES
LRU_NSLAB = LRU_SEG + CONV_W


def _lru_row_maps():
    s = np.arange(V7X_SUBLANES)[None, :]
    src = (LRU_HALO - CONV_W // 2) + LRU_SEG * s + np.arange(LRU_NSLAB)[:, None]
    sel = np.zeros((LRU_NSLAB * V7X_SUBLANES, LRU_TC + 2 * LRU_HALO), np.float32)
    sel[np.arange(sel.shape[0]), src.reshape(-1)] = 1.0
    tok = (LRU_SEG * s + np.arange(LRU_SEG)[:, None]).reshape(-1)
    perm = np.zeros((LRU_TC, LRU_TC), np.float32)
    perm[np.arange(LRU_TC), tok] = 1.0
    return jnp.asarray(sel, BF16), jnp.asarray(perm, BF16), jnp.asarray(perm.T, BF16)


def _slab_scan(a3, b3, carry, reverse):
    n = a3.shape[0]
    hs, cum = [None] * n, [None] * n
    h = cp = None
    for t in (range(n - 1, -1, -1) if reverse else range(n)):
        h = b3[t] if h is None else a3[t] * h + b3[t]
        cp = a3[t] if cp is None else cp * a3[t]
        hs[t], cum[t] = h, cp
    sub = lax.broadcasted_iota(jnp.int32, (V7X_SUBLANES, 1), 0)
    pa, pb = cp, h
    s = 1
    while s < V7X_SUBLANES:
        shift = V7X_SUBLANES - s if reverse else s
        ok = (sub < V7X_SUBLANES - s) if reverse else (sub >= s)
        a_sh = pltpu.roll(pa, shift, axis=0)
        b_sh = pltpu.roll(pb, shift, axis=0)
        pb = jnp.where(ok, pa * b_sh + pb, pb)
        pa = jnp.where(ok, pa * a_sh, pa)
        s *= 2
    leaving = pb + pa * carry
    first, last = (V7X_SUBLANES - 1, 0) if reverse else (0, V7X_SUBLANES - 1)
    entering = jnp.where(sub == first, carry, pltpu.roll(leaving, V7X_SUBLANES - 1 if reverse else 1, axis=0))
    out = jnp.concatenate([hs[t] + cum[t] * entering for t in range(n)], axis=0)
    return out, leaving[last:last + 1, :]


def _sigmoid(x):
    return 0.5 * jnp.tanh(0.5 * x) + 0.5


def _lru_kernel(x_ref, gate_ref, h0_ref, cw_ref, cb_ref, wg_ref, gb_ref, lam_ref, sel_ref, perm_ref, permt_ref,
                y_ref, fin_ref, hf_scr, u_scr):
    t_len = x_ref.shape[1]
    cb = x_ref.shape[2]
    n_chunks = t_len // LRU_TC

    def conv_chunk(c):
        t0 = pl.multiple_of(c * LRU_TC, LRU_TC)
        cur = x_ref[0, pl.ds(t0, LRU_TC), :]
        lo = pl.multiple_of(jnp.maximum(t0 - LRU_HALO, 0), LRU_HALO)
        hi = pl.multiple_of(jnp.minimum(t0 + LRU_TC, t_len - LRU_HALO), LRU_HALO)
        prev = x_ref[0, pl.ds(lo, LRU_HALO), :]
        nxt = x_ref[0, pl.ds(hi, LRU_HALO), :]
        prev = jnp.where(c > 0, prev, jnp.zeros_like(prev))
        nxt = jnp.where(c < n_chunks - 1, nxt, jnp.zeros_like(nxt))
        ext = jnp.concatenate([prev, cur, nxt], axis=0)
        xs = jnp.dot(sel_ref[...], ext, preferred_element_type=F32).reshape(LRU_NSLAB, V7X_SUBLANES, cb)
        u = cb_ref[...][None] + jnp.zeros((LRU_SEG, V7X_SUBLANES, cb), F32)
        for j in range(CONV_W):
            u = u + xs[j:j + LRU_SEG] * cw_ref[j:j + 1, :][None]
        return t0, u.reshape(LRU_TC, cb)

    def slabs(v):
        return v.reshape(LRU_SEG, V7X_SUBLANES, cb)

    def gates(u, d):
        ub = u.astype(BF16)
        pre = [jnp.dot(ub[:, LRU_SUB * s:LRU_SUB * (s + 1)], wg_ref[d, s], preferred_element_type=F32)
               for s in range(cb // LRU_SUB)]
        pre_a = jnp.concatenate([p[:, :LRU_SUB] for p in pre], axis=1)
        pre_x = jnp.concatenate([p[:, LRU_SUB:] for p in pre], axis=1)
        r = _sigmoid(pre_a + gb_ref[2 * d:2 * d + 1, :])
        i = _sigmoid(pre_x + gb_ref[2 * d + 1:2 * d + 2, :])
        log_a = (-LRU_C * jax.nn.softplus(-lam_ref[d:d + 1, :])) * r
        a = jnp.exp(log_a)
        th = jnp.tanh(log_a)
        num = -2.0 * th
        scale = jnp.where(num > 0.0, num * lax.rsqrt(num * (1.0 - th)), 0.0)
        inp = scale * (i * u)
        return a, inp

    def fwd(c, carry):
        t0, u = conv_chunk(c)
        u_scr[pl.ds(t0, LRU_TC), :] = u
        a, inp = gates(u, 0)
        h, carry = _slab_scan(slabs(a), slabs(inp), carry, reverse=False)
        hf_scr[pl.ds(t0, LRU_TC), :] = h
        return carry

    unroll = 4 if n_chunks % 4 == 0 else 1
    fin_f = lax.fori_loop(0, n_chunks, fwd, h0_ref[0, 0:1, :], unroll=unroll)

    def bwd(k, carry):
        t0 = pl.multiple_of((n_chunks - 1 - k) * LRU_TC, LRU_TC)
        a, inp = gates(u_scr[pl.ds(t0, LRU_TC), :], 1)
        h, carry = _slab_scan(slabs(a), slabs(inp), carry, reverse=True)
        g = jnp.dot(perm_ref[...], gate_ref[0, pl.ds(t0, LRU_TC), :], preferred_element_type=F32)
        y = ((hf_scr[pl.ds(t0, LRU_TC), :] + h) * jax.nn.gelu(g)).astype(BF16)
        y_ref[0, pl.ds(t0, LRU_TC), :] = jnp.dot(permt_ref[...], y, preferred_element_type=F32).astype(y_ref.dtype)
        return carry

    fin_b = lax.fori_loop(0, n_chunks, bwd, h0_ref[0, 1:2, :], unroll=unroll)
    fin_ref[0, 0:1, :] = fin_f
    fin_ref[0, 1:2, :] = fin_b


def _lru_gate_weights(lru_wa, lru_wx):
    per = LRU_SUB // LRU_BLOCK
    eye = jnp.eye(per, dtype=F32)

    def dense(w):
        w = w.reshape(2, D_RNN // LRU_SUB, per, LRU_BLOCK, LRU_BLOCK)
        full = w[:, :, :, :, None, :] * eye[None, None, :, None, :, None]
        return full.reshape(2, D_RNN // LRU_SUB, LRU_SUB, LRU_SUB)

    return jnp.concatenate([dense(lru_wa), dense(lru_wx)], axis=-1).astype(BF16)


def lru_pallas(p, h0, conv_w, conv_b, wg, gb, lam):
    b, t, _ = p.shape
    n_cb = D_RNN // LRU_CB
    maps = _lru_row_maps()
    whole = lambda m: pl.BlockSpec(m.shape, lambda bi, ci: (0, 0))
    return pl.pallas_call(
        _lru_kernel,
        grid=(b, n_cb),
        in_specs=[pl.BlockSpec((1, t, LRU_CB), lambda bi, ci: (bi, 0, C_XRNN * n_cb + ci)),
                  pl.BlockSpec((1, t, LRU_CB), lambda bi, ci: (bi, 0, C_GRNN * n_cb + ci)),
                  pl.BlockSpec((1, 2, LRU_CB), lambda bi, ci: (bi, 0, ci)),
                  pl.BlockSpec((CONV_W, LRU_CB), lambda bi, ci: (0, ci)),
                  pl.BlockSpec((1, LRU_CB), lambda bi, ci: (0, ci)),
                  pl.BlockSpec((2, LRU_CB // LRU_SUB, LRU_SUB, 2 * LRU_SUB), lambda bi, ci: (0, ci, 0, 0)),
                  pl.BlockSpec((4, LRU_CB), lambda bi, ci: (0, ci)),
                  pl.BlockSpec((2, LRU_CB), lambda bi, ci: (0, ci))] + [whole(m) for m in maps],
        out_specs=[pl.BlockSpec((1, t, LRU_CB), lambda bi, ci: (bi, 0, ci)),
                   pl.BlockSpec((1, 2, LRU_CB), lambda bi, ci: (bi, 0, ci))],
        out_shape=[jax.ShapeDtypeStruct((b, t, D_RNN), BF16), jax.ShapeDtypeStruct((b, 2, D_RNN), F32)],
        scratch_shapes=[pltpu.VMEM((t, LRU_CB), F32), pltpu.VMEM((t, LRU_CB), F32)],
        compiler_params=_params("arbitrary", "arbitrary"),
        name="lru",
    )(p, p, h0, conv_w, conv_b, wg, gb, lam, *maps)


Q_ROWS = 4
Q_TILE = Q_ROWS * GRID_W
KEY_ROWS = 12
N_Q_TILES = DEC_SEQ // Q_TILE
ATT_LANES = 1024
HEADS_PER_STEP = ATT_LANES // HEAD_DIM
PAIR = 2 * HEAD_DIM


def _window_block(t):
    return jnp.clip(t - 1, 0, N_Q_TILES - KEY_ROWS // Q_ROWS)


N_DR_PAIRS = 2 * WIN_H


def _natten_tables(rpb):
    c = np.arange(GRID_W)
    q_start = np.clip(c - WIN_W // 2, 0, GRID_W - WIN_W)
    col_valid = (c[None, :] >= q_start[:, None]) & (c[None, :] < q_start[:, None] + WIN_W)
    dc = c[None, :] - c[:, None] + WIN_W - 1
    sel_c = ((dc[None] == np.arange(2 * WIN_W - 1)[:, None, None]) & col_valid[None]).astype(np.float32)
    t = jnp.einsum('lhrd,dck->lhrck', rpb.astype(F32), jnp.asarray(sel_c), precision=lax.Precision.HIGHEST)
    t = jnp.where(col_valid[None, None, None], t, NEG_INF)
    t = jnp.pad(t, ((0, 0), (0, 0), (1, 1), (0, 0), (0, 0)), constant_values=NEG_INF)
    return jnp.concatenate([t[:, :, :-1], t[:, :, 1:]], axis=-1)


def _attend_heads(n_pairs, q_pair, key_blocks, val_blocks, bias_blocks):
    lane_head = lax.broadcasted_iota(jnp.int32, (1, PAIR), 1) // HEAD_DIM

    def scores(h):
        hp, j = divmod(h, 2)
        mine = lane_head == j
        q2 = q_pair(hp)
        qh = jnp.where(mine, q2, jnp.zeros_like(q2))
        s = []
        for kb, bb in zip(key_blocks(hp), bias_blocks(h)):
            sd = lax.dot_general(qh, kb, NT_DIMS, preferred_element_type=F32)
            s.append(sd if bb is None else sd + bb)
        m = s[0].max(axis=-1, keepdims=True)
        for sd in s[1:]:
            m = jnp.maximum(m, sd.max(axis=-1, keepdims=True))
        return hp, mine, s, m

    def weighted(hp, mine, s, m):
        o = None
        for sd, vb in zip(s, val_blocks(hp)):
            pv = jnp.dot(jnp.exp(sd - m).astype(BF16), jnp.where(mine, vb, jnp.ones_like(vb)),
                         preferred_element_type=F32)
            o = pv if o is None else o + pv
        return jnp.where(mine, o / pltpu.roll(o, HEAD_DIM, axis=1), 0.0)

    outs = [None] * n_pairs
    pending = None
    for h in range(2 * n_pairs + 1):
        upcoming = scores(h) if h < 2 * n_pairs else None
        if pending is not None:
            o = weighted(*pending)
            outs[pending[0]] = o if outs[pending[0]] is None else outs[pending[0]] + o
        pending = upcoming
    return jnp.concatenate(outs, axis=1)


def _natten_kernel(q_ref, k0_ref, k1_ref, k2_ref, v0_ref, v1_ref, v2_ref, kc_ref, vc_ref, tt_ref, o_ref):
    ti = pl.program_id(1)
    rows = DEC_SEQ // GRID_W
    wstart = _window_block(ti) * Q_ROWS
    first_row = lax.broadcasted_iota(jnp.int32, (1, 2 * GRID_W), 1) < GRID_W
    pieces = {}
    for a in range(Q_ROWS):
        r = ti * Q_ROWS + a
        start_r = jnp.clip(r - WIN_H // 2, 0, rows - WIN_H)
        for i in range(0, KEY_ROWS, 2):
            kr = wstart + i
            ok = [((kr + e >= start_r) & (kr + e < start_r + WIN_H)).astype(jnp.int32) for e in range(2)]
            pieces[a, i] = (jnp.clip(kr - r + WIN_H, 0, N_DR_PAIRS - 1), jnp.where(first_row, ok[0], ok[1]) > 0)

    def bias_block(h, d):
        return jnp.concatenate(
            [jnp.concatenate([jnp.where(pieces[a, i][1], tt_ref[0, h, pieces[a, i][0]], NEG_INF)
                              for i in range(Q_ROWS * d, Q_ROWS * (d + 1), 2)], axis=1)
             for a in range(Q_ROWS)], axis=0)

    k_refs = (k0_ref, k1_ref, k2_ref, kc_ref)
    v_refs = (v0_ref, v1_ref, v2_ref, vc_ref)
    lanes = lambda hp: slice(PAIR * hp, PAIR * (hp + 1))
    o_ref[0] = _attend_heads(
        ATT_LANES // PAIR, lambda hp: q_ref[0, :, lanes(hp)],
        lambda hp: [r[0, :, lanes(hp)] for r in k_refs], lambda hp: [r[0, :, lanes(hp)] for r in v_refs],
        lambda h: [bias_block(h, d) for d in range(KEY_ROWS // Q_ROWS)] + [None]).astype(o_ref.dtype)


def natten_pallas(p, k_ctx, v_ctx, tables, layer):
    b, t, _ = p.shape
    n_lb = D_ATT // ATT_LANES
    q_spec = pl.BlockSpec((1, Q_TILE, ATT_LANES), lambda lb, ti, bi: (bi, ti, C_Q * n_lb + lb))
    win = lambda ch, d: pl.BlockSpec((1, Q_TILE, ATT_LANES),
                                     lambda lb, ti, bi: (bi, _window_block(ti) + d, ch * n_lb + lb))
    ctx = pl.BlockSpec((1, PAST_LEN, ATT_LANES), lambda lb, ti, bi: (bi, 0, lb))
    return pl.pallas_call(
        _natten_kernel,
        grid=(n_lb, N_Q_TILES, b),
        in_specs=[q_spec, win(C_K, 0), win(C_K, 1), win(C_K, 2), win(C_V, 0), win(C_V, 1), win(C_V, 2), ctx, ctx,
                  pl.BlockSpec((1, HEADS_PER_STEP, N_DR_PAIRS, GRID_W, 2 * GRID_W),
                               lambda lb, ti, bi: (layer, lb, 0, 0, 0))],
        out_specs=pl.BlockSpec((1, Q_TILE, ATT_LANES), lambda lb, ti, bi: (bi, ti, lb)),
        out_shape=jax.ShapeDtypeStruct((b, t, D_ATT), BF16),
        compiler_params=_params("arbitrary", "arbitrary", "arbitrary"),
        name="natten",
    )(p, p, p, p, p, p, p, k_ctx, v_ctx, tables)


def _ctxatt_kernel(q_ref, k_ref, v_ref, o_ref):
    lanes = lambda hp: slice(PAIR * hp, PAIR * (hp + 1))
    o_ref[0] = _attend_heads(ATT_LANES // PAIR, lambda hp: q_ref[0, :, lanes(hp)], lambda hp: [k_ref[0, :, lanes(hp)]],
                             lambda hp: [v_ref[0, :, lanes(hp)]], lambda h: [None]).astype(o_ref.dtype)


def ctxatt_pallas(p):
    b, t, _ = p.shape
    n_lb = D_ATT // ATT_LANES
    blk = lambda ch: pl.BlockSpec((1, t, ATT_LANES), lambda lb, bi: (bi, 0, ch * n_lb + lb))
    return pl.pallas_call(
        _ctxatt_kernel,
        grid=(n_lb, b),
        in_specs=[blk(C_Q), blk(C_K), blk(C_V)],
        out_specs=pl.BlockSpec((1, t, ATT_LANES), lambda lb, bi: (bi, 0, lb)),
        out_shape=jax.ShapeDtypeStruct((b, t, D_ATT), BF16),
        compiler_params=_params("arbitrary", "arbitrary"),
        name="ctxatt",
    )(p, p, p)


def _split_bf16(x):
    hi = x.astype(BF16)
    return hi, (x - hi.astype(F32)).astype(BF16)


def _route(logits_t, rb_ref):
    score = [jax.nn.sigmoid(logits_t[e:e + 1, :]) for e in range(N_EXPERTS)]
    sel = [score[e] + rb_ref[e] for e in range(N_EXPERTS)]
    best_g = None
    for g in range(N_GROUPS):
        v = sel[EXPERTS_PER_GROUP * g:EXPERTS_PER_GROUP * (g + 1)]
        top2 = None
        for i in range(EXPERTS_PER_GROUP):
            for j in range(i + 1, EXPERTS_PER_GROUP):
                pair = v[i] + v[j]
                top2 = pair if top2 is None else jnp.maximum(top2, pair)
        if best_g is None:
            best_g, best_v = jnp.zeros_like(top2, dtype=jnp.int32), top2
        else:
            upd = top2 > best_v
            best_g = jnp.where(upd, g, best_g)
            best_v = jnp.where(upd, top2, best_v)

    def in_best(vals, j):
        out = vals[j]
        for g in range(1, N_GROUPS):
            out = jnp.where(best_g == g, vals[EXPERTS_PER_GROUP * g + j], out)
        return out

    v = [in_best(sel, j) for j in range(EXPERTS_PER_GROUP)]
    sc = [in_best(score, j) for j in range(EXPERTS_PER_GROUP)]

    def first_argmax(vals):
        idx, top = jnp.zeros_like(best_g), vals[0]
        for j in range(1, EXPERTS_PER_GROUP):
            upd = vals[j] > top
            idx = jnp.where(upd, j, idx)
            top = jnp.where(upd, vals[j], top)
        return idx

    i1 = first_argmax(v)
    i2 = first_argmax([jnp.where(i1 == j, -jnp.inf, v[j]) for j in range(EXPERTS_PER_GROUP)])
    pick = lambda idx: sum(jnp.where(idx == j, sc[j], 0.0) for j in range(EXPERTS_PER_GROUP))
    w1, w2 = pick(i1), pick(i2)
    den = w1 + w2
    c1, c2 = w1 / den, w2 / den
    rows = []
    for e in range(N_EXPERTS):
        g, j = divmod(e, EXPERTS_PER_GROUP)
        rows.append(jnp.where(best_g == g, jnp.where(i1 == j, c1, 0.0) + jnp.where(i2 == j, c2, 0.0), 0.0))
    return jnp.concatenate(rows, axis=0), best_g


AUX_LANES = V7X_LANES
AUX_MID = N_EXPERTS
AUX_LO = 2 * N_EXPERTS
AUX_GROUP = 3 * N_EXPERTS
D_MOE_IN = D_MODEL + AUX_LANES
MERGE_CHAINS = 2


def _merge_kernel(rb_ref, yr_ref, ya_ref, gr_ref, ga_ref, x_ref, mod_ref, g_ref, wr_ref, wa_ref, wo_ref, wrt_ref,
                  x1_ref, h2_ref, gid_ref):
    w_hi, w_lo = _split_bf16(wrt_ref[...])
    w_parts = jnp.concatenate([w_hi, w_lo], axis=0)
    dot_nt = lambda a, b: lax.dot_general(a, b, NT_DIMS, preferred_element_type=F32)
    n_rows = x_ref.shape[0] // MERGE_CHAINS

    def project(ci):
        rows = slice(ci * n_rows, (ci + 1) * n_rows)
        m = (_sigmoid(gr_ref[rows, :].astype(F32)) * jnp.dot(yr_ref[rows, :], wr_ref[0], preferred_element_type=F32)
             + _sigmoid(ga_ref[rows, :].astype(F32)) * jnp.dot(ya_ref[rows, :], wa_ref[0],
                                                              preferred_element_type=F32))
        o = jnp.dot(m.astype(BF16), wo_ref[0], preferred_element_type=F32)
        x1 = x_ref[rows, :] + mod_ref[0, 2:3, :] * o
        x1_ref[rows, :] = x1
        return rows, _rms(x1, g_ref[...]) * (1.0 + mod_ref[0, 4:5, :]) + mod_ref[0, 3:4, :]

    def route(rows, h2):
        h_hi, h_lo = _split_bf16(h2)
        by_h_hi = dot_nt(w_parts, h_hi)
        logits_t = by_h_hi[:N_EXPERTS] + (dot_nt(w_hi, h_lo) + by_h_hi[N_EXPERTS:])
        comb_t, group = _route(logits_t, rb_ref)
        group = group.astype(F32)
        gid_ref[:, rows] = jnp.concatenate([group, jnp.zeros((V7X_SUBLANES - 1, n_rows), F32)], axis=0)
        padded = jnp.concatenate([comb_t, jnp.zeros((AUX_GROUP - N_EXPERTS, n_rows), F32), group,
                                  jnp.zeros((AUX_LANES - AUX_GROUP - 1, n_rows), F32)], axis=0)
        c = padded.T
        c_hi = c.astype(BF16).astype(F32)
        c_mid = (c - c_hi).astype(BF16).astype(F32)
        c_lo = c - c_hi - c_mid
        aux = c_hi + pltpu.roll(c_mid, AUX_MID, axis=1) + pltpu.roll(c_lo, AUX_LO, axis=1)
        h2_ref[rows, :] = jnp.concatenate([h2.astype(BF16), aux.astype(BF16)], axis=1)

    pending = None
    for ci in range(MERGE_CHAINS + 1):
        upcoming = project(ci) if ci < MERGE_CHAINS else None
        if pending is not None:
            route(*pending)
        pending = upcoming


def merge_pallas(y_rnn, y_att, p, x, mod, g, w_br_rnn, w_br_att, w_out, w_router_t, router_bias, layer, seq_len,
                 per_seq, tm):
    n = x.shape[0]
    row = lambda i: (i, 0)
    full = lambda shape: pl.BlockSpec(shape, lambda i: (0, 0))
    weight = pl.BlockSpec((1, D_MODEL, D_MODEL), lambda i: (layer, 0, 0))
    return pl.pallas_call(
        _merge_kernel,
        grid=(n // tm,),
        in_specs=[pl.BlockSpec(memory_space=pltpu.SMEM),
                  pl.BlockSpec((tm, D_RNN), row), pl.BlockSpec((tm, D_ATT), row),
                  pl.BlockSpec((tm, CHUNK), lambda i: (i, C_GATE_R)), pl.BlockSpec((tm, CHUNK), lambda i: (i, C_GATE_A)),
                  pl.BlockSpec((tm, D_MODEL), row), _mod_spec(tm, seq_len, per_seq), full((1, D_MODEL)),
                  weight, weight, weight, full((N_EXPERTS, D_MODEL))],
        out_specs=[pl.BlockSpec((tm, D_MODEL), row), pl.BlockSpec((tm, D_MOE_IN), row),
                   pl.BlockSpec((V7X_SUBLANES, tm), lambda i: (0, i))],
        out_shape=[jax.ShapeDtypeStruct((n, D_MODEL), F32), jax.ShapeDtypeStruct((n, D_MOE_IN), BF16),
                   jax.ShapeDtypeStruct((V7X_SUBLANES, n), F32)],
        compiler_params=_params("arbitrary"),
        name="merge",
    )(router_bias, y_rnn, y_att, p, p, x, mod, g, w_br_rnn, w_br_att, w_out, w_router_t)


MOE_TB = 1024
MOE_RT = 128
MOE_TBP = MOE_TB + N_GROUPS * MOE_RT


def _group_segments(gid_row):
    sub = lax.broadcasted_iota(jnp.int32, (V7X_SUBLANES, 1), 0).astype(F32)
    onehot = (gid_row == sub).astype(F32)
    cnt = jnp.sum(onehot, axis=1, keepdims=True)
    padded = jnp.floor((cnt + (MOE_RT - 1)) * (1.0 / MOE_RT)) * MOE_RT
    starts, run = [], jnp.zeros((1, 1), F32)
    for g in range(N_GROUPS):
        starts.append(run)
        run = run + padded[g:g + 1, :]
    return onehot, starts, [padded[g:g + 1, :] for g in range(N_GROUPS)], run


def _to_int(v):
    return v[0, 0].astype(jnp.int32)


def _moe_kernel(h_ref, gid_ref, x_ref, mod_ref, gf_ref, wg_ref, wu_ref, wd_ref, o_ref,
                p_scr, xs_scr, cs_scr, ys_scr, *, final_norm):
    step = pl.program_id(1)
    onehot, starts, sizes, used = _group_segments(gid_ref[0:1, :])
    lane = lax.broadcasted_iota(jnp.int32, (1, AUX_LANES), 1)

    @pl.when(step == 0)
    def _():
        t_row = lax.broadcasted_iota(jnp.int32, (MOE_TB, MOE_TB), 0)
        t_col = lax.broadcasted_iota(jnp.int32, (MOE_TB, MOE_TB), 1)
        earlier = (t_row < t_col).astype(BF16)
        rank = jnp.dot(onehot.astype(BF16), earlier, preferred_element_type=F32)
        pos = jnp.zeros((1, MOE_TB), F32)
        for g in range(N_GROUPS):
            pos = pos + onehot[g:g + 1, :] * (starts[g] + rank[g:g + 1, :])
        dest = lax.broadcasted_iota(jnp.int32, (MOE_TBP, 1), 0).astype(F32)
        p_scr[...] = (dest == pos).astype(BF16)
        for r0 in range(0, MOE_TBP, V7X_MXU_DIM):
            rows = pl.ds(r0, V7X_MXU_DIM)
            sorted_rows = jnp.dot(p_scr[rows, :], h_ref[...], preferred_element_type=F32)
            xs_scr[rows, :] = sorted_rows[:, :D_MODEL].astype(BF16)
            aux = sorted_rows[:, D_MODEL:]
            cs_scr[rows, :] = (aux + pltpu.roll(aux, AUX_LANES - AUX_MID, axis=1)
                               + pltpu.roll(aux, AUX_LANES - AUX_LO, axis=1))
        first_free = pl.multiple_of(_to_int(used), MOE_RT)

        def clear(k, carry):
            ys_scr[pl.ds(pl.multiple_of(first_free + k * MOE_RT, MOE_RT), MOE_RT), :] = jnp.zeros(
                (MOE_RT, D_MODEL), BF16)
            return carry

        lax.fori_loop(0, (MOE_TBP - first_free) // MOE_RT, clear, 0)

    start_v, size_v = starts[0], sizes[0]
    for g in range(1, N_GROUPS):
        start_v = jnp.where(step == g, starts[g], start_v)
        size_v = jnp.where(step == g, sizes[g], size_v)
    seg_start = _to_int(start_v)
    n_tiles = _to_int(size_v) // MOE_RT

    def run_expert(r0, n_rows):
        rows = pl.ds(pl.multiple_of(r0, MOE_RT), n_rows)
        x = xs_scr[rows, :]
        cs = cs_scr[rows, :]
        def hidden(k):
            gate = jnp.dot(x, wg_ref[0, k], preferred_element_type=F32)
            up = jnp.dot(x, wu_ref[0, k], preferred_element_type=F32)
            return ((gate * _sigmoid(gate)) * up).astype(BF16)

        def project(k, act):
            yk = jnp.dot(act, wd_ref[0, k], preferred_element_type=F32)
            return jnp.sum(jnp.where(lane == step * EXPERTS_PER_GROUP + k, cs, 0.0), axis=-1, keepdims=True) * yk

        y, act = None, None
        for k in range(EXPERTS_PER_GROUP + 1):
            upcoming = hidden(k) if k < EXPERTS_PER_GROUP else None
            if act is not None:
                yk = project(k - 1, act)
                y = yk if y is None else y + yk
            act = upcoming
        ys_scr[rows, :] = y.astype(BF16)

    def pair(k, carry):
        run_expert(seg_start + k * (2 * MOE_RT), 2 * MOE_RT)
        return carry

    lax.fori_loop(0, n_tiles // 2, pair, 0)

    @pl.when(n_tiles % 2 == 1)
    def _():
        run_expert(seg_start + (n_tiles - 1) * MOE_RT, MOE_RT)

    @pl.when(step == N_GROUPS - 1)
    def _():
        y = lax.dot_general(p_scr[...], ys_scr[...], (((0,), (0,)), ((), ())),
                            preferred_element_type=F32)
        x2 = x_ref[...] + mod_ref[0, 5:6, :] * y
        o_ref[...] = _rms(x2, gf_ref[...]) if final_norm else x2


def moe_pallas(h2x, gid, x1, mod, g_final, w_gate, w_up, w_down, layer, seq_len, per_seq, final_norm):
    n = x1.shape[0]
    row = lambda i, g: (i, 0)
    group = lambda i, g: (layer, g, 0, 0)
    return pl.pallas_call(
        functools.partial(_moe_kernel, final_norm=final_norm),
        grid=(n // MOE_TB, N_GROUPS),
        in_specs=[pl.BlockSpec((MOE_TB, D_MOE_IN), row),
                  pl.BlockSpec((V7X_SUBLANES, MOE_TB), lambda i, g: (0, i)),
                  pl.BlockSpec((MOE_TB, D_MODEL), row, pipeline_mode=pl.Buffered(1)),
                  _mod_spec(MOE_TB, seq_len, per_seq),
                  pl.BlockSpec((1, D_MODEL), lambda i, g: (0, 0)),
                  pl.BlockSpec((1, EXPERTS_PER_GROUP, D_MODEL, D_EXPERT), group),
                  pl.BlockSpec((1, EXPERTS_PER_GROUP, D_MODEL, D_EXPERT), group),
                  pl.BlockSpec((1, EXPERTS_PER_GROUP, D_EXPERT, D_MODEL), group)],
        out_specs=pl.BlockSpec((MOE_TB, D_MODEL), row),
        out_shape=jax.ShapeDtypeStruct((n, D_MODEL), F32),
        scratch_shapes=[pltpu.VMEM((MOE_TBP, MOE_TB), BF16), pltpu.VMEM((MOE_TBP, D_MODEL), BF16),
                        pltpu.VMEM((MOE_TBP, AUX_LANES), F32), pltpu.VMEM((MOE_TBP, D_MODEL), BF16)],
        compiler_params=_params("arbitrary", "arbitrary"),
        name="moe",
    )(h2x, gid, x1, mod, g_final, w_gate, w_up, w_down)


TM_PROJ = 2048
TM_MERGE = 1024


def _layer(x, mod, seq_len, per_seq, lw, h0, ctx_kv, tables, layer, caches=()):
    n = x.shape[0]
    b = n // seq_len
    emit_kv = ctx_kv is None
    outs = inproj_pallas(x, mod, lw['norm_g'][0:1], lw['w_in'], layer, seq_len, per_seq,
                         TM_PROJ // 4 if emit_kv else TM_PROJ, emit_kv, caches)
    p = outs[0]
    p3 = p.reshape(b, seq_len, D_IN)
    y_rnn, h_fin = lru_pallas(p3, h0, lw['conv_w'], lw['conv_b'], lw['wg'], lw['gb'], lw['lam'])
    if emit_kv:
        y_att = ctxatt_pallas(p3)
    else:
        y_att = natten_pallas(p3, ctx_kv[0], ctx_kv[1], tables, layer)
    x1, h2x, gid = merge_pallas(y_rnn.reshape(n, D_RNN), y_att.reshape(n, D_ATT), p, x, mod, lw['norm_g'][1:2],
                                lw['w_br_rnn'], lw['w_br_att'], lw['w_out'], lw['w_router_t'], lw['router_bias'],
                                layer, seq_len, per_seq, TM_MERGE)
    x2 = moe_pallas(h2x, gid, x1, mod, lw['g_final'], lw['w_exp_gate'], lw['w_exp_up'], lw['w_exp_down'], layer,
                    seq_len, per_seq, layer == DEPTH - 1)
    kv = (outs[1], outs[2]) if emit_kv else None
    return x2, kv, h_fin


def kernel(x_prompt, x_sample, cache_k, cache_v, state_lru, c, c_ctx, w_ada, b_ada, norm_g, w_in, conv_w,
           conv_b, lru_wa, lru_ba, lru_wx, lru_bx, lru_lam, rpb, w_br_rnn, w_br_att, w_out, w_router,
           router_bias, w_exp_gate, w_exp_up, w_exp_down, final_norm_g):
    cvecs = jnp.concatenate([c, c_ctx[None, :], jnp.zeros((V7X_SUBLANES - DEC_BATCH - 1, D_MODEL), F32)], axis=0)
    mods = adaln_pallas(cvecs, w_ada, b_ada).reshape(DEPTH, V7X_SUBLANES, N_MOD, D_MODEL)
    tables = _natten_tables(rpb)
    w_router_t = w_router.T
    xp = x_prompt.reshape(BATCH * SEQ, D_MODEL)
    xs = x_sample.reshape(DEC_BATCH * DEC_SEQ, D_MODEL)
    zeros_h0 = jnp.zeros((BATCH, 2, D_RNN), F32)
    w_exp = [w.astype(BF16) for w in (w_exp_gate, w_exp_up, w_exp_down)]
    w_proj = [w.astype(BF16) for w in (w_in, w_br_rnn, w_br_att, w_out)]
    caches, hs = (), []
    for l in range(DEPTH):
        lw = dict(
            norm_g=norm_g[l], w_in=w_proj[0], conv_w=conv_w[l], conv_b=conv_b[l][None, :],
            wg=_lru_gate_weights(lru_wa[l], lru_wx[l]),
            gb=jnp.stack([lru_ba[l, 0], lru_bx[l, 0], lru_ba[l, 1], lru_bx[l, 1]], axis=0), lam=lru_lam[l],
            w_br_rnn=w_proj[1], w_br_att=w_proj[2], w_out=w_proj[3],
            w_router_t=w_router_t, router_bias=router_bias, g_final=final_norm_g[None, :],
            w_exp_gate=w_exp[0], w_exp_up=w_exp[1], w_exp_down=w_exp[2])
        xp, caches, h_l = _layer(xp, mods[l, DEC_BATCH:DEC_BATCH + 1], SEQ, False, lw, zeros_h0, None, None, l,
                                 caches)
        hs.append(h_l)
        ctx_kv = (cache_k[:, l].reshape(DEC_BATCH, PAST_LEN, D_ATT).astype(BF16),
                  cache_v[:, l].reshape(DEC_BATCH, PAST_LEN, D_ATT).astype(BF16))
        xs, _, _ = _layer(xs, mods[l, :DEC_BATCH], DEC_SEQ, True, lw, state_lru[:, l], ctx_kv, tables, l)
    y_prompt = xp.reshape(BATCH, SEQ, D_MODEL)
    y_sample = xs.reshape(DEC_BATCH, DEC_SEQ, D_MODEL)
    return (y_prompt, y_sample, caches[0], caches[1], jnp.stack(hs, axis=1))
```

```python
import functools

import jax
import jax.numpy as jnp
import numpy as np
from jax import lax
from jax.experimental import pallas as pl
from jax.experimental.pallas import tpu as pltpu

D_MODEL = 1024
BATCH = 16
SEQ = 256
DEPTH = 2
DEC_BATCH = 4
DEC_SEQ = 4096
PAST_LEN = 256

GRID_W = 64
D_RNN = 1024
N_LRU_BLOCKS = 16
LRU_BLOCK = D_RNN // N_LRU_BLOCKS
CONV_W = 4
LRU_C = 8.0
N_HEADS = 16
HEAD_DIM = 64
D_ATT = N_HEADS * HEAD_DIM
WIN_H = 8
WIN_W = 16
N_EXPERTS = 16
N_GROUPS = 4
EXPERTS_PER_GROUP = N_EXPERTS // N_GROUPS
D_EXPERT = 512
N_MOD = 6
D_IN = 2 * D_RNN + 3 * D_ATT + 2 * D_MODEL
EPS = 1e-6
NEG_INF = -1e30

BF16 = jnp.bfloat16
F32 = jnp.float32

V7X_LANES = 128
V7X_SUBLANES = 8
V7X_MXU_DIM = 256
V7X_VMEM_BYTES = 64 * 1024 * 1024
VMEM_LIMIT = V7X_VMEM_BYTES - 8 * 1024 * 1024

CHUNK = D_MODEL
N_CHUNKS = D_IN // CHUNK
C_XRNN, C_GRNN, C_Q, C_K, C_V, C_GATE_R, C_GATE_A = range(N_CHUNKS)

NT_DIMS = (((1,), (1,)), ((), ()))


def _params(*sem):
    return pltpu.CompilerParams(dimension_semantics=sem, vmem_limit_bytes=VMEM_LIMIT)


def _adaln_kernel(c_ref, w_ref, b_ref, o_ref):
    cv = c_ref[...]
    s = cv * jax.nn.sigmoid(cv)
    o_ref[0] = jnp.dot(s.astype(BF16), w_ref[0].astype(BF16), preferred_element_type=F32) + b_ref[0]


def adaln_pallas(cvecs, w_ada, b_ada):
    r = cvecs.shape[0]
    return pl.pallas_call(
        _adaln_kernel,
        grid=(DEPTH, N_MOD),
        in_specs=[pl.BlockSpec((r, D_MODEL), lambda l, j: (0, 0)),
                  pl.BlockSpec((1, D_MODEL, D_MODEL), lambda l, j: (l, 0, j)),
                  pl.BlockSpec((1, 1, D_MODEL), lambda l, j: (l, 0, j))],
        out_specs=pl.BlockSpec((1, r, D_MODEL), lambda l, j: (l, 0, j)),
        out_shape=jax.ShapeDtypeStruct((DEPTH, r, N_MOD * D_MODEL), F32),
        compiler_params=_params("arbitrary", "arbitrary"),
        name="adaln",
    )(cvecs, w_ada, b_ada.reshape(DEPTH, 1, N_MOD * D_MODEL))


def _mod_spec(tm, seq_len, per_seq):
    if per_seq:
        return pl.BlockSpec((1, N_MOD, D_MODEL), lambda i, *_: (i * tm // seq_len, 0, 0))
    return pl.BlockSpec((1, N_MOD, D_MODEL), lambda i, *_: (0, 0, 0))


def _rms(x, g):
    return x * lax.rsqrt(jnp.mean(x * x, axis=-1, keepdims=True) + EPS) * g


def _inproj_kernel(x_ref, mod_ref, g_ref, w_ref, *refs, emit_kv):
    if emit_kv:
        p_ref, k32_ref, v32_ref, h_scr = refs[-4:]
    else:
        p_ref, h_scr = refs
    j = pl.program_id(1)

    @pl.when(j == 0)
    def _():
        y = _rms(x_ref[...], g_ref[...])
        h_scr[...] = (y * (1.0 + mod_ref[0, 1:2, :]) + mod_ref[0, 0:1, :]).astype(BF16)

    acc = jnp.dot(h_scr[...], w_ref[0], preferred_element_type=F32)
    p_ref[...] = (acc * jnp.where(j == C_Q, HEAD_DIM ** -0.5, 1.0)).astype(BF16)
    if emit_kv:
        @pl.when(j == C_K)
        def _():
            k32_ref[...] = acc.reshape(k32_ref.shape)

        @pl.when(j == C_V)
        def _():
            v32_ref[...] = acc.reshape(v32_ref.shape)


def inproj_pallas(x, mod, g, w_in, layer, seq_len, per_seq, tm, emit_kv, caches=()):
    n = x.shape[0]
    row = lambda i, j: (i, 0)
    in_specs = [pl.BlockSpec((tm, D_MODEL), row), _mod_spec(tm, seq_len, per_seq),
                pl.BlockSpec((1, D_MODEL), lambda i, j: (0, 0)),
                pl.BlockSpec((1, D_MODEL, CHUNK), lambda i, j: (layer, 0, j))]
    out_shape = [jax.ShapeDtypeStruct((n, D_IN), BF16)]
    out_specs = [pl.BlockSpec((tm, CHUNK), lambda i, j: (i, j))]
    aliases = {}
    if emit_kv:
        out_shape += [jax.ShapeDtypeStruct((n // seq_len, DEPTH, seq_len, N_HEADS, HEAD_DIM), F32)] * 2
        out_specs += [pl.BlockSpec((tm // seq_len, 1, seq_len, N_HEADS, HEAD_DIM),
                                   lambda i, j: (i, layer, 0, 0, 0))] * 2
        aliases = {len(in_specs) + k: 1 + k for k in range(len(caches))}
        in_specs += [pl.BlockSpec(memory_space=pl.ANY)] * len(caches)
    return pl.pallas_call(
        functools.partial(_inproj_kernel, emit_kv=emit_kv),
        grid=(n // tm, N_CHUNKS),
        in_specs=in_specs,
        out_specs=out_specs,
        out_shape=out_shape,
        input_output_aliases=aliases,
        scratch_shapes=[pltpu.VMEM((tm, D_MODEL), BF16)],
        compiler_params=_params("arbitrary", "arbitrary"),
        name="inproj",
    )(x, mod, g, w_in, *caches)


LRU_CB = 512
LRU_TC = 256
LRU_SUB = V7X_MXU_DIM
LRU_HALO = 16
LRU_PIPE = 4


LRU_SEG = LRU_TC // V7X_SUBLANES
LRU_NSLAB = LRU_SEG + CONV_W


def _lru_row_maps():
    s = np.arange(V7X_SUBLANES)[None, :]
    src = (LRU_HALO - CONV_W // 2) + LRU_SEG * s + np.arange(LRU_NSLAB)[:, None]
    sel = np.zeros((LRU_NSLAB * V7X_SUBLANES, LRU_TC + 2 * LRU_HALO), np.float32)
    sel[np.arange(sel.shape[0]), src.reshape(-1)] = 1.0
    tok = (LRU_SEG * s + np.arange(LRU_SEG)[:, None]).reshape(-1)
    perm = np.zeros((LRU_TC, LRU_TC), np.float32)
    perm[np.arange(LRU_TC), tok] = 1.0
    return jnp.asarray(sel, BF16), jnp.asarray(perm, BF16), jnp.asarray(perm.T, BF16)


def _slab_scan(a3, b3, carry, reverse):
    n = a3.shape[0]
    hs, cum = [None] * n, [None] * n
    h = cp = None
    for t in (range(n - 1, -1, -1) if reverse else range(n)):
        h = b3[t] if h is None else a3[t] * h + b3[t]
        cp = a3[t] if cp is None else cp * a3[t]
        hs[t], cum[t] = h, cp
    sub = lax.broadcasted_iota(jnp.int32, (V7X_SUBLANES, 1), 0)
    pa, pb = cp, h
    s = 1
    while s < V7X_SUBLANES:
        shift = V7X_SUBLANES - s if reverse else s
        ok = (sub < V7X_SUBLANES - s) if reverse else (sub >= s)
        a_sh = pltpu.roll(pa, shift, axis=0)
        b_sh = pltpu.roll(pb, shift, axis=0)
        pb = jnp.where(ok, pa * b_sh + pb, pb)
        pa = jnp.where(ok, pa * a_sh, pa)
        s *= 2
    leaving = pb + pa * carry
    first, last = (V7X_SUBLANES - 1, 0) if reverse else (0, V7X_SUBLANES - 1)
    entering = jnp.where(sub == first, carry, pltpu.roll(leaving, V7X_SUBLANES - 1 if reverse else 1, axis=0))
    out = jnp.concatenate([hs[t] + cum[t] * entering for t in range(n)], axis=0)
    return out, leaving[last:last + 1, :]


def _sigmoid(x):
    return 0.5 * jnp.tanh(0.5 * x) + 0.5


def _lru_kernel(x_ref, gate_ref, h0_ref, cw_ref, cb_ref, wg_ref, gb_ref, lam_ref, sel_ref, perm_ref, permt_ref,
                y_ref, fin_ref, hf_scr, u_scr):
    t_len = x_ref.shape[1]
    cb = x_ref.shape[2]
    n_chunks = t_len // LRU_TC

    def conv_chunk(c):
        t0 = pl.multiple_of(c * LRU_TC, LRU_TC)
        cur = x_ref[0, pl.ds(t0, LRU_TC), :]
        lo = pl.multiple_of(jnp.maximum(t0 - LRU_HALO, 0), LRU_HALO)
        hi = pl.multiple_of(jnp.minimum(t0 + LRU_TC, t_len - LRU_HALO), LRU_HALO)
        prev = x_ref[0, pl.ds(lo, LRU_HALO), :]
        nxt = x_ref[0, pl.ds(hi, LRU_HALO), :]
        prev = jnp.where(c > 0, prev, jnp.zeros_like(prev))
        nxt = jnp.where(c < n_chunks - 1, nxt, jnp.zeros_like(nxt))
        ext = jnp.concatenate([prev, cur, nxt], axis=0)
        xs = jnp.dot(sel_ref[...], ext, preferred_element_type=F32).reshape(LRU_NSLAB, V7X_SUBLANES, cb)
        u = cb_ref[...][None] + jnp.zeros((LRU_SEG, V7X_SUBLANES, cb), F32)
        for j in range(CONV_W):
            u = u + xs[j:j + LRU_SEG] * cw_ref[j:j + 1, :][None]
        return t0, u.reshape(LRU_TC, cb)

    def slabs(v):
        return v.reshape(LRU_SEG, V7X_SUBLANES, cb)

    def gate_matmuls(u, d):
        ub = u.astype(BF16)
        pre = [jnp.dot(ub[:, LRU_SUB * s:LRU_SUB * (s + 1)], wg_ref[d, s], preferred_element_type=F32)
               for s in range(cb // LRU_SUB)]
        return (jnp.concatenate([p[:, :LRU_SUB] for p in pre], axis=1),
                jnp.concatenate([p[:, LRU_SUB:] for p in pre], axis=1))

    def gates(u, pre_a, pre_x, d):
        r = _sigmoid(pre_a + gb_ref[2 * d:2 * d + 1, :])
        i = _sigmoid(pre_x + gb_ref[2 * d + 1:2 * d + 2, :])
        log_a = (-LRU_C * jax.nn.softplus(-lam_ref[d:d + 1, :])) * r
        a = jnp.exp(log_a)
        th = jnp.tanh(log_a)
        num = -2.0 * th
        scale = jnp.where(num > 0.0, num * lax.rsqrt(num * (1.0 - th)), 0.0)
        inp = scale * (i * u)
        return a, inp

    group = LRU_PIPE if n_chunks % LRU_PIPE == 0 else 1

    def pipelined(first_stage, second_stage, base, carry):
        pending = None
        for g in range(group + 1):
            upcoming = first_stage(base + g) if g < group else None
            if pending is not None:
                carry = second_stage(pending, carry)
            pending = upcoming
        return carry

    def fwd_matmuls(c):
        t0, u = conv_chunk(c)
        u_scr[pl.ds(t0, LRU_TC), :] = u
        return (t0, u) + gate_matmuls(u, 0)

    def fwd_scan(stage, carry):
        t0, u, pre_a, pre_x = stage
        a, inp = gates(u, pre_a, pre_x, 0)
        h, carry = _slab_scan(slabs(a), slabs(inp), carry, reverse=False)
        hf_scr[pl.ds(t0, LRU_TC), :] = h
        return carry

    fin_f = lax.fori_loop(0, n_chunks // group,
                          lambda k, carry: pipelined(fwd_matmuls, fwd_scan, k * group, carry), h0_ref[0, 0:1, :])

    def bwd_matmuls(k):
        t0 = pl.multiple_of((n_chunks - 1 - k) * LRU_TC, LRU_TC)
        u = u_scr[pl.ds(t0, LRU_TC), :]
        g = jnp.dot(perm_ref[...], gate_ref[0, pl.ds(t0, LRU_TC), :], preferred_element_type=F32)
        return (t0, u, g) + gate_matmuls(u, 1)

    def bwd_scan(stage, carry):
        t0, u, g, pre_a, pre_x = stage
        a, inp = gates(u, pre_a, pre_x, 1)
        h, carry = _slab_scan(slabs(a), slabs(inp), carry, reverse=True)
        y = ((hf_scr[pl.ds(t0, LRU_TC), :] + h) * jax.nn.gelu(g)).astype(BF16)
        y_ref[0, pl.ds(t0, LRU_TC), :] = jnp.dot(permt_ref[...], y, preferred_element_type=F32).astype(y_ref.dtype)
        return carry

    fin_b = lax.fori_loop(0, n_chunks // group,
                          lambda k, carry: pipelined(bwd_matmuls, bwd_scan, k * group, carry), h0_ref[0, 1:2, :])
    fin_ref[0, 0:1, :] = fin_f
    fin_ref[0, 1:2, :] = fin_b


def _lru_gate_weights(lru_wa, lru_wx):
    per = LRU_SUB // LRU_BLOCK
    eye = jnp.eye(per, dtype=F32)

    def dense(w):
        w = w.reshape(2, D_RNN // LRU_SUB, per, LRU_BLOCK, LRU_BLOCK)
        full = w[:, :, :, :, None, :] * eye[None, None, :, None, :, None]
        return full.reshape(2, D_RNN // LRU_SUB, LRU_SUB, LRU_SUB)

    return jnp.concatenate([dense(lru_wa), dense(lru_wx)], axis=-1).astype(BF16)


def lru_pallas(p, h0, conv_w, conv_b, wg, gb, lam):
    b, t, _ = p.shape
    n_cb = D_RNN // LRU_CB
    maps = _lru_row_maps()
    whole = lambda m: pl.BlockSpec(m.shape, lambda bi, ci: (0, 0))
    return pl.pallas_call(
        _lru_kernel,
        grid=(b, n_cb),
        in_specs=[pl.BlockSpec((1, t, LRU_CB), lambda bi, ci: (bi, 0, C_XRNN * n_cb + ci)),
                  pl.BlockSpec((1, t, LRU_CB), lambda bi, ci: (bi, 0, C_GRNN * n_cb + ci)),
                  pl.BlockSpec((1, 2, LRU_CB), lambda bi, ci: (bi, 0, ci)),
                  pl.BlockSpec((CONV_W, LRU_CB), lambda bi, ci: (0, ci)),
                  pl.BlockSpec((1, LRU_CB), lambda bi, ci: (0, ci)),
                  pl.BlockSpec((2, LRU_CB // LRU_SUB, LRU_SUB, 2 * LRU_SUB), lambda bi, ci: (0, ci, 0, 0)),
                  pl.BlockSpec((4, LRU_CB), lambda bi, ci: (0, ci)),
                  pl.BlockSpec((2, LRU_CB), lambda bi, ci: (0, ci))] + [whole(m) for m in maps],
        out_specs=[pl.BlockSpec((1, t, LRU_CB), lambda bi, ci: (bi, 0, ci)),
                   pl.BlockSpec((1, 2, LRU_CB), lambda bi, ci: (bi, 0, ci))],
        out_shape=[jax.ShapeDtypeStruct((b, t, D_RNN), BF16), jax.ShapeDtypeStruct((b, 2, D_RNN), F32)],
        scratch_shapes=[pltpu.VMEM((t, LRU_CB), F32), pltpu.VMEM((t, LRU_CB), F32)],
        compiler_params=_params("arbitrary", "arbitrary"),
        name="lru",
    )(p, p, h0, conv_w, conv_b, wg, gb, lam, *maps)


Q_ROWS = 4
Q_TILE = Q_ROWS * GRID_W
KEY_ROWS = 12
N_Q_TILES = DEC_SEQ // Q_TILE
ATT_LANES = 1024
HEADS_PER_STEP = ATT_LANES // HEAD_DIM
PAIR = 2 * HEAD_DIM


def _window_block(t):
    return jnp.clip(t - 1, 0, N_Q_TILES - KEY_ROWS // Q_ROWS)


N_DR_PAIRS = 2 * WIN_H


def _natten_tables(rpb):
    c = np.arange(GRID_W)
    q_start = np.clip(c - WIN_W // 2, 0, GRID_W - WIN_W)
    col_valid = (c[None, :] >= q_start[:, None]) & (c[None, :] < q_start[:, None] + WIN_W)
    dc = c[None, :] - c[:, None] + WIN_W - 1
    sel_c = ((dc[None] == np.arange(2 * WIN_W - 1)[:, None, None]) & col_valid[None]).astype(np.float32)
    t = jnp.einsum('lhrd,dck->lhrck', rpb.astype(F32), jnp.asarray(sel_c), precision=lax.Precision.HIGHEST)
    t = jnp.where(col_valid[None, None, None], t, NEG_INF)
    t = jnp.pad(t, ((0, 0), (0, 0), (1, 1), (0, 0), (0, 0)), constant_values=NEG_INF)
    return jnp.concatenate([t[:, :, :-1], t[:, :, 1:]], axis=-1)


def _attend_heads(n_pairs, q_pair, key_blocks, val_blocks, bias_blocks):
    lane_head = lax.broadcasted_iota(jnp.int32, (1, PAIR), 1) // HEAD_DIM

    def scores(h):
        hp, j = divmod(h, 2)
        mine = lane_head == j
        q2 = q_pair(hp)
        qh = jnp.where(mine, q2, jnp.zeros_like(q2))
        s = []
        for kb, bb in zip(key_blocks(hp), bias_blocks(h)):
            sd = lax.dot_general(qh, kb, NT_DIMS, preferred_element_type=F32)
            s.append(sd if bb is None else sd + bb)
        m = s[0].max(axis=-1, keepdims=True)
        for sd in s[1:]:
            m = jnp.maximum(m, sd.max(axis=-1, keepdims=True))
        return hp, mine, s, m

    def weighted(hp, mine, s, m):
        o = None
        for sd, vb in zip(s, val_blocks(hp)):
            pv = jnp.dot(jnp.exp(sd - m).astype(BF16), jnp.where(mine, vb, jnp.ones_like(vb)),
                         preferred_element_type=F32)
            o = pv if o is None else o + pv
        return jnp.where(mine, o / pltpu.roll(o, HEAD_DIM, axis=1), 0.0)

    outs = [None] * n_pairs
    pending = None
    for h in range(2 * n_pairs + 1):
        upcoming = scores(h) if h < 2 * n_pairs else None
        if pending is not None:
            o = weighted(*pending)
            outs[pending[0]] = o if outs[pending[0]] is None else outs[pending[0]] + o
        pending = upcoming
    return jnp.concatenate(outs, axis=1)


def _natten_kernel(q_ref, k0_ref, k1_ref, k2_ref, v0_ref, v1_ref, v2_ref, kc_ref, vc_ref, tt_ref, o_ref):
    ti = pl.program_id(1)
    rows = DEC_SEQ // GRID_W
    wstart = _window_block(ti) * Q_ROWS
    first_row = lax.broadcasted_iota(jnp.int32, (1, 2 * GRID_W), 1) < GRID_W
    pieces = {}
    for a in range(Q_ROWS):
        r = ti * Q_ROWS + a
        start_r = jnp.clip(r - WIN_H // 2, 0, rows - WIN_H)
        for i in range(0, KEY_ROWS, 2):
            kr = wstart + i
            ok = [((kr + e >= start_r) & (kr + e < start_r + WIN_H)).astype(jnp.int32) for e in range(2)]
            pieces[a, i] = (jnp.clip(kr - r + WIN_H, 0, N_DR_PAIRS - 1), jnp.where(first_row, ok[0], ok[1]) > 0)

    def bias_block(h, d):
        return jnp.concatenate(
            [jnp.concatenate([jnp.where(pieces[a, i][1], tt_ref[0, h, pieces[a, i][0]], NEG_INF)
                              for i in range(Q_ROWS * d, Q_ROWS * (d + 1), 2)], axis=1)
             for a in range(Q_ROWS)], axis=0)

    k_refs = (k0_ref, k1_ref, k2_ref, kc_ref)
    v_refs = (v0_ref, v1_ref, v2_ref, vc_ref)
    lanes = lambda hp: slice(PAIR * hp, PAIR * (hp + 1))
    o_ref[0] = _attend_heads(
        ATT_LANES // PAIR, lambda hp: q_ref[0, :, lanes(hp)],
        lambda hp: [r[0, :, lanes(hp)] for r in k_refs], lambda hp: [r[0, :, lanes(hp)] for r in v_refs],
        lambda h: [bias_block(h, d) for d in range(KEY_ROWS // Q_ROWS)] + [None]).astype(o_ref.dtype)


def natten_pallas(p, k_ctx, v_ctx, tables, layer):
    b, t, _ = p.shape
    n_lb = D_ATT // ATT_LANES
    q_spec = pl.BlockSpec((1, Q_TILE, ATT_LANES), lambda lb, ti, bi: (bi, ti, C_Q * n_lb + lb))
    win = lambda ch, d: pl.BlockSpec((1, Q_TILE, ATT_LANES),
                                     lambda lb, ti, bi: (bi, _window_block(ti) + d, ch * n_lb + lb))
    ctx = pl.BlockSpec((1, PAST_LEN, ATT_LANES), lambda lb, ti, bi: (bi, 0, lb))
    return pl.pallas_call(
        _natten_kernel,
        grid=(n_lb, N_Q_TILES, b),
        in_specs=[q_spec, win(C_K, 0), win(C_K, 1), win(C_K, 2), win(C_V, 0), win(C_V, 1), win(C_V, 2), ctx, ctx,
                  pl.BlockSpec((1, HEADS_PER_STEP, N_DR_PAIRS, GRID_W, 2 * GRID_W),
                               lambda lb, ti, bi: (layer, lb, 0, 0, 0))],
        out_specs=pl.BlockSpec((1, Q_TILE, ATT_LANES), lambda lb, ti, bi: (bi, ti, lb)),
        out_shape=jax.ShapeDtypeStruct((b, t, D_ATT), BF16),
        compiler_params=_params("arbitrary", "arbitrary", "arbitrary"),
        name="natten",
    )(p, p, p, p, p, p, p, k_ctx, v_ctx, tables)


def _ctxatt_kernel(q_ref, k_ref, v_ref, o_ref):
    lanes = lambda hp: slice(PAIR * hp, PAIR * (hp + 1))
    o_ref[0] = _attend_heads(ATT_LANES // PAIR, lambda hp: q_ref[0, :, lanes(hp)], lambda hp: [k_ref[0, :, lanes(hp)]],
                             lambda hp: [v_ref[0, :, lanes(hp)]], lambda h: [None]).astype(o_ref.dtype)


def ctxatt_pallas(p):
    b, t, _ = p.shape
    n_lb = D_ATT // ATT_LANES
    blk = lambda ch: pl.BlockSpec((1, t, ATT_LANES), lambda lb, bi: (bi, 0, ch * n_lb + lb))
    return pl.pallas_call(
        _ctxatt_kernel,
        grid=(n_lb, b),
        in_specs=[blk(C_Q), blk(C_K), blk(C_V)],
        out_specs=pl.BlockSpec((1, t, ATT_LANES), lambda lb, bi: (bi, 0, lb)),
        out_shape=jax.ShapeDtypeStruct((b, t, D_ATT), BF16),
        compiler_params=_params("arbitrary", "arbitrary"),
        name="ctxatt",
    )(p, p, p)


def _split_bf16(x):
    hi = x.astype(BF16)
    return hi, (x - hi.astype(F32)).astype(BF16)


def _route(logits_t, rb_ref):
    score = [jax.nn.sigmoid(logits_t[e:e + 1, :]) for e in range(N_EXPERTS)]
    sel = [score[e] + rb_ref[e] for e in range(N_EXPERTS)]
    best_g = None
    for g in range(N_GROUPS):
        v = sel[EXPERTS_PER_GROUP * g:EXPERTS_PER_GROUP * (g + 1)]
        top2 = None
        for i in range(EXPERTS_PER_GROUP):
            for j in range(i + 1, EXPERTS_PER_GROUP):
                pair = v[i] + v[j]
                top2 = pair if top2 is None else jnp.maximum(top2, pair)
        if best_g is None:
            best_g, best_v = jnp.zeros_like(top2, dtype=jnp.int32), top2
        else:
            upd = top2 > best_v
            best_g = jnp.where(upd, g, best_g)
            best_v = jnp.where(upd, top2, best_v)

    def in_best(vals, j):
        out = vals[j]
        for g in range(1, N_GROUPS):
            out = jnp.where(best_g == g, vals[EXPERTS_PER_GROUP * g + j], out)
        return out

    v = [in_best(sel, j) for j in range(EXPERTS_PER_GROUP)]
    sc = [in_best(score, j) for j in range(EXPERTS_PER_GROUP)]

    def first_argmax(vals):
        idx, top = jnp.zeros_like(best_g), vals[0]
        for j in range(1, EXPERTS_PER_GROUP):
            upd = vals[j] > top
            idx = jnp.where(upd, j, idx)
            top = jnp.where(upd, vals[j], top)
        return idx

    i1 = first_argmax(v)
    i2 = first_argmax([jnp.where(i1 == j, -jnp.inf, v[j]) for j in range(EXPERTS_PER_GROUP)])
    pick = lambda idx: sum(jnp.where(idx == j, sc[j], 0.0) for j in range(EXPERTS_PER_GROUP))
    w1, w2 = pick(i1), pick(i2)
    den = w1 + w2
    c1, c2 = w1 / den, w2 / den
    rows = []
    for e in range(N_EXPERTS):
        g, j = divmod(e, EXPERTS_PER_GROUP)
        rows.append(jnp.where(best_g == g, jnp.where(i1 == j, c1, 0.0) + jnp.where(i2 == j, c2, 0.0), 0.0))
    return jnp.concatenate(rows, axis=0), best_g


AUX_LANES = V7X_LANES
AUX_MID = N_EXPERTS
AUX_LO = 2 * N_EXPERTS
AUX_GROUP = 3 * N_EXPERTS
D_MOE_IN = D_MODEL + AUX_LANES
MERGE_CHAINS = 2


def _merge_kernel(rb_ref, yr_ref, ya_ref, gr_ref, ga_ref, x_ref, mod_ref, g_ref, wr_ref, wa_ref, wo_ref, wrt_ref,
                  x1_ref, h2_ref, gid_ref):
    w_hi, w_lo = _split_bf16(wrt_ref[...])
    w_parts = jnp.concatenate([w_hi, w_lo], axis=0)
    dot_nt = lambda a, b: lax.dot_general(a, b, NT_DIMS, preferred_element_type=F32)
    n_rows = x_ref.shape[0] // MERGE_CHAINS

    def project(ci):
        rows = slice(ci * n_rows, (ci + 1) * n_rows)
        m = (_sigmoid(gr_ref[rows, :].astype(F32)) * jnp.dot(yr_ref[rows, :], wr_ref[0], preferred_element_type=F32)
             + _sigmoid(ga_ref[rows, :].astype(F32)) * jnp.dot(ya_ref[rows, :], wa_ref[0],
                                                              preferred_element_type=F32))
        o = jnp.dot(m.astype(BF16), wo_ref[0], preferred_element_type=F32)
        x1 = x_ref[rows, :] + mod_ref[0, 2:3, :] * o
        x1_ref[rows, :] = x1
        return rows, _rms(x1, g_ref[...]) * (1.0 + mod_ref[0, 4:5, :]) + mod_ref[0, 3:4, :]

    def route(rows, h2):
        h_hi, h_lo = _split_bf16(h2)
        by_h_hi = dot_nt(w_parts, h_hi)
        logits_t = by_h_hi[:N_EXPERTS] + (dot_nt(w_hi, h_lo) + by_h_hi[N_EXPERTS:])
        comb_t, group = _route(logits_t, rb_ref)
        group = group.astype(F32)
        gid_ref[:, rows] = jnp.concatenate([group, jnp.zeros((V7X_SUBLANES - 1, n_rows), F32)], axis=0)
        padded = jnp.concatenate([comb_t, jnp.zeros((AUX_GROUP - N_EXPERTS, n_rows), F32), group,
                                  jnp.zeros((AUX_LANES - AUX_GROUP - 1, n_rows), F32)], axis=0)
        c = padded.T
        c_hi = c.astype(BF16).astype(F32)
        c_mid = (c - c_hi).astype(BF16).astype(F32)
        c_lo = c - c_hi - c_mid
        aux = c_hi + pltpu.roll(c_mid, AUX_MID, axis=1) + pltpu.roll(c_lo, AUX_LO, axis=1)
        h2_ref[rows, :] = jnp.concatenate([h2.astype(BF16), aux.astype(BF16)], axis=1)

    pending = None
    for ci in range(MERGE_CHAINS + 1):
        upcoming = project(ci) if ci < MERGE_CHAINS else None
        if pending is not None:
            route(*pending)
        pending = upcoming


def merge_pallas(y_rnn, y_att, p, x, mod, g, w_br_rnn, w_br_att, w_out, w_router_t, router_bias, layer, seq_len,
                 per_seq, tm):
    n = x.shape[0]
    row = lambda i: (i, 0)
    full = lambda shape: pl.BlockSpec(shape, lambda i: (0, 0))
    weight = pl.BlockSpec((1, D_MODEL, D_MODEL), lambda i: (layer, 0, 0))
    return pl.pallas_call(
        _merge_kernel,
        grid=(n // tm,),
        in_specs=[pl.BlockSpec(memory_space=pltpu.SMEM),
                  pl.BlockSpec((tm, D_RNN), row), pl.BlockSpec((tm, D_ATT), row),
                  pl.BlockSpec((tm, CHUNK), lambda i: (i, C_GATE_R)), pl.BlockSpec((tm, CHUNK), lambda i: (i, C_GATE_A)),
                  pl.BlockSpec((tm, D_MODEL), row), _mod_spec(tm, seq_len, per_seq), full((1, D_MODEL)),
                  weight, weight, weight, full((N_EXPERTS, D_MODEL))],
        out_specs=[pl.BlockSpec((tm, D_MODEL), row), pl.BlockSpec((tm, D_MOE_IN), row),
                   pl.BlockSpec((V7X_SUBLANES, tm), lambda i: (0, i))],
        out_shape=[jax.ShapeDtypeStruct((n, D_MODEL), F32), jax.ShapeDtypeStruct((n, D_MOE_IN), BF16),
                   jax.ShapeDtypeStruct((V7X_SUBLANES, n), F32)],
        compiler_params=_params("arbitrary"),
        name="merge",
    )(router_bias, y_rnn, y_att, p, p, x, mod, g, w_br_rnn, w_br_att, w_out, w_router_t)


MOE_TB = 1024
MOE_RT = 128
MOE_TBP = MOE_TB + N_GROUPS * MOE_RT


def _group_segments(gid_row):
    sub = lax.broadcasted_iota(jnp.int32, (V7X_SUBLANES, 1), 0).astype(F32)
    onehot = (gid_row == sub).astype(F32)
    cnt = jnp.sum(onehot, axis=1, keepdims=True)
    padded = jnp.floor((cnt + (MOE_RT - 1)) * (1.0 / MOE_RT)) * MOE_RT
    starts, run = [], jnp.zeros((1, 1), F32)
    for g in range(N_GROUPS):
        starts.append(run)
        run = run + padded[g:g + 1, :]
    return onehot, starts, [padded[g:g + 1, :] for g in range(N_GROUPS)], run


def _to_int(v):
    return v[0, 0].astype(jnp.int32)


def _moe_kernel(h_ref, gid_ref, x_ref, mod_ref, gf_ref, wg_ref, wu_ref, wd_ref, o_ref,
                p_scr, xs_scr, cs_scr, ys_scr, *, final_norm):
    step = pl.program_id(1)
    onehot, starts, sizes, used = _group_segments(gid_ref[0:1, :])
    lane = lax.broadcasted_iota(jnp.int32, (1, AUX_LANES), 1)

    @pl.when(step == 0)
    def _():
        t_row = lax.broadcasted_iota(jnp.int32, (MOE_TB, MOE_TB), 0)
        t_col = lax.broadcasted_iota(jnp.int32, (MOE_TB, MOE_TB), 1)
        earlier = (t_row < t_col).astype(BF16)
        rank = jnp.dot(onehot.astype(BF16), earlier, preferred_element_type=F32)
        pos = jnp.zeros((1, MOE_TB), F32)
        for g in range(N_GROUPS):
            pos = pos + onehot[g:g + 1, :] * (starts[g] + rank[g:g + 1, :])
        dest = lax.broadcasted_iota(jnp.int32, (MOE_TBP, 1), 0).astype(F32)
        p_scr[...] = (dest == pos).astype(BF16)
        for r0 in range(0, MOE_TBP, V7X_MXU_DIM):
            rows = pl.ds(r0, V7X_MXU_DIM)
            sorted_rows = jnp.dot(p_scr[rows, :], h_ref[...], preferred_element_type=F32)
            xs_scr[rows, :] = sorted_rows[:, :D_MODEL].astype(BF16)
            aux = sorted_rows[:, D_MODEL:]
            cs_scr[rows, :] = (aux + pltpu.roll(aux, AUX_LANES - AUX_MID, axis=1)
                               + pltpu.roll(aux, AUX_LANES - AUX_LO, axis=1))
        first_free = pl.multiple_of(_to_int(used), MOE_RT)

        def clear(k, carry):
            ys_scr[pl.ds(pl.multiple_of(first_free + k * MOE_RT, MOE_RT), MOE_RT), :] = jnp.zeros(
                (MOE_RT, D_MODEL), BF16)
            return carry

        lax.fori_loop(0, (MOE_TBP - first_free) // MOE_RT, clear, 0)

    start_v, size_v = starts[0], sizes[0]
    for g in range(1, N_GROUPS):
        start_v = jnp.where(step == g, starts[g], start_v)
        size_v = jnp.where(step == g, sizes[g], size_v)
    seg_start = _to_int(start_v)
    n_tiles = _to_int(size_v) // MOE_RT

    def run_expert(r0, n_rows):
        rows = pl.ds(pl.multiple_of(r0, MOE_RT), n_rows)
        x = xs_scr[rows, :]
        cs = cs_scr[rows, :]
        def hidden(k):
            gate = jnp.dot(x, wg_ref[0, k], preferred_element_type=F32)
            up = jnp.dot(x, wu_ref[0, k], preferred_element_type=F32)
            return ((gate * _sigmoid(gate)) * up).astype(BF16)

        def project(k, act):
            yk = jnp.dot(act, wd_ref[0, k], preferred_element_type=F32)
            return jnp.sum(jnp.where(lane == step * EXPERTS_PER_GROUP + k, cs, 0.0), axis=-1, keepdims=True) * yk

        y, act = None, None
        for k in range(EXPERTS_PER_GROUP + 1):
            upcoming = hidden(k) if k < EXPERTS_PER_GROUP else None
            if act is not None:
                yk = project(k - 1, act)
                y = yk if y is None else y + yk
            act = upcoming
        ys_scr[rows, :] = y.astype(BF16)

    def pair(k, carry):
        run_expert(seg_start + k * (2 * MOE_RT), 2 * MOE_RT)
        return carry

    lax.fori_loop(0, n_tiles // 2, pair, 0)

    @pl.when(n_tiles % 2 == 1)
    def _():
        run_expert(seg_start + (n_tiles - 1) * MOE_RT, MOE_RT)

    @pl.when(step == N_GROUPS - 1)
    def _():
        y = lax.dot_general(p_scr[...], ys_scr[...], (((0,), (0,)), ((), ())),
                            preferred_element_type=F32)
        x2 = x_ref[...] + mod_ref[0, 5:6, :] * y
        o_ref[...] = _rms(x2, gf_ref[...]) if final_norm else x2


def moe_pallas(h2x, gid, x1, mod, g_final, w_gate, w_up, w_down, layer, seq_len, per_seq, final_norm):
    n = x1.shape[0]
    row = lambda i, g: (i, 0)
    group = lambda i, g: (layer, g, 0, 0)
    return pl.pallas_call(
        functools.partial(_moe_kernel, final_norm=final_norm),
        grid=(n // MOE_TB, N_GROUPS),
        in_specs=[pl.BlockSpec((MOE_TB, D_MOE_IN), row),
                  pl.BlockSpec((V7X_SUBLANES, MOE_TB), lambda i, g: (0, i)),
                  pl.BlockSpec((MOE_TB, D_MODEL), row, pipeline_mode=pl.Buffered(1)),
                  _mod_spec(MOE_TB, seq_len, per_seq),
                  pl.BlockSpec((1, D_MODEL), lambda i, g: (0, 0)),
                  pl.BlockSpec((1, EXPERTS_PER_GROUP, D_MODEL, D_EXPERT), group),
                  pl.BlockSpec((1, EXPERTS_PER_GROUP, D_MODEL, D_EXPERT), group),
                  pl.BlockSpec((1, EXPERTS_PER_GROUP, D_EXPERT, D_MODEL), group)],
        out_specs=pl.BlockSpec((MOE_TB, D_MODEL), row),
        out_shape=jax.ShapeDtypeStruct((n, D_MODEL), F32),
        scratch_shapes=[pltpu.VMEM((MOE_TBP, MOE_TB), BF16), pltpu.VMEM((MOE_TBP, D_MODEL), BF16),
                        pltpu.VMEM((MOE_TBP, AUX_LANES), F32), pltpu.VMEM((MOE_TBP, D_MODEL), BF16)],
        compiler_params=_params("arbitrary", "arbitrary"),
        name="moe",
    )(h2x, gid, x1, mod, g_final, w_gate, w_up, w_down)


TM_PROJ = 2048
TM_MERGE = 1024


def _layer(x, mod, seq_len, per_seq, lw, h0, ctx_kv, tables, layer, caches=()):
    n = x.shape[0]
    b = n // seq_len
    emit_kv = ctx_kv is None
    outs = inproj_pallas(x, mod, lw['norm_g'][0:1], lw['w_in'], layer, seq_len, per_seq,
                         TM_PROJ // 4 if emit_kv else TM_PROJ, emit_kv, caches)
    p = outs[0]
    p3 = p.reshape(b, seq_len, D_IN)
    y_rnn, h_fin = lru_pallas(p3, h0, lw['conv_w'], lw['conv_b'], lw['wg'], lw['gb'], lw['lam'])
    if emit_kv:
        y_att = ctxatt_pallas(p3)
    else:
        y_att = natten_pallas(p3, ctx_kv[0], ctx_kv[1], tables, layer)
    x1, h2x, gid = merge_pallas(y_rnn.reshape(n, D_RNN), y_att.reshape(n, D_ATT), p, x, mod, lw['norm_g'][1:2],
                                lw['w_br_rnn'], lw['w_br_att'], lw['w_out'], lw['w_router_t'], lw['router_bias'],
                                layer, seq_len, per_seq, TM_MERGE)
    x2 = moe_pallas(h2x, gid, x1, mod, lw['g_final'], lw['w_exp_gate'], lw['w_exp_up'], lw['w_exp_down'], layer,
                    seq_len, per_seq, layer == DEPTH - 1)
    kv = (outs[1], outs[2]) if emit_kv else None
    return x2, kv, h_fin


def kernel(x_prompt, x_sample, cache_k, cache_v, state_lru, c, c_ctx, w_ada, b_ada, norm_g, w_in, conv_w,
           conv_b, lru_wa, lru_ba, lru_wx, lru_bx, lru_lam, rpb, w_br_rnn, w_br_att, w_out, w_router,
           router_bias, w_exp_gate, w_exp_up, w_exp_down, final_norm_g):
    cvecs = jnp.concatenate([c, c_ctx[None, :], jnp.zeros((V7X_SUBLANES - DEC_BATCH - 1, D_MODEL), F32)], axis=0)
    mods = adaln_pallas(cvecs, w_ada, b_ada).reshape(DEPTH, V7X_SUBLANES, N_MOD, D_MODEL)
    tables = _natten_tables(rpb)
    w_router_t = w_router.T
    xp = x_prompt.reshape(BATCH * SEQ, D_MODEL)
    xs = x_sample.reshape(DEC_BATCH * DEC_SEQ, D_MODEL)
    zeros_h0 = jnp.zeros((BATCH, 2, D_RNN), F32)
    w_exp = [w.astype(BF16) for w in (w_exp_gate, w_exp_up, w_exp_down)]
    w_proj = [w.astype(BF16) for w in (w_in, w_br_rnn, w_br_att, w_out)]
    caches, hs = (), []
    for l in range(DEPTH):
        lw = dict(
            norm_g=norm_g[l], w_in=w_proj[0], conv_w=conv_w[l], conv_b=conv_b[l][None, :],
            wg=_lru_gate_weights(lru_wa[l], lru_wx[l]),
            gb=jnp.stack([lru_ba[l, 0], lru_bx[l, 0], lru_ba[l, 1], lru_bx[l, 1]], axis=0), lam=lru_lam[l],
            w_br_rnn=w_proj[1], w_br_att=w_proj[2], w_out=w_proj[3],
            w_router_t=w_router_t, router_bias=router_bias, g_final=final_norm_g[None, :],
            w_exp_gate=w_exp[0], w_exp_up=w_exp[1], w_exp_down=w_exp[2])
        xp, caches, h_l = _layer(xp, mods[l, DEC_BATCH:DEC_BATCH + 1], SEQ, False, lw, zeros_h0, None, None, l,
                                 caches)
        hs.append(h_l)
        ctx_kv = (cache_k[:, l].reshape(DEC_BATCH, PAST_LEN, D_ATT).astype(BF16),
                  cache_v[:, l].reshape(DEC_BATCH, PAST_LEN, D_ATT).astype(BF16))
        xs, _, _ = _layer(xs, mods[l, :DEC_BATCH], DEC_SEQ, True, lw, state_lru[:, l], ctx_kv, tables, l)
    y_prompt = xp.reshape(BATCH, SEQ, D_MODEL)
    y_sample = xs.reshape(DEC_BATCH, DEC_SEQ, D_MODEL)
    return (y_prompt, y_sample, caches[0], caches[1], jnp.stack(hs, axis=1))
```

```python
import functools

import jax
import jax.numpy as jnp
import numpy as np
from jax import lax
from jax.experimental import pallas as pl
from jax.experimental.pallas import tpu as pltpu

D_MODEL = 1024
BATCH = 16
SEQ = 256
DEPTH = 2
DEC_BATCH = 4
DEC_SEQ = 4096
PAST_LEN = 256

GRID_W = 64
D_RNN = 1024
N_LRU_BLOCKS = 16
LRU_BLOCK = D_RNN // N_LRU_BLOCKS
CONV_W = 4
LRU_C = 8.0
N_HEADS = 16
HEAD_DIM = 64
D_ATT = N_HEADS * HEAD_DIM
WIN_H = 8
WIN_W = 16
N_EXPERTS = 16
N_GROUPS = 4
EXPERTS_PER_GROUP = N_EXPERTS // N_GROUPS
D_EXPERT = 512
N_MOD = 6
D_IN = 2 * D_RNN + 3 * D_ATT + 2 * D_MODEL
EPS = 1e-6
NEG_INF = -1e30

BF16 = jnp.bfloat16
F32 = jnp.float32

V7X_LANES = 128
V7X_SUBLANES = 8
V7X_MXU_DIM = 256
V7X_VMEM_BYTES = 64 * 1024 * 1024
VMEM_LIMIT = V7X_VMEM_BYTES - 8 * 1024 * 1024

CHUNK = D_MODEL
N_CHUNKS = D_IN // CHUNK
C_XRNN, C_GRNN, C_Q, C_K, C_V, C_GATE_R, C_GATE_A = range(N_CHUNKS)

NT_DIMS = (((1,), (1,)), ((), ()))


def _params(*sem):
    return pltpu.CompilerParams(dimension_semantics=sem, vmem_limit_bytes=VMEM_LIMIT)


def _adaln_kernel(c_ref, w_ref, b_ref, o_ref):
    cv = c_ref[...]
    s = cv * jax.nn.sigmoid(cv)
    o_ref[0] = jnp.dot(s.astype(BF16), w_ref[0].astype(BF16), preferred_element_type=F32) + b_ref[0]


def adaln_pallas(cvecs, w_ada, b_ada):
    r = cvecs.shape[0]
    return pl.pallas_call(
        _adaln_kernel,
        grid=(DEPTH, N_MOD),
        in_specs=[pl.BlockSpec((r, D_MODEL), lambda l, j: (0, 0)),
                  pl.BlockSpec((1, D_MODEL, D_MODEL), lambda l, j: (l, 0, j)),
                  pl.BlockSpec((1, 1, D_MODEL), lambda l, j: (l, 0, j))],
        out_specs=pl.BlockSpec((1, r, D_MODEL), lambda l, j: (l, 0, j)),
        out_shape=jax.ShapeDtypeStruct((DEPTH, r, N_MOD * D_MODEL), F32),
        compiler_params=_params("arbitrary", "arbitrary"),
        name="adaln",
    )(cvecs, w_ada, b_ada.reshape(DEPTH, 1, N_MOD * D_MODEL))


def _mod_spec(tm, seq_len, per_seq):
    if per_seq:
        return pl.BlockSpec((1, N_MOD, D_MODEL), lambda i, *_: (i * tm // seq_len, 0, 0))
    return pl.BlockSpec((1, N_MOD, D_MODEL), lambda i, *_: (0, 0, 0))


def _rms(x, g):
    return x * lax.rsqrt(jnp.mean(x * x, axis=-1, keepdims=True) + EPS) * g


def _inproj_kernel(x_ref, mod_ref, g_ref, w_ref, *refs, emit_kv):
    if emit_kv:
        p_ref, k32_ref, v32_ref, h_scr = refs[-4:]
    else:
        p_ref, h_scr = refs
    j = pl.program_id(1)

    @pl.when(j == 0)
    def _():
        y = _rms(x_ref[...], g_ref[...])
        h_scr[...] = (y * (1.0 + mod_ref[0, 1:2, :]) + mod_ref[0, 0:1, :]).astype(BF16)

    acc = jnp.dot(h_scr[...], w_ref[0], preferred_element_type=F32)
    p_ref[...] = (acc * jnp.where(j == C_Q, HEAD_DIM ** -0.5, 1.0)).astype(BF16)
    if emit_kv:
        @pl.when(j == C_K)
        def _():
            k32_ref[...] = acc.reshape(k32_ref.shape)

        @pl.when(j == C_V)
        def _():
            v32_ref[...] = acc.reshape(v32_ref.shape)


def inproj_pallas(x, mod, g, w_in, layer, seq_len, per_seq, tm, emit_kv, caches=()):
    n = x.shape[0]
    row = lambda i, j: (i, 0)
    in_specs = [pl.BlockSpec((tm, D_MODEL), row), _mod_spec(tm, seq_len, per_seq),
                pl.BlockSpec((1, D_MODEL), lambda i, j: (0, 0)),
                pl.BlockSpec((1, D_MODEL, CHUNK), lambda i, j: (layer, 0, j))]
    out_shape = [jax.ShapeDtypeStruct((n, D_IN), BF16)]
    out_specs = [pl.BlockSpec((tm, CHUNK), lambda i, j: (i, j))]
    aliases = {}
    if emit_kv:
        out_shape += [jax.ShapeDtypeStruct((n // seq_len, DEPTH, seq_len, N_HEADS, HEAD_DIM), F32)] * 2
        out_specs += [pl.BlockSpec((tm // seq_len, 1, seq_len, N_HEADS, HEAD_DIM),
                                   lambda i, j: (i, layer, 0, 0, 0))] * 2
        aliases = {len(in_specs) + k: 1 + k for k in range(len(caches))}
        in_specs += [pl.BlockSpec(memory_space=pl.ANY)] * len(caches)
    return pl.pallas_call(
        functools.partial(_inproj_kernel, emit_kv=emit_kv),
        grid=(n // tm, N_CHUNKS),
        in_specs=in_specs,
        out_specs=out_specs,
        out_shape=out_shape,
        input_output_aliases=aliases,
        scratch_shapes=[pltpu.VMEM((tm, D_MODEL), BF16)],
        compiler_params=_params("arbitrary", "arbitrary"),
        name="inproj",
    )(x, mod, g, w_in, *caches)


LRU_CB = 512
LRU_TC = 256
LRU_SUB = V7X_MXU_DIM
LRU_HALO = 16
LRU_PIPE = 4


LRU_SEG = LRU_TC // V7X_SUBLANES
LRU_NSLAB = LRU_SEG + CONV_W


def _lru_row_maps():
    s = np.arange(V7X_SUBLANES)[None, :]
    src = (LRU_HALO - CONV_W // 2) + LRU_SEG * s + np.arange(LRU_NSLAB)[:, None]
    sel = np.zeros((LRU_NSLAB * V7X_SUBLANES, LRU_TC + 2 * LRU_HALO), np.float32)
    sel[np.arange(sel.shape[0]), src.reshape(-1)] = 1.0
    tok = (LRU_SEG * s + np.arange(LRU_SEG)[:, None]).reshape(-1)
    perm = np.zeros((LRU_TC, LRU_TC), np.float32)
    perm[np.arange(LRU_TC), tok] = 1.0
    return jnp.asarray(sel, BF16), jnp.asarray(perm, BF16), jnp.asarray(perm.T, BF16)


def _slab_scan(a3, b3, carry, reverse):
    n = a3.shape[0]
    hs, cum = [None] * n, [None] * n
    h = cp = None
    for t in (range(n - 1, -1, -1) if reverse else range(n)):
        h = b3[t] if h is None else a3[t] * h + b3[t]
        cp = a3[t] if cp is None else cp * a3[t]
        hs[t], cum[t] = h, cp
    sub = lax.broadcasted_iota(jnp.int32, (V7X_SUBLANES, 1), 0)
    pa, pb = cp, h
    s = 1
    while s < V7X_SUBLANES:
        shift = V7X_SUBLANES - s if reverse else s
        ok = (sub < V7X_SUBLANES - s) if reverse else (sub >= s)
        a_sh = pltpu.roll(pa, shift, axis=0)
        b_sh = pltpu.roll(pb, shift, axis=0)
        pb = jnp.where(ok, pa * b_sh + pb, pb)
        pa = jnp.where(ok, pa * a_sh, pa)
        s *= 2
    leaving = pb + pa * carry
    first, last = (V7X_SUBLANES - 1, 0) if reverse else (0, V7X_SUBLANES - 1)
    entering = jnp.where(sub == first, carry, pltpu.roll(leaving, V7X_SUBLANES - 1 if reverse else 1, axis=0))
    out = jnp.concatenate([hs[t] + cum[t] * entering for t in range(n)], axis=0)
    return out, leaving[last:last + 1, :]


def _sigmoid(x):
    return 0.5 * jnp.tanh(0.5 * x) + 0.5


def _lru_kernel(x_ref, gate_ref, h0_ref, cw_ref, cb_ref, wg_ref, gb_ref, lam_ref, sel_ref, perm_ref, permt_ref,
                y_ref, fin_ref, hf_scr, u_scr):
    t_len = x_ref.shape[1]
    cb = x_ref.shape[2]
    n_chunks = t_len // LRU_TC

    def conv_chunk(c):
        t0 = pl.multiple_of(c * LRU_TC, LRU_TC)
        cur = x_ref[0, pl.ds(t0, LRU_TC), :]
        lo = pl.multiple_of(jnp.maximum(t0 - LRU_HALO, 0), LRU_HALO)
        hi = pl.multiple_of(jnp.minimum(t0 + LRU_TC, t_len - LRU_HALO), LRU_HALO)
        prev = x_ref[0, pl.ds(lo, LRU_HALO), :]
        nxt = x_ref[0, pl.ds(hi, LRU_HALO), :]
        prev = jnp.where(c > 0, prev, jnp.zeros_like(prev))
        nxt = jnp.where(c < n_chunks - 1, nxt, jnp.zeros_like(nxt))
        ext = jnp.concatenate([prev, cur, nxt], axis=0)
        xs = jnp.dot(sel_ref[...], ext, preferred_element_type=F32).reshape(LRU_NSLAB, V7X_SUBLANES, cb)
        u = cb_ref[...][None] + jnp.zeros((LRU_SEG, V7X_SUBLANES, cb), F32)
        for j in range(CONV_W):
            u = u + xs[j:j + LRU_SEG] * cw_ref[j:j + 1, :][None]
        return t0, u.reshape(LRU_TC, cb)

    def slabs(v):
        return v.reshape(LRU_SEG, V7X_SUBLANES, cb)

    def gate_matmuls(u, d):
        ub = u.astype(BF16)
        pre = [jnp.dot(ub[:, LRU_SUB * s:LRU_SUB * (s + 1)], wg_ref[d, s], preferred_element_type=F32)
               for s in range(cb // LRU_SUB)]
        return (jnp.concatenate([p[:, :LRU_SUB] for p in pre], axis=1),
                jnp.concatenate([p[:, LRU_SUB:] for p in pre], axis=1))

    def gates(u, pre_a, pre_x, d):
        r = _sigmoid(pre_a + gb_ref[2 * d:2 * d + 1, :])
        i = _sigmoid(pre_x + gb_ref[2 * d + 1:2 * d + 2, :])
        log_a = (-LRU_C * jax.nn.softplus(-lam_ref[d:d + 1, :])) * r
        a = jnp.exp(log_a)
        th = jnp.tanh(log_a)
        num = -2.0 * th
        scale = jnp.where(num > 0.0, num * lax.rsqrt(num * (1.0 - th)), 0.0)
        inp = scale * (i * u)
        return a, inp

    group = LRU_PIPE if n_chunks % LRU_PIPE == 0 else 1

    def pipelined(first_stage, second_stage, base, carry):
        pending = None
        for g in range(group + 1):
            upcoming = first_stage(base + g) if g < group else None
            if pending is not None:
                carry = second_stage(pending, carry)
            pending = upcoming
        return carry

    def fwd_matmuls(c):
        t0, u = conv_chunk(c)
        u_scr[pl.ds(t0, LRU_TC), :] = u
        return (t0, u) + gate_matmuls(u, 0)

    def fwd_scan(stage, carry):
        t0, u, pre_a, pre_x = stage
        a, inp = gates(u, pre_a, pre_x, 0)
        h, carry = _slab_scan(slabs(a), slabs(inp), carry, reverse=False)
        hf_scr[pl.ds(t0, LRU_TC), :] = h
        return carry

    fin_f = lax.fori_loop(0, n_chunks // group,
                          lambda k, carry: pipelined(fwd_matmuls, fwd_scan, k * group, carry), h0_ref[0, 0:1, :])

    def bwd_matmuls(k):
        t0 = pl.multiple_of((n_chunks - 1 - k) * LRU_TC, LRU_TC)
        u = u_scr[pl.ds(t0, LRU_TC), :]
        g = jnp.dot(perm_ref[...], gate_ref[0, pl.ds(t0, LRU_TC), :], preferred_element_type=F32)
        return (t0, u, g) + gate_matmuls(u, 1)

    def bwd_scan(stage, carry):
        t0, u, g, pre_a, pre_x = stage
        a, inp = gates(u, pre_a, pre_x, 1)
        h, carry = _slab_scan(slabs(a), slabs(inp), carry, reverse=True)
        y = ((hf_scr[pl.ds(t0, LRU_TC), :] + h) * jax.nn.gelu(g)).astype(BF16)
        y_ref[0, pl.ds(t0, LRU_TC), :] = jnp.dot(permt_ref[...], y, preferred_element_type=F32).astype(y_ref.dtype)
        return carry

    fin_b = lax.fori_loop(0, n_chunks // group,
                          lambda k, carry: pipelined(bwd_matmuls, bwd_scan, k * group, carry), h0_ref[0, 1:2, :])
    fin_ref[0, 0:1, :] = fin_f
    fin_ref[0, 1:2, :] = fin_b


def _lru_gate_weights(lru_wa, lru_wx):
    per = LRU_SUB // LRU_BLOCK
    eye = jnp.eye(per, dtype=F32)

    def dense(w):
        w = w.reshape(2, D_RNN // LRU_SUB, per, LRU_BLOCK, LRU_BLOCK)
        full = w[:, :, :, :, None, :] * eye[None, None, :, None, :, None]
        return full.reshape(2, D_RNN // LRU_SUB, LRU_SUB, LRU_SUB)

    return jnp.concatenate([dense(lru_wa), dense(lru_wx)], axis=-1).astype(BF16)


def lru_pallas(p, h0, conv_w, conv_b, wg, gb, lam):
    b, t, _ = p.shape
    n_cb = D_RNN // LRU_CB
    maps = _lru_row_maps()
    whole = lambda m: pl.BlockSpec(m.shape, lambda bi, ci: (0, 0))
    return pl.pallas_call(
        _lru_kernel,
        grid=(b, n_cb),
        in_specs=[pl.BlockSpec((1, t, LRU_CB), lambda bi, ci: (bi, 0, C_XRNN * n_cb + ci)),
                  pl.BlockSpec((1, t, LRU_CB), lambda bi, ci: (bi, 0, C_GRNN * n_cb + ci)),
                  pl.BlockSpec((1, 2, LRU_CB), lambda bi, ci: (bi, 0, ci)),
                  pl.BlockSpec((CONV_W, LRU_CB), lambda bi, ci: (0, ci)),
                  pl.BlockSpec((1, LRU_CB), lambda bi, ci: (0, ci)),
                  pl.BlockSpec((2, LRU_CB // LRU_SUB, LRU_SUB, 2 * LRU_SUB), lambda bi, ci: (0, ci, 0, 0)),
                  pl.BlockSpec((4, LRU_CB), lambda bi, ci: (0, ci)),
                  pl.BlockSpec((2, LRU_CB), lambda bi, ci: (0, ci))] + [whole(m) for m in maps],
        out_specs=[pl.BlockSpec((1, t, LRU_CB), lambda bi, ci: (bi, 0, ci)),
                   pl.BlockSpec((1, 2, LRU_CB), lambda bi, ci: (bi, 0, ci))],
        out_shape=[jax.ShapeDtypeStruct((b, t, D_RNN), BF16), jax.ShapeDtypeStruct((b, 2, D_RNN), F32)],
        scratch_shapes=[pltpu.VMEM((t, LRU_CB), F32), pltpu.VMEM((t, LRU_CB), F32)],
        compiler_params=_params("arbitrary", "arbitrary"),
        name="lru",
    )(p, p, h0, conv_w, conv_b, wg, gb, lam, *maps)


Q_ROWS = 4
Q_TILE = Q_ROWS * GRID_W
KEY_ROWS = 12
N_Q_TILES = DEC_SEQ // Q_TILE
ATT_LANES = 1024
HEADS_PER_STEP = ATT_LANES // HEAD_DIM
PAIR = 2 * HEAD_DIM


def _window_block(t):
    return jnp.clip(t - 1, 0, N_Q_TILES - KEY_ROWS // Q_ROWS)


N_DR_PAIRS = 2 * WIN_H


def _natten_tables(rpb):
    c = np.arange(GRID_W)
    q_start = np.clip(c - WIN_W // 2, 0, GRID_W - WIN_W)
    col_valid = (c[None, :] >= q_start[:, None]) & (c[None, :] < q_start[:, None] + WIN_W)
    dc = c[None, :] - c[:, None] + WIN_W - 1
    sel_c = ((dc[None] == np.arange(2 * WIN_W - 1)[:, None, None]) & col_valid[None]).astype(np.float32)
    t = jnp.einsum('lhrd,dck->lhrck', rpb.astype(F32), jnp.asarray(sel_c), precision=lax.Precision.HIGHEST)
    t = jnp.where(col_valid[None, None, None], t, NEG_INF)
    t = jnp.pad(t, ((0, 0), (0, 0), (1, 1), (0, 0), (0, 0)), constant_values=NEG_INF)
    return jnp.concatenate([t[:, :, :-1], t[:, :, 1:]], axis=-1)


def _attend_heads(n_pairs, q_pair, key_blocks, val_blocks, bias_blocks):
    lane_head = lax.broadcasted_iota(jnp.int32, (1, PAIR), 1) // HEAD_DIM

    def scores(h):
        hp, j = divmod(h, 2)
        mine = lane_head == j
        q2 = q_pair(hp)
        qh = jnp.where(mine, q2, jnp.zeros_like(q2))
        s = []
        for kb, bb in zip(key_blocks(hp), bias_blocks(h)):
            sd = lax.dot_general(qh, kb, NT_DIMS, preferred_element_type=F32)
            s.append(sd if bb is None else sd + bb)
        m = s[0].max(axis=-1, keepdims=True)
        for sd in s[1:]:
            m = jnp.maximum(m, sd.max(axis=-1, keepdims=True))
        return hp, mine, s, m

    def weighted(hp, mine, s, m):
        o = None
        for sd, vb in zip(s, val_blocks(hp)):
            pv = jnp.dot(jnp.exp(sd - m).astype(BF16), jnp.where(mine, vb, jnp.ones_like(vb)),
                         preferred_element_type=F32)
            o = pv if o is None else o + pv
        return jnp.where(mine, o / pltpu.roll(o, HEAD_DIM, axis=1), 0.0)

    outs = [None] * n_pairs
    pending = None
    for h in range(2 * n_pairs + 1):
        upcoming = scores(h) if h < 2 * n_pairs else None
        if pending is not None:
            o = weighted(*pending)
            outs[pending[0]] = o if outs[pending[0]] is None else outs[pending[0]] + o
        pending = upcoming
    return jnp.concatenate(outs, axis=1)


def _natten_kernel(q_ref, k0_ref, k1_ref, k2_ref, v0_ref, v1_ref, v2_ref, kc_ref, vc_ref, tt_ref, o_ref):
    ti = pl.program_id(1)
    rows = DEC_SEQ // GRID_W
    wstart = _window_block(ti) * Q_ROWS
    first_row = lax.broadcasted_iota(jnp.int32, (1, 2 * GRID_W), 1) < GRID_W
    pieces = {}
    for a in range(Q_ROWS):
        r = ti * Q_ROWS + a
        start_r = jnp.clip(r - WIN_H // 2, 0, rows - WIN_H)
        for i in range(0, KEY_ROWS, 2):
            kr = wstart + i
            ok = [((kr + e >= start_r) & (kr + e < start_r + WIN_H)).astype(jnp.int32) for e in range(2)]
            pieces[a, i] = (jnp.clip(kr - r + WIN_H, 0, N_DR_PAIRS - 1), jnp.where(first_row, ok[0], ok[1]) > 0)

    def bias_block(h, d):
        return jnp.concatenate(
            [jnp.concatenate([jnp.where(pieces[a, i][1], tt_ref[0, h, pieces[a, i][0]], NEG_INF)
                              for i in range(Q_ROWS * d, Q_ROWS * (d + 1), 2)], axis=1)
             for a in range(Q_ROWS)], axis=0)

    k_refs = (k0_ref, k1_ref, k2_ref, kc_ref)
    v_refs = (v0_ref, v1_ref, v2_ref, vc_ref)
    lanes = lambda hp: slice(PAIR * hp, PAIR * (hp + 1))
    o_ref[0] = _attend_heads(
        ATT_LANES // PAIR, lambda hp: q_ref[0, :, lanes(hp)],
        lambda hp: [r[0, :, lanes(hp)] for r in k_refs], lambda hp: [r[0, :, lanes(hp)] for r in v_refs],
        lambda h: [bias_block(h, d) for d in range(KEY_ROWS // Q_ROWS)] + [None]).astype(o_ref.dtype)


def natten_pallas(p, k_ctx, v_ctx, tables, layer):
    b, t, _ = p.shape
    n_lb = D_ATT // ATT_LANES
    q_spec = pl.BlockSpec((1, Q_TILE, ATT_LANES), lambda lb, ti, bi: (bi, ti, C_Q * n_lb + lb))
    win = lambda ch, d: pl.BlockSpec((1, Q_TILE, ATT_LANES),
                                     lambda lb, ti, bi: (bi, _window_block(ti) + d, ch * n_lb + lb))
    ctx = pl.BlockSpec((1, PAST_LEN, ATT_LANES), lambda lb, ti, bi: (bi, 0, lb))
    return pl.pallas_call(
        _natten_kernel,
        grid=(n_lb, N_Q_TILES, b),
        in_specs=[q_spec, win(C_K, 0), win(C_K, 1), win(C_K, 2), win(C_V, 0), win(C_V, 1), win(C_V, 2), ctx, ctx,
                  pl.BlockSpec((1, HEADS_PER_STEP, N_DR_PAIRS, GRID_W, 2 * GRID_W),
                               lambda lb, ti, bi: (layer, lb, 0, 0, 0))],
        out_specs=pl.BlockSpec((1, Q_TILE, ATT_LANES), lambda lb, ti, bi: (bi, ti, lb)),
        out_shape=jax.ShapeDtypeStruct((b, t, D_ATT), BF16),
        compiler_params=_params("arbitrary", "arbitrary", "arbitrary"),
        name="natten",
    )(p, p, p, p, p, p, p, k_ctx, v_ctx, tables)


def _ctxatt_kernel(q_ref, k_ref, v_ref, o_ref):
    lanes = lambda hp: slice(PAIR * hp, PAIR * (hp + 1))
    o_ref[0] = _attend_heads(ATT_LANES // PAIR, lambda hp: q_ref[0, :, lanes(hp)], lambda hp: [k_ref[0, :, lanes(hp)]],
                             lambda hp: [v_ref[0, :, lanes(hp)]], lambda h: [None]).astype(o_ref.dtype)


def ctxatt_pallas(p):
    b, t, _ = p.shape
    n_lb = D_ATT // ATT_LANES
    blk = lambda ch: pl.BlockSpec((1, t, ATT_LANES), lambda lb, bi: (bi, 0, ch * n_lb + lb))
    return pl.pallas_call(
        _ctxatt_kernel,
        grid=(n_lb, b),
        in_specs=[blk(C_Q), blk(C_K), blk(C_V)],
        out_specs=pl.BlockSpec((1, t, ATT_LANES), lambda lb, bi: (bi, 0, lb)),
        out_shape=jax.ShapeDtypeStruct((b, t, D_ATT), BF16),
        compiler_params=_params("arbitrary", "arbitrary"),
        name="ctxatt",
    )(p, p, p)


def _split_bf16(x):
    hi = x.astype(BF16)
    return hi, (x - hi.astype(F32)).astype(BF16)


def _route(logits_t, rb_ref):
    score = [jax.nn.sigmoid(logits_t[e:e + 1, :]) for e in range(N_EXPERTS)]
    sel = [score[e] + rb_ref[e] for e in range(N_EXPERTS)]
    best_g = None
    for g in range(N_GROUPS):
        v = sel[EXPERTS_PER_GROUP * g:EXPERTS_PER_GROUP * (g + 1)]
        top2 = None
        for i in range(EXPERTS_PER_GROUP):
            for j in range(i + 1, EXPERTS_PER_GROUP):
                pair = v[i] + v[j]
                top2 = pair if top2 is None else jnp.maximum(top2, pair)
        if best_g is None:
            best_g, best_v = jnp.zeros_like(top2, dtype=jnp.int32), top2
        else:
            upd = top2 > best_v
            best_g = jnp.where(upd, g, best_g)
            best_v = jnp.where(upd, top2, best_v)

    def in_best(vals, j):
        out = vals[j]
        for g in range(1, N_GROUPS):
            out = jnp.where(best_g == g, vals[EXPERTS_PER_GROUP * g + j], out)
        return out

    v = [in_best(sel, j) for j in range(EXPERTS_PER_GROUP)]
    sc = [in_best(score, j) for j in range(EXPERTS_PER_GROUP)]

    def first_argmax(vals):
        idx, top = jnp.zeros_like(best_g), vals[0]
        for j in range(1, EXPERTS_PER_GROUP):
            upd = vals[j] > top
            idx = jnp.where(upd, j, idx)
            top = jnp.where(upd, vals[j], top)
        return idx

    i1 = first_argmax(v)
    i2 = first_argmax([jnp.where(i1 == j, -jnp.inf, v[j]) for j in range(EXPERTS_PER_GROUP)])
    pick = lambda idx: sum(jnp.where(idx == j, sc[j], 0.0) for j in range(EXPERTS_PER_GROUP))
    w1, w2 = pick(i1), pick(i2)
    den = w1 + w2
    c1, c2 = w1 / den, w2 / den
    rows = []
    for e in range(N_EXPERTS):
        g, j = divmod(e, EXPERTS_PER_GROUP)
        rows.append(jnp.where(best_g == g, jnp.where(i1 == j, c1, 0.0) + jnp.where(i2 == j, c2, 0.0), 0.0))
    return jnp.concatenate(rows, axis=0), best_g


AUX_LANES = V7X_LANES
AUX_MID = N_EXPERTS
AUX_LO = 2 * N_EXPERTS
AUX_GROUP = 3 * N_EXPERTS
D_MOE_IN = D_MODEL + AUX_LANES
MERGE_CHAINS = 2


def _merge_kernel(rb_ref, yr_ref, ya_ref, gr_ref, ga_ref, x_ref, mod_ref, g_ref, wr_ref, wa_ref, wo_ref, wrt_ref,
                  x1_ref, h2_ref, gid_ref):
    w_hi, w_lo = _split_bf16(wrt_ref[...])
    w_parts = jnp.concatenate([w_hi, w_lo], axis=0)
    dot_nt = lambda a, b: lax.dot_general(a, b, NT_DIMS, preferred_element_type=F32)
    n_rows = x_ref.shape[0] // MERGE_CHAINS

    def project(ci):
        rows = slice(ci * n_rows, (ci + 1) * n_rows)
        m = (_sigmoid(gr_ref[rows, :].astype(F32)) * jnp.dot(yr_ref[rows, :], wr_ref[0], preferred_element_type=F32)
             + _sigmoid(ga_ref[rows, :].astype(F32)) * jnp.dot(ya_ref[rows, :], wa_ref[0],
                                                              preferred_element_type=F32))
        o = jnp.dot(m.astype(BF16), wo_ref[0], preferred_element_type=F32)
        x1 = x_ref[rows, :] + mod_ref[0, 2:3, :] * o
        x1_ref[rows, :] = x1
        return rows, _rms(x1, g_ref[...]) * (1.0 + mod_ref[0, 4:5, :]) + mod_ref[0, 3:4, :]

    def route(rows, h2):
        h_hi, h_lo = _split_bf16(h2)
        by_h_hi = dot_nt(w_parts, h_hi)
        logits_t = by_h_hi[:N_EXPERTS] + (dot_nt(w_hi, h_lo) + by_h_hi[N_EXPERTS:])
        comb_t, group = _route(logits_t, rb_ref)
        group = group.astype(F32)
        gid_ref[:, rows] = jnp.concatenate([group, jnp.zeros((V7X_SUBLANES - 1, n_rows), F32)], axis=0)
        padded = jnp.concatenate([comb_t, jnp.zeros((AUX_GROUP - N_EXPERTS, n_rows), F32), group,
                                  jnp.zeros((AUX_LANES - AUX_GROUP - 1, n_rows), F32)], axis=0)
        c = padded.T
        c_hi = c.astype(BF16).astype(F32)
        c_mid = (c - c_hi).astype(BF16).astype(F32)
        c_lo = c - c_hi - c_mid
        aux = c_hi + pltpu.roll(c_mid, AUX_MID, axis=1) + pltpu.roll(c_lo, AUX_LO, axis=1)
        h2_ref[rows, :] = jnp.concatenate([h2.astype(BF16), aux.astype(BF16)], axis=1)

    pending = None
    for ci in range(MERGE_CHAINS + 1):
        upcoming = project(ci) if ci < MERGE_CHAINS else None
        if pending is not None:
            route(*pending)
        pending = upcoming


def merge_pallas(y_rnn, y_att, p, x, mod, g, w_br_rnn, w_br_att, w_out, w_router_t, router_bias, layer, seq_len,
                 per_seq, tm):
    n = x.shape[0]
    row = lambda i: (i, 0)
    full = lambda shape: pl.BlockSpec(shape, lambda i: (0, 0))
    weight = pl.BlockSpec((1, D_MODEL, D_MODEL), lambda i: (layer, 0, 0))
    return pl.pallas_call(
        _merge_kernel,
        grid=(n // tm,),
        in_specs=[pl.BlockSpec(memory_space=pltpu.SMEM),
                  pl.BlockSpec((tm, D_RNN), row), pl.BlockSpec((tm, D_ATT), row),
                  pl.BlockSpec((tm, CHUNK), lambda i: (i, C_GATE_R)), pl.BlockSpec((tm, CHUNK), lambda i: (i, C_GATE_A)),
                  pl.BlockSpec((tm, D_MODEL), row), _mod_spec(tm, seq_len, per_seq), full((1, D_MODEL)),
                  weight, weight, weight, full((N_EXPERTS, D_MODEL))],
        out_specs=[pl.BlockSpec((tm, D_MODEL), row), pl.BlockSpec((tm, D_MOE_IN), row),
                   pl.BlockSpec((V7X_SUBLANES, tm), lambda i: (0, i))],
        out_shape=[jax.ShapeDtypeStruct((n, D_MODEL), F32), jax.ShapeDtypeStruct((n, D_MOE_IN), BF16),
                   jax.ShapeDtypeStruct((V7X_SUBLANES, n), F32)],
        compiler_params=_params("arbitrary"),
        name="merge",
    )(router_bias, y_rnn, y_att, p, p, x, mod, g, w_br_rnn, w_br_att, w_out, w_router_t)


MOE_TB = 1024
MOE_RT = 128
MOE_TBP = MOE_TB + N_GROUPS * MOE_RT


def _group_segments(gid_row):
    sub = lax.broadcasted_iota(jnp.int32, (V7X_SUBLANES, 1), 0).astype(F32)
    onehot = (gid_row == sub).astype(F32)
    cnt = jnp.sum(onehot, axis=1, keepdims=True)
    padded = jnp.floor((cnt + (MOE_RT - 1)) * (1.0 / MOE_RT)) * MOE_RT
    starts, run = [], jnp.zeros((1, 1), F32)
    for g in range(N_GROUPS):
        starts.append(run)
        run = run + padded[g:g + 1, :]
    return onehot, starts, [padded[g:g + 1, :] for g in range(N_GROUPS)], run


def _to_int(v):
    return v[0, 0].astype(jnp.int32)


def _moe_kernel(h_ref, gid_ref, x_ref, mod_ref, gf_ref, wg_ref, wu_ref, wd_ref, o_ref,
                p_scr, xs_scr, cs_scr, ys_scr, *, final_norm):
    step = pl.program_id(1)
    onehot, starts, sizes, used = _group_segments(gid_ref[0:1, :])
    lane = lax.broadcasted_iota(jnp.int32, (1, AUX_LANES), 1)

    @pl.when(step == 0)
    def _():
        t_row = lax.broadcasted_iota(jnp.int32, (MOE_TB, MOE_TB), 0)
        t_col = lax.broadcasted_iota(jnp.int32, (MOE_TB, MOE_TB), 1)
        earlier = (t_row < t_col).astype(BF16)
        rank = jnp.dot(onehot.astype(BF16), earlier, preferred_element_type=F32)
        pos = jnp.zeros((1, MOE_TB), F32)
        for g in range(N_GROUPS):
            pos = pos + onehot[g:g + 1, :] * (starts[g] + rank[g:g + 1, :])
        dest = lax.broadcasted_iota(jnp.int32, (MOE_TBP, 1), 0).astype(F32)
        p_scr[...] = (dest == pos).astype(BF16)
        for r0 in range(0, MOE_TBP, V7X_MXU_DIM):
            rows = pl.ds(r0, V7X_MXU_DIM)
            sorted_rows = jnp.dot(p_scr[rows, :], h_ref[...], preferred_element_type=F32)
            xs_scr[rows, :] = sorted_rows[:, :D_MODEL].astype(BF16)
            aux = sorted_rows[:, D_MODEL:]
            cs_scr[rows, :] = (aux + pltpu.roll(aux, AUX_LANES - AUX_MID, axis=1)
                               + pltpu.roll(aux, AUX_LANES - AUX_LO, axis=1))
        first_free = pl.multiple_of(_to_int(used), MOE_RT)

        def clear(k, carry):
            ys_scr[pl.ds(pl.multiple_of(first_free + k * MOE_RT, MOE_RT), MOE_RT), :] = jnp.zeros(
                (MOE_RT, D_MODEL), BF16)
            return carry

        lax.fori_loop(0, (MOE_TBP - first_free) // MOE_RT, clear, 0)

    start_v, size_v = starts[0], sizes[0]
    for g in range(1, N_GROUPS):
        start_v = jnp.where(step == g, starts[g], start_v)
        size_v = jnp.where(step == g, sizes[g], size_v)
    seg_start = _to_int(start_v)
    n_tiles = _to_int(size_v) // MOE_RT

    def run_expert(r0, n_rows):
        rows = pl.ds(pl.multiple_of(r0, MOE_RT), n_rows)
        x = xs_scr[rows, :]
        cs = cs_scr[rows, :]
        def hidden(k):
            gate = jnp.dot(x, wg_ref[0, k], preferred_element_type=F32)
            up = jnp.dot(x, wu_ref[0, k], preferred_element_type=F32)
            return ((gate * _sigmoid(gate)) * up).astype(BF16)

        def project(k, act):
            yk = jnp.dot(act, wd_ref[0, k], preferred_element_type=F32)
            return jnp.sum(jnp.where(lane == step * EXPERTS_PER_GROUP + k, cs, 0.0), axis=-1, keepdims=True) * yk

        y, act = None, None
        for k in range(EXPERTS_PER_GROUP + 1):
            upcoming = hidden(k) if k < EXPERTS_PER_GROUP else None
            if act is not None:
                yk = project(k - 1, act)
                y = yk if y is None else y + yk
            act = upcoming
        ys_scr[rows, :] = y.astype(BF16)

    def pair(k, carry):
        run_expert(seg_start + k * (2 * MOE_RT), 2 * MOE_RT)
        return carry

    lax.fori_loop(0, n_tiles // 2, pair, 0)

    @pl.when(n_tiles % 2 == 1)
    def _():
        run_expert(seg_start + (n_tiles - 1) * MOE_RT, MOE_RT)

    @pl.when(step == N_GROUPS - 1)
    def _():
        y = lax.dot_general(p_scr[...], ys_scr[...], (((0,), (0,)), ((), ())),
                            preferred_element_type=F32)
        x2 = x_ref[...] + mod_ref[0, 5:6, :] * y
        o_ref[...] = _rms(x2, gf_ref[...]) if final_norm else x2


def moe_pallas(h2x, gid, x1, mod, g_final, w_gate, w_up, w_down, layer, seq_len, per_seq, final_norm):
    n = x1.shape[0]
    row = lambda i, g: (i, 0)
    group = lambda i, g: (layer, g, 0, 0)
    return pl.pallas_call(
        functools.partial(_moe_kernel, final_norm=final_norm),
        grid=(n // MOE_TB, N_GROUPS),
        in_specs=[pl.BlockSpec((MOE_TB, D_MOE_IN), row),
                  pl.BlockSpec((V7X_SUBLANES, MOE_TB), lambda i, g: (0, i)),
                  pl.BlockSpec((MOE_TB, D_MODEL), row, pipeline_mode=pl.Buffered(1)),
                  _mod_spec(MOE_TB, seq_len, per_seq),
                  pl.BlockSpec((1, D_MODEL), lambda i, g: (0, 0)),
                  pl.BlockSpec((1, EXPERTS_PER_GROUP, D_MODEL, D_EXPERT), group),
                  pl.BlockSpec((1, EXPERTS_PER_GROUP, D_MODEL, D_EXPERT), group),
                  pl.BlockSpec((1, EXPERTS_PER_GROUP, D_EXPERT, D_MODEL), group)],
        out_specs=pl.BlockSpec((MOE_TB, D_MODEL), row),
        out_shape=jax.ShapeDtypeStruct((n, D_MODEL), F32),
        scratch_shapes=[pltpu.VMEM((MOE_TBP, MOE_TB), BF16), pltpu.VMEM((MOE_TBP, D_MODEL), BF16),
                        pltpu.VMEM((MOE_TBP, AUX_LANES), F32), pltpu.VMEM((MOE_TBP, D_MODEL), BF16)],
        compiler_params=_params("arbitrary", "arbitrary"),
        name="moe",
    )(h2x, gid, x1, mod, g_final, w_gate, w_up, w_down)


TM_PROJ = 2048
TM_MERGE = 1024


def _layer(x, mod, seq_len, per_seq, lw, h0, ctx_kv, tables, layer, caches=()):
    n = x.shape[0]
    b = n // seq_len
    emit_kv = ctx_kv is None
    outs = inproj_pallas(x, mod, lw['norm_g'][0:1], lw['w_in'], layer, seq_len, per_seq,
                         TM_PROJ // 4 if emit_kv else TM_PROJ, emit_kv, caches)
    p = outs[0]
    p3 = p.reshape(b, seq_len, D_IN)
    y_rnn, h_fin = lru_pallas(p3, h0, lw['conv_w'], lw['conv_b'], lw['wg'], lw['gb'], lw['lam'])
    if emit_kv:
        y_att = ctxatt_pallas(p3)
    else:
        y_att = natten_pallas(p3, ctx_kv[0], ctx_kv[1], tables, layer)
    x1, h2x, gid = merge_pallas(y_rnn.reshape(n, D_RNN), y_att.reshape(n, D_ATT), p, x, mod, lw['norm_g'][1:2],
                                lw['w_br_rnn'], lw['w_br_att'], lw['w_out'], lw['w_router_t'], lw['router_bias'],
                                layer, seq_len, per_seq, TM_MERGE)
    x2 = moe_pallas(h2x, gid, x1, mod, lw['g_final'], lw['w_exp_gate'], lw['w_exp_up'], lw['w_exp_down'], layer,
                    seq_len, per_seq, layer == DEPTH - 1)
    kv = (outs[1], outs[2]) if emit_kv else None
    return x2, kv, h_fin


def kernel(x_prompt, x_sample, cache_k, cache_v, state_lru, c, c_ctx, w_ada, b_ada, norm_g, w_in, conv_w,
           conv_b, lru_wa, lru_ba, lru_wx, lru_bx, lru_lam, rpb, w_br_rnn, w_br_att, w_out, w_router,
           router_bias, w_exp_gate, w_exp_up, w_exp_down, final_norm_g):
    cvecs = jnp.concatenate([c, c_ctx[None, :], jnp.zeros((V7X_SUBLANES - DEC_BATCH - 1, D_MODEL), F32)], axis=0)
    mods = adaln_pallas(cvecs, w_ada, b_ada).reshape(DEPTH, V7X_SUBLANES, N_MOD, D_MODEL)
    tables = _natten_tables(rpb)
    w_router_t = w_router.T
    xp = x_prompt.reshape(BATCH * SEQ, D_MODEL)
    xs = x_sample.reshape(DEC_BATCH * DEC_SEQ, D_MODEL)
    zeros_h0 = jnp.zeros((BATCH, 2, D_RNN), F32)
    w_exp = [w.astype(BF16) for w in (w_exp_gate, w_exp_up, w_exp_down)]
    w_proj = [w.astype(BF16) for w in (w_in, w_br_rnn, w_br_att, w_out)]
    caches = tuple(jnp.zeros((BATCH, DEPTH, SEQ, N_HEADS, HEAD_DIM), F32) for _ in range(2))
    hs = []
    for l in range(DEPTH):
        lw = dict(
            norm_g=norm_g[l], w_in=w_proj[0], conv_w=conv_w[l], conv_b=conv_b[l][None, :],
            wg=_lru_gate_weights(lru_wa[l], lru_wx[l]),
            gb=jnp.stack([lru_ba[l, 0], lru_bx[l, 0], lru_ba[l, 1], lru_bx[l, 1]], axis=0), lam=lru_lam[l],
            w_br_rnn=w_proj[1], w_br_att=w_proj[2], w_out=w_proj[3],
            w_router_t=w_router_t, router_bias=router_bias, g_final=final_norm_g[None, :],
            w_exp_gate=w_exp[0], w_exp_up=w_exp[1], w_exp_down=w_exp[2])
        xp, caches, h_l = _layer(xp, mods[l, DEC_BATCH:DEC_BATCH + 1], SEQ, False, lw, zeros_h0, None, None, l,
                                 caches)
        hs.append(h_l)
        ctx_kv = (cache_k[:, l].reshape(DEC_BATCH, PAST_LEN, D_ATT).astype(BF16),
                  cache_v[:, l].reshape(DEC_BATCH, PAST_LEN, D_ATT).astype(BF16))
        xs, _, _ = _layer(xs, mods[l, :DEC_BATCH], DEC_SEQ, True, lw, state_lru[:, l], ctx_kv, tables, l)
    y_prompt = xp.reshape(BATCH, SEQ, D_MODEL)
    y_sample = xs.reshape(DEC_BATCH, DEC_SEQ, D_MODEL)
    return (y_prompt, y_sample, caches[0], caches[1], jnp.stack(hs, axis=1))
```
